```python
import math
import jax, jax.numpy as jnp
from jax import lax
import numpy as np

D_MODEL = 1024
BATCH = 4
SEQ = 4096
DEPTH = 2

SB_HEADS = 8
SB_HEAD_DIM = 64
SB_WIDTH = SB_HEADS * SB_HEAD_DIM
DIFF_HEADS = 4
DIFF_QK_DIM = 64
DIFF_V_DIM = 2 * DIFF_QK_DIM
DIFF_QK_WIDTH = DIFF_HEADS * 2 * DIFF_QK_DIM
DIFF_V_WIDTH = DIFF_HEADS * DIFF_V_DIM
ROPE_THETA = 500000.0
ROT_DIM = DIFF_QK_DIM // 4
Q_BLOCK = 128
IN_COLS = 3 * SB_WIDTH + 2 * DIFF_QK_WIDTH + DIFF_V_WIDTH + 2 * D_MODEL
PEER_HEADS = 8
PEER_N_KEYS = 128
PEER_N_EXPERTS = PEER_N_KEYS * PEER_N_KEYS
PEER_TOPK = 16
PEER_QUERY_DIM = 256
PEER_HALF_DIM = PEER_QUERY_DIM // 2
PEER_CHUNK = 128
NORM_EPS = 1e-6

kernel_name = "hybrid_sb_diffattn_peer_adaln"


def rmsnorm(x, g):
    xf = x.astype(jnp.float32)
    y = xf * lax.rsqrt(jnp.mean(xf * xf, axis=-1, keepdims=True) + NORM_EPS)
    return (y * g.astype(jnp.float32)).astype(x.dtype)


def rope_tables(seq_len):
    pos = jnp.arange(seq_len, dtype=jnp.float32)
    inv_freq = 1.0 / (ROPE_THETA ** (jnp.arange(0, ROT_DIM, 2, dtype=jnp.float32) / ROT_DIM))
    ang = pos[:, None] * inv_freq[None, :]
    return jnp.cos(ang), jnp.sin(ang)


def partial_rope(t, cos, sin):
    tf = t.astype(jnp.float32)
    half = ROT_DIM // 2
    t1 = tf[..., :half]
    t2 = tf[..., half:ROT_DIM]
    rot = jnp.concatenate([t1 * cos - t2 * sin, t2 * cos + t1 * sin], axis=-1)
    return jnp.concatenate([rot, tf[..., ROT_DIM:]], axis=-1).astype(t.dtype)


def stick_breaking_attention(q, k, v):
    B, H, S, d = q.shape
    nb = S // Q_BLOCK
    qb = q.reshape(B, H, nb, Q_BLOCK, d).transpose(2, 0, 1, 3, 4)
    kf = k.astype(jnp.float32)
    vf = v.astype(jnp.float32)
    key_pos = jnp.arange(S)
    scale = 1.0 / math.sqrt(d)

    def block(args):
        qi, i = args
        z = jnp.einsum('bhqd,bhkd->bhqk', qi.astype(jnp.float32), kf) * scale
        q_pos = i * Q_BLOCK + jnp.arange(Q_BLOCK)
        mask = key_pos[None, :] < q_pos[:, None]
        log_fail = jnp.where(mask, jax.nn.log_sigmoid(-z), 0.0)
        after = lax.cumsum(log_fail, axis=3, reverse=True) - log_fail
        w = jnp.where(mask, jnp.exp(jax.nn.log_sigmoid(z) + after), 0.0)
        return jnp.einsum('bhqk,bhkd->bhqd', w, vf)

    out = lax.map(block, (qb, jnp.arange(nb)))
    return out.transpose(1, 2, 0, 3, 4).reshape(B, H, S, d).astype(v.dtype)


def diff_attention(q, k, v, lam):
    B, H, _, S, d = q.shape
    dv = v.shape[-1]
    nb = S // Q_BLOCK
    qb = q.reshape(B, H, 2, nb, Q_BLOCK, d).transpose(3, 0, 1, 2, 4, 5)
    kf = k.astype(jnp.float32)
    vf = v.astype(jnp.float32)
    key_pos = jnp.arange(S)
    scale = 1.0 / math.sqrt(d)

    def block(args):
        qi, i = args
        s = jnp.einsum('bhcqd,bhckd->bhcqk', qi.astype(jnp.float32), kf) * scale
        q_pos = i * Q_BLOCK + jnp.arange(Q_BLOCK)
        mask = key_pos[None, :] <= q_pos[:, None]
        p = jax.nn.softmax(jnp.where(mask, s, -jnp.inf), axis=-1)
        w = p[:, :, 0] - lam * p[:, :, 1]
        return jnp.einsum('bhqk,bhkd->bhqd', w, vf)

    out = lax.map(block, (qb, jnp.arange(nb)))
    return out.transpose(1, 2, 0, 3, 4).reshape(B, H, S, dv).astype(v.dtype)


def token_mixer(h, w_in, gate_b, lam_params, subln_g, w_b_sb, w_b_diff, w_out, cos, sin, lam_init):
    B, S, _ = h.shape
    proj = h @ w_in
    sizes = [SB_WIDTH, SB_WIDTH, SB_WIDTH, DIFF_QK_WIDTH, DIFF_QK_WIDTH, DIFF_V_WIDTH, D_MODEL]
    splits = [int(s) for s in np.cumsum(sizes)]
    sb_q, sb_k, sb_v, d_q, d_k, d_v, g_sb_pre, g_diff_pre = jnp.split(proj, splits, axis=-1)

    def heads(t, n, d):
        return t.reshape(B, S, n, d).transpose(0, 2, 1, 3)

    y_sb = stick_breaking_attention(heads(sb_q, SB_HEADS, SB_HEAD_DIM),
                                    heads(sb_k, SB_HEADS, SB_HEAD_DIM),
                                    heads(sb_v, SB_HEADS, SB_HEAD_DIM))
    y_sb = y_sb.transpose(0, 2, 1, 3).reshape(B, S, SB_WIDTH)

    dq = d_q.reshape(B, S, DIFF_HEADS, 2, DIFF_QK_DIM).transpose(0, 2, 3, 1, 4)
    dk = d_k.reshape(B, S, DIFF_HEADS, 2, DIFF_QK_DIM).transpose(0, 2, 3, 1, 4)
    dq = partial_rope(dq, cos, sin)
    dk = partial_rope(dk, cos, sin)
    lp = lam_params.astype(jnp.float32)
    lam = jnp.exp(jnp.sum(lp[0] * lp[1])) - jnp.exp(jnp.sum(lp[2] * lp[3])) + lam_init
    y_diff = diff_attention(dq, dk, heads(d_v, DIFF_HEADS, DIFF_V_DIM), lam)
    y_diff = rmsnorm(y_diff, subln_g) * (1.0 - lam_init)
    y_diff = y_diff.transpose(0, 2, 1, 3).reshape(B, S, DIFF_V_WIDTH)

    g_sb = jax.nn.sigmoid(g_sb_pre + gate_b[:D_MODEL])
    g_diff = jax.nn.sigmoid(g_diff_pre + gate_b[D_MODEL:])
    merged = g_sb * (y_sb @ w_b_sb) + g_diff * (y_diff @ w_b_diff)
    return merged @ w_out


def peer_mixer(h, wq, subkeys, u, v):
    B, S, D = h.shape
    q = (h @ wq).astype(jnp.float32).reshape(B, S, PEER_HEADS, 2, PEER_HALF_DIM)
    scores = jnp.einsum('bshcd,hcnd->bshcn', q, subkeys.astype(jnp.float32))
    s_top, i_top = lax.top_k(scores, PEER_TOPK)
    cand = s_top[..., 0, :, None] + s_top[..., 1, None, :]
    cand_idx = i_top[..., 0, :, None] * PEER_N_KEYS + i_top[..., 1, None, :]
    cand = cand.reshape(B, S, PEER_HEADS, PEER_TOPK * PEER_TOPK)
    cand_idx = cand_idx.reshape(B, S, PEER_HEADS, PEER_TOPK * PEER_TOPK)
    best, pos = lax.top_k(cand, PEER_TOPK)
    expert_idx = jnp.take_along_axis(cand_idx, pos, axis=-1)
    gates = jax.nn.softmax(best, axis=-1)

    T = B * S
    E = PEER_HEADS * PEER_TOPK
    nc = T // PEER_CHUNK
    hc = h.reshape(nc, PEER_CHUNK, D)
    ic = expert_idx.reshape(nc, PEER_CHUNK, E)
    gc = gates.reshape(nc, PEER_CHUNK, E)

    def chunk(args):
        hh, ii, gg = args
        u_sel = jnp.take(u, ii, axis=0)
        a = jnp.einsum('cd,ced->ce', hh.astype(jnp.float32), u_sel.astype(jnp.float32))
        act = jax.nn.gelu(a, approximate=False) * gg
        v_sel = jnp.take(v, ii, axis=0)
        return jnp.einsum('ce,ced->cd', act, v_sel.astype(jnp.float32))

    out = lax.map(chunk, (hc, ic, gc))
    return out.reshape(B, S, D).astype(h.dtype)


def setup_inputs(seed: int = 0) -> dict:
    key = jax.random.key(seed)
    ks = jax.random.split(key, 20)
    f32 = jnp.float32
    D = D_MODEL
    nrm = lambda k, shape, s: jax.random.normal(k, shape, f32) * s
    return {
        "x": nrm(ks[0], (BATCH, SEQ, D), 1.0),
        "c": nrm(ks[1], (BATCH, D), 1.0),
        "norm1_g": 1.0 + nrm(ks[2], (DEPTH, D), 0.02),
        "norm2_g": 1.0 + nrm(ks[3], (DEPTH, D), 0.02),
        "ada_w": nrm(ks[4], (DEPTH, D, 6 * D), 0.5 * D ** -0.5),
        "ada_b": nrm(ks[5], (DEPTH, 6 * D), 0.02),
        "w_in": nrm(ks[6], (DEPTH, D, IN_COLS), D ** -0.5),
        "gate_b": nrm(ks[7], (DEPTH, 2 * D), 0.02),
        "diff_lambda": nrm(ks[8], (DEPTH, 4, DIFF_QK_DIM), 0.1),
        "diff_subln_g": 1.0 + nrm(ks[9], (DEPTH, DIFF_V_DIM), 0.02),
        "w_branch_sb": nrm(ks[10], (DEPTH, SB_WIDTH, D), SB_WIDTH ** -0.5),
        "w_branch_diff": nrm(ks[11], (DEPTH, DIFF_V_WIDTH, D), DIFF_V_WIDTH ** -0.5),
        "w_out": nrm(ks[12], (DEPTH, D, D), D ** -0.5),
        "peer_wq": nrm(ks[13], (DEPTH, D, PEER_HEADS * PEER_QUERY_DIM), D ** -0.5),
        "peer_subkeys": nrm(ks[14], (DEPTH, PEER_HEADS, 2, PEER_N_KEYS, PEER_HALF_DIM), PEER_HALF_DIM ** -0.5),
        "peer_u": nrm(ks[15], (DEPTH, PEER_N_EXPERTS, D), D ** -0.5),
        "peer_v": nrm(ks[16], (DEPTH, PEER_N_EXPERTS, D), 1.0),
        "final_g": 1.0 + nrm(ks[17], (D,), 0.02),
    }


def reference(x, c, norm1_g, norm2_g, ada_w, ada_b, w_in, gate_b, diff_lambda, diff_subln_g,
              w_branch_sb, w_branch_diff, w_out, peer_wq, peer_subkeys, peer_u, peer_v, final_g):
    S = x.shape[1]
    cos, sin = rope_tables(S)
    c_act = jax.nn.silu(c)
    for i in range(DEPTH):
        lam_init = 0.8 - 0.6 * math.exp(-0.3 * i)
        mod = c_act @ ada_w[i] + ada_b[i]
        shift1, scale1, gate1, shift2, scale2, gate2 = jnp.split(mod[:, None, :], 6, axis=-1)
        h = rmsnorm(x, norm1_g[i]) * (1.0 + scale1) + shift1
        y = token_mixer(h, w_in[i], gate_b[i], diff_lambda[i], diff_subln_g[i],
                        w_branch_sb[i], w_branch_diff[i], w_out[i], cos, sin, lam_init)
        x = x + (gate1 * y).astype(x.dtype)
        h = rmsnorm(x, norm2_g[i]) * (1.0 + scale2) + shift2
        y = peer_mixer(h, peer_wq[i], peer_subkeys[i], peer_u[i], peer_v[i])
        x = x + (gate2 * y).astype(x.dtype)
    return rmsnorm(x, final_g)
```

```python
import functools
import math

import jax
import jax.numpy as jnp
from jax import lax
from jax.experimental import pallas as pl
from jax.experimental.pallas import tpu as pltpu

F32 = jnp.float32
BF16 = jnp.bfloat16

D_MODEL = 1024
DEPTH = 2
SB_HEADS = 8
SB_HEAD_DIM = 64
SB_WIDTH = SB_HEADS * SB_HEAD_DIM
DIFF_HEADS = 4
DIFF_QK_DIM = 64
DIFF_V_DIM = 2 * DIFF_QK_DIM
DIFF_QK_WIDTH = DIFF_HEADS * 2 * DIFF_QK_DIM
DIFF_V_WIDTH = DIFF_HEADS * DIFF_V_DIM
ROPE_THETA = 500000.0
ROT_DIM = DIFF_QK_DIM // 4
IN_COLS = 3 * SB_WIDTH + 2 * DIFF_QK_WIDTH + DIFF_V_WIDTH + 2 * D_MODEL
PEER_HEADS = 8
PEER_N_KEYS = 128
PEER_N_EXPERTS = PEER_N_KEYS * PEER_N_KEYS
PEER_TOPK = 16
PEER_QUERY_DIM = 256
PEER_CHUNK_KEYS = 8
PEER_CHUNK = PEER_CHUNK_KEYS * PEER_N_KEYS
NORM_EPS = 1e-6

LANES = 128
VMEM_LIMIT = 56 * 1024 * 1024
NEG_INF = float("-inf")

IN_TILE_N = 512
COL_SB_Q, COL_SB_K, COL_SB_V = 0, SB_WIDTH // LANES, 2 * SB_WIDTH // LANES
COL_D_Q = 3 * SB_WIDTH // LANES
COL_D_K = COL_D_Q + DIFF_QK_WIDTH // LANES
COL_D_V = COL_D_K + DIFF_QK_WIDTH // LANES
COL_G = (3 * SB_WIDTH + 2 * DIFF_QK_WIDTH + DIFF_V_WIDTH) // D_MODEL


def _params(*sem):
    return pltpu.CompilerParams(dimension_semantics=sem, vmem_limit_bytes=VMEM_LIMIT)


def _nt_dot(a, b):
    return lax.dot_general(a, b, (((1,), (1,)), ((), ())), preferred_element_type=F32)


def _mod_kernel(c_ref, w_ref, b_ref, o_ref):
    c = c_ref[...]
    c_act = c * jax.nn.sigmoid(c)
    o_ref[0] = jnp.dot(c_act, w_ref[0], preferred_element_type=F32,
                       precision=lax.Precision.HIGHEST) + b_ref[0]


def _modulation(c, ada_w, ada_b):
    B, D = c.shape
    rows = 8
    c_pad = jnp.pad(c, ((0, rows - B), (0, 0)))
    out = pl.pallas_call(
        _mod_kernel,
        grid=(DEPTH, 6),
        in_specs=[
            pl.BlockSpec((rows, D), lambda l, n: (0, 0)),
            pl.BlockSpec((1, D, D), lambda l, n: (l, 0, n)),
            pl.BlockSpec((1, 1, D), lambda l, n: (l, 0, n)),
        ],
        out_specs=pl.BlockSpec((1, rows, D), lambda l, n: (l, 0, n)),
        out_shape=jax.ShapeDtypeStruct((DEPTH, rows, 6 * D), F32),
        compiler_params=_params("arbitrary", "arbitrary"),
        name="adaln_mod",
    )(c_pad, ada_w, ada_b.reshape(DEPTH, 1, 6 * D))
    return out[:, :B].reshape(DEPTH, B, 6, D)


def _norm_mod(x, g, scale, shift):
    ms = jnp.mean(x * x, axis=-1, keepdims=True)
    return (x * lax.rsqrt(ms + NORM_EPS) * g) * (1.0 + scale) + shift


def _inproj_kernel(x_ref, g_ref, sc_ref, sh_ref, w_ref, ra_ref, rb_ref, rc_ref, o_ref, h_sc):
    j = pl.program_id(1)

    @pl.when(j == 0)
    def _():
        h_sc[...] = _norm_mod(x_ref[...], g_ref[...], sc_ref[0], sh_ref[0]).astype(BF16)

    r = jnp.dot(h_sc[...], w_ref[...], preferred_element_type=F32)
    is_q = jnp.logical_or(j == COL_SB_Q * LANES // IN_TILE_N, j == COL_D_Q * LANES // IN_TILE_N)
    r = r * jnp.where(is_q, 1.0 / math.sqrt(SB_HEAD_DIM), 1.0)
    is_rope = jnp.logical_or(j == COL_D_Q * LANES // IN_TILE_N, j == COL_D_K * LANES // IN_TILE_N)

    @pl.when(is_rope)
    def _():
        ra, rb, rc = ra_ref[...], rb_ref[...], rc_ref[...]
        for blk in range(IN_TILE_N // LANES):
            t = r[:, blk * LANES:(blk + 1) * LANES]
            half = ROT_DIM // 2
            rot = t * ra + pltpu.roll(t, LANES - half, 1) * rb + pltpu.roll(t, half, 1) * rc
            o_ref[:, blk * LANES:(blk + 1) * LANES] = rot.astype(BF16)

    @pl.when(jnp.logical_not(is_rope))
    def _():
        o_ref[...] = r.astype(BF16)


def _rope_lane_tables(seq_len):
    pos = jnp.arange(seq_len, dtype=F32)
    inv_freq = 1.0 / (ROPE_THETA ** (jnp.arange(0, ROT_DIM, 2, dtype=F32) / ROT_DIM))
    ang = pos[:, None] * inv_freq[None, :]
    cos, sin = jnp.cos(ang), jnp.sin(ang)
    half = ROT_DIM // 2
    lane = jnp.arange(LANES) % DIFF_QK_DIM
    f = lane % half
    first, second = lane < half, (lane >= half) & (lane < ROT_DIM)
    ra = jnp.where((first | second)[None, :], cos[:, f], 1.0)
    rb = jnp.where(first[None, :], -sin[:, f], 0.0)
    rc = jnp.where(second[None, :], sin[:, f], 0.0)
    return ra, rb, rc


def _input_projection(x2, norm_g, scale, shift, w_in_bf, rope, B, S):
    T, D = x2.shape
    tm = min(512, S)
    tiles_per_b = S // tm
    ra, rb, rc = rope
    rope_spec = pl.BlockSpec((tm, LANES), lambda i, j: (i % tiles_per_b, 0))
    mod_spec = pl.BlockSpec((1, 1, D), lambda i, j: (i // tiles_per_b, 0, 0))
    return pl.pallas_call(
        _inproj_kernel,
        grid=(T // tm, IN_COLS // IN_TILE_N),
        in_specs=[
            pl.BlockSpec((tm, D), lambda i, j: (i, 0)),
            pl.BlockSpec((1, D), lambda i, j: (0, 0)),
            mod_spec, mod_spec,
            pl.BlockSpec((D, IN_TILE_N), lambda i, j: (0, j)),
            rope_spec, rope_spec, rope_spec,
        ],
        out_specs=pl.BlockSpec((tm, IN_TILE_N), lambda i, j: (i, j)),
        out_shape=jax.ShapeDtypeStruct((T, IN_COLS), BF16),
        scratch_shapes=[pltpu.VMEM((tm, D), BF16)],
        compiler_params=_params("parallel", "arbitrary"),
        name="in_proj",
    )(x2, norm_g.reshape(1, D), scale.reshape(B, 1, D), shift.reshape(B, 1, D), w_in_bf, ra, rb, rc)


def _sb_kernel(q_ref, k_ref, v_ref, o_ref, acc_sc, cs_sc, *, tb):
    i = pl.program_id(2)
    lane = lax.broadcasted_iota(jnp.int32, (tb, LANES), 1)
    q = q_ref[...]
    zero = jnp.zeros_like(q)
    q_heads = (jnp.where(lane < SB_HEAD_DIM, q, zero), jnp.where(lane >= SB_HEAD_DIM, q, zero))
    rr = lax.broadcasted_iota(jnp.int32, (tb, tb + LANES), 0)
    cc = lax.broadcasted_iota(jnp.int32, (tb, tb + LANES), 1)
    u_top = jnp.where(jnp.logical_or(rr >= cc, cc >= tb), 1.0, 0.0).astype(BF16)
    u_full = jnp.concatenate([u_top, u_top], axis=0)
    row = lax.broadcasted_iota(jnp.int32, (tb, tb), 0)
    col = lax.broadcasted_iota(jnp.int32, (tb, tb), 1)
    strictly_before = col < row

    acc_sc[...] = jnp.zeros_like(acc_sc)
    cs_sc[...] = jnp.zeros_like(cs_sc)

    def block(j, masked):
        start = pl.multiple_of(j * tb, tb)
        kb = k_ref[pl.ds(start, tb), :]
        vb = v_ref[pl.ds(start, tb), :]
        for h in range(2):
            z = _nt_dot(q_heads[h], kb)
            log_fail = -(jnp.maximum(z, 0.0) + jnp.log(1.0 + jnp.exp(-jnp.abs(z))))
            if masked:
                log_fail = jnp.where(strictly_before, log_fail, 0.0)
            hi = log_fail.astype(BF16)
            lo = (log_fail - hi.astype(F32)).astype(BF16)
            sums = jnp.dot(jnp.concatenate([hi, lo], axis=1), u_full, preferred_element_type=F32)
            later = cs_sc[h]
            w = jnp.exp(z + sums[:, :tb] + jnp.concatenate([later] * (tb // LANES), axis=1))
            if masked:
                w = jnp.where(strictly_before, w, 0.0)
            acc_sc[h] += jnp.dot(w.astype(BF16), vb, preferred_element_type=F32)
            cs_sc[h] = later + sums[:, tb:]

    block(i, True)

    def body(jj, carry):
        block(i - 1 - jj, False)
        return carry

    lax.fori_loop(0, i, body, 0)
    o_ref[...] = jnp.where(lane < SB_HEAD_DIM, acc_sc[0], acc_sc[1]).astype(BF16)


def _sb_attention(proj, B, S):
    T = proj.shape[0]
    tb = min(256, S)
    nq = S // tb
    pairs = SB_WIDTH // LANES
    return pl.pallas_call(
        functools.partial(_sb_kernel, tb=tb),
        grid=(B, pairs, nq),
        in_specs=[
            pl.BlockSpec((tb, LANES), lambda b, p, i: (b * nq + i, COL_SB_Q + p)),
            pl.BlockSpec((S, LANES), lambda b, p, i: (b, COL_SB_K + p)),
            pl.BlockSpec((S, LANES), lambda b, p, i: (b, COL_SB_V + p)),
        ],
        out_specs=pl.BlockSpec((tb, LANES), lambda b, p, i: (b * nq + i, p)),
        out_shape=jax.ShapeDtypeStruct((T, SB_WIDTH), BF16),
        scratch_shapes=[pltpu.VMEM((2, tb, LANES), F32), pltpu.VMEM((2, tb, LANES), F32)],
        compiler_params=_params("parallel", "parallel", "arbitrary"),
        name="sb_attention",
    )(proj, proj, proj)


def _diff_kernel(lp_ref, g_ref, q_ref, k_ref, v_ref, o_ref, m_sc, l_sc, acc_sc, *, tq, tk, lam_init):
    i = pl.program_id(2)
    lane = lax.broadcasted_iota(jnp.int32, (tq, LANES), 1)
    q = q_ref[...]
    zero = jnp.zeros_like(q)
    q_maps = (jnp.where(lane < DIFF_QK_DIM, q, zero), jnp.where(lane >= DIFF_QK_DIM, q, zero))
    m_sc[...] = jnp.full_like(m_sc, NEG_INF)
    l_sc[...] = jnp.zeros_like(l_sc)
    acc_sc[...] = jnp.zeros_like(acc_sc)
    q_pos = i * tq + lax.broadcasted_iota(jnp.int32, (tq, tk), 0)
    k_off = lax.broadcasted_iota(jnp.int32, (tq, tk), 1)

    def block(j, masked):
        start = pl.multiple_of(j * tk, tk)
        kb = k_ref[pl.ds(start, tk), :]
        vb = v_ref[pl.ds(start, tk), :]
        for c in range(2):
            s = _nt_dot(q_maps[c], kb)
            if masked:
                s = jnp.where(j * tk + k_off <= q_pos, s, NEG_INF)
            m_prev = m_sc[c]
            m_new = jnp.maximum(m_prev, jnp.max(s, axis=1, keepdims=True))
            alpha = jnp.exp(m_prev - m_new)
            p = jnp.exp(s - m_new)
            l_sc[c] = alpha * l_sc[c] + jnp.sum(p, axis=1, keepdims=True)
            acc_sc[c] = alpha * acc_sc[c] + jnp.dot(p.astype(BF16), vb, preferred_element_type=F32)
            m_sc[c] = m_new

    n_full = (i * tq) // tk

    def body(j, carry):
        block(j, False)
        return carry

    lax.fori_loop(0, n_full, body, 0)
    block(n_full, True)

    lp = lp_ref[...]
    lam = (jnp.exp(jnp.sum(lp[0:1] * lp[1:2], axis=1, keepdims=True))
           - jnp.exp(jnp.sum(lp[2:3] * lp[3:4], axis=1, keepdims=True)) + lam_init)
    o = acc_sc[0] / l_sc[0] - lam * (acc_sc[1] / l_sc[1])
    ms = jnp.mean(o * o, axis=-1, keepdims=True)
    y = (o * lax.rsqrt(ms + NORM_EPS) * g_ref[...]) * (1.0 - lam_init)
    o_ref[...] = y.astype(BF16)


def _diff_attention(proj, lam_params, subln_g, lam_init, B, S):
    T = proj.shape[0]
    tq = min(256, S)
    tk = min(512, S)
    nq = S // tq
    return pl.pallas_call(
        functools.partial(_diff_kernel, tq=tq, tk=tk, lam_init=lam_init),
        grid=(B, DIFF_HEADS, nq),
        in_specs=[
            pl.BlockSpec((4, DIFF_QK_DIM), lambda b, h, i: (0, 0)),
            pl.BlockSpec((1, DIFF_V_DIM), lambda b, h, i: (0, 0)),
            pl.BlockSpec((tq, LANES), lambda b, h, i: (b * nq + i, COL_D_Q + h)),
            pl.BlockSpec((S, LANES), lambda b, h, i: (b, COL_D_K + h)),
            pl.BlockSpec((S, LANES), lambda b, h, i: (b, COL_D_V + h)),
        ],
        out_specs=pl.BlockSpec((tq, LANES), lambda b, h, i: (b * nq + i, h)),
        out_shape=jax.ShapeDtypeStruct((T, DIFF_V_WIDTH), BF16),
        scratch_shapes=[pltpu.VMEM((2, tq, 1), F32), pltpu.VMEM((2, tq, 1), F32),
                        pltpu.VMEM((2, tq, LANES), F32)],
        compiler_params=_params("parallel", "parallel", "arbitrary"),
        name="diff_attention",
    )(lam_params, subln_g.reshape(1, DIFF_V_DIM), proj, proj, proj)


def _merge_kernel(ysb_ref, ydf_ref, gs_ref, gd_ref, gb_ref, wbs_ref, wbd_ref, wo_ref,
                  x_ref, g1_ref, n2_ref, sc2_ref, sh2_ref, xo_ref, h2_ref):
    D = x_ref.shape[1]
    a = jnp.dot(ysb_ref[...], wbs_ref[...], preferred_element_type=F32)
    b = jnp.dot(ydf_ref[...], wbd_ref[...], preferred_element_type=F32)
    gb = gb_ref[...]
    g_sb = jax.nn.sigmoid(gs_ref[...].astype(F32) + gb[:, :D])
    g_df = jax.nn.sigmoid(gd_ref[...].astype(F32) + gb[:, D:])
    merged = g_sb * a + g_df * b
    y = jnp.dot(merged.astype(BF16), wo_ref[...], preferred_element_type=F32)
    xn = x_ref[...] + g1_ref[0] * y
    xo_ref[...] = xn
    h2_ref[...] = _norm_mod(xn, n2_ref[...], sc2_ref[0], sh2_ref[0]).astype(BF16)


def _merge_project(y_sb, y_df, proj, gate_b, wbs, wbd, wo, x2, gate1, norm2_g, scale2, shift2, B, S):
    T, D = x2.shape
    tm = min(512, S)
    tiles_per_b = S // tm
    mod_spec = pl.BlockSpec((1, 1, D), lambda i: (i // tiles_per_b, 0, 0))
    const = lambda shape: pl.BlockSpec(shape, lambda i: (0, 0))
    return pl.pallas_call(
        _merge_kernel,
        grid=(T // tm,),
        in_specs=[
            pl.BlockSpec((tm, SB_WIDTH), lambda i: (i, 0)),
            pl.BlockSpec((tm, DIFF_V_WIDTH), lambda i: (i, 0)),
            pl.BlockSpec((tm, D), lambda i: (i, COL_G)),
            pl.BlockSpec((tm, D), lambda i: (i, COL_G + 1)),
            const((1, 2 * D)),
            const((SB_WIDTH, D)), const((DIFF_V_WIDTH, D)), const((D, D)),
            pl.BlockSpec((tm, D), lambda i: (i, 0)),
            mod_spec,
            const((1, D)),
            mod_spec, mod_spec,
        ],
        out_specs=[pl.BlockSpec((tm, D), lambda i: (i, 0)), pl.BlockSpec((tm, D), lambda i: (i, 0))],
        out_shape=[jax.ShapeDtypeStruct((T, D), F32), jax.ShapeDtypeStruct((T, D), BF16)],
        compiler_params=_params("parallel"),
        name="merge_out_proj",
    )(y_sb, y_df, proj, proj, gate_b.reshape(1, 2 * D), wbs, wbd, wo, x2,
      gate1.reshape(B, 1, D), norm2_g.reshape(1, D), scale2.reshape(B, 1, D), shift2.reshape(B, 1, D))


def _column_max(a):
    m = a[0:8]
    for r in range(8, a.shape[0], 8):
        m = jnp.maximum(m, a[r:r + 8])
    return jnp.max(m, axis=0, keepdims=True)


def _top_values(scores, k):
    vals = []
    cur = scores
    for _ in range(k):
        m = _column_max(cur)
        vals.append(m)
        cur = jnp.where(cur == m, NEG_INF, cur)
    return vals


def _peer_select_kernel(h_ref, wq_ref, sk_ref, ht_ref, s1_ref, e1_ref, s2_ref, e2_ref, tau_ref, *, tm):
    hb = h_ref[...]
    ht_ref[...] = hb.T
    K = PEER_TOPK
    rows_needed = [K // (a + 1) for a in range(K)]
    row_id = lax.broadcasted_iota(jnp.int32, (8, LANES), 0)

    def head(h, carry):
        q_t = _nt_dot(wq_ref[h], hb).astype(BF16)
        half = PEER_QUERY_DIM // 2
        sc = [jnp.dot(sk_ref[h, c], q_t[c * half:(c + 1) * half], preferred_element_type=F32)
              for c in range(2)]
        for lc in range(tm // LANES):
            ls = slice(lc * LANES, (lc + 1) * LANES)
            s1, s2 = sc[0][:, ls], sc[1][:, ls]
            v1 = _top_values(s1, K)
            v2 = _top_values(s2, K)
            v2_all = jnp.concatenate(v2, axis=0)
            cands = [v1[0] + v2_all]
            for a in range(1, K):
                ca = v1[a] + v2_all[0:8]
                cands.append(jnp.where(row_id < rows_needed[a], ca, NEG_INF))
            best = cands[0][0:1]
            z_sum = jnp.zeros_like(best)
            m = best
            for r in range(K):
                m = _column_max(cands[0])
                for ca in cands[1:]:
                    m = jnp.maximum(m, jnp.max(ca, axis=0, keepdims=True))
                z_sum = z_sum + jnp.exp(m - best)
                if r + 1 < K:
                    cands = [jnp.where(ca == m, NEG_INF, ca) for ca in cands]
            inv_z = 1.0 / z_sum
            chunked = (PEER_N_KEYS // PEER_CHUNK_KEYS, PEER_CHUNK_KEYS, LANES)
            s1_ref[:, h, :, ls] = s1.reshape(chunked)
            s2_ref[h, :, ls] = s2
            e1_ref[:, h, :, ls] = (jnp.exp(s1 - v1[0]) * inv_z).reshape(chunked)
            e2_ref[h, :, ls] = jnp.exp(s2 - v2[0])
            tau_ref[h, :, ls] = m
        return carry

    lax.fori_loop(0, PEER_HEADS, head, 0)


def _peer_select(h2, wq_t, subkeys_bf):
    T, D = h2.shape
    tm = min(512, T)
    tok_spec = pl.BlockSpec((PEER_HEADS, PEER_N_KEYS, tm), lambda i: (0, 0, i))
    tok_shape = jax.ShapeDtypeStruct((PEER_HEADS, PEER_N_KEYS, T), F32)
    n_chunks = PEER_N_KEYS // PEER_CHUNK_KEYS
    row_spec = pl.BlockSpec((n_chunks, PEER_HEADS, PEER_CHUNK_KEYS, tm), lambda i: (0, 0, 0, i))
    row_shape = jax.ShapeDtypeStruct((n_chunks, PEER_HEADS, PEER_CHUNK_KEYS, T), F32)
    return pl.pallas_call(
        functools.partial(_peer_select_kernel, tm=tm),
        grid=(T // tm,),
        in_specs=[
            pl.BlockSpec((tm, D), lambda i: (i, 0)),
            pl.BlockSpec((PEER_HEADS, PEER_QUERY_DIM, D), lambda i: (0, 0, 0)),
            pl.BlockSpec((PEER_HEADS, 2, PEER_N_KEYS, PEER_QUERY_DIM // 2), lambda i: (0, 0, 0, 0)),
        ],
        out_specs=[
            pl.BlockSpec((D, tm), lambda i: (0, i)),
            row_spec, row_spec, tok_spec, tok_spec,
            pl.BlockSpec((PEER_HEADS, 1, tm), lambda i: (0, 0, i)),
        ],
        out_shape=[
            jax.ShapeDtypeStruct((D, T), BF16),
            row_shape, row_shape, tok_shape, tok_shape,
            jax.ShapeDtypeStruct((PEER_HEADS, 1, T), F32),
        ],
        compiler_params=_params("parallel"),
        name="peer_select",
    )(h2, wq_t, subkeys_bf)


def _peer_mix_kernel(ht_ref, u_ref, vt_ref, s1_ref, e1_ref, s2_ref, e2_ref, tau_ref, x_ref, g2_ref,
                     xo_ref, acc_sc, a_sc, act_sc, *, tm):
    j = pl.program_id(1)

    @pl.when(j == 0)
    def _():
        acc_sc[...] = jnp.zeros_like(acc_sc)

    a_sc[...] = jnp.dot(u_ref[...], ht_ref[...], preferred_element_type=F32)

    def lane_chunk(lc, carry):
        ls = pl.ds(pl.multiple_of(lc * LANES, LANES), LANES)
        for ii in range(PEER_CHUNK_KEYS):
            gate = jnp.zeros((PEER_N_KEYS, LANES), F32)
            for h in range(PEER_HEADS):
                pair_sum = s1_ref[0, h, ii:ii + 1, ls] + s2_ref[h, :, ls]
                weight = e1_ref[0, h, ii:ii + 1, ls] * e2_ref[h, :, ls]
                gate = gate + jnp.where(pair_sum >= tau_ref[h, :, ls], weight, 0.0)
            rows = slice(ii * PEER_N_KEYS, (ii + 1) * PEER_N_KEYS)
            a = a_sc[rows, ls]
            gelu = 0.5 * a * (1.0 + lax.erf(a * (1.0 / math.sqrt(2.0))))
            act_sc[rows, ls] = (gelu * gate).astype(BF16)
        return carry

    lax.fori_loop(0, tm // LANES, lane_chunk, 0)

    acc_sc[...] += jnp.dot(vt_ref[...], act_sc[...], preferred_element_type=F32)

    @pl.when(j == pl.num_programs(1) - 1)
    def _():
        xo_ref[...] = x_ref[...] + g2_ref[0] * acc_sc[...].T


def _peer_mix(h_t, u_bf, v_t, s1, e1, s2, e2, tau, x2, gate2, B, S):
    T, D = x2.shape
    tm = min(512, S)
    tiles_per_b = S // tm
    tok_spec = pl.BlockSpec((PEER_HEADS, PEER_N_KEYS, tm), lambda i, j: (0, 0, i))
    row_spec = pl.BlockSpec((1, PEER_HEADS, PEER_CHUNK_KEYS, tm), lambda i, j: (j, 0, 0, i))
    return pl.pallas_call(
        functools.partial(_peer_mix_kernel, tm=tm),
        grid=(T // tm, PEER_N_EXPERTS // PEER_CHUNK),
        in_specs=[
            pl.BlockSpec((D, tm), lambda i, j: (0, i)),
            pl.BlockSpec((PEER_CHUNK, D), lambda i, j: (j, 0)),
            pl.BlockSpec((D, PEER_CHUNK), lambda i, j: (0, j)),
            row_spec, row_spec, tok_spec, tok_spec,
            pl.BlockSpec((PEER_HEADS, 1, tm), lambda i, j: (0, 0, i)),
            pl.BlockSpec((tm, D), lambda i, j: (i, 0)),
            pl.BlockSpec((1, 1, D), lambda i, j: (i // tiles_per_b, 0, 0)),
        ],
        out_specs=pl.BlockSpec((tm, D), lambda i, j: (i, 0)),
        out_shape=jax.ShapeDtypeStruct((T, D), F32),
        scratch_shapes=[pltpu.VMEM((D, tm), F32), pltpu.VMEM((PEER_CHUNK, tm), F32),
                        pltpu.VMEM((PEER_CHUNK, tm), BF16)],
        compiler_params=_params("parallel", "arbitrary"),
        name="peer_mix",
    )(h_t, u_bf, v_t, s1, e1, s2, e2, tau, x2, gate2.reshape(B, 1, D))


def _final_norm_kernel(x_ref, g_ref, o_ref):
    x = x_ref[...]
    ms = jnp.mean(x * x, axis=-1, keepdims=True)
    o_ref[...] = x * lax.rsqrt(ms + NORM_EPS) * g_ref[...]


def _final_norm(x2, g):
    T, D = x2.shape
    tm = min(1024, T)
    return pl.pallas_call(
        _final_norm_kernel,
        grid=(T // tm,),
        in_specs=[pl.BlockSpec((tm, D), lambda i: (i, 0)), pl.BlockSpec((1, D), lambda i: (0, 0))],
        out_specs=pl.BlockSpec((tm, D), lambda i: (i, 0)),
        out_shape=jax.ShapeDtypeStruct((T, D), F32),
        compiler_params=_params("parallel"),
        name="final_norm",
    )(x2, g.reshape(1, D))


def kernel(x, c, norm1_g, norm2_g, ada_w, ada_b, w_in, gate_b, diff_lambda, diff_subln_g,
           w_branch_sb, w_branch_diff, w_out, peer_wq, peer_subkeys, peer_u, peer_v, final_g):
    B, S, D = x.shape
    assert D == D_MODEL and S % LANES == 0
    x2 = x.reshape(B * S, D)
    mod = _modulation(c, ada_w, ada_b)
    rope = _rope_lane_tables(S)
    for l in range(DEPTH):
        lam_init = 0.8 - 0.6 * math.exp(-0.3 * l)
        shift1, scale1, gate1, shift2, scale2, gate2 = (mod[l, :, n] for n in range(6))
        proj = _input_projection(x2, norm1_g[l], scale1, shift1, w_in[l].astype(BF16), rope, B, S)
        y_sb = _sb_attention(proj, B, S)
        y_df = _diff_attention(proj, diff_lambda[l], diff_subln_g[l], lam_init, B, S)
        x2, h2 = _merge_project(y_sb, y_df, proj, gate_b[l], w_branch_sb[l].astype(BF16),
                                w_branch_diff[l].astype(BF16), w_out[l].astype(BF16), x2, gate1,
                                norm2_g[l], scale2, shift2, B, S)
        wq_t = peer_wq[l].T.astype(BF16).reshape(PEER_HEADS, PEER_QUERY_DIM, D)
        h_t, s1, e1, s2, e2, tau = _peer_select(h2, wq_t, peer_subkeys[l].astype(BF16))
        x2 = _peer_mix(h_t, peer_u[l].astype(BF16), peer_v[l].T.astype(BF16),
                       s1, e1, s2, e2, tau, x2, gate2, B, S)
    return _final_norm(x2, final_g).reshape(B, S, D)
```

```python
import functools
import math

import jax
import jax.numpy as jnp
from jax import lax
from jax.experimental import pallas as pl
from jax.experimental.pallas import tpu as pltpu

F32 = jnp.float32
BF16 = jnp.bfloat16

D_MODEL = 1024
DEPTH = 2
SB_HEADS = 8
SB_HEAD_DIM = 64
SB_WIDTH = SB_HEADS * SB_HEAD_DIM
DIFF_HEADS = 4
DIFF_QK_DIM = 64
DIFF_V_DIM = 2 * DIFF_QK_DIM
DIFF_QK_WIDTH = DIFF_HEADS * 2 * DIFF_QK_DIM
DIFF_V_WIDTH = DIFF_HEADS * DIFF_V_DIM
ROPE_THETA = 500000.0
ROT_DIM = DIFF_QK_DIM // 4
IN_COLS = 3 * SB_WIDTH + 2 * DIFF_QK_WIDTH + DIFF_V_WIDTH + 2 * D_MODEL
PEER_HEADS = 8
PEER_N_KEYS = 128
PEER_N_EXPERTS = PEER_N_KEYS * PEER_N_KEYS
PEER_TOPK = 16
PEER_QUERY_DIM = 256
PEER_CHUNK_KEYS = 8
PEER_CHUNK = PEER_CHUNK_KEYS * PEER_N_KEYS
NORM_EPS = 1e-6

LANES = 128
VMEM_LIMIT = 56 * 1024 * 1024
NEG_INF = float("-inf")

IN_TILE_N = 512
COL_SB_Q, COL_SB_K, COL_SB_V = 0, SB_WIDTH // LANES, 2 * SB_WIDTH // LANES
COL_D_Q = 3 * SB_WIDTH // LANES
COL_D_K = COL_D_Q + DIFF_QK_WIDTH // LANES
COL_D_V = COL_D_K + DIFF_QK_WIDTH // LANES
COL_G = (3 * SB_WIDTH + 2 * DIFF_QK_WIDTH + DIFF_V_WIDTH) // D_MODEL


def _params(*sem):
    return pltpu.CompilerParams(dimension_semantics=sem, vmem_limit_bytes=VMEM_LIMIT)


def _nt_dot(a, b):
    return lax.dot_general(a, b, (((1,), (1,)), ((), ())), preferred_element_type=F32)


def _mod_kernel(c_ref, w_ref, b_ref, o_ref):
    c = c_ref[...]
    c_act = c * jax.nn.sigmoid(c)
    o_ref[0] = jnp.dot(c_act, w_ref[0], preferred_element_type=F32,
                       precision=lax.Precision.HIGHEST) + b_ref[0]


def _modulation(c, ada_w, ada_b):
    B, D = c.shape
    rows = 8
    c_pad = jnp.pad(c, ((0, rows - B), (0, 0)))
    out = pl.pallas_call(
        _mod_kernel,
        grid=(DEPTH, 6),
        in_specs=[
            pl.BlockSpec((rows, D), lambda l, n: (0, 0)),
            pl.BlockSpec((1, D, D), lambda l, n: (l, 0, n)),
            pl.BlockSpec((1, 1, D), lambda l, n: (l, 0, n)),
        ],
        out_specs=pl.BlockSpec((1, rows, D), lambda l, n: (l, 0, n)),
        out_shape=jax.ShapeDtypeStruct((DEPTH, rows, 6 * D), F32),
        compiler_params=_params("arbitrary", "arbitrary"),
        name="adaln_mod",
    )(c_pad, ada_w, ada_b.reshape(DEPTH, 1, 6 * D))
    return out[:, :B].reshape(DEPTH, B, 6, D)


def _norm_mod(x, g, scale, shift):
    ms = jnp.mean(x * x, axis=-1, keepdims=True)
    return (x * lax.rsqrt(ms + NORM_EPS) * g) * (1.0 + scale) + shift


def _inproj_kernel(x_ref, g_ref, sc_ref, sh_ref, w_ref, ra_ref, rb_ref, rc_ref, o_ref, h_sc):
    j = pl.program_id(1)

    @pl.when(j == 0)
    def _():
        h_sc[...] = _norm_mod(x_ref[...], g_ref[...], sc_ref[0], sh_ref[0]).astype(BF16)

    r = jnp.dot(h_sc[...], w_ref[...], preferred_element_type=F32)
    is_q = jnp.logical_or(j == COL_SB_Q * LANES // IN_TILE_N, j == COL_D_Q * LANES // IN_TILE_N)
    r = r * jnp.where(is_q, 1.0 / math.sqrt(SB_HEAD_DIM), 1.0)
    is_rope = jnp.logical_or(j == COL_D_Q * LANES // IN_TILE_N, j == COL_D_K * LANES // IN_TILE_N)

    @pl.when(is_rope)
    def _():
        ra, rb, rc = ra_ref[...], rb_ref[...], rc_ref[...]
        for blk in range(IN_TILE_N // LANES):
            t = r[:, blk * LANES:(blk + 1) * LANES]
            half = ROT_DIM // 2
            rot = t * ra + pltpu.roll(t, LANES - half, 1) * rb + pltpu.roll(t, half, 1) * rc
            o_ref[:, blk * LANES:(blk + 1) * LANES] = rot.astype(BF16)

    @pl.when(jnp.logical_not(is_rope))
    def _():
        o_ref[...] = r.astype(BF16)


def _rope_lane_tables(seq_len):
    pos = jnp.arange(seq_len, dtype=F32)
    inv_freq = 1.0 / (ROPE_THETA ** (jnp.arange(0, ROT_DIM, 2, dtype=F32) / ROT_DIM))
    ang = pos[:, None] * inv_freq[None, :]
    cos, sin = jnp.cos(ang), jnp.sin(ang)
    half = ROT_DIM // 2
    lane = jnp.arange(LANES) % DIFF_QK_DIM
    f = lane % half
    first, second = lane < half, (lane >= half) & (lane < ROT_DIM)
    ra = jnp.where((first | second)[None, :], cos[:, f], 1.0)
    rb = jnp.where(first[None, :], -sin[:, f], 0.0)
    rc = jnp.where(second[None, :], sin[:, f], 0.0)
    return ra, rb, rc


def _input_projection(x2, norm_g, scale, shift, w_in_bf, rope, B, S):
    T, D = x2.shape
    tm = min(512, S)
    tiles_per_b = S // tm
    ra, rb, rc = rope
    rope_spec = pl.BlockSpec((tm, LANES), lambda i, j: (i % tiles_per_b, 0))
    mod_spec = pl.BlockSpec((1, 1, D), lambda i, j: (i // tiles_per_b, 0, 0))
    return pl.pallas_call(
        _inproj_kernel,
        grid=(T // tm, IN_COLS // IN_TILE_N),
        in_specs=[
            pl.BlockSpec((tm, D), lambda i, j: (i, 0)),
            pl.BlockSpec((1, D), lambda i, j: (0, 0)),
            mod_spec, mod_spec,
            pl.BlockSpec((D, IN_TILE_N), lambda i, j: (0, j)),
            rope_spec, rope_spec, rope_spec,
        ],
        out_specs=pl.BlockSpec((tm, IN_TILE_N), lambda i, j: (i, j)),
        out_shape=jax.ShapeDtypeStruct((T, IN_COLS), BF16),
        scratch_shapes=[pltpu.VMEM((tm, D), BF16)],
        compiler_params=_params("parallel", "arbitrary"),
        name="in_proj",
    )(x2, norm_g.reshape(1, D), scale.reshape(B, 1, D), shift.reshape(B, 1, D), w_in_bf, ra, rb, rc)


def _sb_kernel(q_ref, k_ref, v_ref, o_ref, acc_sc, cs_sc, *, tb):
    i = pl.program_id(2)
    lane = lax.broadcasted_iota(jnp.int32, (tb, LANES), 1)
    q = q_ref[...]
    zero = jnp.zeros_like(q)
    q_heads = (jnp.where(lane < SB_HEAD_DIM, q, zero), jnp.where(lane >= SB_HEAD_DIM, q, zero))
    rr = lax.broadcasted_iota(jnp.int32, (tb, tb + LANES), 0)
    cc = lax.broadcasted_iota(jnp.int32, (tb, tb + LANES), 1)
    u_top = jnp.where(jnp.logical_or(rr >= cc, cc >= tb), 1.0, 0.0).astype(BF16)
    u_full = jnp.concatenate([u_top, u_top], axis=0)
    row = lax.broadcasted_iota(jnp.int32, (tb, tb), 0)
    col = lax.broadcasted_iota(jnp.int32, (tb, tb), 1)
    strictly_before = col < row

    acc_sc[...] = jnp.zeros_like(acc_sc)
    cs_sc[...] = jnp.zeros_like(cs_sc)

    def block(j, masked):
        start = pl.multiple_of(j * tb, tb)
        kb = k_ref[pl.ds(start, tb), :]
        vb = v_ref[pl.ds(start, tb), :]
        for h in range(2):
            z = _nt_dot(q_heads[h], kb)
            log_fail = -(jnp.maximum(z, 0.0) + jnp.log(1.0 + jnp.exp(-jnp.abs(z))))
            if masked:
                log_fail = jnp.where(strictly_before, log_fail, 0.0)
            hi = log_fail.astype(BF16)
            lo = (log_fail - hi.astype(F32)).astype(BF16)
            sums = jnp.dot(jnp.concatenate([hi, lo], axis=1), u_full, preferred_element_type=F32)
            later = cs_sc[h]
            w = jnp.exp(z + sums[:, :tb] + jnp.concatenate([later] * (tb // LANES), axis=1))
            if masked:
                w = jnp.where(strictly_before, w, 0.0)
            acc_sc[h] += jnp.dot(w.astype(BF16), vb, preferred_element_type=F32)
            cs_sc[h] = later + sums[:, tb:]

    block(i, True)

    def body(jj, carry):
        block(i - 1 - jj, False)
        return carry

    lax.fori_loop(0, i, body, 0)
    o_ref[...] = jnp.where(lane < SB_HEAD_DIM, acc_sc[0], acc_sc[1]).astype(BF16)


def _sb_attention(proj, B, S):
    T = proj.shape[0]
    tb = min(256, S)
    nq = S // tb
    pairs = SB_WIDTH // LANES
    return pl.pallas_call(
        functools.partial(_sb_kernel, tb=tb),
        grid=(B, pairs, nq),
        in_specs=[
            pl.BlockSpec((tb, LANES), lambda b, p, i: (b * nq + i, COL_SB_Q + p)),
            pl.BlockSpec((S, LANES), lambda b, p, i: (b, COL_SB_K + p)),
            pl.BlockSpec((S, LANES), lambda b, p, i: (b, COL_SB_V + p)),
        ],
        out_specs=pl.BlockSpec((tb, LANES), lambda b, p, i: (b * nq + i, p)),
        out_shape=jax.ShapeDtypeStruct((T, SB_WIDTH), BF16),
        scratch_shapes=[pltpu.VMEM((2, tb, LANES), F32), pltpu.VMEM((2, tb, LANES), F32)],
        compiler_params=_params("parallel", "parallel", "arbitrary"),
        name="sb_attention",
    )(proj, proj, proj)


def _diff_kernel(lp_ref, g_ref, q_ref, k_ref, v_ref, o_ref, m_sc, l_sc, acc_sc, *, tq, tk, lam_init):
    i = pl.program_id(2)
    lane = lax.broadcasted_iota(jnp.int32, (tq, LANES), 1)
    q = q_ref[...]
    zero = jnp.zeros_like(q)
    q_maps = (jnp.where(lane < DIFF_QK_DIM, q, zero), jnp.where(lane >= DIFF_QK_DIM, q, zero))
    m_sc[...] = jnp.full_like(m_sc, NEG_INF)
    l_sc[...] = jnp.zeros_like(l_sc)
    acc_sc[...] = jnp.zeros_like(acc_sc)
    q_pos = i * tq + lax.broadcasted_iota(jnp.int32, (tq, tk), 0)
    k_off = lax.broadcasted_iota(jnp.int32, (tq, tk), 1)

    def block(j, masked):
        start = pl.multiple_of(j * tk, tk)
        kb = k_ref[pl.ds(start, tk), :]
        vb = v_ref[pl.ds(start, tk), :]
        for c in range(2):
            s = _nt_dot(q_maps[c], kb)
            if masked:
                s = jnp.where(j * tk + k_off <= q_pos, s, NEG_INF)
            m_prev = m_sc[c]
            m_new = jnp.maximum(m_prev, jnp.max(s, axis=1, keepdims=True))
            alpha = jnp.exp(m_prev - m_new)
            p = jnp.exp(s - m_new)
            l_sc[c] = alpha * l_sc[c] + jnp.sum(p, axis=1, keepdims=True)
            acc_sc[c] = alpha * acc_sc[c] + jnp.dot(p.astype(BF16), vb, preferred_element_type=F32)
            m_sc[c] = m_new

    n_full = (i * tq) // tk

    def body(j, carry):
        block(j, False)
        return carry

    lax.fori_loop(0, n_full, body, 0)
    block(n_full, True)

    lp = lp_ref[...]
    lam = (jnp.exp(jnp.sum(lp[0:1] * lp[1:2], axis=1, keepdims=True))
           - jnp.exp(jnp.sum(lp[2:3] * lp[3:4], axis=1, keepdims=True)) + lam_init)
    o = acc_sc[0] / l_sc[0] - lam * (acc_sc[1] / l_sc[1])
    ms = jnp.mean(o * o, axis=-1, keepdims=True)
    y = (o * lax.rsqrt(ms + NORM_EPS) * g_ref[...]) * (1.0 - lam_init)
    o_ref[...] = y.astype(BF16)


def _diff_attention(proj, lam_params, subln_g, lam_init, B, S):
    T = proj.shape[0]
    tq = min(256, S)
    tk = min(512, S)
    nq = S // tq
    return pl.pallas_call(
        functools.partial(_diff_kernel, tq=tq, tk=tk, lam_init=lam_init),
        grid=(B, DIFF_HEADS, nq),
        in_specs=[
            pl.BlockSpec((4, DIFF_QK_DIM), lambda b, h, i: (0, 0)),
            pl.BlockSpec((1, DIFF_V_DIM), lambda b, h, i: (0, 0)),
            pl.BlockSpec((tq, LANES), lambda b, h, i: (b * nq + i, COL_D_Q + h)),
            pl.BlockSpec((S, LANES), lambda b, h, i: (b, COL_D_K + h)),
            pl.BlockSpec((S, LANES), lambda b, h, i: (b, COL_D_V + h)),
        ],
        out_specs=pl.BlockSpec((tq, LANES), lambda b, h, i: (b * nq + i, h)),
        out_shape=jax.ShapeDtypeStruct((T, DIFF_V_WIDTH), BF16),
        scratch_shapes=[pltpu.VMEM((2, tq, 1), F32), pltpu.VMEM((2, tq, 1), F32),
                        pltpu.VMEM((2, tq, LANES), F32)],
        compiler_params=_params("parallel", "parallel", "arbitrary"),
        name="diff_attention",
    )(lam_params, subln_g.reshape(1, DIFF_V_DIM), proj, proj, proj)


def _merge_kernel(ysb_ref, ydf_ref, gs_ref, gd_ref, gb_ref, wbs_ref, wbd_ref, wo_ref,
                  x_ref, g1_ref, n2_ref, sc2_ref, sh2_ref, xo_ref, h2_ref):
    D = x_ref.shape[1]
    a = jnp.dot(ysb_ref[...], wbs_ref[...], preferred_element_type=F32)
    b = jnp.dot(ydf_ref[...], wbd_ref[...], preferred_element_type=F32)
    gb = gb_ref[...]
    g_sb = jax.nn.sigmoid(gs_ref[...].astype(F32) + gb[:, :D])
    g_df = jax.nn.sigmoid(gd_ref[...].astype(F32) + gb[:, D:])
    merged = g_sb * a + g_df * b
    y = jnp.dot(merged.astype(BF16), wo_ref[...], preferred_element_type=F32)
    xn = x_ref[...] + g1_ref[0] * y
    xo_ref[...] = xn
    h2_ref[...] = _norm_mod(xn, n2_ref[...], sc2_ref[0], sh2_ref[0]).astype(BF16)


def _merge_project(y_sb, y_df, proj, gate_b, wbs, wbd, wo, x2, gate1, norm2_g, scale2, shift2, B, S):
    T, D = x2.shape
    tm = min(512, S)
    tiles_per_b = S // tm
    mod_spec = pl.BlockSpec((1, 1, D), lambda i: (i // tiles_per_b, 0, 0))
    const = lambda shape: pl.BlockSpec(shape, lambda i: (0, 0))
    return pl.pallas_call(
        _merge_kernel,
        grid=(T // tm,),
        in_specs=[
            pl.BlockSpec((tm, SB_WIDTH), lambda i: (i, 0)),
            pl.BlockSpec((tm, DIFF_V_WIDTH), lambda i: (i, 0)),
            pl.BlockSpec((tm, D), lambda i: (i, COL_G)),
            pl.BlockSpec((tm, D), lambda i: (i, COL_G + 1)),
            const((1, 2 * D)),
            const((SB_WIDTH, D)), const((DIFF_V_WIDTH, D)), const((D, D)),
            pl.BlockSpec((tm, D), lambda i: (i, 0)),
            mod_spec,
            const((1, D)),
            mod_spec, mod_spec,
        ],
        out_specs=[pl.BlockSpec((tm, D), lambda i: (i, 0)), pl.BlockSpec((tm, D), lambda i: (i, 0))],
        out_shape=[jax.ShapeDtypeStruct((T, D), F32), jax.ShapeDtypeStruct((T, D), BF16)],
        compiler_params=_params("parallel"),
        name="merge_out_proj",
    )(y_sb, y_df, proj, proj, gate_b.reshape(1, 2 * D), wbs, wbd, wo, x2,
      gate1.reshape(B, 1, D), norm2_g.reshape(1, D), scale2.reshape(B, 1, D), shift2.reshape(B, 1, D))


def _column_max(a):
    m = a[0:8]
    for r in range(8, a.shape[0], 8):
        m = jnp.maximum(m, a[r:r + 8])
    return jnp.max(m, axis=0, keepdims=True)


def _top_values(scores, k, with_rank=False):
    vals = []
    cur = scores
    rank = jnp.full(scores.shape, float(k), F32)
    for r in range(k):
        m = _column_max(cur)
        vals.append(m)
        hit = cur == m
        if with_rank:
            rank = jnp.where(hit, float(r), rank)
        cur = jnp.where(hit, NEG_INF, cur)
    return (vals, rank) if with_rank else vals


def _peer_select_kernel(h_ref, wq_ref, sk_ref, ht_ref, c1_ref, e1_ref, r2_ref, e2_ref, *, tm):
    hb = h_ref[...]
    ht_ref[...] = hb.T
    K = PEER_TOPK + 1
    rows_needed = [K // (a + 1) for a in range(K)]
    row_id = lax.broadcasted_iota(jnp.int32, (8, LANES), 0)
    pad_rows = jnp.full((24 - K, LANES), NEG_INF, F32)

    def head(h, carry):
        q_t = _nt_dot(wq_ref[h], hb).astype(BF16)
        half = PEER_QUERY_DIM // 2
        sc = [jnp.dot(sk_ref[h, c], q_t[c * half:(c + 1) * half], preferred_element_type=F32)
              for c in range(2)]
        for lc in range(tm // LANES):
            ls = slice(lc * LANES, (lc + 1) * LANES)
            s1, s2 = sc[0][:, ls], sc[1][:, ls]
            v1 = _top_values(s1, K)
            v2, rank2 = _top_values(s2, K, with_rank=True)
            v2_all = jnp.concatenate(v2 + [pad_rows], axis=0)
            cands = [v1[0] + v2_all]
            for a in range(1, K):
                ca = v1[a] + v2_all[0:8]
                cands.append(jnp.where(row_id < rows_needed[a], ca, NEG_INF))
            best = cands[0][0:1]
            z_sum = jnp.zeros_like(best)
            last = best
            for r in range(K):
                m = _column_max(cands[0])
                for ca in cands[1:]:
                    m = jnp.maximum(m, jnp.max(ca, axis=0, keepdims=True))
                if r < PEER_TOPK:
                    z_sum = z_sum + jnp.exp(m - best)
                    last = m
                    cands = [jnp.where(ca == m, NEG_INF, ca) for ca in cands]
            tau = 0.5 * (last + m)
            count = jnp.zeros_like(s1)
            for b in range(PEER_TOPK):
                count = count + jnp.where(s1 >= tau - v2[b], 1.0, 0.0)
            half_inv_z = 0.5 / z_sum
            chunked = (PEER_N_KEYS // PEER_CHUNK_KEYS, PEER_CHUNK_KEYS, LANES)
            c1_ref[:, h, :, ls] = count.reshape(chunked)
            e1_ref[:, h, :, ls] = (jnp.exp(s1 - v1[0]) * half_inv_z).reshape(chunked)
            r2_ref[h, :, ls] = rank2
            e2_ref[h, :, ls] = jnp.exp(s2 - v2[0])
        return carry

    lax.fori_loop(0, PEER_HEADS, head, 0)


def _peer_select(h2, wq_t, subkeys_bf):
    T, D = h2.shape
    tm = min(512, T)
    tok_spec = pl.BlockSpec((PEER_HEADS, PEER_N_KEYS, tm), lambda i: (0, 0, i))
    tok_shape = jax.ShapeDtypeStruct((PEER_HEADS, PEER_N_KEYS, T), F32)
    n_chunks = PEER_N_KEYS // PEER_CHUNK_KEYS
    row_spec = pl.BlockSpec((n_chunks, PEER_HEADS, PEER_CHUNK_KEYS, tm), lambda i: (0, 0, 0, i))
    row_shape = jax.ShapeDtypeStruct((n_chunks, PEER_HEADS, PEER_CHUNK_KEYS, T), F32)
    return pl.pallas_call(
        functools.partial(_peer_select_kernel, tm=tm),
        grid=(T // tm,),
        in_specs=[
            pl.BlockSpec((tm, D), lambda i: (i, 0)),
            pl.BlockSpec((PEER_HEADS, PEER_QUERY_DIM, D), lambda i: (0, 0, 0)),
            pl.BlockSpec((PEER_HEADS, 2, PEER_N_KEYS, PEER_QUERY_DIM // 2), lambda i: (0, 0, 0, 0)),
        ],
        out_specs=[
            pl.BlockSpec((D, tm), lambda i: (0, i)),
            row_spec, row_spec, tok_spec, tok_spec,
        ],
        out_shape=[
            jax.ShapeDtypeStruct((D, T), BF16),
            row_shape, row_shape, tok_shape, tok_shape,
        ],
        compiler_params=_params("parallel"),
        name="peer_select",
    )(h2, wq_t, subkeys_bf)


PEER_ROW_BLOCK_KEYS = 2


def _peer_mix_kernel(ht_ref, u_ref, vt_ref, c1_ref, e1_ref, r2_ref, e2_ref, x_ref, g2_ref,
                     xo_ref, acc_sc, act_sc, r2_sc, e2_sc, *, tm):
    j = pl.program_id(1)

    @pl.when(j == 0)
    def _():
        acc_sc[...] = jnp.zeros_like(acc_sc)
        r2_sc[...] = r2_ref[...].astype(BF16)
        e2_sc[...] = e2_ref[...].astype(BF16)

    rb = PEER_ROW_BLOCK_KEYS * PEER_N_KEYS
    tw = 2 * LANES
    pieces = [(r, c) for r in range(PEER_CHUNK // rb) for c in range(tm // tw)]

    def scores(p):
        r, c = pieces[p]
        return jnp.dot(u_ref[r * rb:(r + 1) * rb, :], ht_ref[:, c * tw:(c + 1) * tw],
                       preferred_element_type=F32)

    zero = jnp.zeros((PEER_N_KEYS, LANES), BF16)

    def gates(p, a_blk, k):
        r, c = pieces[p]
        ii = r * PEER_ROW_BLOCK_KEYS + k
        for lc in range(tw // LANES):
            ls = slice(c * tw + lc * LANES, c * tw + (lc + 1) * LANES)
            gate = None
            for h in range(PEER_HEADS):
                count = jnp.broadcast_to(c1_ref[0, h, ii:ii + 1, ls], (PEER_N_KEYS, LANES)).astype(BF16)
                e1 = jnp.broadcast_to(e1_ref[0, h, ii:ii + 1, ls], (PEER_N_KEYS, LANES)).astype(BF16)
                term = jnp.where(r2_sc[h, :, ls] < count, e1 * e2_sc[h, :, ls], zero)
                gate = term if gate is None else gate + term
            a = a_blk[k * PEER_N_KEYS:(k + 1) * PEER_N_KEYS, lc * LANES:(lc + 1) * LANES]
            gelu2 = a * (1.0 + lax.erf(a * (1.0 / math.sqrt(2.0))))
            act_sc[ii * PEER_N_KEYS:(ii + 1) * PEER_N_KEYS, ls] = gelu2.astype(BF16) * gate

    def mix(p):
        r, c = pieces[p]
        cols = slice(c * tw, (c + 1) * tw)
        acc_sc[:, cols] += jnp.dot(vt_ref[:, r * rb:(r + 1) * rb], act_sc[r * rb:(r + 1) * rb, cols],
                                   preferred_element_type=F32)

    a_next = scores(0)
    for p in range(len(pieces)):
        a_cur = a_next
        if p + 1 < len(pieces):
            a_next = scores(p + 1)
        gates(p, a_cur, 0)
        if p > 0:
            mix(p - 1)
        gates(p, a_cur, 1)
    mix(len(pieces) - 1)

    @pl.when(j == pl.num_programs(1) - 1)
    def _():
        xo_ref[...] = x_ref[...] + g2_ref[0] * acc_sc[...].T


def _peer_mix(h_t, u_bf, v_t, c1, e1, r2, e2, x2, gate2, B, S):
    T, D = x2.shape
    tm = min(512, S)
    tiles_per_b = S // tm
    tok_spec = pl.BlockSpec((PEER_HEADS, PEER_N_KEYS, tm), lambda i, j: (0, 0, i))
    row_spec = pl.BlockSpec((1, PEER_HEADS, PEER_CHUNK_KEYS, tm), lambda i, j: (j, 0, 0, i))
    return pl.pallas_call(
        functools.partial(_peer_mix_kernel, tm=tm),
        grid=(T // tm, PEER_N_EXPERTS // PEER_CHUNK),
        in_specs=[
            pl.BlockSpec((D, tm), lambda i, j: (0, i)),
            pl.BlockSpec((PEER_CHUNK, D), lambda i, j: (j, 0)),
            pl.BlockSpec((D, PEER_CHUNK), lambda i, j: (0, j)),
            row_spec, row_spec, tok_spec, tok_spec,
            pl.BlockSpec((tm, D), lambda i, j: (i, 0)),
            pl.BlockSpec((1, 1, D), lambda i, j: (i // tiles_per_b, 0, 0)),
        ],
        out_specs=pl.BlockSpec((tm, D), lambda i, j: (i, 0)),
        out_shape=jax.ShapeDtypeStruct((T, D), F32),
        scratch_shapes=[pltpu.VMEM((D, tm), F32), pltpu.VMEM((PEER_CHUNK, tm), BF16),
                        pltpu.VMEM((PEER_HEADS, PEER_N_KEYS, tm), BF16),
                        pltpu.VMEM((PEER_HEADS, PEER_N_KEYS, tm), BF16)],
        compiler_params=_params("parallel", "arbitrary"),
        name="peer_mix",
    )(h_t, u_bf, v_t, c1, e1, r2, e2, x2, gate2.reshape(B, 1, D))


def _final_norm_kernel(x_ref, g_ref, o_ref):
    x = x_ref[...]
    ms = jnp.mean(x * x, axis=-1, keepdims=True)
    o_ref[...] = x * lax.rsqrt(ms + NORM_EPS) * g_ref[...]


def _final_norm(x2, g):
    T, D = x2.shape
    tm = min(1024, T)
    return pl.pallas_call(
        _final_norm_kernel,
        grid=(T // tm,),
        in_specs=[pl.BlockSpec((tm, D), lambda i: (i, 0)), pl.BlockSpec((1, D), lambda i: (0, 0))],
        out_specs=pl.BlockSpec((tm, D), lambda i: (i, 0)),
        out_shape=jax.ShapeDtypeStruct((T, D), F32),
        compiler_params=_params("parallel"),
        name="final_norm",
    )(x2, g.reshape(1, D))


def kernel(x, c, norm1_g, norm2_g, ada_w, ada_b, w_in, gate_b, diff_lambda, diff_subln_g,
           w_branch_sb, w_branch_diff, w_out, peer_wq, peer_subkeys, peer_u, peer_v, final_g):
    B, S, D = x.shape
    assert D == D_MODEL and S % LANES == 0
    x2 = x.reshape(B * S, D)
    mod = _modulation(c, ada_w, ada_b)
    rope = _rope_lane_tables(S)
    for l in range(DEPTH):
        lam_init = 0.8 - 0.6 * math.exp(-0.3 * l)
        shift1, scale1, gate1, shift2, scale2, gate2 = (mod[l, :, n] for n in range(6))
        proj = _input_projection(x2, norm1_g[l], scale1, shift1, w_in[l].astype(BF16), rope, B, S)
        y_sb = _sb_attention(proj, B, S)
        y_df = _diff_attention(proj, diff_lambda[l], diff_subln_g[l], lam_init, B, S)
        x2, h2 = _merge_project(y_sb, y_df, proj, gate_b[l], w_branch_sb[l].astype(BF16),
                                w_branch_diff[l].astype(BF16), w_out[l].astype(BF16), x2, gate1,
                                norm2_g[l], scale2, shift2, B, S)
        wq_t = peer_wq[l].T.astype(BF16).reshape(PEER_HEADS, PEER_QUERY_DIM, D)
        h_t, c1, e1, r2, e2 = _peer_select(h2, wq_t, peer_subkeys[l].astype(BF16))
        x2 = _peer_mix(h_t, peer_u[l].astype(BF16), peer_v[l].T.astype(BF16),
                       c1, e1, r2, e2, x2, gate2, B, S)
    return _final_norm(x2, final_g).reshape(B, S, D)
```

```python
import functools
import math

import jax
import jax.numpy as jnp
from jax import lax
from jax.experimental import pallas as pl
from jax.experimental.pallas import tpu as pltpu

F32 = jnp.float32
BF16 = jnp.bfloat16

D_MODEL = 1024
DEPTH = 2
SB_HEADS = 8
SB_HEAD_DIM = 64
SB_WIDTH = SB_HEADS * SB_HEAD_DIM
DIFF_HEADS = 4
DIFF_QK_DIM = 64
DIFF_V_DIM = 2 * DIFF_QK_DIM
DIFF_QK_WIDTH = DIFF_HEADS * 2 * DIFF_QK_DIM
DIFF_V_WIDTH = DIFF_HEADS * DIFF_V_DIM
ROPE_THETA = 500000.0
ROT_DIM = DIFF_QK_DIM // 4
IN_COLS = 3 * SB_WIDTH + 2 * DIFF_QK_WIDTH + DIFF_V_WIDTH + 2 * D_MODEL
PEER_HEADS = 8
PEER_N_KEYS = 128
PEER_N_EXPERTS = PEER_N_KEYS * PEER_N_KEYS
PEER_TOPK = 16
PEER_QUERY_DIM = 256
PEER_CHUNK_KEYS = 8
PEER_CHUNK = PEER_CHUNK_KEYS * PEER_N_KEYS
NORM_EPS = 1e-6

LANES = 128
VMEM_LIMIT = 56 * 1024 * 1024
NEG_INF = float("-inf")
LOG2_E = 1.4426950408889634

IN_TILE_N = 512
COL_SB_Q, COL_SB_K, COL_SB_V = 0, SB_WIDTH // LANES, 2 * SB_WIDTH // LANES
COL_D_Q = 3 * SB_WIDTH // LANES
COL_D_K = COL_D_Q + DIFF_QK_WIDTH // LANES
COL_D_V = COL_D_K + DIFF_QK_WIDTH // LANES
COL_G = (3 * SB_WIDTH + 2 * DIFF_QK_WIDTH + DIFF_V_WIDTH) // D_MODEL


def _params(*sem):
    return pltpu.CompilerParams(dimension_semantics=sem, vmem_limit_bytes=VMEM_LIMIT)


def _nt_dot(a, b):
    return lax.dot_general(a, b, (((1,), (1,)), ((), ())), preferred_element_type=F32)


def _mod_kernel(c_ref, w_ref, b_ref, o_ref):
    c = c_ref[...]
    c_act = c * jax.nn.sigmoid(c)
    o_ref[0] = jnp.dot(c_act, w_ref[0], preferred_element_type=F32,
                       precision=lax.Precision.HIGHEST) + b_ref[0]


def _modulation(c, ada_w, ada_b):
    B, D = c.shape
    rows = 8
    c_pad = jnp.pad(c, ((0, rows - B), (0, 0)))
    out = pl.pallas_call(
        _mod_kernel,
        grid=(DEPTH, 6),
        in_specs=[
            pl.BlockSpec((rows, D), lambda l, n: (0, 0)),
            pl.BlockSpec((1, D, D), lambda l, n: (l, 0, n)),
            pl.BlockSpec((1, 1, D), lambda l, n: (l, 0, n)),
        ],
        out_specs=pl.BlockSpec((1, rows, D), lambda l, n: (l, 0, n)),
        out_shape=jax.ShapeDtypeStruct((DEPTH, rows, 6 * D), F32),
        compiler_params=_params("arbitrary", "arbitrary"),
        name="adaln_mod",
    )(c_pad, ada_w, ada_b.reshape(DEPTH, 1, 6 * D))
    return out[:, :B].reshape(DEPTH, B, 6, D)


def _norm_mod(x, g, scale, shift):
    ms = jnp.mean(x * x, axis=-1, keepdims=True)
    return (x * lax.rsqrt(ms + NORM_EPS) * g) * (1.0 + scale) + shift


def _inproj_kernel(x_ref, g_ref, sc_ref, sh_ref, w_ref, ra_ref, rb_ref, rc_ref, o_ref):
    h = _norm_mod(x_ref[...], g_ref[...], sc_ref[0], sh_ref[0]).astype(BF16)
    half = ROT_DIM // 2
    for j in range(IN_COLS // IN_TILE_N):
        cols = slice(j * IN_TILE_N, (j + 1) * IN_TILE_N)
        r = jnp.dot(h, w_ref[:, cols], preferred_element_type=F32)
        first_lane_block = j * IN_TILE_N // LANES
        if first_lane_block in (COL_SB_Q, COL_D_Q):
            r = r * (LOG2_E / math.sqrt(SB_HEAD_DIM))
        if first_lane_block in (COL_D_Q, COL_D_K):
            ra, rb, rc = ra_ref[...], rb_ref[...], rc_ref[...]
            for blk in range(IN_TILE_N // LANES):
                t = r[:, blk * LANES:(blk + 1) * LANES]
                rot = t * ra + pltpu.roll(t, LANES - half, 1) * rb + pltpu.roll(t, half, 1) * rc
                o_ref[:, j * IN_TILE_N + blk * LANES:j * IN_TILE_N + (blk + 1) * LANES] = rot.astype(BF16)
        else:
            o_ref[:, cols] = r.astype(BF16)


def _rope_lane_tables(seq_len):
    pos = jnp.arange(seq_len, dtype=F32)
    inv_freq = 1.0 / (ROPE_THETA ** (jnp.arange(0, ROT_DIM, 2, dtype=F32) / ROT_DIM))
    ang = pos[:, None] * inv_freq[None, :]
    cos, sin = jnp.cos(ang), jnp.sin(ang)
    half = ROT_DIM // 2
    lane = jnp.arange(LANES) % DIFF_QK_DIM
    f = lane % half
    first, second = lane < half, (lane >= half) & (lane < ROT_DIM)
    ra = jnp.where((first | second)[None, :], cos[:, f], 1.0)
    rb = jnp.where(first[None, :], -sin[:, f], 0.0)
    rc = jnp.where(second[None, :], sin[:, f], 0.0)
    return ra, rb, rc


def _input_projection(x2, norm_g, scale, shift, w_in_bf, rope, B, S):
    T, D = x2.shape
    tm = min(512, S)
    tiles_per_b = S // tm
    ra, rb, rc = rope
    rope_spec = pl.BlockSpec((tm, LANES), lambda i: (i % tiles_per_b, 0))
    mod_spec = pl.BlockSpec((1, 1, D), lambda i: (i // tiles_per_b, 0, 0))
    assert COL_D_K - COL_D_Q == IN_TILE_N // LANES and COL_D_V - COL_D_K == IN_TILE_N // LANES
    return pl.pallas_call(
        _inproj_kernel,
        grid=(T // tm,),
        in_specs=[
            pl.BlockSpec((tm, D), lambda i: (i, 0)),
            pl.BlockSpec((1, D), lambda i: (0, 0)),
            mod_spec, mod_spec,
            pl.BlockSpec((D, IN_COLS), lambda i: (0, 0)),
            rope_spec, rope_spec, rope_spec,
        ],
        out_specs=pl.BlockSpec((tm, IN_COLS), lambda i: (i, 0)),
        out_shape=jax.ShapeDtypeStruct((T, IN_COLS), BF16),
        compiler_params=_params("parallel"),
        name="in_proj",
    )(x2, norm_g.reshape(1, D), scale.reshape(B, 1, D), shift.reshape(B, 1, D), w_in_bf, ra, rb, rc)


def _sb_kernel(q_ref, k_ref, v_ref, o_ref, acc_sc, cs_sc, *, tb, n_pairs):
    i = pl.program_id(2)
    n_heads = 2 * n_pairs
    lane = lax.broadcasted_iota(jnp.int32, (tb, LANES), 1)
    q_heads = []
    for p in range(n_pairs):
        q = q_ref[:, p * LANES:(p + 1) * LANES]
        zero = jnp.zeros_like(q)
        q_heads += [jnp.where(lane < SB_HEAD_DIM, q, zero), jnp.where(lane >= SB_HEAD_DIM, q, zero)]
    row = lax.broadcasted_iota(jnp.int32, (tb, tb), 0)
    col = lax.broadcasted_iota(jnp.int32, (tb, tb), 1)
    strictly_before = col < row
    suffix_ones = jnp.where(row >= col, 1.0, 0.0).astype(BF16)

    acc_sc[...] = jnp.zeros_like(acc_sc)
    cs_sc[...] = jnp.zeros_like(cs_sc)

    def block(j, masked):
        start = pl.multiple_of(j * tb, tb)
        kb = k_ref[pl.ds(start, tb), :]
        vb = v_ref[pl.ds(start, tb), :]
        ys = [_nt_dot(q_heads[h], kb[:, (h // 2) * LANES:(h // 2 + 1) * LANES]) for h in range(n_heads)]
        costs = []
        for y in ys:
            neg_abs = lax.bitcast_convert_type(
                lax.bitcast_convert_type(y, jnp.uint32) | jnp.uint32(0x80000000), F32)
            cost = jnp.maximum(y, 0.0) + jnp.log2(1.0 + jnp.exp2(neg_abs))
            if masked:
                cost = jnp.where(strictly_before, cost, 0.0)
            costs.append(cost)
        sums = [jnp.dot(cost.astype(BF16), suffix_ones, preferred_element_type=F32) for cost in costs]
        for h in range(n_heads):
            later = cs_sc[h]
            w = jnp.exp2(ys[h] - sums[h] - later)
            if masked:
                w = jnp.where(strictly_before, w, 0.0)
            acc_sc[h] += jnp.dot(w.astype(BF16), vb[:, (h // 2) * LANES:(h // 2 + 1) * LANES],
                                 preferred_element_type=F32)
            cs_sc[h] = later + sums[h][:, 0:1]

    block(i, True)

    def body(jj, carry):
        block(i - 1 - jj, False)
        return carry

    lax.fori_loop(0, i, body, 0)
    for p in range(n_pairs):
        o_ref[:, p * LANES:(p + 1) * LANES] = jnp.where(
            lane < SB_HEAD_DIM, acc_sc[2 * p], acc_sc[2 * p + 1]).astype(BF16)


SB_PAIRS_PER_STEP = 4


def _sb_attention(proj, B, S):
    T = proj.shape[0]
    tb = min(256, S)
    nq = S // tb
    n_pairs = SB_PAIRS_PER_STEP
    width = n_pairs * LANES
    groups = SB_WIDTH // width
    return pl.pallas_call(
        functools.partial(_sb_kernel, tb=tb, n_pairs=n_pairs),
        grid=(B, groups, nq),
        in_specs=[
            pl.BlockSpec((tb, width), lambda b, p, i: (b * nq + i, COL_SB_Q // n_pairs + p)),
            pl.BlockSpec((S, width), lambda b, p, i: (b, COL_SB_K // n_pairs + p)),
            pl.BlockSpec((S, width), lambda b, p, i: (b, COL_SB_V // n_pairs + p)),
        ],
        out_specs=pl.BlockSpec((tb, width), lambda b, p, i: (b * nq + i, p)),
        out_shape=jax.ShapeDtypeStruct((T, SB_WIDTH), BF16),
        scratch_shapes=[pltpu.VMEM((2 * n_pairs, tb, LANES), F32),
                        pltpu.VMEM((2 * n_pairs, tb, 1), F32)],
        compiler_params=_params("parallel", "parallel", "arbitrary"),
        name="sb_attention",
    )(proj, proj, proj)


def _diff_kernel(lp_ref, g_ref, q_ref, k_ref, v_ref, o_ref, m_sc, acc_sc, *, tq, tk, lam_init):
    i = pl.program_id(1)
    lane = lax.broadcasted_iota(jnp.int32, (tq, LANES), 1)
    q_maps = []
    for h in range(DIFF_HEADS):
        q = q_ref[:, h * LANES:(h + 1) * LANES]
        zero = jnp.zeros_like(q)
        q_maps += [jnp.where(lane < DIFF_QK_DIM, q, zero), jnp.where(lane >= DIFF_QK_DIM, q, zero)]
    n_chains = len(q_maps)
    m_sc[...] = jnp.full_like(m_sc, NEG_INF)
    acc_sc[...] = jnp.zeros_like(acc_sc)
    q_pos = i * tq + lax.broadcasted_iota(jnp.int32, (tq, tk), 0)
    k_off = lax.broadcasted_iota(jnp.int32, (tq, tk), 1)
    ones = jnp.ones((tk, LANES), BF16)

    def block(j, masked):
        start = pl.multiple_of(j * tk, tk)
        kb = k_ref[pl.ds(start, tk), :]
        vb = v_ref[pl.ds(start, tk), :]
        head_cols = lambda c: slice((c // 2) * LANES, (c // 2 + 1) * LANES)
        scores = [_nt_dot(q_maps[c], kb[:, head_cols(c)]) for c in range(n_chains)]
        if masked:
            visible = j * tk + k_off <= q_pos
            scores = [jnp.where(visible, s, NEG_INF) for s in scores]
        m_prev = [m_sc[c] for c in range(n_chains)]
        m_new = [jnp.maximum(m_prev[c], jnp.max(scores[c], axis=1, keepdims=True)) for c in range(n_chains)]
        probs = [jnp.exp2(scores[c] - jnp.concatenate([m_new[c]] * (tk // LANES), axis=1))
                 for c in range(n_chains)]
        for c in range(n_chains):
            alpha = jnp.exp2(m_prev[c] - m_new[c])
            v_ext = jnp.concatenate([vb[:, head_cols(c)], ones], axis=1)
            acc_sc[c] = (jnp.concatenate([alpha, alpha], axis=1) * acc_sc[c]
                         + jnp.dot(probs[c].astype(BF16), v_ext, preferred_element_type=F32))
            m_sc[c] = m_new[c]

    n_full = (i * tq) // tk

    def body(j, carry):
        block(j, False)
        return carry

    lax.fori_loop(0, n_full, body, 0)
    block(n_full, True)

    lp = lp_ref[...]
    lam = (jnp.exp(jnp.sum(lp[0:1] * lp[1:2], axis=1, keepdims=True))
           - jnp.exp(jnp.sum(lp[2:3] * lp[3:4], axis=1, keepdims=True)) + lam_init)
    for h in range(DIFF_HEADS):
        a1, a2 = acc_sc[2 * h], acc_sc[2 * h + 1]
        o = a1[:, :LANES] / a1[:, LANES:] - lam * (a2[:, :LANES] / a2[:, LANES:])
        ms = jnp.mean(o * o, axis=-1, keepdims=True)
        y = (o * lax.rsqrt(ms + NORM_EPS) * g_ref[...]) * (1.0 - lam_init)
        o_ref[:, h * LANES:(h + 1) * LANES] = y.astype(BF16)


def _diff_attention(proj, lam_params, subln_g, lam_init, B, S):
    T = proj.shape[0]
    tq = min(256, S)
    tk = min(512, S)
    nq = S // tq
    n_chains = 2 * DIFF_HEADS
    return pl.pallas_call(
        functools.partial(_diff_kernel, tq=tq, tk=tk, lam_init=lam_init),
        grid=(B, nq),
        in_specs=[
            pl.BlockSpec((4, DIFF_QK_DIM), lambda b, i: (0, 0)),
            pl.BlockSpec((1, DIFF_V_DIM), lambda b, i: (0, 0)),
            pl.BlockSpec((tq, DIFF_QK_WIDTH), lambda b, i: (b * nq + i, COL_D_Q * LANES // DIFF_QK_WIDTH)),
            pl.BlockSpec((S, DIFF_QK_WIDTH), lambda b, i: (b, COL_D_K * LANES // DIFF_QK_WIDTH)),
            pl.BlockSpec((S, DIFF_V_WIDTH), lambda b, i: (b, COL_D_V * LANES // DIFF_V_WIDTH)),
        ],
        out_specs=pl.BlockSpec((tq, DIFF_V_WIDTH), lambda b, i: (b * nq + i, 0)),
        out_shape=jax.ShapeDtypeStruct((T, DIFF_V_WIDTH), BF16),
        scratch_shapes=[pltpu.VMEM((n_chains, tq, LANES), F32),
                        pltpu.VMEM((n_chains, tq, 2 * LANES), F32)],
        compiler_params=_params("parallel", "arbitrary"),
        name="diff_attention",
    )(lam_params, subln_g.reshape(1, DIFF_V_DIM), proj, proj, proj)


def _merge_kernel(ysb_ref, ydf_ref, gs_ref, gd_ref, gb_ref, wbs_ref, wbd_ref, wo_ref,
                  x_ref, g1_ref, n2_ref, sc2_ref, sh2_ref, xo_ref, h2_ref):
    D = x_ref.shape[1]
    a = jnp.dot(ysb_ref[...], wbs_ref[...], preferred_element_type=F32)
    b = jnp.dot(ydf_ref[...], wbd_ref[...], preferred_element_type=F32)
    gb = gb_ref[...]
    g_sb = jax.nn.sigmoid(gs_ref[...].astype(F32) + gb[:, :D])
    g_df = jax.nn.sigmoid(gd_ref[...].astype(F32) + gb[:, D:])
    merged = g_sb * a + g_df * b
    y = jnp.dot(merged.astype(BF16), wo_ref[...], preferred_element_type=F32)
    xn = x_ref[...] + g1_ref[0] * y
    xo_ref[...] = xn
    h2_ref[...] = _norm_mod(xn, n2_ref[...], sc2_ref[0], sh2_ref[0]).astype(BF16)


def _merge_project(y_sb, y_df, proj, gate_b, wbs, wbd, wo, x2, gate1, norm2_g, scale2, shift2, B, S):
    T, D = x2.shape
    tm = min(512, S)
    tiles_per_b = S // tm
    mod_spec = pl.BlockSpec((1, 1, D), lambda i: (i // tiles_per_b, 0, 0))
    const = lambda shape: pl.BlockSpec(shape, lambda i: (0, 0))
    return pl.pallas_call(
        _merge_kernel,
        grid=(T // tm,),
        in_specs=[
            pl.BlockSpec((tm, SB_WIDTH), lambda i: (i, 0)),
            pl.BlockSpec((tm, DIFF_V_WIDTH), lambda i: (i, 0)),
            pl.BlockSpec((tm, D), lambda i: (i, COL_G)),
            pl.BlockSpec((tm, D), lambda i: (i, COL_G + 1)),
            const((1, 2 * D)),
            const((SB_WIDTH, D)), const((DIFF_V_WIDTH, D)), const((D, D)),
            pl.BlockSpec((tm, D), lambda i: (i, 0)),
            mod_spec,
            const((1, D)),
            mod_spec, mod_spec,
        ],
        out_specs=[pl.BlockSpec((tm, D), lambda i: (i, 0)), pl.BlockSpec((tm, D), lambda i: (i, 0))],
        out_shape=[jax.ShapeDtypeStruct((T, D), F32), jax.ShapeDtypeStruct((T, D), BF16)],
        compiler_params=_params("parallel"),
        name="merge_out_proj",
    )(y_sb, y_df, proj, proj, gate_b.reshape(1, 2 * D), wbs, wbd, wo, x2,
      gate1.reshape(B, 1, D), norm2_g.reshape(1, D), scale2.reshape(B, 1, D), shift2.reshape(B, 1, D))


def _column_max(a):
    m = a[0:8]
    for r in range(8, a.shape[0], 8):
        m = jnp.maximum(m, a[r:r + 8])
    return jnp.max(m, axis=0, keepdims=True)


def _top_values(scores, k, with_rank=False):
    vals = []
    cur = scores
    rank = jnp.full(scores.shape, float(k), F32)
    for r in range(k):
        m = _column_max(cur)
        vals.append(m)
        hit = cur == m
        if with_rank:
            rank = jnp.where(hit, float(r), rank)
        cur = jnp.where(hit, NEG_INF, cur)
    return (vals, rank) if with_rank else vals


def _candidate_sums(v1, v1_all, v2, v2_all, row_id):
    assert PEER_TOPK == 16
    lo2 = v2_all[0:8]
    hi1 = v1_all[8:16]
    shift = lambda x, n: pltpu.roll(x, n, 0)
    groups = [v1[0] + v2_all[0:8], v1[0] + v2_all[8:16], v1[0] + v2_all[16:24],
              v1[1] + lo2]
    groups.append(jnp.where(row_id < 5, v1[2] + lo2, v1[4] + shift(lo2, 5)))
    groups.append(jnp.where(row_id < 4, v1[3] + lo2,
                            jnp.where(row_id < 6, v1[5] + shift(lo2, 4),
                                      v1[6] + shift(lo2, 6))))
    tail = shift(hi1, 2) + v2[0]
    groups.append(jnp.where(row_id < 2, v1[7] + lo2, tail))
    groups.append(jnp.where(row_id < 2, tail,
                            jnp.where(row_id == 2, v1[16] + v2[0], NEG_INF)))
    return groups


def _peer_select_kernel(h_ref, wq_ref, sk_ref, ht_ref, c1_ref, e1_ref, r2_ref, e2_ref, *, tm):
    hb = h_ref[...]
    ht_ref[...] = hb.T
    K = PEER_TOPK + 1
    row_id = lax.broadcasted_iota(jnp.int32, (8, LANES), 0)
    pad_rows = jnp.full((24 - K, LANES), NEG_INF, F32)

    def head(h, carry):
        q_t = _nt_dot(wq_ref[h], hb).astype(BF16)
        half = PEER_QUERY_DIM // 2
        sc = [jnp.dot(sk_ref[h, c], q_t[c * half:(c + 1) * half], preferred_element_type=F32)
              for c in range(2)]
        for lc in range(tm // LANES):
            ls = slice(lc * LANES, (lc + 1) * LANES)
            s1, s2 = sc[0][:, ls], sc[1][:, ls]
            v1 = _top_values(s1, K)
            v2, rank2 = _top_values(s2, K, with_rank=True)
            v2_all = jnp.concatenate(v2 + [pad_rows], axis=0)
            v1_all = jnp.concatenate(v1 + [pad_rows], axis=0)
            cands = _candidate_sums(v1, v1_all, v2, v2_all, row_id)
            best = cands[0][0:1]
            z_sum = jnp.zeros_like(best)
            last = best
            for r in range(K):
                stacked = cands[0]
                for ca in cands[1:]:
                    stacked = jnp.maximum(stacked, ca)
                m = jnp.max(stacked, axis=0, keepdims=True)
                if r < PEER_TOPK:
                    z_sum = z_sum + jnp.exp(m - best)
                    last = m
                    cands = [jnp.where(ca == m, NEG_INF, ca) for ca in cands]
            tau = 0.5 * (last + m)
            count = jnp.zeros_like(s1)
            for b in range(PEER_TOPK):
                count = count + jnp.where(s1 >= tau - v2[b], 1.0, 0.0)
            half_inv_z = 0.5 / z_sum
            chunked = (PEER_N_KEYS // PEER_CHUNK_KEYS, PEER_CHUNK_KEYS, LANES)
            c1_ref[:, h, :, ls] = count.reshape(chunked)
            e1_ref[:, h, :, ls] = (jnp.exp(s1 - v1[0]) * half_inv_z).reshape(chunked)
            r2_ref[h, :, ls] = rank2
            e2_ref[h, :, ls] = jnp.exp(s2 - v2[0])
        return carry

    lax.fori_loop(0, PEER_HEADS, head, 0)


def _peer_select(h2, wq_t, subkeys_bf):
    T, D = h2.shape
    tm = min(512, T)
    tok_spec = pl.BlockSpec((PEER_HEADS, PEER_N_KEYS, tm), lambda i: (0, 0, i))
    tok_shape = jax.ShapeDtypeStruct((PEER_HEADS, PEER_N_KEYS, T), F32)
    n_chunks = PEER_N_KEYS // PEER_CHUNK_KEYS
    row_spec = pl.BlockSpec((n_chunks, PEER_HEADS, PEER_CHUNK_KEYS, tm), lambda i: (0, 0, 0, i))
    row_shape = jax.ShapeDtypeStruct((n_chunks, PEER_HEADS, PEER_CHUNK_KEYS, T), F32)
    return pl.pallas_call(
        functools.partial(_peer_select_kernel, tm=tm),
        grid=(T // tm,),
        in_specs=[
            pl.BlockSpec((tm, D), lambda i: (i, 0)),
            pl.BlockSpec((PEER_HEADS, PEER_QUERY_DIM, D), lambda i: (0, 0, 0)),
            pl.BlockSpec((PEER_HEADS, 2, PEER_N_KEYS, PEER_QUERY_DIM // 2), lambda i: (0, 0, 0, 0)),
        ],
        out_specs=[
            pl.BlockSpec((D, tm), lambda i: (0, i)),
            row_spec, row_spec, tok_spec, tok_spec,
        ],
        out_shape=[
            jax.ShapeDtypeStruct((D, T), BF16),
            row_shape, row_shape, tok_shape, tok_shape,
        ],
        compiler_params=_params("parallel"),
        name="peer_select",
    )(h2, wq_t, subkeys_bf)


PEER_ROW_BLOCK_KEYS = 2


def _peer_mix_kernel(ht_ref, u_ref, vt_ref, c1_ref, e1_ref, r2_ref, e2_ref, x_ref, g2_ref,
                     xo_ref, acc_sc, act_sc, r2_sc, e2_sc, *, tm):
    j = pl.program_id(1)

    @pl.when(j == 0)
    def _():
        acc_sc[...] = jnp.zeros_like(acc_sc)
        r2_sc[...] = r2_ref[...].astype(BF16)
        e2_sc[...] = e2_ref[...].astype(BF16)

    rb = PEER_ROW_BLOCK_KEYS * PEER_N_KEYS
    tw = 2 * LANES
    pieces = [(r, c) for r in range(PEER_CHUNK // rb) for c in range(tm // tw)]

    def scores(p):
        r, c = pieces[p]
        return jnp.dot(u_ref[r * rb:(r + 1) * rb, :], ht_ref[:, c * tw:(c + 1) * tw],
                       preferred_element_type=F32)

    zero = jnp.zeros((PEER_N_KEYS, LANES), BF16)

    def gates(p, a_blk, k):
        r, c = pieces[p]
        ii = r * PEER_ROW_BLOCK_KEYS + k
        for lc in range(tw // LANES):
            ls = slice(c * tw + lc * LANES, c * tw + (lc + 1) * LANES)
            gate = None
            for h in range(PEER_HEADS):
                count = jnp.broadcast_to(c1_ref[0, h, ii:ii + 1, ls], (PEER_N_KEYS, LANES)).astype(BF16)
                e1 = jnp.broadcast_to(e1_ref[0, h, ii:ii + 1, ls], (PEER_N_KEYS, LANES)).astype(BF16)
                term = jnp.where(r2_sc[h, :, ls] < count, e1 * e2_sc[h, :, ls], zero)
                gate = term if gate is None else gate + term
            a = a_blk[k * PEER_N_KEYS:(k + 1) * PEER_N_KEYS, lc * LANES:(lc + 1) * LANES]
            gelu2 = a * (1.0 + lax.erf(a * (1.0 / math.sqrt(2.0))))
            act_sc[ii * PEER_N_KEYS:(ii + 1) * PEER_N_KEYS, ls] = gelu2.astype(BF16) * gate

    def mix(p):
        r, c = pieces[p]
        cols = slice(c * tw, (c + 1) * tw)
        acc_sc[:, cols] += jnp.dot(vt_ref[:, r * rb:(r + 1) * rb], act_sc[r * rb:(r + 1) * rb, cols],
                                   preferred_element_type=F32)

    a_next = scores(0)
    for p in range(len(pieces)):
        a_cur = a_next
        if p + 1 < len(pieces):
            a_next = scores(p + 1)
        gates(p, a_cur, 0)
        if p > 0:
            mix(p - 1)
        gates(p, a_cur, 1)
    mix(len(pieces) - 1)

    @pl.when(j == pl.num_programs(1) - 1)
    def _():
        xo_ref[...] = x_ref[...] + g2_ref[0] * acc_sc[...].T


def _peer_mix(h_t, u_bf, v_t, c1, e1, r2, e2, x2, gate2, B, S):
    T, D = x2.shape
    tm = min(512, S)
    tiles_per_b = S // tm
    tok_spec = pl.BlockSpec((PEER_HEADS, PEER_N_KEYS, tm), lambda i, j: (0, 0, i))
    row_spec = pl.BlockSpec((1, PEER_HEADS, PEER_CHUNK_KEYS, tm), lambda i, j: (j, 0, 0, i))
    return pl.pallas_call(
        functools.partial(_peer_mix_kernel, tm=tm),
        grid=(T // tm, PEER_N_EXPERTS // PEER_CHUNK),
        in_specs=[
            pl.BlockSpec((D, tm), lambda i, j: (0, i)),
            pl.BlockSpec((PEER_CHUNK, D), lambda i, j: (j, 0)),
            pl.BlockSpec((D, PEER_CHUNK), lambda i, j: (0, j)),
            row_spec, row_spec, tok_spec, tok_spec,
            pl.BlockSpec((tm, D), lambda i, j: (i, 0)),
            pl.BlockSpec((1, 1, D), lambda i, j: (i // tiles_per_b, 0, 0)),
        ],
        out_specs=pl.BlockSpec((tm, D), lambda i, j: (i, 0)),
        out_shape=jax.ShapeDtypeStruct((T, D), F32),
        scratch_shapes=[pltpu.VMEM((D, tm), F32), pltpu.VMEM((PEER_CHUNK, tm), BF16),
                        pltpu.VMEM((PEER_HEADS, PEER_N_KEYS, tm), BF16),
                        pltpu.VMEM((PEER_HEADS, PEER_N_KEYS, tm), BF16)],
        compiler_params=_params("parallel", "arbitrary"),
        name="peer_mix",
    )(h_t, u_bf, v_t, c1, e1, r2, e2, x2, gate2.reshape(B, 1, D))


def _final_norm_kernel(x_ref, g_ref, o_ref):
    x = x_ref[...]
    ms = jnp.mean(x * x, axis=-1, keepdims=True)
    o_ref[...] = x * lax.rsqrt(ms + NORM_EPS) * g_ref[...]


def _final_norm(x2, g):
    T, D = x2.shape
    tm = min(1024, T)
    return pl.pallas_call(
        _final_norm_kernel,
        grid=(T // tm,),
        in_specs=[pl.BlockSpec((tm, D), lambda i: (i, 0)), pl.BlockSpec((1, D), lambda i: (0, 0))],
        out_specs=pl.BlockSpec((tm, D), lambda i: (i, 0)),
        out_shape=jax.ShapeDtypeStruct((T, D), F32),
        compiler_params=_params("parallel"),
        name="final_norm",
    )(x2, g.reshape(1, D))


def kernel(x, c, norm1_g, norm2_g, ada_w, ada_b, w_in, gate_b, diff_lambda, diff_subln_g,
           w_branch_sb, w_branch_diff, w_out, peer_wq, peer_subkeys, peer_u, peer_v, final_g):
    B, S, D = x.shape
    assert D == D_MODEL and S % LANES == 0
    x2 = x.reshape(B * S, D)
    mod = _modulation(c, ada_w, ada_b)
    rope = _rope_lane_tables(S)
    for l in range(DEPTH):
        lam_init = 0.8 - 0.6 * math.exp(-0.3 * l)
        shift1, scale1, gate1, shift2, scale2, gate2 = (mod[l, :, n] for n in range(6))
        proj = _input_projection(x2, norm1_g[l], scale1, shift1, w_in[l].astype(BF16), rope, B, S)
        y_sb = _sb_attention(proj, B, S)
        y_df = _diff_attention(proj, diff_lambda[l], diff_subln_g[l], lam_init, B, S)
        x2, h2 = _merge_project(y_sb, y_df, proj, gate_b[l], w_branch_sb[l].astype(BF16),
                                w_branch_diff[l].astype(BF16), w_out[l].astype(BF16), x2, gate1,
                                norm2_g[l], scale2, shift2, B, S)
        wq_t = peer_wq[l].T.astype(BF16).reshape(PEER_HEADS, PEER_QUERY_DIM, D)
        h_t, c1, e1, r2, e2 = _peer_select(h2, wq_t, peer_subkeys[l].astype(BF16))
        x2 = _peer_mix(h_t, peer_u[l].astype(BF16), peer_v[l].T.astype(BF16),
                       c1, e1, r2, e2, x2, gate2, B, S)
    return _final_norm(x2, final_g).reshape(B, S, D)
```

```python
import functools
import math

import jax
import jax.numpy as jnp
from jax import lax
from jax.experimental import pallas as pl
from jax.experimental.pallas import tpu as pltpu

F32 = jnp.float32
BF16 = jnp.bfloat16

D_MODEL = 1024
DEPTH = 2
SB_HEADS = 8
SB_HEAD_DIM = 64
SB_WIDTH = SB_HEADS * SB_HEAD_DIM
DIFF_HEADS = 4
DIFF_QK_DIM = 64
DIFF_V_DIM = 2 * DIFF_QK_DIM
DIFF_QK_WIDTH = DIFF_HEADS * 2 * DIFF_QK_DIM
DIFF_V_WIDTH = DIFF_HEADS * DIFF_V_DIM
ROPE_THETA = 500000.0
ROT_DIM = DIFF_QK_DIM // 4
IN_COLS = 3 * SB_WIDTH + 2 * DIFF_QK_WIDTH + DIFF_V_WIDTH + 2 * D_MODEL
PEER_HEADS = 8
PEER_N_KEYS = 128
PEER_N_EXPERTS = PEER_N_KEYS * PEER_N_KEYS
PEER_TOPK = 16
PEER_QUERY_DIM = 256
PEER_CHUNK_KEYS = 8
PEER_CHUNK = PEER_CHUNK_KEYS * PEER_N_KEYS
PEER_TOKEN_TILE = 512
PEER_TILES_PER_STEP = 2
NORM_EPS = 1e-6

LANES = 128
VMEM_LIMIT = 56 * 1024 * 1024
NEG_INF = float("-inf")
LOG2_E = 1.4426950408889634

IN_TILE_N = 512
COL_SB_Q, COL_SB_K, COL_SB_V = 0, SB_WIDTH // LANES, 2 * SB_WIDTH // LANES
COL_D_Q = 3 * SB_WIDTH // LANES
COL_D_K = COL_D_Q + DIFF_QK_WIDTH // LANES
COL_D_V = COL_D_K + DIFF_QK_WIDTH // LANES
COL_G = (3 * SB_WIDTH + 2 * DIFF_QK_WIDTH + DIFF_V_WIDTH) // D_MODEL


def _params(*sem):
    return pltpu.CompilerParams(dimension_semantics=sem, vmem_limit_bytes=VMEM_LIMIT)


def _nt_dot(a, b):
    return lax.dot_general(a, b, (((1,), (1,)), ((), ())), preferred_element_type=F32)


def _mod_kernel(c_ref, w_ref, b_ref, o_ref):
    c = c_ref[...]
    c_act = c * jax.nn.sigmoid(c)
    o_ref[0] = jnp.dot(c_act, w_ref[0], preferred_element_type=F32,
                       precision=lax.Precision.HIGHEST) + b_ref[0]


def _modulation(c, ada_w, ada_b):
    B, D = c.shape
    rows = 8
    c_pad = jnp.pad(c, ((0, rows - B), (0, 0)))
    out = pl.pallas_call(
        _mod_kernel,
        grid=(DEPTH, 6),
        in_specs=[
            pl.BlockSpec((rows, D), lambda l, n: (0, 0)),
            pl.BlockSpec((1, D, D), lambda l, n: (l, 0, n)),
            pl.BlockSpec((1, 1, D), lambda l, n: (l, 0, n)),
        ],
        out_specs=pl.BlockSpec((1, rows, D), lambda l, n: (l, 0, n)),
        out_shape=jax.ShapeDtypeStruct((DEPTH, rows, 6 * D), F32),
        compiler_params=_params("arbitrary", "arbitrary"),
        name="adaln_mod",
    )(c_pad, ada_w, ada_b.reshape(DEPTH, 1, 6 * D))
    return out[:, :B].reshape(DEPTH, B, 6, D)


def _norm_mod(x, g, scale, shift):
    ms = jnp.mean(x * x, axis=-1, keepdims=True)
    return (x * lax.rsqrt(ms + NORM_EPS) * g) * (1.0 + scale) + shift


def _inproj_kernel(x_ref, g_ref, sc_ref, sh_ref, w_ref, ra_ref, rb_ref, rc_ref, o_ref):
    h = _norm_mod(x_ref[...], g_ref[...], sc_ref[0], sh_ref[0]).astype(BF16)
    half = ROT_DIM // 2
    for j in range(IN_COLS // IN_TILE_N):
        cols = slice(j * IN_TILE_N, (j + 1) * IN_TILE_N)
        r = jnp.dot(h, w_ref[:, cols], preferred_element_type=F32)
        first_lane_block = j * IN_TILE_N // LANES
        if first_lane_block in (COL_SB_Q, COL_D_Q):
            r = r * (LOG2_E / math.sqrt(SB_HEAD_DIM))
        if first_lane_block in (COL_D_Q, COL_D_K):
            ra, rb, rc = ra_ref[...], rb_ref[...], rc_ref[...]
            for blk in range(IN_TILE_N // LANES):
                t = r[:, blk * LANES:(blk + 1) * LANES]
                rot = t * ra + pltpu.roll(t, LANES - half, 1) * rb + pltpu.roll(t, half, 1) * rc
                o_ref[:, j * IN_TILE_N + blk * LANES:j * IN_TILE_N + (blk + 1) * LANES] = rot.astype(BF16)
        else:
            o_ref[:, cols] = r.astype(BF16)


def _rope_lane_tables(seq_len):
    pos = jnp.arange(seq_len, dtype=F32)
    inv_freq = 1.0 / (ROPE_THETA ** (jnp.arange(0, ROT_DIM, 2, dtype=F32) / ROT_DIM))
    ang = pos[:, None] * inv_freq[None, :]
    cos, sin = jnp.cos(ang), jnp.sin(ang)
    half = ROT_DIM // 2
    lane = jnp.arange(LANES) % DIFF_QK_DIM
    f = lane % half
    first, second = lane < half, (lane >= half) & (lane < ROT_DIM)
    ra = jnp.where((first | second)[None, :], cos[:, f], 1.0)
    rb = jnp.where(first[None, :], -sin[:, f], 0.0)
    rc = jnp.where(second[None, :], sin[:, f], 0.0)
    return ra, rb, rc


def _input_projection(x2, norm_g, scale, shift, w_in_bf, rope, B, S):
    T, D = x2.shape
    tm = min(512, S)
    tiles_per_b = S // tm
    ra, rb, rc = rope
    rope_spec = pl.BlockSpec((tm, LANES), lambda i: (i % tiles_per_b, 0))
    mod_spec = pl.BlockSpec((1, 1, D), lambda i: (i // tiles_per_b, 0, 0))
    assert COL_D_K - COL_D_Q == IN_TILE_N // LANES and COL_D_V - COL_D_K == IN_TILE_N // LANES
    return pl.pallas_call(
        _inproj_kernel,
        grid=(T // tm,),
        in_specs=[
            pl.BlockSpec((tm, D), lambda i: (i, 0)),
            pl.BlockSpec((1, D), lambda i: (0, 0)),
            mod_spec, mod_spec,
            pl.BlockSpec((D, IN_COLS), lambda i: (0, 0)),
            rope_spec, rope_spec, rope_spec,
        ],
        out_specs=pl.BlockSpec((tm, IN_COLS), lambda i: (i, 0)),
        out_shape=jax.ShapeDtypeStruct((T, IN_COLS), BF16),
        compiler_params=_params("parallel"),
        name="in_proj",
    )(x2, norm_g.reshape(1, D), scale.reshape(B, 1, D), shift.reshape(B, 1, D), w_in_bf, ra, rb, rc)


def _sb_kernel(q_ref, k_ref, v_ref, o_ref, acc_sc, cs_sc, *, tb, n_pairs):
    i = pl.program_id(2)
    n_heads = 2 * n_pairs
    lane = lax.broadcasted_iota(jnp.int32, (tb, LANES), 1)
    q_heads = []
    for p in range(n_pairs):
        q = q_ref[:, p * LANES:(p + 1) * LANES]
        zero = jnp.zeros_like(q)
        q_heads += [jnp.where(lane < SB_HEAD_DIM, q, zero), jnp.where(lane >= SB_HEAD_DIM, q, zero)]
    row = lax.broadcasted_iota(jnp.int32, (tb, tb), 0)
    col = lax.broadcasted_iota(jnp.int32, (tb, tb), 1)
    strictly_before = col < row
    suffix_ones = jnp.where(row >= col, 1.0, 0.0).astype(BF16)

    acc_sc[...] = jnp.zeros_like(acc_sc)
    cs_sc[...] = jnp.zeros_like(cs_sc)

    def block(j, masked):
        start = pl.multiple_of(j * tb, tb)
        kb = k_ref[pl.ds(start, tb), :]
        vb = v_ref[pl.ds(start, tb), :]
        ys = [_nt_dot(q_heads[h], kb[:, (h // 2) * LANES:(h // 2 + 1) * LANES]) for h in range(n_heads)]
        costs = []
        for y in ys:
            neg_abs = lax.bitcast_convert_type(
                lax.bitcast_convert_type(y, jnp.uint32) | jnp.uint32(0x80000000), F32)
            cost = jnp.maximum(y, 0.0) + jnp.log2(1.0 + jnp.exp2(neg_abs))
            if masked:
                cost = jnp.where(strictly_before, cost, 0.0)
            costs.append(cost)
        sums = [jnp.dot(cost.astype(BF16), suffix_ones, preferred_element_type=F32) for cost in costs]
        for h in range(n_heads):
            later = cs_sc[h]
            w = jnp.exp2(ys[h] - sums[h] - later)
            if masked:
                w = jnp.where(strictly_before, w, 0.0)
            acc_sc[h] += jnp.dot(w.astype(BF16), vb[:, (h // 2) * LANES:(h // 2 + 1) * LANES],
                                 preferred_element_type=F32)
            cs_sc[h] = later + sums[h][:, 0:1]

    block(i, True)

    def body(jj, carry):
        block(i - 1 - jj, False)
        return carry

    lax.fori_loop(0, i, body, 0)
    for p in range(n_pairs):
        o_ref[:, p * LANES:(p + 1) * LANES] = jnp.where(
            lane < SB_HEAD_DIM, acc_sc[2 * p], acc_sc[2 * p + 1]).astype(BF16)


SB_PAIRS_PER_STEP = 4


def _sb_attention(proj, B, S):
    T = proj.shape[0]
    tb = min(256, S)
    nq = S // tb
    n_pairs = SB_PAIRS_PER_STEP
    width = n_pairs * LANES
    groups = SB_WIDTH // width
    return pl.pallas_call(
        functools.partial(_sb_kernel, tb=tb, n_pairs=n_pairs),
        grid=(B, groups, nq),
        in_specs=[
            pl.BlockSpec((tb, width), lambda b, p, i: (b * nq + i, COL_SB_Q // n_pairs + p)),
            pl.BlockSpec((S, width), lambda b, p, i: (b, COL_SB_K // n_pairs + p)),
            pl.BlockSpec((S, width), lambda b, p, i: (b, COL_SB_V // n_pairs + p)),
        ],
        out_specs=pl.BlockSpec((tb, width), lambda b, p, i: (b * nq + i, p)),
        out_shape=jax.ShapeDtypeStruct((T, SB_WIDTH), BF16),
        scratch_shapes=[pltpu.VMEM((2 * n_pairs, tb, LANES), F32),
                        pltpu.VMEM((2 * n_pairs, tb, 1), F32)],
        compiler_params=_params("parallel", "parallel", "arbitrary"),
        name="sb_attention",
    )(proj, proj, proj)


def _diff_kernel(lp_ref, g_ref, q_ref, k_ref, v_ref, o_ref, m_sc, acc_sc, *, tq, tk, lam_init):
    i = pl.program_id(1)
    lane = lax.broadcasted_iota(jnp.int32, (tq, LANES), 1)
    q_maps = []
    for h in range(DIFF_HEADS):
        q = q_ref[:, h * LANES:(h + 1) * LANES]
        zero = jnp.zeros_like(q)
        q_maps += [jnp.where(lane < DIFF_QK_DIM, q, zero), jnp.where(lane >= DIFF_QK_DIM, q, zero)]
    n_chains = len(q_maps)
    m_sc[...] = jnp.full_like(m_sc, NEG_INF)
    acc_sc[...] = jnp.zeros_like(acc_sc)
    q_pos = i * tq + lax.broadcasted_iota(jnp.int32, (tq, tk), 0)
    k_off = lax.broadcasted_iota(jnp.int32, (tq, tk), 1)
    ones = jnp.ones((tk, LANES), BF16)

    def block(j, masked):
        start = pl.multiple_of(j * tk, tk)
        kb = k_ref[pl.ds(start, tk), :]
        vb = v_ref[pl.ds(start, tk), :]
        head_cols = lambda c: slice((c // 2) * LANES, (c // 2 + 1) * LANES)
        scores = [_nt_dot(q_maps[c], kb[:, head_cols(c)]) for c in range(n_chains)]
        if masked:
            visible = j * tk + k_off <= q_pos
            scores = [jnp.where(visible, s, NEG_INF) for s in scores]
        m_prev = [m_sc[c] for c in range(n_chains)]
        m_new = [jnp.maximum(m_prev[c], jnp.max(scores[c], axis=1, keepdims=True)) for c in range(n_chains)]
        probs = [jnp.exp2(scores[c] - jnp.concatenate([m_new[c]] * (tk // LANES), axis=1))
                 for c in range(n_chains)]
        for c in range(n_chains):
            alpha = jnp.exp2(m_prev[c] - m_new[c])
            v_ext = jnp.concatenate([vb[:, head_cols(c)], ones], axis=1)
            acc_sc[c] = (jnp.concatenate([alpha, alpha], axis=1) * acc_sc[c]
                         + jnp.dot(probs[c].astype(BF16), v_ext, preferred_element_type=F32))
            m_sc[c] = m_new[c]

    n_full = (i * tq) // tk

    def body(j, carry):
        block(j, False)
        return carry

    lax.fori_loop(0, n_full, body, 0)
    block(n_full, True)

    lp = lp_ref[...]
    lam = (jnp.exp(jnp.sum(lp[0:1] * lp[1:2], axis=1, keepdims=True))
           - jnp.exp(jnp.sum(lp[2:3] * lp[3:4], axis=1, keepdims=True)) + lam_init)
    for h in range(DIFF_HEADS):
        a1, a2 = acc_sc[2 * h], acc_sc[2 * h + 1]
        o = a1[:, :LANES] / a1[:, LANES:] - lam * (a2[:, :LANES] / a2[:, LANES:])
        ms = jnp.mean(o * o, axis=-1, keepdims=True)
        y = (o * lax.rsqrt(ms + NORM_EPS) * g_ref[...]) * (1.0 - lam_init)
        o_ref[:, h * LANES:(h + 1) * LANES] = y.astype(BF16)


def _diff_attention(proj, lam_params, subln_g, lam_init, B, S):
    T = proj.shape[0]
    tq = min(256, S)
    tk = min(512, S)
    nq = S // tq
    n_chains = 2 * DIFF_HEADS
    return pl.pallas_call(
        functools.partial(_diff_kernel, tq=tq, tk=tk, lam_init=lam_init),
        grid=(B, nq),
        in_specs=[
            pl.BlockSpec((4, DIFF_QK_DIM), lambda b, i: (0, 0)),
            pl.BlockSpec((1, DIFF_V_DIM), lambda b, i: (0, 0)),
            pl.BlockSpec((tq, DIFF_QK_WIDTH), lambda b, i: (b * nq + i, COL_D_Q * LANES // DIFF_QK_WIDTH)),
            pl.BlockSpec((S, DIFF_QK_WIDTH), lambda b, i: (b, COL_D_K * LANES // DIFF_QK_WIDTH)),
            pl.BlockSpec((S, DIFF_V_WIDTH), lambda b, i: (b, COL_D_V * LANES // DIFF_V_WIDTH)),
        ],
        out_specs=pl.BlockSpec((tq, DIFF_V_WIDTH), lambda b, i: (b * nq + i, 0)),
        out_shape=jax.ShapeDtypeStruct((T, DIFF_V_WIDTH), BF16),
        scratch_shapes=[pltpu.VMEM((n_chains, tq, LANES), F32),
                        pltpu.VMEM((n_chains, tq, 2 * LANES), F32)],
        compiler_params=_params("parallel", "arbitrary"),
        name="diff_attention",
    )(lam_params, subln_g.reshape(1, DIFF_V_DIM), proj, proj, proj)


def _merge_kernel(ysb_ref, ydf_ref, gs_ref, gd_ref, gb_ref, wbs_ref, wbd_ref, wo_ref,
                  x_ref, g1_ref, n2_ref, sc2_ref, sh2_ref, xo_ref, h2_ref):
    D = x_ref.shape[1]
    a = jnp.dot(ysb_ref[...], wbs_ref[...], preferred_element_type=F32)
    b = jnp.dot(ydf_ref[...], wbd_ref[...], preferred_element_type=F32)
    gb = gb_ref[...]
    g_sb = jax.nn.sigmoid(gs_ref[...].astype(F32) + gb[:, :D])
    g_df = jax.nn.sigmoid(gd_ref[...].astype(F32) + gb[:, D:])
    merged = g_sb * a + g_df * b
    y = jnp.dot(merged.astype(BF16), wo_ref[...], preferred_element_type=F32)
    xn = x_ref[...] + g1_ref[0] * y
    xo_ref[...] = xn
    h2_ref[...] = _norm_mod(xn, n2_ref[...], sc2_ref[0], sh2_ref[0]).astype(BF16)


def _merge_project(y_sb, y_df, proj, gate_b, wbs, wbd, wo, x2, gate1, norm2_g, scale2, shift2, B, S):
    T, D = x2.shape
    tm = min(512, S)
    tiles_per_b = S // tm
    mod_spec = pl.BlockSpec((1, 1, D), lambda i: (i // tiles_per_b, 0, 0))
    const = lambda shape: pl.BlockSpec(shape, lambda i: (0, 0))
    return pl.pallas_call(
        _merge_kernel,
        grid=(T // tm,),
        in_specs=[
            pl.BlockSpec((tm, SB_WIDTH), lambda i: (i, 0)),
            pl.BlockSpec((tm, DIFF_V_WIDTH), lambda i: (i, 0)),
            pl.BlockSpec((tm, D), lambda i: (i, COL_G)),
            pl.BlockSpec((tm, D), lambda i: (i, COL_G + 1)),
            const((1, 2 * D)),
            const((SB_WIDTH, D)), const((DIFF_V_WIDTH, D)), const((D, D)),
            pl.BlockSpec((tm, D), lambda i: (i, 0)),
            mod_spec,
            const((1, D)),
            mod_spec, mod_spec,
        ],
        out_specs=[pl.BlockSpec((tm, D), lambda i: (i, 0)), pl.BlockSpec((tm, D), lambda i: (i, 0))],
        out_shape=[jax.ShapeDtypeStruct((T, D), F32), jax.ShapeDtypeStruct((T, D), BF16)],
        compiler_params=_params("parallel"),
        name="merge_out_proj",
    )(y_sb, y_df, proj, proj, gate_b.reshape(1, 2 * D), wbs, wbd, wo, x2,
      gate1.reshape(B, 1, D), norm2_g.reshape(1, D), scale2.reshape(B, 1, D), shift2.reshape(B, 1, D))


def _column_max(a):
    m = a[0:8]
    for r in range(8, a.shape[0], 8):
        m = jnp.maximum(m, a[r:r + 8])
    return jnp.max(m, axis=0, keepdims=True)


def _top_values(scores, k, with_rank=False):
    vals = []
    cur = scores
    rank = jnp.full(scores.shape, float(k), F32)
    for r in range(k):
        m = _column_max(cur)
        vals.append(m)
        hit = cur == m
        if with_rank:
            rank = jnp.where(hit, float(r), rank)
        cur = jnp.where(hit, NEG_INF, cur)
    return (vals, rank) if with_rank else vals


def _candidate_sums(v1, v1_all, v2, v2_all, row_id):
    assert PEER_TOPK == 16
    lo2 = v2_all[0:8]
    hi1 = v1_all[8:16]
    shift = lambda x, n: pltpu.roll(x, n, 0)
    groups = [v1[0] + v2_all[0:8], v1[0] + v2_all[8:16], v1[0] + v2_all[16:24],
              v1[1] + lo2]
    groups.append(jnp.where(row_id < 5, v1[2] + lo2, v1[4] + shift(lo2, 5)))
    groups.append(jnp.where(row_id < 4, v1[3] + lo2,
                            jnp.where(row_id < 6, v1[5] + shift(lo2, 4),
                                      v1[6] + shift(lo2, 6))))
    tail = shift(hi1, 2) + v2[0]
    groups.append(jnp.where(row_id < 2, v1[7] + lo2, tail))
    groups.append(jnp.where(row_id < 2, tail,
                            jnp.where(row_id == 2, v1[16] + v2[0], NEG_INF)))
    return groups


def _peer_select_kernel(h_ref, wq_ref, sk_ref, ht_ref, c1_ref, e1_ref, r2_ref, e2_ref, *, tm):
    hb = h_ref[...]
    ht_ref[0] = hb.T
    K = PEER_TOPK + 1
    row_id = lax.broadcasted_iota(jnp.int32, (8, LANES), 0)
    pad_rows = jnp.full((24 - K, LANES), NEG_INF, F32)

    def head(h, carry):
        q_t = _nt_dot(wq_ref[h], hb).astype(BF16)
        half = PEER_QUERY_DIM // 2
        sc = [jnp.dot(sk_ref[h, c], q_t[c * half:(c + 1) * half], preferred_element_type=F32)
              for c in range(2)]
        for lc in range(tm // LANES):
            ls = slice(lc * LANES, (lc + 1) * LANES)
            s1, s2 = sc[0][:, ls], sc[1][:, ls]
            v1 = _top_values(s1, K)
            v2, rank2 = _top_values(s2, K, with_rank=True)
            v2_all = jnp.concatenate(v2 + [pad_rows], axis=0)
            v1_all = jnp.concatenate(v1 + [pad_rows], axis=0)
            cands = _candidate_sums(v1, v1_all, v2, v2_all, row_id)
            best = cands[0][0:1]
            z_sum = jnp.zeros_like(best)
            last = best
            for r in range(K):
                stacked = cands[0]
                for ca in cands[1:]:
                    stacked = jnp.maximum(stacked, ca)
                m = jnp.max(stacked, axis=0, keepdims=True)
                if r < PEER_TOPK:
                    z_sum = z_sum + jnp.exp(m - best)
                    last = m
                    cands = [jnp.where(ca == m, NEG_INF, ca) for ca in cands]
            tau = 0.5 * (last + m)
            count = jnp.zeros_like(s1)
            for b in range(PEER_TOPK):
                count = count + jnp.where(s1 >= tau - v2[b], 1.0, 0.0)
            half_inv_z = 0.5 / z_sum
            chunked = (PEER_N_KEYS // PEER_CHUNK_KEYS, PEER_CHUNK_KEYS, LANES)
            c1_ref[:, 0, h, :, ls] = count.reshape(chunked)
            e1_ref[:, 0, h, :, ls] = (jnp.exp(s1 - v1[0]) * half_inv_z).reshape(chunked)
            r2_ref[0, h, :, ls] = rank2.astype(BF16)
            e2_ref[0, h, :, ls] = jnp.exp(s2 - v2[0]).astype(BF16)
        return carry

    lax.fori_loop(0, PEER_HEADS, head, 0)


def _peer_select(h2, wq_t, subkeys_bf):
    T, D = h2.shape
    tm = PEER_TOKEN_TILE
    n_tiles = T // tm
    tok_spec = pl.BlockSpec((1, PEER_HEADS, PEER_N_KEYS, tm), lambda i: (i, 0, 0, 0))
    tok_shape = jax.ShapeDtypeStruct((n_tiles, PEER_HEADS, PEER_N_KEYS, tm), BF16)
    n_chunks = PEER_N_KEYS // PEER_CHUNK_KEYS
    row_spec = pl.BlockSpec((n_chunks, 1, PEER_HEADS, PEER_CHUNK_KEYS, tm), lambda i: (0, i, 0, 0, 0))
    row_shape = jax.ShapeDtypeStruct((n_chunks, n_tiles, PEER_HEADS, PEER_CHUNK_KEYS, tm), F32)
    return pl.pallas_call(
        functools.partial(_peer_select_kernel, tm=tm),
        grid=(n_tiles,),
        in_specs=[
            pl.BlockSpec((tm, D), lambda i: (i, 0)),
            pl.BlockSpec((PEER_HEADS, PEER_QUERY_DIM, D), lambda i: (0, 0, 0)),
            pl.BlockSpec((PEER_HEADS, 2, PEER_N_KEYS, PEER_QUERY_DIM // 2), lambda i: (0, 0, 0, 0)),
        ],
        out_specs=[
            pl.BlockSpec((1, D, tm), lambda i: (i, 0, 0)),
            row_spec, row_spec, tok_spec, tok_spec,
        ],
        out_shape=[
            jax.ShapeDtypeStruct((n_tiles, D, tm), BF16),
            row_shape, row_shape, tok_shape, tok_shape,
        ],
        compiler_params=_params("parallel"),
        name="peer_select",
    )(h2, wq_t, subkeys_bf)


PEER_ROW_BLOCK_KEYS = 2


def _peer_mix_kernel(ht_ref, u_ref, vt_ref, c1_ref, e1_ref, r2_ref, e2_ref, x_ref, g2_ref,
                     xo_ref, acc_sc, act_sc, r2_sc, e2_sc, *, tm, n_tiles):
    j = pl.program_id(1)

    @pl.when(j == 0)
    def _():
        acc_sc[...] = jnp.zeros_like(acc_sc)
        r2_sc[...] = r2_ref[...]
        e2_sc[...] = e2_ref[...]

    rb = PEER_ROW_BLOCK_KEYS * PEER_N_KEYS
    tw = 2 * LANES
    pieces = [(r, c) for r in range(PEER_CHUNK // rb) for c in range(tm // tw)]
    zero = jnp.zeros((PEER_N_KEYS, LANES), BF16)

    def tile(t, carry):
        def scores(p):
            r, c = pieces[p]
            return jnp.dot(u_ref[r * rb:(r + 1) * rb, :], ht_ref[t, :, c * tw:(c + 1) * tw],
                           preferred_element_type=F32)

        def gates(p, a_blk, k):
            r, c = pieces[p]
            ii = r * PEER_ROW_BLOCK_KEYS + k
            for lc in range(tw // LANES):
                ls = slice(c * tw + lc * LANES, c * tw + (lc + 1) * LANES)
                gate = None
                for h in range(PEER_HEADS):
                    count = jnp.broadcast_to(c1_ref[0, t, h, ii:ii + 1, ls], (PEER_N_KEYS, LANES)).astype(BF16)
                    e1 = jnp.broadcast_to(e1_ref[0, t, h, ii:ii + 1, ls], (PEER_N_KEYS, LANES)).astype(BF16)
                    term = jnp.where(r2_sc[t, h, :, ls] < count, e1 * e2_sc[t, h, :, ls], zero)
                    gate = term if gate is None else gate + term
                a = a_blk[k * PEER_N_KEYS:(k + 1) * PEER_N_KEYS, lc * LANES:(lc + 1) * LANES]
                gelu2 = a * (1.0 + lax.erf(a * (1.0 / math.sqrt(2.0))))
                act_sc[ii * PEER_N_KEYS:(ii + 1) * PEER_N_KEYS, ls] = gelu2.astype(BF16) * gate

        def mix(p):
            r, c = pieces[p]
            cols = slice(c * tw, (c + 1) * tw)
            acc_sc[t, :, cols] += jnp.dot(vt_ref[0, :, r * rb:(r + 1) * rb],
                                          act_sc[r * rb:(r + 1) * rb, cols],
                                          preferred_element_type=F32)

        a_next = scores(0)
        for p in range(len(pieces)):
            a_cur = a_next
            if p + 1 < len(pieces):
                a_next = scores(p + 1)
            gates(p, a_cur, 0)
            if p > 0:
                mix(p - 1)
            gates(p, a_cur, 1)
        mix(len(pieces) - 1)
        return carry

    lax.fori_loop(0, n_tiles, tile, 0)

    @pl.when(j == pl.num_programs(1) - 1)
    def _():
        for t in range(n_tiles):
            rows = slice(t * tm, (t + 1) * tm)
            xo_ref[rows, :] = x_ref[rows, :] + g2_ref[0] * acc_sc[t].T


def _peer_mix(h_t, u_bf, v_t, c1, e1, r2, e2, x2, gate2, B, S):
    T, D = x2.shape
    tm = PEER_TOKEN_TILE
    n_tiles = min(PEER_TILES_PER_STEP, S // tm)
    span = n_tiles * tm
    steps_per_b = S // span
    tok_spec = pl.BlockSpec((n_tiles, PEER_HEADS, PEER_N_KEYS, tm), lambda i, j: (i, 0, 0, 0))
    row_spec = pl.BlockSpec((1, n_tiles, PEER_HEADS, PEER_CHUNK_KEYS, tm), lambda i, j: (j, i, 0, 0, 0))
    return pl.pallas_call(
        functools.partial(_peer_mix_kernel, tm=tm, n_tiles=n_tiles),
        grid=(T // span, PEER_N_EXPERTS // PEER_CHUNK),
        in_specs=[
            pl.BlockSpec((n_tiles, D, tm), lambda i, j: (i, 0, 0)),
            pl.BlockSpec((PEER_CHUNK, D), lambda i, j: (j, 0)),
            pl.BlockSpec((1, D, PEER_CHUNK), lambda i, j: (j, 0, 0)),
            row_spec, row_spec, tok_spec, tok_spec,
            pl.BlockSpec((span, D), lambda i, j: (i, 0)),
            pl.BlockSpec((1, 1, D), lambda i, j: (i // steps_per_b, 0, 0)),
        ],
        out_specs=pl.BlockSpec((span, D), lambda i, j: (i, 0)),
        out_shape=jax.ShapeDtypeStruct((T, D), F32),
        scratch_shapes=[pltpu.VMEM((n_tiles, D, tm), F32), pltpu.VMEM((PEER_CHUNK, tm), BF16),
                        pltpu.VMEM((n_tiles, PEER_HEADS, PEER_N_KEYS, tm), BF16),
                        pltpu.VMEM((n_tiles, PEER_HEADS, PEER_N_KEYS, tm), BF16)],
        compiler_params=_params("parallel", "arbitrary"),
        name="peer_mix",
    )(h_t, u_bf, v_t, c1, e1, r2, e2, x2, gate2.reshape(B, 1, D))


def _final_norm_kernel(x_ref, g_ref, o_ref):
    x = x_ref[...]
    ms = jnp.mean(x * x, axis=-1, keepdims=True)
    o_ref[...] = x * lax.rsqrt(ms + NORM_EPS) * g_ref[...]


def _final_norm(x2, g):
    T, D = x2.shape
    tm = min(1024, T)
    return pl.pallas_call(
        _final_norm_kernel,
        grid=(T // tm,),
        in_specs=[pl.BlockSpec((tm, D), lambda i: (i, 0)), pl.BlockSpec((1, D), lambda i: (0, 0))],
        out_specs=pl.BlockSpec((tm, D), lambda i: (i, 0)),
        out_shape=jax.ShapeDtypeStruct((T, D), F32),
        compiler_params=_params("parallel"),
        name="final_norm",
    )(x2, g.reshape(1, D))


def kernel(x, c, norm1_g, norm2_g, ada_w, ada_b, w_in, gate_b, diff_lambda, diff_subln_g,
           w_branch_sb, w_branch_diff, w_out, peer_wq, peer_subkeys, peer_u, peer_v, final_g):
    B, S, D = x.shape
    assert D == D_MODEL and S % LANES == 0
    x2 = x.reshape(B * S, D)
    mod = _modulation(c, ada_w, ada_b)
    rope = _rope_lane_tables(S)
    for l in range(DEPTH):
        lam_init = 0.8 - 0.6 * math.exp(-0.3 * l)
        shift1, scale1, gate1, shift2, scale2, gate2 = (mod[l, :, n] for n in range(6))
        proj = _input_projection(x2, norm1_g[l], scale1, shift1, w_in[l].astype(BF16), rope, B, S)
        y_sb = _sb_attention(proj, B, S)
        y_df = _diff_attention(proj, diff_lambda[l], diff_subln_g[l], lam_init, B, S)
        x2, h2 = _merge_project(y_sb, y_df, proj, gate_b[l], w_branch_sb[l].astype(BF16),
                                w_branch_diff[l].astype(BF16), w_out[l].astype(BF16), x2, gate1,
                                norm2_g[l], scale2, shift2, B, S)
        wq_t = peer_wq[l].T.astype(BF16).reshape(PEER_HEADS, PEER_QUERY_DIM, D)
        h_t, c1, e1, r2, e2 = _peer_select(h2, wq_t, peer_subkeys[l].astype(BF16))
        v_t = jnp.swapaxes(peer_v[l].astype(BF16).reshape(-1, PEER_CHUNK, D), 1, 2)
        x2 = _peer_mix(h_t, peer_u[l].astype(BF16), v_t,
                       c1, e1, r2, e2, x2, gate2, B, S)
    return _final_norm(x2, final_g).reshape(B, S, D)
```

```python
import functools
import math

import jax
import jax.numpy as jnp
from jax import lax
from jax.experimental import pallas as pl
from jax.experimental.pallas import tpu as pltpu

F32 = jnp.float32
BF16 = jnp.bfloat16

D_MODEL = 1024
DEPTH = 2
SB_HEADS = 8
SB_HEAD_DIM = 64
SB_WIDTH = SB_HEADS * SB_HEAD_DIM
DIFF_HEADS = 4
DIFF_QK_DIM = 64
DIFF_V_DIM = 2 * DIFF_QK_DIM
DIFF_QK_WIDTH = DIFF_HEADS * 2 * DIFF_QK_DIM
DIFF_V_WIDTH = DIFF_HEADS * DIFF_V_DIM
ROPE_THETA = 500000.0
ROT_DIM = DIFF_QK_DIM // 4
IN_COLS = 3 * SB_WIDTH + 2 * DIFF_QK_WIDTH + DIFF_V_WIDTH + 2 * D_MODEL
PEER_HEADS = 8
PEER_N_KEYS = 128
PEER_N_EXPERTS = PEER_N_KEYS * PEER_N_KEYS
PEER_TOPK = 16
PEER_QUERY_DIM = 256
PEER_CHUNK_KEYS = 8
PEER_CHUNK = PEER_CHUNK_KEYS * PEER_N_KEYS
PEER_TOKEN_TILE = 512
PEER_TILES_PER_STEP = 2
NORM_EPS = 1e-6

LANES = 128
VMEM_LIMIT = 56 * 1024 * 1024
NEG_INF = float("-inf")
LOG2_E = 1.4426950408889634

IN_TILE_N = 512
COL_SB_Q, COL_SB_K, COL_SB_V = 0, SB_WIDTH // LANES, 2 * SB_WIDTH // LANES
COL_D_Q = 3 * SB_WIDTH // LANES
COL_D_K = COL_D_Q + DIFF_QK_WIDTH // LANES
COL_D_V = COL_D_K + DIFF_QK_WIDTH // LANES
COL_G = (3 * SB_WIDTH + 2 * DIFF_QK_WIDTH + DIFF_V_WIDTH) // D_MODEL


def _params(*sem):
    return pltpu.CompilerParams(dimension_semantics=sem, vmem_limit_bytes=VMEM_LIMIT)


def _nt_dot(a, b):
    return lax.dot_general(a, b, (((1,), (1,)), ((), ())), preferred_element_type=F32)


def _mod_kernel(c_ref, w_ref, b_ref, o_ref):
    c = c_ref[...]
    c_act = c * jax.nn.sigmoid(c)
    o_ref[0] = jnp.dot(c_act, w_ref[0], preferred_element_type=F32,
                       precision=lax.Precision.HIGHEST) + b_ref[0]


def _modulation(c, ada_w, ada_b):
    B, D = c.shape
    rows = 8
    c_pad = jnp.pad(c, ((0, rows - B), (0, 0)))
    out = pl.pallas_call(
        _mod_kernel,
        grid=(DEPTH, 6),
        in_specs=[
            pl.BlockSpec((rows, D), lambda l, n: (0, 0)),
            pl.BlockSpec((1, D, D), lambda l, n: (l, 0, n)),
            pl.BlockSpec((1, 1, D), lambda l, n: (l, 0, n)),
        ],
        out_specs=pl.BlockSpec((1, rows, D), lambda l, n: (l, 0, n)),
        out_shape=jax.ShapeDtypeStruct((DEPTH, rows, 6 * D), F32),
        compiler_params=_params("arbitrary", "arbitrary"),
        name="adaln_mod",
    )(c_pad, ada_w, ada_b.reshape(DEPTH, 1, 6 * D))
    return out[:, :B].reshape(DEPTH, B, 6, D)


def _norm_mod(x, g, scale, shift):
    ms = jnp.mean(x * x, axis=-1, keepdims=True)
    return (x * lax.rsqrt(ms + NORM_EPS) * g) * (1.0 + scale) + shift


def _inproj_kernel(x_ref, g_ref, sc_ref, sh_ref, w_ref, ra_ref, rb_ref, rc_ref, o_ref):
    h = _norm_mod(x_ref[...], g_ref[...], sc_ref[0], sh_ref[0]).astype(BF16)
    half = ROT_DIM // 2
    for j in range(IN_COLS // IN_TILE_N):
        cols = slice(j * IN_TILE_N, (j + 1) * IN_TILE_N)
        r = jnp.dot(h, w_ref[:, cols], preferred_element_type=F32)
        first_lane_block = j * IN_TILE_N // LANES
        if first_lane_block in (COL_SB_Q, COL_D_Q):
            r = r * (LOG2_E / math.sqrt(SB_HEAD_DIM))
        if first_lane_block in (COL_D_Q, COL_D_K):
            ra, rb, rc = ra_ref[...], rb_ref[...], rc_ref[...]
            for blk in range(IN_TILE_N // LANES):
                t = r[:, blk * LANES:(blk + 1) * LANES]
                rot = t * ra + pltpu.roll(t, LANES - half, 1) * rb + pltpu.roll(t, half, 1) * rc
                o_ref[:, j * IN_TILE_N + blk * LANES:j * IN_TILE_N + (blk + 1) * LANES] = rot.astype(BF16)
        else:
            o_ref[:, cols] = r.astype(BF16)


def _rope_lane_tables(seq_len):
    pos = jnp.arange(seq_len, dtype=F32)
    inv_freq = 1.0 / (ROPE_THETA ** (jnp.arange(0, ROT_DIM, 2, dtype=F32) / ROT_DIM))
    ang = pos[:, None] * inv_freq[None, :]
    cos, sin = jnp.cos(ang), jnp.sin(ang)
    half = ROT_DIM // 2
    lane = jnp.arange(LANES) % DIFF_QK_DIM
    f = lane % half
    first, second = lane < half, (lane >= half) & (lane < ROT_DIM)
    ra = jnp.where((first | second)[None, :], cos[:, f], 1.0)
    rb = jnp.where(first[None, :], -sin[:, f], 0.0)
    rc = jnp.where(second[None, :], sin[:, f], 0.0)
    return ra, rb, rc


def _input_projection(x2, norm_g, scale, shift, w_in_bf, rope, B, S):
    T, D = x2.shape
    tm = min(512, S)
    tiles_per_b = S // tm
    ra, rb, rc = rope
    rope_spec = pl.BlockSpec((tm, LANES), lambda i: (i % tiles_per_b, 0))
    mod_spec = pl.BlockSpec((1, 1, D), lambda i: (i // tiles_per_b, 0, 0))
    assert COL_D_K - COL_D_Q == IN_TILE_N // LANES and COL_D_V - COL_D_K == IN_TILE_N // LANES
    return pl.pallas_call(
        _inproj_kernel,
        grid=(T // tm,),
        in_specs=[
            pl.BlockSpec((tm, D), lambda i: (i, 0)),
            pl.BlockSpec((1, D), lambda i: (0, 0)),
            mod_spec, mod_spec,
            pl.BlockSpec((D, IN_COLS), lambda i: (0, 0)),
            rope_spec, rope_spec, rope_spec,
        ],
        out_specs=pl.BlockSpec((tm, IN_COLS), lambda i: (i, 0)),
        out_shape=jax.ShapeDtypeStruct((T, IN_COLS), BF16),
        compiler_params=_params("parallel"),
        name="in_proj",
    )(x2, norm_g.reshape(1, D), scale.reshape(B, 1, D), shift.reshape(B, 1, D), w_in_bf, ra, rb, rc)


def _sb_kernel(q_ref, k_ref, v_ref, o_ref, acc_sc, cs_sc, *, tb, n_pairs):
    i = pl.program_id(2)
    n_heads = 2 * n_pairs
    lane = lax.broadcasted_iota(jnp.int32, (tb, LANES), 1)
    q_heads = []
    for p in range(n_pairs):
        q = q_ref[:, p * LANES:(p + 1) * LANES]
        zero = jnp.zeros_like(q)
        q_heads += [jnp.where(lane < SB_HEAD_DIM, q, zero), jnp.where(lane >= SB_HEAD_DIM, q, zero)]
    row = lax.broadcasted_iota(jnp.int32, (tb, tb), 0)
    col = lax.broadcasted_iota(jnp.int32, (tb, tb), 1)
    strictly_before = col < row
    suffix_ones = jnp.where(row >= col, 1.0, 0.0).astype(BF16)

    acc_sc[...] = jnp.zeros_like(acc_sc)
    cs_sc[...] = jnp.zeros_like(cs_sc)

    def block(j, masked):
        start = pl.multiple_of(j * tb, tb)
        kb = k_ref[pl.ds(start, tb), :]
        vb = v_ref[pl.ds(start, tb), :]
        ys = [_nt_dot(q_heads[h], kb[:, (h // 2) * LANES:(h // 2 + 1) * LANES]) for h in range(n_heads)]
        costs = []
        for y in ys:
            neg_abs = lax.bitcast_convert_type(
                lax.bitcast_convert_type(y, jnp.uint32) | jnp.uint32(0x80000000), F32)
            cost = jnp.maximum(y, 0.0) + jnp.log2(1.0 + jnp.exp2(neg_abs))
            if masked:
                cost = jnp.where(strictly_before, cost, 0.0)
            costs.append(cost)
        sums = [jnp.dot(cost.astype(BF16), suffix_ones, preferred_element_type=F32) for cost in costs]
        for h in range(n_heads):
            later = cs_sc[h]
            w = jnp.exp2(ys[h] - sums[h] - later)
            if masked:
                w = jnp.where(strictly_before, w, 0.0)
            acc_sc[h] += jnp.dot(w.astype(BF16), vb[:, (h // 2) * LANES:(h // 2 + 1) * LANES],
                                 preferred_element_type=F32)
            cs_sc[h] = later + sums[h][:, 0:1]

    block(i, True)

    def body(jj, carry):
        block(i - 1 - jj, False)
        return carry

    lax.fori_loop(0, i, body, 0)
    for p in range(n_pairs):
        o_ref[:, p * LANES:(p + 1) * LANES] = jnp.where(
            lane < SB_HEAD_DIM, acc_sc[2 * p], acc_sc[2 * p + 1]).astype(BF16)


SB_PAIRS_PER_STEP = 4


def _sb_attention(proj, B, S):
    T = proj.shape[0]
    tb = min(256, S)
    nq = S // tb
    n_pairs = SB_PAIRS_PER_STEP
    width = n_pairs * LANES
    groups = SB_WIDTH // width
    return pl.pallas_call(
        functools.partial(_sb_kernel, tb=tb, n_pairs=n_pairs),
        grid=(B, groups, nq),
        in_specs=[
            pl.BlockSpec((tb, width), lambda b, p, i: (b * nq + i, COL_SB_Q // n_pairs + p)),
            pl.BlockSpec((S, width), lambda b, p, i: (b, COL_SB_K // n_pairs + p)),
            pl.BlockSpec((S, width), lambda b, p, i: (b, COL_SB_V // n_pairs + p)),
        ],
        out_specs=pl.BlockSpec((tb, width), lambda b, p, i: (b * nq + i, p)),
        out_shape=jax.ShapeDtypeStruct((T, SB_WIDTH), BF16),
        scratch_shapes=[pltpu.VMEM((2 * n_pairs, tb, LANES), F32),
                        pltpu.VMEM((2 * n_pairs, tb, 1), F32)],
        compiler_params=_params("parallel", "parallel", "arbitrary"),
        name="sb_attention",
    )(proj, proj, proj)


def _diff_kernel(lp_ref, g_ref, q_ref, k_ref, v_ref, o_ref, m_sc, acc_sc, *, tq, tk, lam_init):
    i = pl.program_id(1)
    lane = lax.broadcasted_iota(jnp.int32, (tq, LANES), 1)
    q_maps = []
    for h in range(DIFF_HEADS):
        q = q_ref[:, h * LANES:(h + 1) * LANES]
        zero = jnp.zeros_like(q)
        q_maps += [jnp.where(lane < DIFF_QK_DIM, q, zero), jnp.where(lane >= DIFF_QK_DIM, q, zero)]
    n_chains = len(q_maps)
    m_sc[...] = jnp.full_like(m_sc, NEG_INF)
    acc_sc[...] = jnp.zeros_like(acc_sc)
    q_pos = i * tq + lax.broadcasted_iota(jnp.int32, (tq, tk), 0)
    k_off = lax.broadcasted_iota(jnp.int32, (tq, tk), 1)
    ones = jnp.ones((tk, LANES), BF16)

    def block(j, masked):
        start = pl.multiple_of(j * tk, tk)
        kb = k_ref[pl.ds(start, tk), :]
        vb = v_ref[pl.ds(start, tk), :]
        head_cols = lambda c: slice((c // 2) * LANES, (c // 2 + 1) * LANES)
        scores = [_nt_dot(q_maps[c], kb[:, head_cols(c)]) for c in range(n_chains)]
        if masked:
            visible = j * tk + k_off <= q_pos
            scores = [jnp.where(visible, s, NEG_INF) for s in scores]
        m_prev = [m_sc[c] for c in range(n_chains)]
        m_new = [jnp.maximum(m_prev[c], jnp.max(scores[c], axis=1, keepdims=True)) for c in range(n_chains)]
        probs = [jnp.exp2(scores[c] - jnp.concatenate([m_new[c]] * (tk // LANES), axis=1))
                 for c in range(n_chains)]
        for c in range(n_chains):
            alpha = jnp.exp2(m_prev[c] - m_new[c])
            v_ext = jnp.concatenate([vb[:, head_cols(c)], ones], axis=1)
            acc_sc[c] = (jnp.concatenate([alpha, alpha], axis=1) * acc_sc[c]
                         + jnp.dot(probs[c].astype(BF16), v_ext, preferred_element_type=F32))
            m_sc[c] = m_new[c]

    n_full = (i * tq) // tk

    def body(j, carry):
        block(j, False)
        return carry

    lax.fori_loop(0, n_full, body, 0)
    block(n_full, True)

    lp = lp_ref[...]
    lam = (jnp.exp(jnp.sum(lp[0:1] * lp[1:2], axis=1, keepdims=True))
           - jnp.exp(jnp.sum(lp[2:3] * lp[3:4], axis=1, keepdims=True)) + lam_init)
    for h in range(DIFF_HEADS):
        a1, a2 = acc_sc[2 * h], acc_sc[2 * h + 1]
        o = a1[:, :LANES] / a1[:, LANES:] - lam * (a2[:, :LANES] / a2[:, LANES:])
        ms = jnp.mean(o * o, axis=-1, keepdims=True)
        y = (o * lax.rsqrt(ms + NORM_EPS) * g_ref[...]) * (1.0 - lam_init)
        o_ref[:, h * LANES:(h + 1) * LANES] = y.astype(BF16)


def _diff_attention(proj, lam_params, subln_g, lam_init, B, S):
    T = proj.shape[0]
    tq = min(256, S)
    tk = min(512, S)
    nq = S // tq
    n_chains = 2 * DIFF_HEADS
    return pl.pallas_call(
        functools.partial(_diff_kernel, tq=tq, tk=tk, lam_init=lam_init),
        grid=(B, nq),
        in_specs=[
            pl.BlockSpec((4, DIFF_QK_DIM), lambda b, i: (0, 0)),
            pl.BlockSpec((1, DIFF_V_DIM), lambda b, i: (0, 0)),
            pl.BlockSpec((tq, DIFF_QK_WIDTH), lambda b, i: (b * nq + i, COL_D_Q * LANES // DIFF_QK_WIDTH)),
            pl.BlockSpec((S, DIFF_QK_WIDTH), lambda b, i: (b, COL_D_K * LANES // DIFF_QK_WIDTH)),
            pl.BlockSpec((S, DIFF_V_WIDTH), lambda b, i: (b, COL_D_V * LANES // DIFF_V_WIDTH)),
        ],
        out_specs=pl.BlockSpec((tq, DIFF_V_WIDTH), lambda b, i: (b * nq + i, 0)),
        out_shape=jax.ShapeDtypeStruct((T, DIFF_V_WIDTH), BF16),
        scratch_shapes=[pltpu.VMEM((n_chains, tq, LANES), F32),
                        pltpu.VMEM((n_chains, tq, 2 * LANES), F32)],
        compiler_params=_params("parallel", "arbitrary"),
        name="diff_attention",
    )(lam_params, subln_g.reshape(1, DIFF_V_DIM), proj, proj, proj)


def _merge_kernel(ysb_ref, ydf_ref, gs_ref, gd_ref, gb_ref, wbs_ref, wbd_ref, wo_ref,
                  x_ref, g1_ref, n2_ref, sc2_ref, sh2_ref, xo_ref, h2_ref):
    D = x_ref.shape[1]
    a = jnp.dot(ysb_ref[...], wbs_ref[...], preferred_element_type=F32)
    b = jnp.dot(ydf_ref[...], wbd_ref[...], preferred_element_type=F32)
    gb = gb_ref[...]
    g_sb = jax.nn.sigmoid(gs_ref[...].astype(F32) + gb[:, :D])
    g_df = jax.nn.sigmoid(gd_ref[...].astype(F32) + gb[:, D:])
    merged = g_sb * a + g_df * b
    y = jnp.dot(merged.astype(BF16), wo_ref[...], preferred_element_type=F32)
    xn = x_ref[...] + g1_ref[0] * y
    xo_ref[...] = xn
    h2_ref[...] = _norm_mod(xn, n2_ref[...], sc2_ref[0], sh2_ref[0]).astype(BF16)


def _merge_project(y_sb, y_df, proj, gate_b, wbs, wbd, wo, x2, gate1, norm2_g, scale2, shift2, B, S):
    T, D = x2.shape
    tm = min(512, S)
    tiles_per_b = S // tm
    mod_spec = pl.BlockSpec((1, 1, D), lambda i: (i // tiles_per_b, 0, 0))
    const = lambda shape: pl.BlockSpec(shape, lambda i: (0, 0))
    return pl.pallas_call(
        _merge_kernel,
        grid=(T // tm,),
        in_specs=[
            pl.BlockSpec((tm, SB_WIDTH), lambda i: (i, 0)),
            pl.BlockSpec((tm, DIFF_V_WIDTH), lambda i: (i, 0)),
            pl.BlockSpec((tm, D), lambda i: (i, COL_G)),
            pl.BlockSpec((tm, D), lambda i: (i, COL_G + 1)),
            const((1, 2 * D)),
            const((SB_WIDTH, D)), const((DIFF_V_WIDTH, D)), const((D, D)),
            pl.BlockSpec((tm, D), lambda i: (i, 0)),
            mod_spec,
            const((1, D)),
            mod_spec, mod_spec,
        ],
        out_specs=[pl.BlockSpec((tm, D), lambda i: (i, 0)), pl.BlockSpec((tm, D), lambda i: (i, 0))],
        out_shape=[jax.ShapeDtypeStruct((T, D), F32), jax.ShapeDtypeStruct((T, D), BF16)],
        compiler_params=_params("parallel"),
        name="merge_out_proj",
    )(y_sb, y_df, proj, proj, gate_b.reshape(1, 2 * D), wbs, wbd, wo, x2,
      gate1.reshape(B, 1, D), norm2_g.reshape(1, D), scale2.reshape(B, 1, D), shift2.reshape(B, 1, D))


def _column_max(a):
    m = a[0:8]
    for r in range(8, a.shape[0], 8):
        m = jnp.maximum(m, a[r:r + 8])
    return jnp.max(m, axis=0, keepdims=True)


def _top_values(scores, k, with_rank=False):
    vals = []
    cur = scores
    rank = jnp.full(scores.shape, float(k), F32)
    for r in range(k):
        m = _column_max(cur)
        vals.append(m)
        hit = cur == m
        if with_rank:
            rank = jnp.where(hit, float(r), rank)
        cur = jnp.where(hit, NEG_INF, cur)
    return (vals, rank) if with_rank else vals


def _candidate_sums(v1, v1_all, v2, v2_all, row_id):
    assert PEER_TOPK == 16
    lo2 = v2_all[0:8]
    hi1 = v1_all[8:16]
    shift = lambda x, n: pltpu.roll(x, n, 0)
    groups = [v1[0] + v2_all[0:8], v1[0] + v2_all[8:16], v1[0] + v2_all[16:24],
              v1[1] + lo2]
    groups.append(jnp.where(row_id < 5, v1[2] + lo2, v1[4] + shift(lo2, 5)))
    groups.append(jnp.where(row_id < 4, v1[3] + lo2,
                            jnp.where(row_id < 6, v1[5] + shift(lo2, 4),
                                      v1[6] + shift(lo2, 6))))
    tail = shift(hi1, 2) + v2[0]
    groups.append(jnp.where(row_id < 2, v1[7] + lo2, tail))
    groups.append(jnp.where(row_id < 2, tail,
                            jnp.where(row_id == 2, v1[16] + v2[0], NEG_INF)))
    return groups


def _peer_select_kernel(h_ref, wq_ref, sk_ref, ht_ref, c1_ref, e1_ref, r2_ref, e2_ref, *, tm):
    hb = h_ref[...]
    ht_ref[0] = hb.T
    K = PEER_TOPK + 1
    row_id = lax.broadcasted_iota(jnp.int32, (8, LANES), 0)
    pad_rows = jnp.full((24 - K, LANES), NEG_INF, F32)

    def head(h, carry):
        q_t = _nt_dot(wq_ref[h], hb).astype(BF16)
        half = PEER_QUERY_DIM // 2
        sc = [jnp.dot(sk_ref[h, c], q_t[c * half:(c + 1) * half], preferred_element_type=F32)
              for c in range(2)]
        for lc in range(tm // LANES):
            ls = slice(lc * LANES, (lc + 1) * LANES)
            s1, s2 = sc[0][:, ls], sc[1][:, ls]
            v1 = _top_values(s1, K)
            v2, rank2 = _top_values(s2, K, with_rank=True)
            v2_all = jnp.concatenate(v2 + [pad_rows], axis=0)
            v1_all = jnp.concatenate(v1 + [pad_rows], axis=0)
            cands = _candidate_sums(v1, v1_all, v2, v2_all, row_id)
            best = cands[0][0:1]
            z_sum = jnp.zeros_like(best)
            last = best
            for r in range(K):
                stacked = cands[0]
                for ca in cands[1:]:
                    stacked = jnp.maximum(stacked, ca)
                m = jnp.max(stacked, axis=0, keepdims=True)
                if r < PEER_TOPK:
                    z_sum = z_sum + jnp.exp(m - best)
                    last = m
                    cands = [jnp.where(ca == m, NEG_INF, ca) for ca in cands]
            tau = 0.5 * (last + m)
            count = jnp.zeros_like(s1)
            for b in range(PEER_TOPK):
                count = count + jnp.where(s1 >= tau - v2[b], 1.0, 0.0)
            half_inv_z = 0.5 / z_sum
            chunked = (PEER_N_KEYS // PEER_CHUNK_KEYS, PEER_CHUNK_KEYS, LANES)
            c1_ref[:, 0, h, :, ls] = count.reshape(chunked)
            e1_ref[:, 0, h, :, ls] = (jnp.exp(s1 - v1[0]) * half_inv_z).reshape(chunked)
            r2_ref[0, h, :, ls] = rank2.astype(BF16)
            e2_ref[0, h, :, ls] = jnp.exp(s2 - v2[0]).astype(BF16)
        return carry

    lax.fori_loop(0, PEER_HEADS, head, 0)


def _peer_select(h2, wq_t, subkeys_bf):
    T, D = h2.shape
    tm = PEER_TOKEN_TILE
    n_tiles = T // tm
    tok_spec = pl.BlockSpec((1, PEER_HEADS, PEER_N_KEYS, tm), lambda i: (i, 0, 0, 0))
    tok_shape = jax.ShapeDtypeStruct((n_tiles, PEER_HEADS, PEER_N_KEYS, tm), BF16)
    n_chunks = PEER_N_KEYS // PEER_CHUNK_KEYS
    row_spec = pl.BlockSpec((n_chunks, 1, PEER_HEADS, PEER_CHUNK_KEYS, tm), lambda i: (0, i, 0, 0, 0))
    row_shape = jax.ShapeDtypeStruct((n_chunks, n_tiles, PEER_HEADS, PEER_CHUNK_KEYS, tm), F32)
    return pl.pallas_call(
        functools.partial(_peer_select_kernel, tm=tm),
        grid=(n_tiles,),
        in_specs=[
            pl.BlockSpec((tm, D), lambda i: (i, 0)),
            pl.BlockSpec((PEER_HEADS, PEER_QUERY_DIM, D), lambda i: (0, 0, 0)),
            pl.BlockSpec((PEER_HEADS, 2, PEER_N_KEYS, PEER_QUERY_DIM // 2), lambda i: (0, 0, 0, 0)),
        ],
        out_specs=[
            pl.BlockSpec((1, D, tm), lambda i: (i, 0, 0)),
            row_spec, row_spec, tok_spec, tok_spec,
        ],
        out_shape=[
            jax.ShapeDtypeStruct((n_tiles, D, tm), BF16),
            row_shape, row_shape, tok_shape, tok_shape,
        ],
        compiler_params=_params("parallel"),
        name="peer_select",
    )(h2, wq_t, subkeys_bf)


PEER_ROW_BLOCK_KEYS = 4
PEER_PIECE_TOKENS = 2 * LANES


def _peer_mix_kernel(ht_ref, u_ref, vt_ref, c1_ref, e1_ref, r2_ref, e2_ref, x_ref, g2_ref,
                     xo_ref, acc_sc, act_sc, r2_sc, e2_sc, *, tm, n_tiles):
    j = pl.program_id(1)

    @pl.when(j == 0)
    def _():
        acc_sc[...] = jnp.zeros_like(acc_sc)
        for lc in range(tm // LANES):
            r2_sc[:, :, lc] = r2_ref[:, :, :, lc * LANES:(lc + 1) * LANES]
            e2_sc[:, :, lc] = e2_ref[:, :, :, lc * LANES:(lc + 1) * LANES]

    rb = PEER_ROW_BLOCK_KEYS * PEER_N_KEYS
    tw = PEER_PIECE_TOKENS
    n_rb = PEER_CHUNK // rb
    pieces = [(r, c) for c in range(tm // tw) for r in range(n_rb)]
    zero = jnp.zeros((PEER_N_KEYS, LANES), BF16)

    def tile(t, carry):
        def scores(p):
            r, c = pieces[p]
            return jnp.dot(u_ref[r * rb:(r + 1) * rb, :], ht_ref[t, :, c * tw:(c + 1) * tw],
                           preferred_element_type=F32)

        def gates(p, a_blk, k):
            r, c = pieces[p]
            ii = r * PEER_ROW_BLOCK_KEYS + k
            for lc in range(tw // LANES):
                chunk = c * (tw // LANES) + lc
                ls = slice(chunk * LANES, (chunk + 1) * LANES)
                gate = None
                for h in range(PEER_HEADS):
                    count = jnp.broadcast_to(c1_ref[0, t, h, ii:ii + 1, ls], (PEER_N_KEYS, LANES)).astype(BF16)
                    e1 = jnp.broadcast_to(e1_ref[0, t, h, ii:ii + 1, ls], (PEER_N_KEYS, LANES)).astype(BF16)
                    term = jnp.where(r2_sc[t, h, chunk] < count, e1 * e2_sc[t, h, chunk], zero)
                    gate = term if gate is None else gate + term
                a = a_blk[k * PEER_N_KEYS:(k + 1) * PEER_N_KEYS, lc * LANES:(lc + 1) * LANES]
                gelu2 = a * (1.0 + lax.erf(a * (1.0 / math.sqrt(2.0))))
                act_sc[c, ii * PEER_N_KEYS:(ii + 1) * PEER_N_KEYS, lc * LANES:(lc + 1) * LANES] = (
                    gelu2.astype(BF16) * gate)

        def mix(c):
            acc_sc[t, c] += jnp.dot(vt_ref[0], act_sc[c], preferred_element_type=F32)

        a_next = scores(0)
        for p in range(len(pieces)):
            r, c = pieces[p]
            a_cur = a_next
            if p + 1 < len(pieces):
                a_next = scores(p + 1)
            for k in range(PEER_ROW_BLOCK_KEYS):
                if k == PEER_ROW_BLOCK_KEYS // 2 and r == 0 and c > 0:
                    mix(c - 1)
                gates(p, a_cur, k)
        mix(tm // tw - 1)
        return carry

    lax.fori_loop(0, n_tiles, tile, 0)

    @pl.when(j == pl.num_programs(1) - 1)
    def _():
        for t in range(n_tiles):
            rows = slice(t * tm, (t + 1) * tm)
            mixed_t = jnp.concatenate([acc_sc[t, c] for c in range(tm // tw)], axis=1)
            xo_ref[rows, :] = x_ref[rows, :] + g2_ref[0] * mixed_t.T


def _peer_mix(h_t, u_bf, v_t, c1, e1, r2, e2, x2, gate2, B, S):
    T, D = x2.shape
    tm = PEER_TOKEN_TILE
    n_tiles = min(PEER_TILES_PER_STEP, S // tm)
    span = n_tiles * tm
    steps_per_b = S // span
    tok_spec = pl.BlockSpec((n_tiles, PEER_HEADS, PEER_N_KEYS, tm), lambda i, j: (i, 0, 0, 0))
    row_spec = pl.BlockSpec((1, n_tiles, PEER_HEADS, PEER_CHUNK_KEYS, tm), lambda i, j: (j, i, 0, 0, 0))
    return pl.pallas_call(
        functools.partial(_peer_mix_kernel, tm=tm, n_tiles=n_tiles),
        grid=(T // span, PEER_N_EXPERTS // PEER_CHUNK),
        in_specs=[
            pl.BlockSpec((n_tiles, D, tm), lambda i, j: (i, 0, 0)),
            pl.BlockSpec((PEER_CHUNK, D), lambda i, j: (j, 0)),
            pl.BlockSpec((1, D, PEER_CHUNK), lambda i, j: (j, 0, 0)),
            row_spec, row_spec, tok_spec, tok_spec,
            pl.BlockSpec((span, D), lambda i, j: (i, 0)),
            pl.BlockSpec((1, 1, D), lambda i, j: (i // steps_per_b, 0, 0)),
        ],
        out_specs=pl.BlockSpec((span, D), lambda i, j: (i, 0)),
        out_shape=jax.ShapeDtypeStruct((T, D), F32),
        scratch_shapes=[pltpu.VMEM((n_tiles, tm // PEER_PIECE_TOKENS, D, PEER_PIECE_TOKENS), F32),
                        pltpu.VMEM((tm // PEER_PIECE_TOKENS, PEER_CHUNK, PEER_PIECE_TOKENS), BF16),
                        pltpu.VMEM((n_tiles, PEER_HEADS, tm // LANES, PEER_N_KEYS, LANES), BF16),
                        pltpu.VMEM((n_tiles, PEER_HEADS, tm // LANES, PEER_N_KEYS, LANES), BF16)],
        compiler_params=_params("parallel", "arbitrary"),
        name="peer_mix",
    )(h_t, u_bf, v_t, c1, e1, r2, e2, x2, gate2.reshape(B, 1, D))


def _final_norm_kernel(x_ref, g_ref, o_ref):
    x = x_ref[...]
    ms = jnp.mean(x * x, axis=-1, keepdims=True)
    o_ref[...] = x * lax.rsqrt(ms + NORM_EPS) * g_ref[...]


def _final_norm(x2, g):
    T, D = x2.shape
    tm = min(1024, T)
    return pl.pallas_call(
        _final_norm_kernel,
        grid=(T // tm,),
        in_specs=[pl.BlockSpec((tm, D), lambda i: (i, 0)), pl.BlockSpec((1, D), lambda i: (0, 0))],
        out_specs=pl.BlockSpec((tm, D), lambda i: (i, 0)),
        out_shape=jax.ShapeDtypeStruct((T, D), F32),
        compiler_params=_params("parallel"),
        name="final_norm",
    )(x2, g.reshape(1, D))


def kernel(x, c, norm1_g, norm2_g, ada_w, ada_b, w_in, gate_b, diff_lambda, diff_subln_g,
           w_branch_sb, w_branch_diff, w_out, peer_wq, peer_subkeys, peer_u, peer_v, final_g):
    B, S, D = x.shape
    assert D == D_MODEL and S % LANES == 0
    x2 = x.reshape(B * S, D)
    mod = _modulation(c, ada_w, ada_b)
    rope = _rope_lane_tables(S)
    for l in range(DEPTH):
        lam_init = 0.8 - 0.6 * math.exp(-0.3 * l)
        shift1, scale1, gate1, shift2, scale2, gate2 = (mod[l, :, n] for n in range(6))
        proj = _input_projection(x2, norm1_g[l], scale1, shift1, w_in[l].astype(BF16), rope, B, S)
        y_sb = _sb_attention(proj, B, S)
        y_df = _diff_attention(proj, diff_lambda[l], diff_subln_g[l], lam_init, B, S)
        x2, h2 = _merge_project(y_sb, y_df, proj, gate_b[l], w_branch_sb[l].astype(BF16),
                                w_branch_diff[l].astype(BF16), w_out[l].astype(BF16), x2, gate1,
                                norm2_g[l], scale2, shift2, B, S)
        wq_t = peer_wq[l].T.astype(BF16).reshape(PEER_HEADS, PEER_QUERY_DIM, D)
        h_t, c1, e1, r2, e2 = _peer_select(h2, wq_t, peer_subkeys[l].astype(BF16))
        v_t = jnp.swapaxes(peer_v[l].astype(BF16).reshape(-1, PEER_CHUNK, D), 1, 2)
        x2 = _peer_mix(h_t, peer_u[l].astype(BF16), v_t,
                       c1, e1, r2, e2, x2, gate2, B, S)
    return _final_norm(x2, final_g).reshape(B, S, D)
```

```python
import functools
import math

import jax
import jax.numpy as jnp
from jax import lax
from jax.experimental import pallas as pl
from jax.experimental.pallas import tpu as pltpu

F32 = jnp.float32
BF16 = jnp.bfloat16

D_MODEL = 1024
DEPTH = 2
SB_HEADS = 8
SB_HEAD_DIM = 64
SB_WIDTH = SB_HEADS * SB_HEAD_DIM
DIFF_HEADS = 4
DIFF_QK_DIM = 64
DIFF_V_DIM = 2 * DIFF_QK_DIM
DIFF_QK_WIDTH = DIFF_HEADS * 2 * DIFF_QK_DIM
DIFF_V_WIDTH = DIFF_HEADS * DIFF_V_DIM
ROPE_THETA = 500000.0
ROT_DIM = DIFF_QK_DIM // 4
IN_COLS = 3 * SB_WIDTH + 2 * DIFF_QK_WIDTH + DIFF_V_WIDTH + 2 * D_MODEL
PEER_HEADS = 8
PEER_N_KEYS = 128
PEER_N_EXPERTS = PEER_N_KEYS * PEER_N_KEYS
PEER_TOPK = 16
PEER_QUERY_DIM = 256
PEER_CHUNK_KEYS = 8
PEER_CHUNK = PEER_CHUNK_KEYS * PEER_N_KEYS
PEER_TOKEN_TILE = 512
PEER_TILES_PER_STEP = 2
NORM_EPS = 1e-6

LANES = 128
VMEM_LIMIT = 56 * 1024 * 1024
NEG_INF = float("-inf")
LOG2_E = 1.4426950408889634

IN_TILE_N = 512
COL_SB_Q, COL_SB_K, COL_SB_V = 0, SB_WIDTH // LANES, 2 * SB_WIDTH // LANES
COL_D_Q = 3 * SB_WIDTH // LANES
COL_D_K = COL_D_Q + DIFF_QK_WIDTH // LANES
COL_D_V = COL_D_K + DIFF_QK_WIDTH // LANES
COL_G = (3 * SB_WIDTH + 2 * DIFF_QK_WIDTH + DIFF_V_WIDTH) // D_MODEL


def _params(*sem):
    return pltpu.CompilerParams(dimension_semantics=sem, vmem_limit_bytes=VMEM_LIMIT)


def _nt_dot(a, b):
    return lax.dot_general(a, b, (((1,), (1,)), ((), ())), preferred_element_type=F32)


def _mod_kernel(c_ref, w_ref, b_ref, o_ref):
    c = c_ref[...]
    c_act = c * jax.nn.sigmoid(c)
    o_ref[0] = jnp.dot(c_act, w_ref[0], preferred_element_type=F32,
                       precision=lax.Precision.HIGHEST) + b_ref[0]


def _modulation(c, ada_w, ada_b):
    B, D = c.shape
    rows = 8
    c_pad = jnp.pad(c, ((0, rows - B), (0, 0)))
    out = pl.pallas_call(
        _mod_kernel,
        grid=(DEPTH, 6),
        in_specs=[
            pl.BlockSpec((rows, D), lambda l, n: (0, 0)),
            pl.BlockSpec((1, D, D), lambda l, n: (l, 0, n)),
            pl.BlockSpec((1, 1, D), lambda l, n: (l, 0, n)),
        ],
        out_specs=pl.BlockSpec((1, rows, D), lambda l, n: (l, 0, n)),
        out_shape=jax.ShapeDtypeStruct((DEPTH, rows, 6 * D), F32),
        compiler_params=_params("arbitrary", "arbitrary"),
        name="adaln_mod",
    )(c_pad, ada_w, ada_b.reshape(DEPTH, 1, 6 * D))
    return out[:, :B].reshape(DEPTH, B, 6, D)


def _norm_mod(x, g, scale, shift):
    ms = jnp.mean(x * x, axis=-1, keepdims=True)
    return (x * lax.rsqrt(ms + NORM_EPS) * g) * (1.0 + scale) + shift


def _inproj_kernel(x_ref, g_ref, sc_ref, sh_ref, w_ref, ra_ref, rb_ref, rc_ref, o_ref):
    h = _norm_mod(x_ref[...], g_ref[...], sc_ref[0], sh_ref[0]).astype(BF16)
    half = ROT_DIM // 2
    for j in range(IN_COLS // IN_TILE_N):
        cols = slice(j * IN_TILE_N, (j + 1) * IN_TILE_N)
        r = jnp.dot(h, w_ref[:, cols], preferred_element_type=F32)
        first_lane_block = j * IN_TILE_N // LANES
        if first_lane_block in (COL_SB_Q, COL_D_Q):
            r = r * (LOG2_E / math.sqrt(SB_HEAD_DIM))
        if first_lane_block in (COL_D_Q, COL_D_K):
            ra, rb, rc = ra_ref[...], rb_ref[...], rc_ref[...]
            for blk in range(IN_TILE_N // LANES):
                t = r[:, blk * LANES:(blk + 1) * LANES]
                rot = t * ra + pltpu.roll(t, LANES - half, 1) * rb + pltpu.roll(t, half, 1) * rc
                o_ref[:, j * IN_TILE_N + blk * LANES:j * IN_TILE_N + (blk + 1) * LANES] = rot.astype(BF16)
        else:
            o_ref[:, cols] = r.astype(BF16)


def _rope_lane_tables(seq_len):
    pos = jnp.arange(seq_len, dtype=F32)
    inv_freq = 1.0 / (ROPE_THETA ** (jnp.arange(0, ROT_DIM, 2, dtype=F32) / ROT_DIM))
    ang = pos[:, None] * inv_freq[None, :]
    cos, sin = jnp.cos(ang), jnp.sin(ang)
    half = ROT_DIM // 2
    lane = jnp.arange(LANES) % DIFF_QK_DIM
    f = lane % half
    first, second = lane < half, (lane >= half) & (lane < ROT_DIM)
    ra = jnp.where((first | second)[None, :], cos[:, f], 1.0)
    rb = jnp.where(first[None, :], -sin[:, f], 0.0)
    rc = jnp.where(second[None, :], sin[:, f], 0.0)
    return ra, rb, rc


def _input_projection(x2, norm_g, scale, shift, w_in_bf, rope, B, S):
    T, D = x2.shape
    tm = min(512, S)
    tiles_per_b = S // tm
    ra, rb, rc = rope
    rope_spec = pl.BlockSpec((tm, LANES), lambda i: (i % tiles_per_b, 0))
    mod_spec = pl.BlockSpec((1, 1, D), lambda i: (i // tiles_per_b, 0, 0))
    assert COL_D_K - COL_D_Q == IN_TILE_N // LANES and COL_D_V - COL_D_K == IN_TILE_N // LANES
    return pl.pallas_call(
        _inproj_kernel,
        grid=(T // tm,),
        in_specs=[
            pl.BlockSpec((tm, D), lambda i: (i, 0)),
            pl.BlockSpec((1, D), lambda i: (0, 0)),
            mod_spec, mod_spec,
            pl.BlockSpec((D, IN_COLS), lambda i: (0, 0)),
            rope_spec, rope_spec, rope_spec,
        ],
        out_specs=pl.BlockSpec((tm, IN_COLS), lambda i: (i, 0)),
        out_shape=jax.ShapeDtypeStruct((T, IN_COLS), BF16),
        compiler_params=_params("parallel"),
        name="in_proj",
    )(x2, norm_g.reshape(1, D), scale.reshape(B, 1, D), shift.reshape(B, 1, D), w_in_bf, ra, rb, rc)


def _sb_kernel(q_ref, k_ref, v_ref, o_ref, acc_sc, cs_sc, *, tb, n_pairs):
    i = pl.program_id(2)
    n_heads = 2 * n_pairs
    lane = lax.broadcasted_iota(jnp.int32, (tb, LANES), 1)
    q_heads = []
    for p in range(n_pairs):
        q = q_ref[:, p * LANES:(p + 1) * LANES]
        zero = jnp.zeros_like(q)
        q_heads += [jnp.where(lane < SB_HEAD_DIM, q, zero), jnp.where(lane >= SB_HEAD_DIM, q, zero)]
    row = lax.broadcasted_iota(jnp.int32, (tb, tb), 0)
    col = lax.broadcasted_iota(jnp.int32, (tb, tb), 1)
    strictly_before = col < row
    suffix_ones = jnp.where(row >= col, 1.0, 0.0).astype(BF16)

    acc_sc[...] = jnp.zeros_like(acc_sc)
    cs_sc[...] = jnp.zeros_like(cs_sc)

    def block(j, masked):
        start = pl.multiple_of(j * tb, tb)
        kb = k_ref[pl.ds(start, tb), :]
        vb = v_ref[pl.ds(start, tb), :]
        ys = [_nt_dot(q_heads[h], kb[:, (h // 2) * LANES:(h // 2 + 1) * LANES]) for h in range(n_heads)]
        costs = []
        for y in ys:
            neg_abs = lax.bitcast_convert_type(
                lax.bitcast_convert_type(y, jnp.uint32) | jnp.uint32(0x80000000), F32)
            cost = jnp.maximum(y, 0.0) + jnp.log2(1.0 + jnp.exp2(neg_abs))
            if masked:
                cost = jnp.where(strictly_before, cost, 0.0)
            costs.append(cost)
        sums = [jnp.dot(cost.astype(BF16), suffix_ones, preferred_element_type=F32) for cost in costs]
        for h in range(n_heads):
            later = cs_sc[h]
            w = jnp.exp2(ys[h] - sums[h] - later)
            if masked:
                w = jnp.where(strictly_before, w, 0.0)
            acc_sc[h] += jnp.dot(w.astype(BF16), vb[:, (h // 2) * LANES:(h // 2 + 1) * LANES],
                                 preferred_element_type=F32)
            cs_sc[h] = later + sums[h][:, 0:1]

    block(i, True)

    def body(jj, carry):
        block(i - 1 - jj, False)
        return carry

    lax.fori_loop(0, i, body, 0)
    for p in range(n_pairs):
        o_ref[:, p * LANES:(p + 1) * LANES] = jnp.where(
            lane < SB_HEAD_DIM, acc_sc[2 * p], acc_sc[2 * p + 1]).astype(BF16)


SB_PAIRS_PER_STEP = 4


def _sb_attention(proj, B, S):
    T = proj.shape[0]
    tb = min(256, S)
    nq = S // tb
    n_pairs = SB_PAIRS_PER_STEP
    width = n_pairs * LANES
    groups = SB_WIDTH // width
    return pl.pallas_call(
        functools.partial(_sb_kernel, tb=tb, n_pairs=n_pairs),
        grid=(B, groups, nq),
        in_specs=[
            pl.BlockSpec((tb, width), lambda b, p, i: (b * nq + i, COL_SB_Q // n_pairs + p)),
            pl.BlockSpec((S, width), lambda b, p, i: (b, COL_SB_K // n_pairs + p)),
            pl.BlockSpec((S, width), lambda b, p, i: (b, COL_SB_V // n_pairs + p)),
        ],
        out_specs=pl.BlockSpec((tb, width), lambda b, p, i: (b * nq + i, p)),
        out_shape=jax.ShapeDtypeStruct((T, SB_WIDTH), BF16),
        scratch_shapes=[pltpu.VMEM((2 * n_pairs, tb, LANES), F32),
                        pltpu.VMEM((2 * n_pairs, tb, 1), F32)],
        compiler_params=_params("parallel", "parallel", "arbitrary"),
        name="sb_attention",
    )(proj, proj, proj)


def _diff_kernel(lp_ref, g_ref, q_ref, k_ref, v_ref, o_ref, m_sc, acc_sc, *, tq, tk, lam_init):
    i = pl.program_id(1)
    lane = lax.broadcasted_iota(jnp.int32, (tq, LANES), 1)
    q_maps = []
    for h in range(DIFF_HEADS):
        q = q_ref[:, h * LANES:(h + 1) * LANES]
        zero = jnp.zeros_like(q)
        q_maps += [jnp.where(lane < DIFF_QK_DIM, q, zero), jnp.where(lane >= DIFF_QK_DIM, q, zero)]
    n_chains = len(q_maps)
    m_sc[...] = jnp.full_like(m_sc, NEG_INF)
    acc_sc[...] = jnp.zeros_like(acc_sc)
    q_pos = i * tq + lax.broadcasted_iota(jnp.int32, (tq, tk), 0)
    k_off = lax.broadcasted_iota(jnp.int32, (tq, tk), 1)
    ones = jnp.ones((tk, LANES), BF16)

    def block(j, masked):
        start = pl.multiple_of(j * tk, tk)
        kb = k_ref[pl.ds(start, tk), :]
        vb = v_ref[pl.ds(start, tk), :]
        head_cols = lambda c: slice((c // 2) * LANES, (c // 2 + 1) * LANES)
        scores = [_nt_dot(q_maps[c], kb[:, head_cols(c)]) for c in range(n_chains)]
        if masked:
            visible = j * tk + k_off <= q_pos
            scores = [jnp.where(visible, s, NEG_INF) for s in scores]
        m_prev = [m_sc[c] for c in range(n_chains)]
        m_new = [jnp.maximum(m_prev[c], jnp.max(scores[c], axis=1, keepdims=True)) for c in range(n_chains)]
        probs = [jnp.exp2(scores[c] - jnp.concatenate([m_new[c]] * (tk // LANES), axis=1))
                 for c in range(n_chains)]
        for c in range(n_chains):
            alpha = jnp.exp2(m_prev[c] - m_new[c])
            v_ext = jnp.concatenate([vb[:, head_cols(c)], ones], axis=1)
            acc_sc[c] = (jnp.concatenate([alpha, alpha], axis=1) * acc_sc[c]
                         + jnp.dot(probs[c].astype(BF16), v_ext, preferred_element_type=F32))
            m_sc[c] = m_new[c]

    n_full = (i * tq) // tk

    def body(j, carry):
        block(j, False)
        return carry

    lax.fori_loop(0, n_full, body, 0)
    block(n_full, True)

    lp = lp_ref[...]
    lam = (jnp.exp(jnp.sum(lp[0:1] * lp[1:2], axis=1, keepdims=True))
           - jnp.exp(jnp.sum(lp[2:3] * lp[3:4], axis=1, keepdims=True)) + lam_init)
    for h in range(DIFF_HEADS):
        a1, a2 = acc_sc[2 * h], acc_sc[2 * h + 1]
        o = a1[:, :LANES] / a1[:, LANES:] - lam * (a2[:, :LANES] / a2[:, LANES:])
        ms = jnp.mean(o * o, axis=-1, keepdims=True)
        y = (o * lax.rsqrt(ms + NORM_EPS) * g_ref[...]) * (1.0 - lam_init)
        o_ref[:, h * LANES:(h + 1) * LANES] = y.astype(BF16)


def _diff_attention(proj, lam_params, subln_g, lam_init, B, S):
    T = proj.shape[0]
    tq = min(256, S)
    tk = min(512, S)
    nq = S // tq
    n_chains = 2 * DIFF_HEADS
    return pl.pallas_call(
        functools.partial(_diff_kernel, tq=tq, tk=tk, lam_init=lam_init),
        grid=(B, nq),
        in_specs=[
            pl.BlockSpec((4, DIFF_QK_DIM), lambda b, i: (0, 0)),
            pl.BlockSpec((1, DIFF_V_DIM), lambda b, i: (0, 0)),
            pl.BlockSpec((tq, DIFF_QK_WIDTH), lambda b, i: (b * nq + i, COL_D_Q * LANES // DIFF_QK_WIDTH)),
            pl.BlockSpec((S, DIFF_QK_WIDTH), lambda b, i: (b, COL_D_K * LANES // DIFF_QK_WIDTH)),
            pl.BlockSpec((S, DIFF_V_WIDTH), lambda b, i: (b, COL_D_V * LANES // DIFF_V_WIDTH)),
        ],
        out_specs=pl.BlockSpec((tq, DIFF_V_WIDTH), lambda b, i: (b * nq + i, 0)),
        out_shape=jax.ShapeDtypeStruct((T, DIFF_V_WIDTH), BF16),
        scratch_shapes=[pltpu.VMEM((n_chains, tq, LANES), F32),
                        pltpu.VMEM((n_chains, tq, 2 * LANES), F32)],
        compiler_params=_params("parallel", "arbitrary"),
        name="diff_attention",
    )(lam_params, subln_g.reshape(1, DIFF_V_DIM), proj, proj, proj)


def _merge_kernel(ysb_ref, ydf_ref, gs_ref, gd_ref, gb_ref, wbs_ref, wbd_ref, wo_ref,
                  x_ref, g1_ref, n2_ref, sc2_ref, sh2_ref, xo_ref, h2_ref):
    D = x_ref.shape[1]
    a = jnp.dot(ysb_ref[...], wbs_ref[...], preferred_element_type=F32)
    b = jnp.dot(ydf_ref[...], wbd_ref[...], preferred_element_type=F32)
    gb = gb_ref[...]
    g_sb = jax.nn.sigmoid(gs_ref[...].astype(F32) + gb[:, :D])
    g_df = jax.nn.sigmoid(gd_ref[...].astype(F32) + gb[:, D:])
    merged = g_sb * a + g_df * b
    y = jnp.dot(merged.astype(BF16), wo_ref[...], preferred_element_type=F32)
    xn = x_ref[...] + g1_ref[0] * y
    xo_ref[...] = xn
    h2_ref[...] = _norm_mod(xn, n2_ref[...], sc2_ref[0], sh2_ref[0]).astype(BF16)


def _merge_project(y_sb, y_df, proj, gate_b, wbs, wbd, wo, x2, gate1, norm2_g, scale2, shift2, B, S):
    T, D = x2.shape
    tm = min(512, S)
    tiles_per_b = S // tm
    mod_spec = pl.BlockSpec((1, 1, D), lambda i: (i // tiles_per_b, 0, 0))
    const = lambda shape: pl.BlockSpec(shape, lambda i: (0, 0))
    return pl.pallas_call(
        _merge_kernel,
        grid=(T // tm,),
        in_specs=[
            pl.BlockSpec((tm, SB_WIDTH), lambda i: (i, 0)),
            pl.BlockSpec((tm, DIFF_V_WIDTH), lambda i: (i, 0)),
            pl.BlockSpec((tm, D), lambda i: (i, COL_G)),
            pl.BlockSpec((tm, D), lambda i: (i, COL_G + 1)),
            const((1, 2 * D)),
            const((SB_WIDTH, D)), const((DIFF_V_WIDTH, D)), const((D, D)),
            pl.BlockSpec((tm, D), lambda i: (i, 0)),
            mod_spec,
            const((1, D)),
            mod_spec, mod_spec,
        ],
        out_specs=[pl.BlockSpec((tm, D), lambda i: (i, 0)), pl.BlockSpec((tm, D), lambda i: (i, 0))],
        out_shape=[jax.ShapeDtypeStruct((T, D), F32), jax.ShapeDtypeStruct((T, D), BF16)],
        compiler_params=_params("parallel"),
        name="merge_out_proj",
    )(y_sb, y_df, proj, proj, gate_b.reshape(1, 2 * D), wbs, wbd, wo, x2,
      gate1.reshape(B, 1, D), norm2_g.reshape(1, D), scale2.reshape(B, 1, D), shift2.reshape(B, 1, D))


def _column_max(a):
    m = a[0:8]
    for r in range(8, a.shape[0], 8):
        m = jnp.maximum(m, a[r:r + 8])
    return jnp.max(m, axis=0, keepdims=True)


def _top_values(scores, k, with_rank=False):
    vals = []
    cur = scores
    rank = jnp.full(scores.shape, float(k), F32)
    for r in range(k):
        m = _column_max(cur)
        vals.append(m)
        hit = cur == m
        if with_rank:
            rank = jnp.where(hit, float(r), rank)
        cur = jnp.where(hit, NEG_INF, cur)
    return (vals, rank) if with_rank else vals


def _candidate_sums(v1, v1_all, v2, v2_all, row_id):
    assert PEER_TOPK == 16
    lo2 = v2_all[0:8]
    hi1 = v1_all[8:16]
    shift = lambda x, n: pltpu.roll(x, n, 0)
    groups = [v1[0] + v2_all[0:8], v1[0] + v2_all[8:16], v1[0] + v2_all[16:24],
              v1[1] + lo2]
    groups.append(jnp.where(row_id < 5, v1[2] + lo2, v1[4] + shift(lo2, 5)))
    groups.append(jnp.where(row_id < 4, v1[3] + lo2,
                            jnp.where(row_id < 6, v1[5] + shift(lo2, 4),
                                      v1[6] + shift(lo2, 6))))
    tail = shift(hi1, 2) + v2[0]
    groups.append(jnp.where(row_id < 2, v1[7] + lo2, tail))
    groups.append(jnp.where(row_id < 2, tail,
                            jnp.where(row_id == 2, v1[16] + v2[0], NEG_INF)))
    return groups


def _peer_select_kernel(h_ref, wq_ref, sk_ref, ht_ref, c1_ref, e1_ref, r2_ref, e2_ref, *, tm):
    hb = h_ref[...]
    ht_ref[0] = hb.T
    K = PEER_TOPK + 1
    row_id = lax.broadcasted_iota(jnp.int32, (8, LANES), 0)
    pad_rows = jnp.full((24 - K, LANES), NEG_INF, F32)

    def head(h, carry):
        q_t = _nt_dot(wq_ref[h], hb).astype(BF16)
        half = PEER_QUERY_DIM // 2
        sc = [jnp.dot(sk_ref[h, c], q_t[c * half:(c + 1) * half], preferred_element_type=F32)
              for c in range(2)]
        for lc in range(tm // LANES):
            ls = slice(lc * LANES, (lc + 1) * LANES)
            s1, s2 = sc[0][:, ls], sc[1][:, ls]
            v1 = _top_values(s1, K)
            v2, rank2 = _top_values(s2, K, with_rank=True)
            v2_all = jnp.concatenate(v2 + [pad_rows], axis=0)
            v1_all = jnp.concatenate(v1 + [pad_rows], axis=0)
            cands = _candidate_sums(v1, v1_all, v2, v2_all, row_id)
            best = cands[0][0:1]
            z_sum = jnp.zeros_like(best)
            last = best
            for r in range(K):
                stacked = cands[0]
                for ca in cands[1:]:
                    stacked = jnp.maximum(stacked, ca)
                m = jnp.max(stacked, axis=0, keepdims=True)
                if r < PEER_TOPK:
                    z_sum = z_sum + jnp.exp(m - best)
                    last = m
                    cands = [jnp.where(ca == m, NEG_INF, ca) for ca in cands]
            tau = 0.5 * (last + m)
            count = jnp.zeros_like(s1)
            for b in range(PEER_TOPK):
                count = count + jnp.where(s1 >= tau - v2[b], 1.0, 0.0)
            half_inv_z = 0.5 / z_sum
            chunked = (PEER_N_KEYS // PEER_CHUNK_KEYS, PEER_CHUNK_KEYS, LANES)
            c1_ref[:, 0, h, :, ls] = count.reshape(chunked)
            e1_ref[:, 0, h, :, ls] = (jnp.exp(s1 - v1[0]) * half_inv_z).reshape(chunked)
            r2_ref[0, h, :, ls] = rank2.astype(BF16)
            e2_ref[0, h, :, ls] = jnp.exp(s2 - v2[0]).astype(BF16)
        return carry

    lax.fori_loop(0, PEER_HEADS, head, 0)


def _peer_select(h2, wq_t, subkeys_bf):
    T, D = h2.shape
    tm = PEER_TOKEN_TILE
    n_tiles = T // tm
    tok_spec = pl.BlockSpec((1, PEER_HEADS, PEER_N_KEYS, tm), lambda i: (i, 0, 0, 0))
    tok_shape = jax.ShapeDtypeStruct((n_tiles, PEER_HEADS, PEER_N_KEYS, tm), BF16)
    n_chunks = PEER_N_KEYS // PEER_CHUNK_KEYS
    row_spec = pl.BlockSpec((n_chunks, 1, PEER_HEADS, PEER_CHUNK_KEYS, tm), lambda i: (0, i, 0, 0, 0))
    row_shape = jax.ShapeDtypeStruct((n_chunks, n_tiles, PEER_HEADS, PEER_CHUNK_KEYS, tm), F32)
    return pl.pallas_call(
        functools.partial(_peer_select_kernel, tm=tm),
        grid=(n_tiles,),
        in_specs=[
            pl.BlockSpec((tm, D), lambda i: (i, 0)),
            pl.BlockSpec((PEER_HEADS, PEER_QUERY_DIM, D), lambda i: (0, 0, 0)),
            pl.BlockSpec((PEER_HEADS, 2, PEER_N_KEYS, PEER_QUERY_DIM // 2), lambda i: (0, 0, 0, 0)),
        ],
        out_specs=[
            pl.BlockSpec((1, D, tm), lambda i: (i, 0, 0)),
            row_spec, row_spec, tok_spec, tok_spec,
        ],
        out_shape=[
            jax.ShapeDtypeStruct((n_tiles, D, tm), BF16),
            row_shape, row_shape, tok_shape, tok_shape,
        ],
        compiler_params=_params("parallel"),
        name="peer_select",
    )(h2, wq_t, subkeys_bf)


PEER_ROW_BLOCK_KEYS = 4
PEER_PIECE_TOKENS = 2 * LANES


def _peer_mix_kernel(ht_ref, u_ref, vt_ref, c1_ref, e1_ref, r2_ref, e2_ref, x_ref, g2_ref,
                     xo_ref, acc_sc, act_sc, r2_sc, e2_sc, *, tm, n_tiles):
    j = pl.program_id(1)

    @pl.when(j == 0)
    def _():
        acc_sc[...] = jnp.zeros_like(acc_sc)
        for lc in range(tm // LANES):
            r2_sc[:, :, lc] = r2_ref[:, :, :, lc * LANES:(lc + 1) * LANES].astype(F32)
            e2_sc[:, :, lc] = e2_ref[:, :, :, lc * LANES:(lc + 1) * LANES].astype(F32)

    rb = PEER_ROW_BLOCK_KEYS * PEER_N_KEYS
    tw = PEER_PIECE_TOKENS
    n_rb = PEER_CHUNK // rb
    pieces = [(r, c) for c in range(tm // tw) for r in range(n_rb)]
    zero = jnp.zeros((PEER_N_KEYS, LANES), BF16)

    def tile(t, carry):
        def scores(p):
            r, c = pieces[p]
            return jnp.dot(u_ref[r * rb:(r + 1) * rb, :], ht_ref[t, :, c * tw:(c + 1) * tw],
                           preferred_element_type=F32)

        def gates(p, a_blk, k):
            r, c = pieces[p]
            ii = r * PEER_ROW_BLOCK_KEYS + k
            for lc in range(tw // LANES):
                chunk = c * (tw // LANES) + lc
                ls = slice(chunk * LANES, (chunk + 1) * LANES)
                gate = None
                for h in range(PEER_HEADS):
                    count = c1_ref[0, t, h, ii:ii + 1, ls]
                    e1 = e1_ref[0, t, h, ii:ii + 1, ls]
                    term = jnp.where(r2_sc[t, h, chunk] < count, e1 * e2_sc[t, h, chunk], 0.0)
                    gate = term if gate is None else gate + term
                a = a_blk[k * PEER_N_KEYS:(k + 1) * PEER_N_KEYS, lc * LANES:(lc + 1) * LANES]
                gelu2 = a * (1.0 + lax.erf(a * (1.0 / math.sqrt(2.0))))
                act_sc[c, ii * PEER_N_KEYS:(ii + 1) * PEER_N_KEYS, lc * LANES:(lc + 1) * LANES] = (
                    gelu2 * gate).astype(BF16)

        def mix(c):
            acc_sc[t, c] += jnp.dot(vt_ref[0], act_sc[c], preferred_element_type=F32)

        a_next = scores(0)
        for p in range(len(pieces)):
            r, c = pieces[p]
            a_cur = a_next
            if p + 1 < len(pieces):
                a_next = scores(p + 1)
            for k in range(PEER_ROW_BLOCK_KEYS):
                if k == PEER_ROW_BLOCK_KEYS // 2 and r == 0 and c > 0:
                    mix(c - 1)
                gates(p, a_cur, k)
        mix(tm // tw - 1)
        return carry

    lax.fori_loop(0, n_tiles, tile, 0)

    @pl.when(j == pl.num_programs(1) - 1)
    def _():
        for t in range(n_tiles):
            rows = slice(t * tm, (t + 1) * tm)
            mixed_t = jnp.concatenate([acc_sc[t, c] for c in range(tm // tw)], axis=1)
            xo_ref[rows, :] = x_ref[rows, :] + g2_ref[0] * mixed_t.T


def _peer_mix(h_t, u_bf, v_t, c1, e1, r2, e2, x2, gate2, B, S):
    T, D = x2.shape
    tm = PEER_TOKEN_TILE
    n_tiles = min(PEER_TILES_PER_STEP, S // tm)
    span = n_tiles * tm
    steps_per_b = S // span
    tok_spec = pl.BlockSpec((n_tiles, PEER_HEADS, PEER_N_KEYS, tm), lambda i, j: (i, 0, 0, 0))
    row_spec = pl.BlockSpec((1, n_tiles, PEER_HEADS, PEER_CHUNK_KEYS, tm), lambda i, j: (j, i, 0, 0, 0))
    return pl.pallas_call(
        functools.partial(_peer_mix_kernel, tm=tm, n_tiles=n_tiles),
        grid=(T // span, PEER_N_EXPERTS // PEER_CHUNK),
        in_specs=[
            pl.BlockSpec((n_tiles, D, tm), lambda i, j: (i, 0, 0)),
            pl.BlockSpec((PEER_CHUNK, D), lambda i, j: (j, 0)),
            pl.BlockSpec((1, D, PEER_CHUNK), lambda i, j: (j, 0, 0)),
            row_spec, row_spec, tok_spec, tok_spec,
            pl.BlockSpec((span, D), lambda i, j: (i, 0)),
            pl.BlockSpec((1, 1, D), lambda i, j: (i // steps_per_b, 0, 0)),
        ],
        out_specs=pl.BlockSpec((span, D), lambda i, j: (i, 0)),
        out_shape=jax.ShapeDtypeStruct((T, D), F32),
        scratch_shapes=[pltpu.VMEM((n_tiles, tm // PEER_PIECE_TOKENS, D, PEER_PIECE_TOKENS), F32),
                        pltpu.VMEM((tm // PEER_PIECE_TOKENS, PEER_CHUNK, PEER_PIECE_TOKENS), BF16),
                        pltpu.VMEM((n_tiles, PEER_HEADS, tm // LANES, PEER_N_KEYS, LANES), F32),
                        pltpu.VMEM((n_tiles, PEER_HEADS, tm // LANES, PEER_N_KEYS, LANES), F32)],
        compiler_params=_params("parallel", "arbitrary"),
        name="peer_mix",
    )(h_t, u_bf, v_t, c1, e1, r2, e2, x2, gate2.reshape(B, 1, D))


def _final_norm_kernel(x_ref, g_ref, o_ref):
    x = x_ref[...]
    ms = jnp.mean(x * x, axis=-1, keepdims=True)
    o_ref[...] = x * lax.rsqrt(ms + NORM_EPS) * g_ref[...]


def _final_norm(x2, g):
    T, D = x2.shape
    tm = min(1024, T)
    return pl.pallas_call(
        _final_norm_kernel,
        grid=(T // tm,),
        in_specs=[pl.BlockSpec((tm, D), lambda i: (i, 0)), pl.BlockSpec((1, D), lambda i: (0, 0))],
        out_specs=pl.BlockSpec((tm, D), lambda i: (i, 0)),
        out_shape=jax.ShapeDtypeStruct((T, D), F32),
        compiler_params=_params("parallel"),
        name="final_norm",
    )(x2, g.reshape(1, D))


def kernel(x, c, norm1_g, norm2_g, ada_w, ada_b, w_in, gate_b, diff_lambda, diff_subln_g,
           w_branch_sb, w_branch_diff, w_out, peer_wq, peer_subkeys, peer_u, peer_v, final_g):
    B, S, D = x.shape
    assert D == D_MODEL and S % LANES == 0
    x2 = x.reshape(B * S, D)
    mod = _modulation(c, ada_w, ada_b)
    rope = _rope_lane_tables(S)
    for l in range(DEPTH):
        lam_init = 0.8 - 0.6 * math.exp(-0.3 * l)
        shift1, scale1, gate1, shift2, scale2, gate2 = (mod[l, :, n] for n in range(6))
        proj = _input_projection(x2, norm1_g[l], scale1, shift1, w_in[l].astype(BF16), rope, B, S)
        y_sb = _sb_attention(proj, B, S)
        y_df = _diff_attention(proj, diff_lambda[l], diff_subln_g[l], lam_init, B, S)
        x2, h2 = _merge_project(y_sb, y_df, proj, gate_b[l], w_branch_sb[l].astype(BF16),
                                w_branch_diff[l].astype(BF16), w_out[l].astype(BF16), x2, gate1,
                                norm2_g[l], scale2, shift2, B, S)
        wq_t = peer_wq[l].T.astype(BF16).reshape(PEER_HEADS, PEER_QUERY_DIM, D)
        h_t, c1, e1, r2, e2 = _peer_select(h2, wq_t, peer_subkeys[l].astype(BF16))
        v_t = jnp.swapaxes(peer_v[l].astype(BF16).reshape(-1, PEER_CHUNK, D), 1, 2)
        x2 = _peer_mix(h_t, peer_u[l].astype(BF16), v_t,
                       c1, e1, r2, e2, x2, gate2, B, S)
    return _final_norm(x2, final_g).reshape(B, S, D)
```

```python
import functools
import math

import jax
import jax.numpy as jnp
from jax import lax
from jax.experimental import pallas as pl
from jax.experimental.pallas import tpu as pltpu

F32 = jnp.float32
BF16 = jnp.bfloat16

D_MODEL = 1024
DEPTH = 2
SB_HEADS = 8
SB_HEAD_DIM = 64
SB_WIDTH = SB_HEADS * SB_HEAD_DIM
DIFF_HEADS = 4
DIFF_QK_DIM = 64
DIFF_V_DIM = 2 * DIFF_QK_DIM
DIFF_QK_WIDTH = DIFF_HEADS * 2 * DIFF_QK_DIM
DIFF_V_WIDTH = DIFF_HEADS * DIFF_V_DIM
ROPE_THETA = 500000.0
ROT_DIM = DIFF_QK_DIM // 4
IN_COLS = 3 * SB_WIDTH + 2 * DIFF_QK_WIDTH + DIFF_V_WIDTH + 2 * D_MODEL
PEER_HEADS = 8
PEER_N_KEYS = 128
PEER_N_EXPERTS = PEER_N_KEYS * PEER_N_KEYS
PEER_TOPK = 16
PEER_QUERY_DIM = 256
PEER_CHUNK_KEYS = 8
PEER_CHUNK = PEER_CHUNK_KEYS * PEER_N_KEYS
PEER_TOKEN_TILE = 512
PEER_TILES_PER_STEP = 2
NORM_EPS = 1e-6

LANES = 128
VMEM_LIMIT = 56 * 1024 * 1024
NEG_INF = float("-inf")
LOG2_E = 1.4426950408889634

IN_TILE_N = 512
COL_SB_Q, COL_SB_K, COL_SB_V = 0, SB_WIDTH // LANES, 2 * SB_WIDTH // LANES
COL_D_Q = 3 * SB_WIDTH // LANES
COL_D_K = COL_D_Q + DIFF_QK_WIDTH // LANES
COL_D_V = COL_D_K + DIFF_QK_WIDTH // LANES
COL_G = (3 * SB_WIDTH + 2 * DIFF_QK_WIDTH + DIFF_V_WIDTH) // D_MODEL


def _params(*sem):
    return pltpu.CompilerParams(dimension_semantics=sem, vmem_limit_bytes=VMEM_LIMIT)


def _nt_dot(a, b):
    return lax.dot_general(a, b, (((1,), (1,)), ((), ())), preferred_element_type=F32)


def _mod_kernel(c_ref, w_ref, b_ref, o_ref):
    c = c_ref[...]
    c_act = c * jax.nn.sigmoid(c)
    o_ref[0] = jnp.dot(c_act, w_ref[0], preferred_element_type=F32,
                       precision=lax.Precision.HIGHEST) + b_ref[0]


def _modulation(c, ada_w, ada_b):
    B, D = c.shape
    rows = 8
    c_pad = jnp.pad(c, ((0, rows - B), (0, 0)))
    out = pl.pallas_call(
        _mod_kernel,
        grid=(DEPTH, 6),
        in_specs=[
            pl.BlockSpec((rows, D), lambda l, n: (0, 0)),
            pl.BlockSpec((1, D, D), lambda l, n: (l, 0, n)),
            pl.BlockSpec((1, 1, D), lambda l, n: (l, 0, n)),
        ],
        out_specs=pl.BlockSpec((1, rows, D), lambda l, n: (l, 0, n)),
        out_shape=jax.ShapeDtypeStruct((DEPTH, rows, 6 * D), F32),
        compiler_params=_params("arbitrary", "arbitrary"),
        name="adaln_mod",
    )(c_pad, ada_w, ada_b.reshape(DEPTH, 1, 6 * D))
    return out[:, :B].reshape(DEPTH, B, 6, D)


def _norm_mod(x, g, scale, shift):
    ms = jnp.mean(x * x, axis=-1, keepdims=True)
    return (x * lax.rsqrt(ms + NORM_EPS) * g) * (1.0 + scale) + shift


def _inproj_kernel(x_ref, g_ref, sc_ref, sh_ref, w_ref, ra_ref, rb_ref, rc_ref, o_ref):
    h = _norm_mod(x_ref[...], g_ref[...], sc_ref[0], sh_ref[0]).astype(BF16)
    half = ROT_DIM // 2
    for j in range(IN_COLS // IN_TILE_N):
        cols = slice(j * IN_TILE_N, (j + 1) * IN_TILE_N)
        r = jnp.dot(h, w_ref[:, cols], preferred_element_type=F32)
        first_lane_block = j * IN_TILE_N // LANES
        if first_lane_block in (COL_SB_Q, COL_D_Q):
            r = r * (LOG2_E / math.sqrt(SB_HEAD_DIM))
        if first_lane_block in (COL_D_Q, COL_D_K):
            ra, rb, rc = ra_ref[...], rb_ref[...], rc_ref[...]
            for blk in range(IN_TILE_N // LANES):
                t = r[:, blk * LANES:(blk + 1) * LANES]
                rot = t * ra + pltpu.roll(t, LANES - half, 1) * rb + pltpu.roll(t, half, 1) * rc
                o_ref[:, j * IN_TILE_N + blk * LANES:j * IN_TILE_N + (blk + 1) * LANES] = rot.astype(BF16)
        else:
            o_ref[:, cols] = r.astype(BF16)


def _rope_lane_tables(seq_len):
    pos = jnp.arange(seq_len, dtype=F32)
    inv_freq = 1.0 / (ROPE_THETA ** (jnp.arange(0, ROT_DIM, 2, dtype=F32) / ROT_DIM))
    ang = pos[:, None] * inv_freq[None, :]
    cos, sin = jnp.cos(ang), jnp.sin(ang)
    half = ROT_DIM // 2
    lane = jnp.arange(LANES) % DIFF_QK_DIM
    f = lane % half
    first, second = lane < half, (lane >= half) & (lane < ROT_DIM)
    ra = jnp.where((first | second)[None, :], cos[:, f], 1.0)
    rb = jnp.where(first[None, :], -sin[:, f], 0.0)
    rc = jnp.where(second[None, :], sin[:, f], 0.0)
    return ra, rb, rc


def _input_projection(x2, norm_g, scale, shift, w_in_bf, rope, B, S):
    T, D = x2.shape
    tm = min(512, S)
    tiles_per_b = S // tm
    ra, rb, rc = rope
    rope_spec = pl.BlockSpec((tm, LANES), lambda i: (i % tiles_per_b, 0))
    mod_spec = pl.BlockSpec((1, 1, D), lambda i: (i // tiles_per_b, 0, 0))
    assert COL_D_K - COL_D_Q == IN_TILE_N // LANES and COL_D_V - COL_D_K == IN_TILE_N // LANES
    return pl.pallas_call(
        _inproj_kernel,
        grid=(T // tm,),
        in_specs=[
            pl.BlockSpec((tm, D), lambda i: (i, 0)),
            pl.BlockSpec((1, D), lambda i: (0, 0)),
            mod_spec, mod_spec,
            pl.BlockSpec((D, IN_COLS), lambda i: (0, 0)),
            rope_spec, rope_spec, rope_spec,
        ],
        out_specs=pl.BlockSpec((tm, IN_COLS), lambda i: (i, 0)),
        out_shape=jax.ShapeDtypeStruct((T, IN_COLS), BF16),
        compiler_params=_params("parallel"),
        name="in_proj",
    )(x2, norm_g.reshape(1, D), scale.reshape(B, 1, D), shift.reshape(B, 1, D), w_in_bf, ra, rb, rc)


def _sb_kernel(q_ref, k_ref, v_ref, o_ref, acc_sc, cs_sc, *, tb, n_pairs):
    i = pl.program_id(2)
    n_heads = 2 * n_pairs
    lane = lax.broadcasted_iota(jnp.int32, (tb, LANES), 1)
    q_heads = []
    for p in range(n_pairs):
        q = q_ref[:, p * LANES:(p + 1) * LANES]
        zero = jnp.zeros_like(q)
        q_heads += [jnp.where(lane < SB_HEAD_DIM, q, zero), jnp.where(lane >= SB_HEAD_DIM, q, zero)]
    row = lax.broadcasted_iota(jnp.int32, (tb, tb), 0)
    col = lax.broadcasted_iota(jnp.int32, (tb, tb), 1)
    strictly_before = col < row
    suffix_ones = jnp.where(row >= col, 1.0, 0.0).astype(BF16)

    acc_sc[...] = jnp.zeros_like(acc_sc)
    cs_sc[...] = jnp.zeros_like(cs_sc)

    def block(j, masked):
        start = pl.multiple_of(j * tb, tb)
        kb = k_ref[pl.ds(start, tb), :]
        vb = v_ref[pl.ds(start, tb), :]
        ys = [_nt_dot(q_heads[h], kb[:, (h // 2) * LANES:(h // 2 + 1) * LANES]) for h in range(n_heads)]
        costs = []
        for y in ys:
            neg_abs = lax.bitcast_convert_type(
                lax.bitcast_convert_type(y, jnp.uint32) | jnp.uint32(0x80000000), F32)
            cost = jnp.maximum(y, 0.0) + jnp.log2(1.0 + jnp.exp2(neg_abs))
            if masked:
                cost = jnp.where(strictly_before, cost, 0.0)
            costs.append(cost)
        sums = [jnp.dot(cost.astype(BF16), suffix_ones, preferred_element_type=F32) for cost in costs]
        for h in range(n_heads):
            later = cs_sc[h]
            w = jnp.exp2(ys[h] - sums[h] - later)
            if masked:
                w = jnp.where(strictly_before, w, 0.0)
            acc_sc[h] += jnp.dot(w.astype(BF16), vb[:, (h // 2) * LANES:(h // 2 + 1) * LANES],
                                 preferred_element_type=F32)
            cs_sc[h] = later + sums[h][:, 0:1]

    block(i, True)

    def body(jj, carry):
        block(i - 1 - jj, False)
        return carry

    lax.fori_loop(0, i, body, 0)
    for p in range(n_pairs):
        o_ref[:, p * LANES:(p + 1) * LANES] = jnp.where(
            lane < SB_HEAD_DIM, acc_sc[2 * p], acc_sc[2 * p + 1]).astype(BF16)


SB_PAIRS_PER_STEP = 4


def _sb_attention(proj, B, S):
    T = proj.shape[0]
    tb = min(256, S)
    nq = S // tb
    n_pairs = SB_PAIRS_PER_STEP
    width = n_pairs * LANES
    groups = SB_WIDTH // width
    return pl.pallas_call(
        functools.partial(_sb_kernel, tb=tb, n_pairs=n_pairs),
        grid=(B, groups, nq),
        in_specs=[
            pl.BlockSpec((tb, width), lambda b, p, i: (b * nq + i, COL_SB_Q // n_pairs + p)),
            pl.BlockSpec((S, width), lambda b, p, i: (b, COL_SB_K // n_pairs + p)),
            pl.BlockSpec((S, width), lambda b, p, i: (b, COL_SB_V // n_pairs + p)),
        ],
        out_specs=pl.BlockSpec((tb, width), lambda b, p, i: (b * nq + i, p)),
        out_shape=jax.ShapeDtypeStruct((T, SB_WIDTH), BF16),
        scratch_shapes=[pltpu.VMEM((2 * n_pairs, tb, LANES), F32),
                        pltpu.VMEM((2 * n_pairs, tb, 1), F32)],
        compiler_params=_params("parallel", "parallel", "arbitrary"),
        name="sb_attention",
    )(proj, proj, proj)


def _diff_kernel(lp_ref, g_ref, q_ref, k_ref, v_ref, o_ref, m_sc, acc_sc, *, tq, tk, lam_init):
    i = pl.program_id(1)
    lane = lax.broadcasted_iota(jnp.int32, (tq, LANES), 1)
    q_maps = []
    for h in range(DIFF_HEADS):
        q = q_ref[:, h * LANES:(h + 1) * LANES]
        zero = jnp.zeros_like(q)
        q_maps += [jnp.where(lane < DIFF_QK_DIM, q, zero), jnp.where(lane >= DIFF_QK_DIM, q, zero)]
    n_chains = len(q_maps)
    m_sc[...] = jnp.full_like(m_sc, NEG_INF)
    acc_sc[...] = jnp.zeros_like(acc_sc)
    q_pos = i * tq + lax.broadcasted_iota(jnp.int32, (tq, tk), 0)
    k_off = lax.broadcasted_iota(jnp.int32, (tq, tk), 1)
    ones = jnp.ones((tk, LANES), BF16)

    def block(j, masked):
        start = pl.multiple_of(j * tk, tk)
        kb = k_ref[pl.ds(start, tk), :]
        vb = v_ref[pl.ds(start, tk), :]
        head_cols = lambda c: slice((c // 2) * LANES, (c // 2 + 1) * LANES)
        scores = [_nt_dot(q_maps[c], kb[:, head_cols(c)]) for c in range(n_chains)]
        if masked:
            visible = j * tk + k_off <= q_pos
            scores = [jnp.where(visible, s, NEG_INF) for s in scores]
        m_prev = [m_sc[c] for c in range(n_chains)]
        m_new = [jnp.maximum(m_prev[c], jnp.max(scores[c], axis=1, keepdims=True)) for c in range(n_chains)]
        probs = [jnp.exp2(scores[c] - jnp.concatenate([m_new[c]] * (tk // LANES), axis=1))
                 for c in range(n_chains)]
        for c in range(n_chains):
            alpha = jnp.exp2(m_prev[c] - m_new[c])
            v_ext = jnp.concatenate([vb[:, head_cols(c)], ones], axis=1)
            acc_sc[c] = (jnp.concatenate([alpha, alpha], axis=1) * acc_sc[c]
                         + jnp.dot(probs[c].astype(BF16), v_ext, preferred_element_type=F32))
            m_sc[c] = m_new[c]

    n_full = (i * tq) // tk

    def body(j, carry):
        block(j, False)
        return carry

    lax.fori_loop(0, n_full, body, 0)
    block(n_full, True)

    lp = lp_ref[...]
    lam = (jnp.exp(jnp.sum(lp[0:1] * lp[1:2], axis=1, keepdims=True))
           - jnp.exp(jnp.sum(lp[2:3] * lp[3:4], axis=1, keepdims=True)) + lam_init)
    for h in range(DIFF_HEADS):
        a1, a2 = acc_sc[2 * h], acc_sc[2 * h + 1]
        o = a1[:, :LANES] / a1[:, LANES:] - lam * (a2[:, :LANES] / a2[:, LANES:])
        ms = jnp.mean(o * o, axis=-1, keepdims=True)
        y = (o * lax.rsqrt(ms + NORM_EPS) * g_ref[...]) * (1.0 - lam_init)
        o_ref[:, h * LANES:(h + 1) * LANES] = y.astype(BF16)


def _diff_attention(proj, lam_params, subln_g, lam_init, B, S):
    T = proj.shape[0]
    tq = min(256, S)
    tk = min(512, S)
    nq = S // tq
    n_chains = 2 * DIFF_HEADS
    return pl.pallas_call(
        functools.partial(_diff_kernel, tq=tq, tk=tk, lam_init=lam_init),
        grid=(B, nq),
        in_specs=[
            pl.BlockSpec((4, DIFF_QK_DIM), lambda b, i: (0, 0)),
            pl.BlockSpec((1, DIFF_V_DIM), lambda b, i: (0, 0)),
            pl.BlockSpec((tq, DIFF_QK_WIDTH), lambda b, i: (b * nq + i, COL_D_Q * LANES // DIFF_QK_WIDTH)),
            pl.BlockSpec((S, DIFF_QK_WIDTH), lambda b, i: (b, COL_D_K * LANES // DIFF_QK_WIDTH)),
            pl.BlockSpec((S, DIFF_V_WIDTH), lambda b, i: (b, COL_D_V * LANES // DIFF_V_WIDTH)),
        ],
        out_specs=pl.BlockSpec((tq, DIFF_V_WIDTH), lambda b, i: (b * nq + i, 0)),
        out_shape=jax.ShapeDtypeStruct((T, DIFF_V_WIDTH), BF16),
        scratch_shapes=[pltpu.VMEM((n_chains, tq, LANES), F32),
                        pltpu.VMEM((n_chains, tq, 2 * LANES), F32)],
        compiler_params=_params("parallel", "arbitrary"),
        name="diff_attention",
    )(lam_params, subln_g.reshape(1, DIFF_V_DIM), proj, proj, proj)


def _merge_kernel(ysb_ref, ydf_ref, gs_ref, gd_ref, gb_ref, wbs_ref, wbd_ref, wo_ref,
                  x_ref, g1_ref, n2_ref, sc2_ref, sh2_ref, xo_ref, h2_ref):
    D = x_ref.shape[1]
    a = jnp.dot(ysb_ref[...], wbs_ref[...], preferred_element_type=F32)
    b = jnp.dot(ydf_ref[...], wbd_ref[...], preferred_element_type=F32)
    gb = gb_ref[...]
    g_sb = jax.nn.sigmoid(gs_ref[...].astype(F32) + gb[:, :D])
    g_df = jax.nn.sigmoid(gd_ref[...].astype(F32) + gb[:, D:])
    merged = g_sb * a + g_df * b
    y = jnp.dot(merged.astype(BF16), wo_ref[...], preferred_element_type=F32)
    xn = x_ref[...] + g1_ref[0] * y
    xo_ref[...] = xn
    h2_ref[...] = _norm_mod(xn, n2_ref[...], sc2_ref[0], sh2_ref[0]).astype(BF16)


def _merge_project(y_sb, y_df, proj, gate_b, wbs, wbd, wo, x2, gate1, norm2_g, scale2, shift2, B, S):
    T, D = x2.shape
    tm = min(512, S)
    tiles_per_b = S // tm
    mod_spec = pl.BlockSpec((1, 1, D), lambda i: (i // tiles_per_b, 0, 0))
    const = lambda shape: pl.BlockSpec(shape, lambda i: (0, 0))
    return pl.pallas_call(
        _merge_kernel,
        grid=(T // tm,),
        in_specs=[
            pl.BlockSpec((tm, SB_WIDTH), lambda i: (i, 0)),
            pl.BlockSpec((tm, DIFF_V_WIDTH), lambda i: (i, 0)),
            pl.BlockSpec((tm, D), lambda i: (i, COL_G)),
            pl.BlockSpec((tm, D), lambda i: (i, COL_G + 1)),
            const((1, 2 * D)),
            const((SB_WIDTH, D)), const((DIFF_V_WIDTH, D)), const((D, D)),
            pl.BlockSpec((tm, D), lambda i: (i, 0)),
            mod_spec,
            const((1, D)),
            mod_spec, mod_spec,
        ],
        out_specs=[pl.BlockSpec((tm, D), lambda i: (i, 0)), pl.BlockSpec((tm, D), lambda i: (i, 0))],
        out_shape=[jax.ShapeDtypeStruct((T, D), F32), jax.ShapeDtypeStruct((T, D), BF16)],
        compiler_params=_params("parallel"),
        name="merge_out_proj",
    )(y_sb, y_df, proj, proj, gate_b.reshape(1, 2 * D), wbs, wbd, wo, x2,
      gate1.reshape(B, 1, D), norm2_g.reshape(1, D), scale2.reshape(B, 1, D), shift2.reshape(B, 1, D))


def _sort16_network():
    def merge(lo, hi, r):
        step = r * 2
        if step < hi - lo:
            yield from merge(lo, hi, step)
            yield from merge(lo + r, hi, step)
            yield from [(i, i + r) for i in range(lo + r, hi - r, step)]
        else:
            yield (lo, lo + r)

    def sort(lo, hi):
        if hi - lo >= 1:
            mid = lo + (hi - lo) // 2
            yield from sort(lo, mid)
            yield from sort(mid + 1, hi)
            yield from merge(lo, hi, 1)

    return list(sort(0, 15))


def _top_values(scores, k):
    groups = scores.shape[0] // 8
    assert groups == 16 and k <= groups + 1
    v = [scores[8 * g:8 * (g + 1)] for g in range(groups)]
    for a, b in _sort16_network():
        v[a], v[b] = jnp.maximum(v[a], v[b]), jnp.minimum(v[a], v[b])
    v.append(jnp.full_like(v[0], NEG_INF))
    vals = []
    for r in range(k):
        m = jnp.max(v[0], axis=0, keepdims=True)
        vals.append(m)
        still_needed = k - 1 - r
        if still_needed:
            hit = v[0] == m
            for d in range(min(still_needed, groups)):
                v[d] = jnp.where(hit, v[d + 1], v[d])
    return vals


def _candidate_sums(v1, v1_all, v2, v2_all, row_id):
    assert PEER_TOPK == 16
    lo2 = v2_all[0:8]
    hi1 = v1_all[8:16]
    shift = lambda x, n: pltpu.roll(x, n, 0)
    groups = [v1[0] + v2_all[0:8], v1[0] + v2_all[8:16], v1[0] + v2_all[16:24],
              v1[1] + lo2]
    groups.append(jnp.where(row_id < 5, v1[2] + lo2, v1[4] + shift(lo2, 5)))
    groups.append(jnp.where(row_id < 4, v1[3] + lo2,
                            jnp.where(row_id < 6, v1[5] + shift(lo2, 4),
                                      v1[6] + shift(lo2, 6))))
    tail = shift(hi1, 2) + v2[0]
    groups.append(jnp.where(row_id < 2, v1[7] + lo2, tail))
    groups.append(jnp.where(row_id < 2, tail,
                            jnp.where(row_id == 2, v1[16] + v2[0], NEG_INF)))
    return groups


def _peer_select_kernel(h_ref, wq_ref, sk_ref, ht_ref, th_ref, e1_ref, s2_ref, e2_ref, *, tm):
    hb = h_ref[...]
    ht_ref[0] = hb.T
    K = PEER_TOPK + 1
    row_id = lax.broadcasted_iota(jnp.int32, (8, LANES), 0)
    pad_rows = jnp.full((24 - K, LANES), NEG_INF, F32)

    def head(h, carry):
        q_t = _nt_dot(wq_ref[h], hb).astype(BF16)
        half = PEER_QUERY_DIM // 2
        sc = [jnp.dot(sk_ref[h, c], q_t[c * half:(c + 1) * half], preferred_element_type=F32)
              for c in range(2)]
        for lc in range(tm // LANES):
            ls = slice(lc * LANES, (lc + 1) * LANES)
            s1, s2 = sc[0][:, ls], sc[1][:, ls]
            v1 = _top_values(s1, K)
            v2 = _top_values(s2, K)
            v2_all = jnp.concatenate(v2 + [pad_rows], axis=0)
            v1_all = jnp.concatenate(v1 + [pad_rows], axis=0)
            cands = _candidate_sums(v1, v1_all, v2, v2_all, row_id)
            best = cands[0][0:1]
            z_sum = jnp.zeros_like(best)
            last = best
            for r in range(K):
                stacked = cands[0]
                for ca in cands[1:]:
                    stacked = jnp.maximum(stacked, ca)
                m = jnp.max(stacked, axis=0, keepdims=True)
                if r < PEER_TOPK:
                    z_sum = z_sum + jnp.exp(m - best)
                    last = m
                    cands = [jnp.where(ca == m, NEG_INF, ca) for ca in cands]
            tau = 0.5 * (last + m)
            half_inv_z = 0.5 / z_sum
            chunked = (PEER_N_KEYS // PEER_CHUNK_KEYS, PEER_CHUNK_KEYS, LANES)
            th_ref[:, 0, h, :, ls] = (tau - s1).reshape(chunked)
            e1_ref[:, 0, h, :, ls] = (jnp.exp(s1 - v1[0]) * half_inv_z).reshape(chunked)
            s2_ref[0, h, :, ls] = s2
            e2_ref[0, h, :, ls] = jnp.exp(s2 - v2[0])
        return carry

    lax.fori_loop(0, PEER_HEADS, head, 0)


def _peer_select(h2, wq_t, subkeys_bf):
    T, D = h2.shape
    tm = PEER_TOKEN_TILE
    n_tiles = T // tm
    tok_spec = pl.BlockSpec((1, PEER_HEADS, PEER_N_KEYS, tm), lambda i: (i, 0, 0, 0))
    tok_shape = jax.ShapeDtypeStruct((n_tiles, PEER_HEADS, PEER_N_KEYS, tm), F32)
    n_chunks = PEER_N_KEYS // PEER_CHUNK_KEYS
    row_spec = pl.BlockSpec((n_chunks, 1, PEER_HEADS, PEER_CHUNK_KEYS, tm), lambda i: (0, i, 0, 0, 0))
    row_shape = jax.ShapeDtypeStruct((n_chunks, n_tiles, PEER_HEADS, PEER_CHUNK_KEYS, tm), F32)
    return pl.pallas_call(
        functools.partial(_peer_select_kernel, tm=tm),
        grid=(n_tiles,),
        in_specs=[
            pl.BlockSpec((tm, D), lambda i: (i, 0)),
            pl.BlockSpec((PEER_HEADS, PEER_QUERY_DIM, D), lambda i: (0, 0, 0)),
            pl.BlockSpec((PEER_HEADS, 2, PEER_N_KEYS, PEER_QUERY_DIM // 2), lambda i: (0, 0, 0, 0)),
        ],
        out_specs=[
            pl.BlockSpec((1, D, tm), lambda i: (i, 0, 0)),
            row_spec, row_spec, tok_spec, tok_spec,
        ],
        out_shape=[
            jax.ShapeDtypeStruct((n_tiles, D, tm), BF16),
            row_shape, row_shape, tok_shape, tok_shape,
        ],
        compiler_params=_params("parallel"),
        name="peer_select",
    )(h2, wq_t, subkeys_bf)


PEER_ROW_BLOCK_KEYS = 4
PEER_PIECE_TOKENS = 2 * LANES


def _peer_mix_kernel(ht_ref, u_ref, vt_ref, th_ref, e1_ref, s2_ref, e2_ref, x_ref, g2_ref,
                     xo_ref, acc_sc, act_sc, s2_sc, e2_sc, *, tm, n_tiles):
    j = pl.program_id(1)

    @pl.when(j == 0)
    def _():
        acc_sc[...] = jnp.zeros_like(acc_sc)
        for lc in range(tm // LANES):
            s2_sc[:, :, lc] = s2_ref[:, :, :, lc * LANES:(lc + 1) * LANES]
            e2_sc[:, :, lc] = e2_ref[:, :, :, lc * LANES:(lc + 1) * LANES]

    rb = PEER_ROW_BLOCK_KEYS * PEER_N_KEYS
    tw = PEER_PIECE_TOKENS
    n_rb = PEER_CHUNK // rb
    pieces = [(r, c) for c in range(tm // tw) for r in range(n_rb)]

    def tile(t, carry):
        def scores(p):
            r, c = pieces[p]
            return jnp.dot(u_ref[r * rb:(r + 1) * rb, :], ht_ref[t, :, c * tw:(c + 1) * tw],
                           preferred_element_type=F32)

        def gates(p, a_blk, k):
            r, c = pieces[p]
            ii = r * PEER_ROW_BLOCK_KEYS + k
            for lc in range(tw // LANES):
                chunk = c * (tw // LANES) + lc
                ls = slice(chunk * LANES, (chunk + 1) * LANES)
                gate = None
                for h in range(PEER_HEADS):
                    theta = th_ref[0, t, h, ii:ii + 1, ls]
                    e1 = e1_ref[0, t, h, ii:ii + 1, ls]
                    term = jnp.where(s2_sc[t, h, chunk] >= theta, e1 * e2_sc[t, h, chunk], 0.0)
                    gate = term if gate is None else gate + term
                a = a_blk[k * PEER_N_KEYS:(k + 1) * PEER_N_KEYS, lc * LANES:(lc + 1) * LANES]
                gelu2 = a * (1.0 + lax.erf(a * (1.0 / math.sqrt(2.0))))
                act_sc[c, ii * PEER_N_KEYS:(ii + 1) * PEER_N_KEYS, lc * LANES:(lc + 1) * LANES] = (
                    gelu2 * gate).astype(BF16)

        def mix(c):
            acc_sc[t, c] += jnp.dot(vt_ref[0], act_sc[c], preferred_element_type=F32)

        a_next = scores(0)
        for p in range(len(pieces)):
            r, c = pieces[p]
            a_cur = a_next
            if p + 1 < len(pieces):
                a_next = scores(p + 1)
            for k in range(PEER_ROW_BLOCK_KEYS):
                if k == PEER_ROW_BLOCK_KEYS // 2 and r == 0 and c > 0:
                    mix(c - 1)
                gates(p, a_cur, k)
        mix(tm // tw - 1)
        return carry

    lax.fori_loop(0, n_tiles, tile, 0)

    @pl.when(j == pl.num_programs(1) - 1)
    def _():
        for t in range(n_tiles):
            rows = slice(t * tm, (t + 1) * tm)
            mixed_t = jnp.concatenate([acc_sc[t, c] for c in range(tm // tw)], axis=1)
            xo_ref[rows, :] = x_ref[rows, :] + g2_ref[0] * mixed_t.T


def _peer_mix(h_t, u_bf, v_t, theta, e1, s2, e2, x2, gate2, B, S):
    T, D = x2.shape
    tm = PEER_TOKEN_TILE
    n_tiles = min(PEER_TILES_PER_STEP, S // tm)
    span = n_tiles * tm
    steps_per_b = S // span
    tok_spec = pl.BlockSpec((n_tiles, PEER_HEADS, PEER_N_KEYS, tm), lambda i, j: (i, 0, 0, 0),
                            pipeline_mode=pl.Buffered(1))
    row_spec = pl.BlockSpec((1, n_tiles, PEER_HEADS, PEER_CHUNK_KEYS, tm), lambda i, j: (j, i, 0, 0, 0))
    return pl.pallas_call(
        functools.partial(_peer_mix_kernel, tm=tm, n_tiles=n_tiles),
        grid=(T // span, PEER_N_EXPERTS // PEER_CHUNK),
        in_specs=[
            pl.BlockSpec((n_tiles, D, tm), lambda i, j: (i, 0, 0)),
            pl.BlockSpec((PEER_CHUNK, D), lambda i, j: (j, 0)),
            pl.BlockSpec((1, D, PEER_CHUNK), lambda i, j: (j, 0, 0)),
            row_spec, row_spec, tok_spec, tok_spec,
            pl.BlockSpec((span, D), lambda i, j: (i, 0)),
            pl.BlockSpec((1, 1, D), lambda i, j: (i // steps_per_b, 0, 0)),
        ],
        out_specs=pl.BlockSpec((span, D), lambda i, j: (i, 0)),
        out_shape=jax.ShapeDtypeStruct((T, D), F32),
        scratch_shapes=[pltpu.VMEM((n_tiles, tm // PEER_PIECE_TOKENS, D, PEER_PIECE_TOKENS), F32),
                        pltpu.VMEM((tm // PEER_PIECE_TOKENS, PEER_CHUNK, PEER_PIECE_TOKENS), BF16),
                        pltpu.VMEM((n_tiles, PEER_HEADS, tm // LANES, PEER_N_KEYS, LANES), F32),
                        pltpu.VMEM((n_tiles, PEER_HEADS, tm // LANES, PEER_N_KEYS, LANES), F32)],
        compiler_params=_params("parallel", "arbitrary"),
        name="peer_mix",
    )(h_t, u_bf, v_t, theta, e1, s2, e2, x2, gate2.reshape(B, 1, D))


def _final_norm_kernel(x_ref, g_ref, o_ref):
    x = x_ref[...]
    ms = jnp.mean(x * x, axis=-1, keepdims=True)
    o_ref[...] = x * lax.rsqrt(ms + NORM_EPS) * g_ref[...]


def _final_norm(x2, g):
    T, D = x2.shape
    tm = min(1024, T)
    return pl.pallas_call(
        _final_norm_kernel,
        grid=(T // tm,),
        in_specs=[pl.BlockSpec((tm, D), lambda i: (i, 0)), pl.BlockSpec((1, D), lambda i: (0, 0))],
        out_specs=pl.BlockSpec((tm, D), lambda i: (i, 0)),
        out_shape=jax.ShapeDtypeStruct((T, D), F32),
        compiler_params=_params("parallel"),
        name="final_norm",
    )(x2, g.reshape(1, D))


def kernel(x, c, norm1_g, norm2_g, ada_w, ada_b, w_in, gate_b, diff_lambda, diff_subln_g,
           w_branch_sb, w_branch_diff, w_out, peer_wq, peer_subkeys, peer_u, peer_v, final_g):
    B, S, D = x.shape
    assert D == D_MODEL and S % LANES == 0
    x2 = x.reshape(B * S, D)
    mod = _modulation(c, ada_w, ada_b)
    rope = _rope_lane_tables(S)
    for l in range(DEPTH):
        lam_init = 0.8 - 0.6 * math.exp(-0.3 * l)
        shift1, scale1, gate1, shift2, scale2, gate2 = (mod[l, :, n] for n in range(6))
        proj = _input_projection(x2, norm1_g[l], scale1, shift1, w_in[l].astype(BF16), rope, B, S)
        y_sb = _sb_attention(proj, B, S)
        y_df = _diff_attention(proj, diff_lambda[l], diff_subln_g[l], lam_init, B, S)
        x2, h2 = _merge_project(y_sb, y_df, proj, gate_b[l], w_branch_sb[l].astype(BF16),
                                w_branch_diff[l].astype(BF16), w_out[l].astype(BF16), x2, gate1,
                                norm2_g[l], scale2, shift2, B, S)
        wq_t = peer_wq[l].T.astype(BF16).reshape(PEER_HEADS, PEER_QUERY_DIM, D)
        h_t, theta, e1, s2, e2 = _peer_select(h2, wq_t, peer_subkeys[l].astype(BF16))
        v_t = jnp.swapaxes(peer_v[l].astype(BF16).reshape(-1, PEER_CHUNK, D), 1, 2)
        x2 = _peer_mix(h_t, peer_u[l].astype(BF16), v_t,
                       theta, e1, s2, e2, x2, gate2, B, S)
    return _final_norm(x2, final_g).reshape(B, S, D)
```

```python
import functools
import math

import jax
import jax.numpy as jnp
from jax import lax
from jax.experimental import pallas as pl
from jax.experimental.pallas import tpu as pltpu

F32 = jnp.float32
BF16 = jnp.bfloat16

D_MODEL = 1024
DEPTH = 2
SB_HEADS = 8
SB_HEAD_DIM = 64
SB_WIDTH = SB_HEADS * SB_HEAD_DIM
DIFF_HEADS = 4
DIFF_QK_DIM = 64
DIFF_V_DIM = 2 * DIFF_QK_DIM
DIFF_QK_WIDTH = DIFF_HEADS * 2 * DIFF_QK_DIM
DIFF_V_WIDTH = DIFF_HEADS * DIFF_V_DIM
ROPE_THETA = 500000.0
ROT_DIM = DIFF_QK_DIM // 4
IN_COLS = 3 * SB_WIDTH + 2 * DIFF_QK_WIDTH + DIFF_V_WIDTH + 2 * D_MODEL
PEER_HEADS = 8
PEER_N_KEYS = 128
PEER_N_EXPERTS = PEER_N_KEYS * PEER_N_KEYS
PEER_TOPK = 16
PEER_QUERY_DIM = 256
PEER_CHUNK_KEYS = 8
PEER_CHUNK = PEER_CHUNK_KEYS * PEER_N_KEYS
PEER_TOKEN_TILE = 512
PEER_TILES_PER_STEP = 2
NORM_EPS = 1e-6

LANES = 128
VMEM_LIMIT = 56 * 1024 * 1024
NEG_INF = float("-inf")
LOG2_E = 1.4426950408889634

IN_TILE_N = 512
COL_SB_Q, COL_SB_K, COL_SB_V = 0, SB_WIDTH // LANES, 2 * SB_WIDTH // LANES
COL_D_Q = 3 * SB_WIDTH // LANES
COL_D_K = COL_D_Q + DIFF_QK_WIDTH // LANES
COL_D_V = COL_D_K + DIFF_QK_WIDTH // LANES
COL_G = (3 * SB_WIDTH + 2 * DIFF_QK_WIDTH + DIFF_V_WIDTH) // D_MODEL


def _params(*sem):
    return pltpu.CompilerParams(dimension_semantics=sem, vmem_limit_bytes=VMEM_LIMIT)


def _pack_row_pairs(w):
    bits = lax.bitcast_convert_type(w, jnp.uint16).astype(jnp.uint32)
    return (bits[..., 1::2, :] << 16) | bits[..., 0::2, :]


def _nt_dot(a, b):
    return lax.dot_general(a, b, (((1,), (1,)), ((), ())), preferred_element_type=F32)


def _mod_kernel(c_ref, w_ref, b_ref, o_ref):
    c = c_ref[...]
    c_act = c * jax.nn.sigmoid(c)
    o_ref[0] = jnp.dot(c_act, w_ref[0], preferred_element_type=F32,
                       precision=lax.Precision.HIGHEST) + b_ref[0]


def _modulation(c, ada_w, ada_b):
    B, D = c.shape
    rows = 8
    c_pad = jnp.pad(c, ((0, rows - B), (0, 0)))
    out = pl.pallas_call(
        _mod_kernel,
        grid=(DEPTH, 6),
        in_specs=[
            pl.BlockSpec((rows, D), lambda l, n: (0, 0)),
            pl.BlockSpec((1, D, D), lambda l, n: (l, 0, n)),
            pl.BlockSpec((1, 1, D), lambda l, n: (l, 0, n)),
        ],
        out_specs=pl.BlockSpec((1, rows, D), lambda l, n: (l, 0, n)),
        out_shape=jax.ShapeDtypeStruct((DEPTH, rows, 6 * D), F32),
        compiler_params=_params("arbitrary", "arbitrary"),
        name="adaln_mod",
    )(c_pad, ada_w, ada_b.reshape(DEPTH, 1, 6 * D))
    return out[:, :B].reshape(DEPTH, B, 6, D)


def _norm_mod(x, g, scale, shift):
    ms = jnp.mean(x * x, axis=-1, keepdims=True)
    return (x * lax.rsqrt(ms + NORM_EPS) * g) * (1.0 + scale) + shift


def _inproj_kernel(x_ref, g_ref, sc_ref, sh_ref, w_ref, ra_ref, rb_ref, rc_ref, o_ref):
    h = _norm_mod(x_ref[...], g_ref[...], sc_ref[0], sh_ref[0]).astype(BF16)
    half = ROT_DIM // 2
    for j in range(IN_COLS // IN_TILE_N):
        cols = slice(j * IN_TILE_N, (j + 1) * IN_TILE_N)
        r = jnp.dot(h, w_ref[:, cols], preferred_element_type=F32)
        first_lane_block = j * IN_TILE_N // LANES
        if first_lane_block in (COL_SB_Q, COL_D_Q):
            r = r * (LOG2_E / math.sqrt(SB_HEAD_DIM))
        if first_lane_block in (COL_D_Q, COL_D_K):
            ra, rb, rc = ra_ref[...], rb_ref[...], rc_ref[...]
            for blk in range(IN_TILE_N // LANES):
                t = r[:, blk * LANES:(blk + 1) * LANES]
                rot = t * ra + pltpu.roll(t, LANES - half, 1) * rb + pltpu.roll(t, half, 1) * rc
                o_ref[:, j * IN_TILE_N + blk * LANES:j * IN_TILE_N + (blk + 1) * LANES] = rot.astype(BF16)
        else:
            o_ref[:, cols] = r.astype(BF16)


def _rope_lane_tables(seq_len):
    pos = jnp.arange(seq_len, dtype=F32)
    inv_freq = 1.0 / (ROPE_THETA ** (jnp.arange(0, ROT_DIM, 2, dtype=F32) / ROT_DIM))
    ang = pos[:, None] * inv_freq[None, :]
    cos, sin = jnp.cos(ang), jnp.sin(ang)
    half = ROT_DIM // 2
    lane = jnp.arange(LANES) % DIFF_QK_DIM
    f = lane % half
    first, second = lane < half, (lane >= half) & (lane < ROT_DIM)
    ra = jnp.where((first | second)[None, :], cos[:, f], 1.0)
    rb = jnp.where(first[None, :], -sin[:, f], 0.0)
    rc = jnp.where(second[None, :], sin[:, f], 0.0)
    return ra, rb, rc


def _input_projection(x2, norm_g, scale, shift, w_in_bf, rope, B, S):
    T, D = x2.shape
    tm = min(512, S)
    tiles_per_b = S // tm
    ra, rb, rc = rope
    rope_spec = pl.BlockSpec((tm, LANES), lambda i: (i % tiles_per_b, 0))
    mod_spec = pl.BlockSpec((1, 1, D), lambda i: (i // tiles_per_b, 0, 0))
    assert COL_D_K - COL_D_Q == IN_TILE_N // LANES and COL_D_V - COL_D_K == IN_TILE_N // LANES
    return pl.pallas_call(
        _inproj_kernel,
        grid=(T // tm,),
        in_specs=[
            pl.BlockSpec((tm, D), lambda i: (i, 0)),
            pl.BlockSpec((1, D), lambda i: (0, 0)),
            mod_spec, mod_spec,
            pl.BlockSpec((D, IN_COLS), lambda i: (0, 0)),
            rope_spec, rope_spec, rope_spec,
        ],
        out_specs=pl.BlockSpec((tm, IN_COLS), lambda i: (i, 0)),
        out_shape=jax.ShapeDtypeStruct((T, IN_COLS), BF16),
        compiler_params=_params("parallel"),
        name="in_proj",
    )(x2, norm_g.reshape(1, D), scale.reshape(B, 1, D), shift.reshape(B, 1, D), w_in_bf, ra, rb, rc)


def _sb_kernel(q_ref, k_ref, v_ref, o_ref, acc_sc, cs_sc, *, tb, n_pairs):
    i = pl.program_id(2)
    n_heads = 2 * n_pairs
    lane = lax.broadcasted_iota(jnp.int32, (tb, LANES), 1)
    q_heads = []
    for p in range(n_pairs):
        q = q_ref[:, p * LANES:(p + 1) * LANES]
        zero = jnp.zeros_like(q)
        q_heads += [jnp.where(lane < SB_HEAD_DIM, q, zero), jnp.where(lane >= SB_HEAD_DIM, q, zero)]
    row = lax.broadcasted_iota(jnp.int32, (tb, tb), 0)
    col = lax.broadcasted_iota(jnp.int32, (tb, tb), 1)
    strictly_before = col < row
    suffix_ones = jnp.where(row >= col, 1.0, 0.0).astype(BF16)

    acc_sc[...] = jnp.zeros_like(acc_sc)
    cs_sc[...] = jnp.zeros_like(cs_sc)

    def block(j, masked):
        start = pl.multiple_of(j * tb, tb)
        kb = k_ref[pl.ds(start, tb), :]
        vb = v_ref[pl.ds(start, tb), :]
        ys = [_nt_dot(q_heads[h], kb[:, (h // 2) * LANES:(h // 2 + 1) * LANES]) for h in range(n_heads)]
        costs = []
        for y in ys:
            neg_abs = lax.bitcast_convert_type(
                lax.bitcast_convert_type(y, jnp.uint32) | jnp.uint32(0x80000000), F32)
            cost = jnp.maximum(y, 0.0) + jnp.log2(1.0 + jnp.exp2(neg_abs))
            if masked:
                cost = jnp.where(strictly_before, cost, 0.0)
            costs.append(cost)
        sums = [jnp.dot(cost.astype(BF16), suffix_ones, preferred_element_type=F32) for cost in costs]
        for h in range(n_heads):
            later = cs_sc[h]
            w = jnp.exp2(ys[h] - sums[h] - later)
            if masked:
                w = jnp.where(strictly_before, w, 0.0)
            acc_sc[h] += jnp.dot(w.astype(BF16), vb[:, (h // 2) * LANES:(h // 2 + 1) * LANES],
                                 preferred_element_type=F32)
            cs_sc[h] = later + sums[h][:, 0:1]

    block(i, True)

    def body(jj, carry):
        block(i - 1 - jj, False)
        return carry

    lax.fori_loop(0, i, body, 0)
    for p in range(n_pairs):
        o_ref[:, p * LANES:(p + 1) * LANES] = jnp.where(
            lane < SB_HEAD_DIM, acc_sc[2 * p], acc_sc[2 * p + 1]).astype(BF16)


SB_PAIRS_PER_STEP = 4


def _sb_attention(proj, B, S):
    T = proj.shape[0]
    tb = min(256, S)
    nq = S // tb
    n_pairs = SB_PAIRS_PER_STEP
    width = n_pairs * LANES
    groups = SB_WIDTH // width
    return pl.pallas_call(
        functools.partial(_sb_kernel, tb=tb, n_pairs=n_pairs),
        grid=(B, groups, nq),
        in_specs=[
            pl.BlockSpec((tb, width), lambda b, p, i: (b * nq + i, COL_SB_Q // n_pairs + p)),
            pl.BlockSpec((S, width), lambda b, p, i: (b, COL_SB_K // n_pairs + p)),
            pl.BlockSpec((S, width), lambda b, p, i: (b, COL_SB_V // n_pairs + p)),
        ],
        out_specs=pl.BlockSpec((tb, width), lambda b, p, i: (b * nq + i, p)),
        out_shape=jax.ShapeDtypeStruct((T, SB_WIDTH), BF16),
        scratch_shapes=[pltpu.VMEM((2 * n_pairs, tb, LANES), F32),
                        pltpu.VMEM((2 * n_pairs, tb, 1), F32)],
        compiler_params=_params("parallel", "parallel", "arbitrary"),
        name="sb_attention",
    )(proj, proj, proj)


def _diff_kernel(lp_ref, g_ref, q_ref, k_ref, v_ref, o_ref, m_sc, acc_sc, *, tq, tk, lam_init):
    i = pl.program_id(1)
    lane = lax.broadcasted_iota(jnp.int32, (tq, LANES), 1)
    q_maps = []
    for h in range(DIFF_HEADS):
        q = q_ref[:, h * LANES:(h + 1) * LANES]
        zero = jnp.zeros_like(q)
        q_maps += [jnp.where(lane < DIFF_QK_DIM, q, zero), jnp.where(lane >= DIFF_QK_DIM, q, zero)]
    n_chains = len(q_maps)
    m_sc[...] = jnp.full_like(m_sc, NEG_INF)
    acc_sc[...] = jnp.zeros_like(acc_sc)
    q_pos = i * tq + lax.broadcasted_iota(jnp.int32, (tq, tk), 0)
    k_off = lax.broadcasted_iota(jnp.int32, (tq, tk), 1)
    ones = jnp.ones((tk, LANES), BF16)

    def block(j, masked):
        start = pl.multiple_of(j * tk, tk)
        kb = k_ref[pl.ds(start, tk), :]
        vb = v_ref[pl.ds(start, tk), :]
        head_cols = lambda c: slice((c // 2) * LANES, (c // 2 + 1) * LANES)
        scores = [_nt_dot(q_maps[c], kb[:, head_cols(c)]) for c in range(n_chains)]
        if masked:
            visible = j * tk + k_off <= q_pos
            scores = [jnp.where(visible, s, NEG_INF) for s in scores]
        m_prev = [m_sc[c] for c in range(n_chains)]
        m_new = [jnp.maximum(m_prev[c], jnp.max(scores[c], axis=1, keepdims=True)) for c in range(n_chains)]
        probs = [jnp.exp2(scores[c] - jnp.concatenate([m_new[c]] * (tk // LANES), axis=1))
                 for c in range(n_chains)]
        for c in range(n_chains):
            alpha = jnp.exp2(m_prev[c] - m_new[c])
            v_ext = jnp.concatenate([vb[:, head_cols(c)], ones], axis=1)
            acc_sc[c] = (jnp.concatenate([alpha, alpha], axis=1) * acc_sc[c]
                         + jnp.dot(probs[c].astype(BF16), v_ext, preferred_element_type=F32))
            m_sc[c] = m_new[c]

    n_full = (i * tq) // tk

    def body(j, carry):
        block(j, False)
        return carry

    lax.fori_loop(0, n_full, body, 0)
    block(n_full, True)

    lp = lp_ref[...]
    lam = (jnp.exp(jnp.sum(lp[0:1] * lp[1:2], axis=1, keepdims=True))
           - jnp.exp(jnp.sum(lp[2:3] * lp[3:4], axis=1, keepdims=True)) + lam_init)
    for h in range(DIFF_HEADS):
        a1, a2 = acc_sc[2 * h], acc_sc[2 * h + 1]
        o = a1[:, :LANES] / a1[:, LANES:] - lam * (a2[:, :LANES] / a2[:, LANES:])
        ms = jnp.mean(o * o, axis=-1, keepdims=True)
        y = (o * lax.rsqrt(ms + NORM_EPS) * g_ref[...]) * (1.0 - lam_init)
        o_ref[:, h * LANES:(h + 1) * LANES] = y.astype(BF16)


def _diff_attention(proj, lam_params, subln_g, lam_init, B, S):
    T = proj.shape[0]
    tq = min(256, S)
    tk = min(512, S)
    nq = S // tq
    n_chains = 2 * DIFF_HEADS
    return pl.pallas_call(
        functools.partial(_diff_kernel, tq=tq, tk=tk, lam_init=lam_init),
        grid=(B, nq),
        in_specs=[
            pl.BlockSpec((4, DIFF_QK_DIM), lambda b, i: (0, 0)),
            pl.BlockSpec((1, DIFF_V_DIM), lambda b, i: (0, 0)),
            pl.BlockSpec((tq, DIFF_QK_WIDTH), lambda b, i: (b * nq + i, COL_D_Q * LANES // DIFF_QK_WIDTH)),
            pl.BlockSpec((S, DIFF_QK_WIDTH), lambda b, i: (b, COL_D_K * LANES // DIFF_QK_WIDTH)),
            pl.BlockSpec((S, DIFF_V_WIDTH), lambda b, i: (b, COL_D_V * LANES // DIFF_V_WIDTH)),
        ],
        out_specs=pl.BlockSpec((tq, DIFF_V_WIDTH), lambda b, i: (b * nq + i, 0)),
        out_shape=jax.ShapeDtypeStruct((T, DIFF_V_WIDTH), BF16),
        scratch_shapes=[pltpu.VMEM((n_chains, tq, LANES), F32),
                        pltpu.VMEM((n_chains, tq, 2 * LANES), F32)],
        compiler_params=_params("parallel", "arbitrary"),
        name="diff_attention",
    )(lam_params, subln_g.reshape(1, DIFF_V_DIM), proj, proj, proj)


def _merge_kernel(ysb_ref, ydf_ref, gs_ref, gd_ref, gb_ref, wbs_ref, wbd_ref, wo_ref,
                  x_ref, g1_ref, n2_ref, sc2_ref, sh2_ref, xo_ref, h2_ref):
    D = x_ref.shape[1]
    a = jnp.dot(ysb_ref[...], wbs_ref[...], preferred_element_type=F32)
    b = jnp.dot(ydf_ref[...], wbd_ref[...], preferred_element_type=F32)
    gb = gb_ref[...]
    g_sb = jax.nn.sigmoid(gs_ref[...].astype(F32) + gb[:, :D])
    g_df = jax.nn.sigmoid(gd_ref[...].astype(F32) + gb[:, D:])
    merged = g_sb * a + g_df * b
    y = jnp.dot(merged.astype(BF16), wo_ref[...], preferred_element_type=F32)
    xn = x_ref[...] + g1_ref[0] * y
    xo_ref[...] = xn
    h2_ref[...] = _norm_mod(xn, n2_ref[...], sc2_ref[0], sh2_ref[0]).astype(BF16)


def _merge_project(y_sb, y_df, proj, gate_b, wbs, wbd, wo, x2, gate1, norm2_g, scale2, shift2, B, S):
    T, D = x2.shape
    tm = min(512, S)
    tiles_per_b = S // tm
    mod_spec = pl.BlockSpec((1, 1, D), lambda i: (i // tiles_per_b, 0, 0))
    const = lambda shape: pl.BlockSpec(shape, lambda i: (0, 0))
    return pl.pallas_call(
        _merge_kernel,
        grid=(T // tm,),
        in_specs=[
            pl.BlockSpec((tm, SB_WIDTH), lambda i: (i, 0)),
            pl.BlockSpec((tm, DIFF_V_WIDTH), lambda i: (i, 0)),
            pl.BlockSpec((tm, D), lambda i: (i, COL_G)),
            pl.BlockSpec((tm, D), lambda i: (i, COL_G + 1)),
            const((1, 2 * D)),
            const((SB_WIDTH, D)), const((DIFF_V_WIDTH, D)), const((D, D)),
            pl.BlockSpec((tm, D), lambda i: (i, 0)),
            mod_spec,
            const((1, D)),
            mod_spec, mod_spec,
        ],
        out_specs=[pl.BlockSpec((tm, D), lambda i: (i, 0)), pl.BlockSpec((tm, D), lambda i: (i, 0))],
        out_shape=[jax.ShapeDtypeStruct((T, D), F32), jax.ShapeDtypeStruct((T, D), BF16)],
        compiler_params=_params("parallel"),
        name="merge_out_proj",
    )(y_sb, y_df, proj, proj, gate_b.reshape(1, 2 * D), wbs, wbd, wo, x2,
      gate1.reshape(B, 1, D), norm2_g.reshape(1, D), scale2.reshape(B, 1, D), shift2.reshape(B, 1, D))


def _sort16_network():
    def merge(lo, hi, r):
        step = r * 2
        if step < hi - lo:
            yield from merge(lo, hi, step)
            yield from merge(lo + r, hi, step)
            yield from [(i, i + r) for i in range(lo + r, hi - r, step)]
        else:
            yield (lo, lo + r)

    def sort(lo, hi):
        if hi - lo >= 1:
            mid = lo + (hi - lo) // 2
            yield from sort(lo, mid)
            yield from sort(mid + 1, hi)
            yield from merge(lo, hi, 1)

    return list(sort(0, 15))


def _top_values(scores, k):
    groups = scores.shape[0] // 8
    assert groups == 16 and k <= groups + 1
    v = [scores[8 * g:8 * (g + 1)] for g in range(groups)]
    for a, b in _sort16_network():
        v[a], v[b] = jnp.maximum(v[a], v[b]), jnp.minimum(v[a], v[b])
    v.append(jnp.full_like(v[0], NEG_INF))
    vals = []
    for r in range(k):
        m = jnp.max(v[0], axis=0, keepdims=True)
        vals.append(m)
        still_needed = k - 1 - r
        if still_needed:
            hit = v[0] == m
            for d in range(min(still_needed, groups)):
                v[d] = jnp.where(hit, v[d + 1], v[d])
    return vals


def _candidate_sums(v1, v1_all, v2, v2_all, row_id):
    assert PEER_TOPK == 16
    lo2 = v2_all[0:8]
    hi1 = v1_all[8:16]
    shift = lambda x, n: pltpu.roll(x, n, 0)
    groups = [v1[0] + v2_all[0:8], v1[0] + v2_all[8:16], v1[0] + v2_all[16:24],
              v1[1] + lo2]
    groups.append(jnp.where(row_id < 5, v1[2] + lo2, v1[4] + shift(lo2, 5)))
    groups.append(jnp.where(row_id < 4, v1[3] + lo2,
                            jnp.where(row_id < 6, v1[5] + shift(lo2, 4),
                                      v1[6] + shift(lo2, 6))))
    tail = shift(hi1, 2) + v2[0]
    groups.append(jnp.where(row_id < 2, v1[7] + lo2, tail))
    groups.append(jnp.where(row_id < 2, tail,
                            jnp.where(row_id == 2, v1[16] + v2[0], NEG_INF)))
    return groups


def _peer_select_kernel(h_ref, wq_ref, sk_ref, ht_ref, th_ref, e1_ref, s2_ref, e2_ref, *, tm):
    hb = h_ref[...]
    ht_ref[0] = pltpu.bitcast(hb.T, jnp.uint32)
    K = PEER_TOPK + 1
    row_id = lax.broadcasted_iota(jnp.int32, (8, LANES), 0)
    pad_rows = jnp.full((24 - K, LANES), NEG_INF, F32)

    def head(h, carry):
        q_t = _nt_dot(wq_ref[h], hb).astype(BF16)
        half = PEER_QUERY_DIM // 2
        sc = [jnp.dot(sk_ref[h, c], q_t[c * half:(c + 1) * half], preferred_element_type=F32)
              for c in range(2)]
        for lc in range(tm // LANES):
            ls = slice(lc * LANES, (lc + 1) * LANES)
            s1, s2 = sc[0][:, ls], sc[1][:, ls]
            v1 = _top_values(s1, K)
            v2 = _top_values(s2, K)
            v2_all = jnp.concatenate(v2 + [pad_rows], axis=0)
            v1_all = jnp.concatenate(v1 + [pad_rows], axis=0)
            cands = _candidate_sums(v1, v1_all, v2, v2_all, row_id)
            best = cands[0][0:1]
            z_sum = jnp.zeros_like(best)
            last = best
            for r in range(K):
                stacked = cands[0]
                for ca in cands[1:]:
                    stacked = jnp.maximum(stacked, ca)
                m = jnp.max(stacked, axis=0, keepdims=True)
                if r < PEER_TOPK:
                    z_sum = z_sum + jnp.exp(m - best)
                    last = m
                    cands = [jnp.where(ca == m, NEG_INF, ca) for ca in cands]
            tau = 0.5 * (last + m)
            half_inv_z = 0.5 / z_sum
            chunked = (PEER_N_KEYS // PEER_CHUNK_KEYS, PEER_CHUNK_KEYS, LANES)
            th_ref[:, 0, h, :, ls] = (tau - s1).reshape(chunked)
            e1_ref[:, 0, h, :, ls] = (jnp.exp(s1 - v1[0]) * half_inv_z).reshape(chunked)
            s2_ref[0, h, :, ls] = s2
            e2_ref[0, h, :, ls] = jnp.exp(s2 - v2[0])
        return carry

    lax.fori_loop(0, PEER_HEADS, head, 0)


def _peer_select(h2, wq_t, subkeys_bf):
    T, D = h2.shape
    tm = PEER_TOKEN_TILE
    n_tiles = T // tm
    tok_spec = pl.BlockSpec((1, PEER_HEADS, PEER_N_KEYS, tm), lambda i: (i, 0, 0, 0))
    tok_shape = jax.ShapeDtypeStruct((n_tiles, PEER_HEADS, PEER_N_KEYS, tm), F32)
    n_chunks = PEER_N_KEYS // PEER_CHUNK_KEYS
    row_spec = pl.BlockSpec((n_chunks, 1, PEER_HEADS, PEER_CHUNK_KEYS, tm), lambda i: (0, i, 0, 0, 0))
    row_shape = jax.ShapeDtypeStruct((n_chunks, n_tiles, PEER_HEADS, PEER_CHUNK_KEYS, tm), F32)
    return pl.pallas_call(
        functools.partial(_peer_select_kernel, tm=tm),
        grid=(n_tiles,),
        in_specs=[
            pl.BlockSpec((tm, D), lambda i: (i, 0)),
            pl.BlockSpec((PEER_HEADS, PEER_QUERY_DIM, D), lambda i: (0, 0, 0)),
            pl.BlockSpec((PEER_HEADS, 2, PEER_N_KEYS, PEER_QUERY_DIM // 2), lambda i: (0, 0, 0, 0)),
        ],
        out_specs=[
            pl.BlockSpec((1, D // 2, tm), lambda i: (i, 0, 0)),
            row_spec, row_spec, tok_spec, tok_spec,
        ],
        out_shape=[
            jax.ShapeDtypeStruct((n_tiles, D // 2, tm), jnp.uint32),
            row_shape, row_shape, tok_shape, tok_shape,
        ],
        compiler_params=_params("parallel"),
        name="peer_select",
    )(h2, wq_t, subkeys_bf)


PEER_ROW_BLOCK_KEYS = 4
PEER_PIECE_TOKENS = 2 * LANES


def _peer_mix_kernel(ht_ref, u_ref, vt_ref, th_ref, e1_ref, s2_ref, e2_ref, x_ref, g2_ref,
                     xo_ref, acc_sc, act_sc, s2_sc, e2_sc, *, tm, n_tiles):
    j = pl.program_id(1)

    @pl.when(j == 0)
    def _():
        acc_sc[...] = jnp.zeros_like(acc_sc)
        for lc in range(tm // LANES):
            s2_sc[:, :, lc] = s2_ref[:, :, :, lc * LANES:(lc + 1) * LANES]
            e2_sc[:, :, lc] = e2_ref[:, :, :, lc * LANES:(lc + 1) * LANES]

    rb = PEER_ROW_BLOCK_KEYS * PEER_N_KEYS
    tw = PEER_PIECE_TOKENS
    n_rb = PEER_CHUNK // rb
    pieces = [(r, c) for c in range(tm // tw) for r in range(n_rb)]

    def tile(t, carry):
        def scores(p):
            r, c = pieces[p]
            u_rows = pltpu.bitcast(u_ref[r * rb // 2:(r + 1) * rb // 2, :], BF16)
            h_cols = pltpu.bitcast(ht_ref[t, :, c * tw:(c + 1) * tw], BF16)
            return jnp.dot(u_rows, h_cols, preferred_element_type=F32)

        def gates(p, a_blk, k):
            r, c = pieces[p]
            ii = r * PEER_ROW_BLOCK_KEYS + k
            for lc in range(tw // LANES):
                chunk = c * (tw // LANES) + lc
                ls = slice(chunk * LANES, (chunk + 1) * LANES)
                gate = None
                for h in range(PEER_HEADS):
                    theta = th_ref[0, t, h, ii:ii + 1, ls]
                    e1 = e1_ref[0, t, h, ii:ii + 1, ls]
                    term = jnp.where(s2_sc[t, h, chunk] >= theta, e1 * e2_sc[t, h, chunk], 0.0)
                    gate = term if gate is None else gate + term
                a = a_blk[k * PEER_N_KEYS:(k + 1) * PEER_N_KEYS, lc * LANES:(lc + 1) * LANES]
                gelu2 = a * (1.0 + lax.erf(a * (1.0 / math.sqrt(2.0))))
                act_sc[c, ii * PEER_N_KEYS:(ii + 1) * PEER_N_KEYS, lc * LANES:(lc + 1) * LANES] = (
                    gelu2 * gate).astype(BF16)

        def mix(c):
            acc_sc[t, c] += jnp.dot(pltpu.bitcast(vt_ref[0], BF16), act_sc[c], preferred_element_type=F32)

        a_next = scores(0)
        for p in range(len(pieces)):
            r, c = pieces[p]
            a_cur = a_next
            if p + 1 < len(pieces):
                a_next = scores(p + 1)
            for k in range(PEER_ROW_BLOCK_KEYS):
                if k == PEER_ROW_BLOCK_KEYS // 2 and r == 0 and c > 0:
                    mix(c - 1)
                gates(p, a_cur, k)
        mix(tm // tw - 1)
        return carry

    lax.fori_loop(0, n_tiles, tile, 0)

    @pl.when(j == pl.num_programs(1) - 1)
    def _():
        for t in range(n_tiles):
            rows = slice(t * tm, (t + 1) * tm)
            mixed_t = jnp.concatenate([acc_sc[t, c] for c in range(tm // tw)], axis=1)
            xo_ref[rows, :] = x_ref[rows, :] + g2_ref[0] * mixed_t.T


def _peer_mix(h_t, u_bf, v_t, theta, e1, s2, e2, x2, gate2, B, S):
    T, D = x2.shape
    tm = PEER_TOKEN_TILE
    n_tiles = min(PEER_TILES_PER_STEP, S // tm)
    span = n_tiles * tm
    steps_per_b = S // span
    tok_spec = pl.BlockSpec((n_tiles, PEER_HEADS, PEER_N_KEYS, tm), lambda i, j: (i, 0, 0, 0),
                            pipeline_mode=pl.Buffered(1))
    row_spec = pl.BlockSpec((1, n_tiles, PEER_HEADS, PEER_CHUNK_KEYS, tm), lambda i, j: (j, i, 0, 0, 0))
    return pl.pallas_call(
        functools.partial(_peer_mix_kernel, tm=tm, n_tiles=n_tiles),
        grid=(T // span, PEER_N_EXPERTS // PEER_CHUNK),
        in_specs=[
            pl.BlockSpec((n_tiles, D // 2, tm), lambda i, j: (i, 0, 0)),
            pl.BlockSpec((PEER_CHUNK // 2, D), lambda i, j: (j, 0)),
            pl.BlockSpec((1, D // 2, PEER_CHUNK), lambda i, j: (j, 0, 0)),
            row_spec, row_spec, tok_spec, tok_spec,
            pl.BlockSpec((span, D), lambda i, j: (i, 0)),
            pl.BlockSpec((1, 1, D), lambda i, j: (i // steps_per_b, 0, 0)),
        ],
        out_specs=pl.BlockSpec((span, D), lambda i, j: (i, 0)),
        out_shape=jax.ShapeDtypeStruct((T, D), F32),
        scratch_shapes=[pltpu.VMEM((n_tiles, tm // PEER_PIECE_TOKENS, D, PEER_PIECE_TOKENS), F32),
                        pltpu.VMEM((tm // PEER_PIECE_TOKENS, PEER_CHUNK, PEER_PIECE_TOKENS), BF16),
                        pltpu.VMEM((n_tiles, PEER_HEADS, tm // LANES, PEER_N_KEYS, LANES), F32),
                        pltpu.VMEM((n_tiles, PEER_HEADS, tm // LANES, PEER_N_KEYS, LANES), F32)],
        compiler_params=_params("parallel", "arbitrary"),
        name="peer_mix",
    )(h_t, u_bf, v_t, theta, e1, s2, e2, x2, gate2.reshape(B, 1, D))


def _final_norm_kernel(x_ref, g_ref, o_ref):
    x = x_ref[...]
    ms = jnp.mean(x * x, axis=-1, keepdims=True)
    o_ref[...] = x * lax.rsqrt(ms + NORM_EPS) * g_ref[...]


def _final_norm(x2, g):
    T, D = x2.shape
    tm = min(1024, T)
    return pl.pallas_call(
        _final_norm_kernel,
        grid=(T // tm,),
        in_specs=[pl.BlockSpec((tm, D), lambda i: (i, 0)), pl.BlockSpec((1, D), lambda i: (0, 0))],
        out_specs=pl.BlockSpec((tm, D), lambda i: (i, 0)),
        out_shape=jax.ShapeDtypeStruct((T, D), F32),
        compiler_params=_params("parallel"),
        name="final_norm",
    )(x2, g.reshape(1, D))


def kernel(x, c, norm1_g, norm2_g, ada_w, ada_b, w_in, gate_b, diff_lambda, diff_subln_g,
           w_branch_sb, w_branch_diff, w_out, peer_wq, peer_subkeys, peer_u, peer_v, final_g):
    B, S, D = x.shape
    assert D == D_MODEL and S % LANES == 0
    x2 = x.reshape(B * S, D)
    mod = _modulation(c, ada_w, ada_b)
    rope = _rope_lane_tables(S)
    for l in range(DEPTH):
        lam_init = 0.8 - 0.6 * math.exp(-0.3 * l)
        shift1, scale1, gate1, shift2, scale2, gate2 = (mod[l, :, n] for n in range(6))
        proj = _input_projection(x2, norm1_g[l], scale1, shift1, w_in[l].astype(BF16), rope, B, S)
        y_sb = _sb_attention(proj, B, S)
        y_df = _diff_attention(proj, diff_lambda[l], diff_subln_g[l], lam_init, B, S)
        x2, h2 = _merge_project(y_sb, y_df, proj, gate_b[l], w_branch_sb[l].astype(BF16),
                                w_branch_diff[l].astype(BF16), w_out[l].astype(BF16), x2, gate1,
                                norm2_g[l], scale2, shift2, B, S)
        wq_t = peer_wq[l].T.astype(BF16).reshape(PEER_HEADS, PEER_QUERY_DIM, D)
        h_t, theta, e1, s2, e2 = _peer_select(h2, wq_t, peer_subkeys[l].astype(BF16))
        v_t = _pack_row_pairs(jnp.swapaxes(peer_v[l].astype(BF16).reshape(-1, PEER_CHUNK, D), 1, 2))
        x2 = _peer_mix(h_t, _pack_row_pairs(peer_u[l].astype(BF16)), v_t,
                       theta, e1, s2, e2, x2, gate2, B, S)
    return _final_norm(x2, final_g).reshape(B, S, D)
```

```python
import functools
import math

import jax
import jax.numpy as jnp
from jax import lax
from jax.experimental import pallas as pl
from jax.experimental.pallas import tpu as pltpu

F32 = jnp.float32
BF16 = jnp.bfloat16

D_MODEL = 1024
DEPTH = 2
SB_HEADS = 8
SB_HEAD_DIM = 64
SB_WIDTH = SB_HEADS * SB_HEAD_DIM
DIFF_HEADS = 4
DIFF_QK_DIM = 64
DIFF_V_DIM = 2 * DIFF_QK_DIM
DIFF_QK_WIDTH = DIFF_HEADS * 2 * DIFF_QK_DIM
DIFF_V_WIDTH = DIFF_HEADS * DIFF_V_DIM
ROPE_THETA = 500000.0
ROT_DIM = DIFF_QK_DIM // 4
IN_COLS = 3 * SB_WIDTH + 2 * DIFF_QK_WIDTH + DIFF_V_WIDTH + 2 * D_MODEL
PEER_HEADS = 8
PEER_N_KEYS = 128
PEER_N_EXPERTS = PEER_N_KEYS * PEER_N_KEYS
PEER_TOPK = 16
PEER_QUERY_DIM = 256
PEER_CHUNK_KEYS = 8
PEER_CHUNK = PEER_CHUNK_KEYS * PEER_N_KEYS
PEER_TOKEN_TILE = 512
PEER_TILES_PER_STEP = 2
NORM_EPS = 1e-6

LANES = 128
VMEM_LIMIT = 56 * 1024 * 1024
NEG_INF = float("-inf")
LOG2_E = 1.4426950408889634

IN_TILE_N = 512
COL_SB_Q, COL_SB_K, COL_SB_V = 0, SB_WIDTH // LANES, 2 * SB_WIDTH // LANES
COL_D_Q = 3 * SB_WIDTH // LANES
COL_D_K = COL_D_Q + DIFF_QK_WIDTH // LANES
COL_D_V = COL_D_K + DIFF_QK_WIDTH // LANES
COL_G = (3 * SB_WIDTH + 2 * DIFF_QK_WIDTH + DIFF_V_WIDTH) // D_MODEL


def _params(*sem):
    return pltpu.CompilerParams(dimension_semantics=sem, vmem_limit_bytes=VMEM_LIMIT)


def _nt_dot(a, b):
    return lax.dot_general(a, b, (((1,), (1,)), ((), ())), preferred_element_type=F32)


def _mod_kernel(c_ref, w_ref, b_ref, o_ref):
    c = c_ref[...]
    c_act = c * jax.nn.sigmoid(c)
    o_ref[0] = jnp.dot(c_act, w_ref[0], preferred_element_type=F32,
                       precision=lax.Precision.HIGHEST) + b_ref[0]


def _modulation(c, ada_w, ada_b):
    B, D = c.shape
    rows = 8
    c_pad = jnp.pad(c, ((0, rows - B), (0, 0)))
    out = pl.pallas_call(
        _mod_kernel,
        grid=(DEPTH, 6),
        in_specs=[
            pl.BlockSpec((rows, D), lambda l, n: (0, 0)),
            pl.BlockSpec((1, D, D), lambda l, n: (l, 0, n)),
            pl.BlockSpec((1, 1, D), lambda l, n: (l, 0, n)),
        ],
        out_specs=pl.BlockSpec((1, rows, D), lambda l, n: (l, 0, n)),
        out_shape=jax.ShapeDtypeStruct((DEPTH, rows, 6 * D), F32),
        compiler_params=_params("arbitrary", "arbitrary"),
        name="adaln_mod",
    )(c_pad, ada_w, ada_b.reshape(DEPTH, 1, 6 * D))
    return out[:, :B].reshape(DEPTH, B, 6, D)


def _norm_mod(x, g, scale, shift):
    ms = jnp.mean(x * x, axis=-1, keepdims=True)
    return (x * lax.rsqrt(ms + NORM_EPS) * g) * (1.0 + scale) + shift


def _inproj_kernel(x_ref, g_ref, sc_ref, sh_ref, w_ref, ra_ref, rb_ref, rc_ref, o_ref):
    h = _norm_mod(x_ref[...], g_ref[...], sc_ref[0], sh_ref[0]).astype(BF16)
    half = ROT_DIM // 2
    for j in range(IN_COLS // IN_TILE_N):
        cols = slice(j * IN_TILE_N, (j + 1) * IN_TILE_N)
        r = jnp.dot(h, w_ref[:, cols], preferred_element_type=F32)
        first_lane_block = j * IN_TILE_N // LANES
        if first_lane_block in (COL_SB_Q, COL_D_Q):
            r = r * (LOG2_E / math.sqrt(SB_HEAD_DIM))
        if first_lane_block in (COL_D_Q, COL_D_K):
            ra, rb, rc = ra_ref[...], rb_ref[...], rc_ref[...]
            for blk in range(IN_TILE_N // LANES):
                t = r[:, blk * LANES:(blk + 1) * LANES]
                rot = t * ra + pltpu.roll(t, LANES - half, 1) * rb + pltpu.roll(t, half, 1) * rc
                o_ref[:, j * IN_TILE_N + blk * LANES:j * IN_TILE_N + (blk + 1) * LANES] = rot.astype(BF16)
        else:
            o_ref[:, cols] = r.astype(BF16)


def _rope_lane_tables(seq_len):
    pos = jnp.arange(seq_len, dtype=F32)
    inv_freq = 1.0 / (ROPE_THETA ** (jnp.arange(0, ROT_DIM, 2, dtype=F32) / ROT_DIM))
    ang = pos[:, None] * inv_freq[None, :]
    cos, sin = jnp.cos(ang), jnp.sin(ang)
    half = ROT_DIM // 2
    lane = jnp.arange(LANES) % DIFF_QK_DIM
    f = lane % half
    first, second = lane < half, (lane >= half) & (lane < ROT_DIM)
    ra = jnp.where((first | second)[None, :], cos[:, f], 1.0)
    rb = jnp.where(first[None, :], -sin[:, f], 0.0)
    rc = jnp.where(second[None, :], sin[:, f], 0.0)
    return ra, rb, rc


def _input_projection(x2, norm_g, scale, shift, w_in_bf, rope, B, S):
    T, D = x2.shape
    tm = min(512, S)
    tiles_per_b = S // tm
    ra, rb, rc = rope
    rope_spec = pl.BlockSpec((tm, LANES), lambda i: (i % tiles_per_b, 0))
    mod_spec = pl.BlockSpec((1, 1, D), lambda i: (i // tiles_per_b, 0, 0))
    assert COL_D_K - COL_D_Q == IN_TILE_N // LANES and COL_D_V - COL_D_K == IN_TILE_N // LANES
    return pl.pallas_call(
        _inproj_kernel,
        grid=(T // tm,),
        in_specs=[
            pl.BlockSpec((tm, D), lambda i: (i, 0)),
            pl.BlockSpec((1, D), lambda i: (0, 0)),
            mod_spec, mod_spec,
            pl.BlockSpec((D, IN_COLS), lambda i: (0, 0)),
            rope_spec, rope_spec, rope_spec,
        ],
        out_specs=pl.BlockSpec((tm, IN_COLS), lambda i: (i, 0)),
        out_shape=jax.ShapeDtypeStruct((T, IN_COLS), BF16),
        compiler_params=_params("parallel"),
        name="in_proj",
    )(x2, norm_g.reshape(1, D), scale.reshape(B, 1, D), shift.reshape(B, 1, D), w_in_bf, ra, rb, rc)


def _sb_kernel(q_ref, k_ref, v_ref, o_ref, acc_sc, cs_sc, *, tb, n_pairs):
    i = pl.program_id(2)
    n_heads = 2 * n_pairs
    lane = lax.broadcasted_iota(jnp.int32, (tb, LANES), 1)
    q_heads = []
    for p in range(n_pairs):
        q = q_ref[:, p * LANES:(p + 1) * LANES]
        zero = jnp.zeros_like(q)
        q_heads += [jnp.where(lane < SB_HEAD_DIM, q, zero), jnp.where(lane >= SB_HEAD_DIM, q, zero)]
    row = lax.broadcasted_iota(jnp.int32, (tb, tb), 0)
    col = lax.broadcasted_iota(jnp.int32, (tb, tb), 1)
    strictly_before = col < row
    suffix_ones = jnp.where(row >= col, 1.0, 0.0).astype(BF16)

    acc_sc[...] = jnp.zeros_like(acc_sc)
    cs_sc[...] = jnp.zeros_like(cs_sc)

    def block(j, masked):
        start = pl.multiple_of(j * tb, tb)
        kb = k_ref[pl.ds(start, tb), :]
        vb = v_ref[pl.ds(start, tb), :]
        ys = [_nt_dot(q_heads[h], kb[:, (h // 2) * LANES:(h // 2 + 1) * LANES]) for h in range(n_heads)]
        costs = []
        for y in ys:
            neg_abs = lax.bitcast_convert_type(
                lax.bitcast_convert_type(y, jnp.uint32) | jnp.uint32(0x80000000), F32)
            cost = jnp.maximum(y, 0.0) + jnp.log2(1.0 + jnp.exp2(neg_abs))
            if masked:
                cost = jnp.where(strictly_before, cost, 0.0)
            costs.append(cost)
        sums = [jnp.dot(cost.astype(BF16), suffix_ones, preferred_element_type=F32) for cost in costs]
        for h in range(n_heads):
            later = cs_sc[h]
            w = jnp.exp2(ys[h] - sums[h] - later)
            if masked:
                w = jnp.where(strictly_before, w, 0.0)
            acc_sc[h] += jnp.dot(w.astype(BF16), vb[:, (h // 2) * LANES:(h // 2 + 1) * LANES],
                                 preferred_element_type=F32)
            cs_sc[h] = later + sums[h][:, 0:1]

    block(i, True)

    def body(jj, carry):
        block(i - 1 - jj, False)
        return carry

    lax.fori_loop(0, i, body, 0)
    for p in range(n_pairs):
        o_ref[:, p * LANES:(p + 1) * LANES] = jnp.where(
            lane < SB_HEAD_DIM, acc_sc[2 * p], acc_sc[2 * p + 1]).astype(BF16)


SB_PAIRS_PER_STEP = 4


def _sb_attention(proj, B, S):
    T = proj.shape[0]
    tb = min(256, S)
    nq = S // tb
    n_pairs = SB_PAIRS_PER_STEP
    width = n_pairs * LANES
    groups = SB_WIDTH // width
    return pl.pallas_call(
        functools.partial(_sb_kernel, tb=tb, n_pairs=n_pairs),
        grid=(B, groups, nq),
        in_specs=[
            pl.BlockSpec((tb, width), lambda b, p, i: (b * nq + i, COL_SB_Q // n_pairs + p)),
            pl.BlockSpec((S, width), lambda b, p, i: (b, COL_SB_K // n_pairs + p)),
            pl.BlockSpec((S, width), lambda b, p, i: (b, COL_SB_V // n_pairs + p)),
        ],
        out_specs=pl.BlockSpec((tb, width), lambda b, p, i: (b * nq + i, p)),
        out_shape=jax.ShapeDtypeStruct((T, SB_WIDTH), BF16),
        scratch_shapes=[pltpu.VMEM((2 * n_pairs, tb, LANES), F32),
                        pltpu.VMEM((2 * n_pairs, tb, 1), F32)],
        compiler_params=_params("parallel", "parallel", "arbitrary"),
        name="sb_attention",
    )(proj, proj, proj)


def _diff_kernel(lp_ref, g_ref, q_ref, k_ref, v_ref, o_ref, m_sc, acc_sc, *, tq, tk, lam_init):
    i = pl.program_id(1)
    lane = lax.broadcasted_iota(jnp.int32, (tq, LANES), 1)
    q_maps = []
    for h in range(DIFF_HEADS):
        q = q_ref[:, h * LANES:(h + 1) * LANES]
        zero = jnp.zeros_like(q)
        q_maps += [jnp.where(lane < DIFF_QK_DIM, q, zero), jnp.where(lane >= DIFF_QK_DIM, q, zero)]
    n_chains = len(q_maps)
    m_sc[...] = jnp.full_like(m_sc, NEG_INF)
    acc_sc[...] = jnp.zeros_like(acc_sc)
    q_pos = i * tq + lax.broadcasted_iota(jnp.int32, (tq, tk), 0)
    k_off = lax.broadcasted_iota(jnp.int32, (tq, tk), 1)
    ones = jnp.ones((tk, LANES), BF16)

    def block(j, masked):
        start = pl.multiple_of(j * tk, tk)
        kb = k_ref[pl.ds(start, tk), :]
        vb = v_ref[pl.ds(start, tk), :]
        head_cols = lambda c: slice((c // 2) * LANES, (c // 2 + 1) * LANES)
        scores = [_nt_dot(q_maps[c], kb[:, head_cols(c)]) for c in range(n_chains)]
        if masked:
            visible = j * tk + k_off <= q_pos
            scores = [jnp.where(visible, s, NEG_INF) for s in scores]
        m_prev = [m_sc[c] for c in range(n_chains)]
        m_new = [jnp.maximum(m_prev[c], jnp.max(scores[c], axis=1, keepdims=True)) for c in range(n_chains)]
        probs = [jnp.exp2(scores[c] - jnp.concatenate([m_new[c]] * (tk // LANES), axis=1))
                 for c in range(n_chains)]
        for c in range(n_chains):
            alpha = jnp.exp2(m_prev[c] - m_new[c])
            v_ext = jnp.concatenate([vb[:, head_cols(c)], ones], axis=1)
            acc_sc[c] = (jnp.concatenate([alpha, alpha], axis=1) * acc_sc[c]
                         + jnp.dot(probs[c].astype(BF16), v_ext, preferred_element_type=F32))
            m_sc[c] = m_new[c]

    n_full = (i * tq) // tk

    def body(j, carry):
        block(j, False)
        return carry

    lax.fori_loop(0, n_full, body, 0)
    block(n_full, True)

    lp = lp_ref[...]
    lam = (jnp.exp(jnp.sum(lp[0:1] * lp[1:2], axis=1, keepdims=True))
           - jnp.exp(jnp.sum(lp[2:3] * lp[3:4], axis=1, keepdims=True)) + lam_init)
    for h in range(DIFF_HEADS):
        a1, a2 = acc_sc[2 * h], acc_sc[2 * h + 1]
        o = a1[:, :LANES] / a1[:, LANES:] - lam * (a2[:, :LANES] / a2[:, LANES:])
        ms = jnp.mean(o * o, axis=-1, keepdims=True)
        y = (o * lax.rsqrt(ms + NORM_EPS) * g_ref[...]) * (1.0 - lam_init)
        o_ref[:, h * LANES:(h + 1) * LANES] = y.astype(BF16)


def _diff_attention(proj, lam_params, subln_g, lam_init, B, S):
    T = proj.shape[0]
    tq = min(256, S)
    tk = min(512, S)
    nq = S // tq
    n_chains = 2 * DIFF_HEADS
    return pl.pallas_call(
        functools.partial(_diff_kernel, tq=tq, tk=tk, lam_init=lam_init),
        grid=(B, nq),
        in_specs=[
            pl.BlockSpec((4, DIFF_QK_DIM), lambda b, i: (0, 0)),
            pl.BlockSpec((1, DIFF_V_DIM), lambda b, i: (0, 0)),
            pl.BlockSpec((tq, DIFF_QK_WIDTH), lambda b, i: (b * nq + i, COL_D_Q * LANES // DIFF_QK_WIDTH)),
            pl.BlockSpec((S, DIFF_QK_WIDTH), lambda b, i: (b, COL_D_K * LANES // DIFF_QK_WIDTH)),
            pl.BlockSpec((S, DIFF_V_WIDTH), lambda b, i: (b, COL_D_V * LANES // DIFF_V_WIDTH)),
        ],
        out_specs=pl.BlockSpec((tq, DIFF_V_WIDTH), lambda b, i: (b * nq + i, 0)),
        out_shape=jax.ShapeDtypeStruct((T, DIFF_V_WIDTH), BF16),
        scratch_shapes=[pltpu.VMEM((n_chains, tq, LANES), F32),
                        pltpu.VMEM((n_chains, tq, 2 * LANES), F32)],
        compiler_params=_params("parallel", "arbitrary"),
        name="diff_attention",
    )(lam_params, subln_g.reshape(1, DIFF_V_DIM), proj, proj, proj)


def _merge_kernel(ysb_ref, ydf_ref, gs_ref, gd_ref, gb_ref, wbs_ref, wbd_ref, wo_ref,
                  x_ref, g1_ref, n2_ref, sc2_ref, sh2_ref, xo_ref, h2_ref):
    D = x_ref.shape[1]
    a = jnp.dot(ysb_ref[...], wbs_ref[...], preferred_element_type=F32)
    b = jnp.dot(ydf_ref[...], wbd_ref[...], preferred_element_type=F32)
    gb = gb_ref[...]
    g_sb = jax.nn.sigmoid(gs_ref[...].astype(F32) + gb[:, :D])
    g_df = jax.nn.sigmoid(gd_ref[...].astype(F32) + gb[:, D:])
    merged = g_sb * a + g_df * b
    y = jnp.dot(merged.astype(BF16), wo_ref[...], preferred_element_type=F32)
    xn = x_ref[...] + g1_ref[0] * y
    xo_ref[...] = xn
    h2_ref[...] = _norm_mod(xn, n2_ref[...], sc2_ref[0], sh2_ref[0]).astype(BF16)


def _merge_project(y_sb, y_df, proj, gate_b, wbs, wbd, wo, x2, gate1, norm2_g, scale2, shift2, B, S):
    T, D = x2.shape
    tm = min(512, S)
    tiles_per_b = S // tm
    mod_spec = pl.BlockSpec((1, 1, D), lambda i: (i // tiles_per_b, 0, 0))
    const = lambda shape: pl.BlockSpec(shape, lambda i: (0, 0))
    return pl.pallas_call(
        _merge_kernel,
        grid=(T // tm,),
        in_specs=[
            pl.BlockSpec((tm, SB_WIDTH), lambda i: (i, 0)),
            pl.BlockSpec((tm, DIFF_V_WIDTH), lambda i: (i, 0)),
            pl.BlockSpec((tm, D), lambda i: (i, COL_G)),
            pl.BlockSpec((tm, D), lambda i: (i, COL_G + 1)),
            const((1, 2 * D)),
            const((SB_WIDTH, D)), const((DIFF_V_WIDTH, D)), const((D, D)),
            pl.BlockSpec((tm, D), lambda i: (i, 0)),
            mod_spec,
            const((1, D)),
            mod_spec, mod_spec,
        ],
        out_specs=[pl.BlockSpec((tm, D), lambda i: (i, 0)), pl.BlockSpec((tm, D), lambda i: (i, 0))],
        out_shape=[jax.ShapeDtypeStruct((T, D), F32), jax.ShapeDtypeStruct((T, D), BF16)],
        compiler_params=_params("parallel"),
        name="merge_out_proj",
    )(y_sb, y_df, proj, proj, gate_b.reshape(1, 2 * D), wbs, wbd, wo, x2,
      gate1.reshape(B, 1, D), norm2_g.reshape(1, D), scale2.reshape(B, 1, D), shift2.reshape(B, 1, D))


def _sort16_network():
    def merge(lo, hi, r):
        step = r * 2
        if step < hi - lo:
            yield from merge(lo, hi, step)
            yield from merge(lo + r, hi, step)
            yield from [(i, i + r) for i in range(lo + r, hi - r, step)]
        else:
            yield (lo, lo + r)

    def sort(lo, hi):
        if hi - lo >= 1:
            mid = lo + (hi - lo) // 2
            yield from sort(lo, mid)
            yield from sort(mid + 1, hi)
            yield from merge(lo, hi, 1)

    return list(sort(0, 15))


def _top_values(scores, k):
    groups = scores.shape[0] // 8
    assert groups == 16 and k <= groups + 1
    v = [scores[8 * g:8 * (g + 1)] for g in range(groups)]
    for a, b in _sort16_network():
        v[a], v[b] = jnp.maximum(v[a], v[b]), jnp.minimum(v[a], v[b])
    v.append(jnp.full_like(v[0], NEG_INF))
    vals = []
    for r in range(k):
        m = jnp.max(v[0], axis=0, keepdims=True)
        vals.append(m)
        still_needed = k - 1 - r
        if still_needed:
            hit = v[0] == m
            for d in range(min(still_needed, groups)):
                v[d] = jnp.where(hit, v[d + 1], v[d])
    return vals


def _candidate_sums(v1, v1_all, v2, v2_all, row_id):
    assert PEER_TOPK == 16
    lo2 = v2_all[0:8]
    hi1 = v1_all[8:16]
    shift = lambda x, n: pltpu.roll(x, n, 0)
    groups = [v1[0] + v2_all[0:8], v1[0] + v2_all[8:16], v1[0] + v2_all[16:24],
              v1[1] + lo2]
    groups.append(jnp.where(row_id < 5, v1[2] + lo2, v1[4] + shift(lo2, 5)))
    groups.append(jnp.where(row_id < 4, v1[3] + lo2,
                            jnp.where(row_id < 6, v1[5] + shift(lo2, 4),
                                      v1[6] + shift(lo2, 6))))
    tail = shift(hi1, 2) + v2[0]
    groups.append(jnp.where(row_id < 2, v1[7] + lo2, tail))
    groups.append(jnp.where(row_id < 2, tail,
                            jnp.where(row_id == 2, v1[16] + v2[0], NEG_INF)))
    return groups


def _peer_select_kernel(h_ref, wq_ref, sk_ref, ht_ref, th_ref, e1_ref, e2_ref, *, tm):
    hb = h_ref[...]
    ht_ref[0] = hb.T
    K = PEER_TOPK + 1
    row_id = lax.broadcasted_iota(jnp.int32, (8, LANES), 0)
    pad_rows = jnp.full((24 - K, LANES), NEG_INF, F32)

    def head(h, carry):
        q_t = _nt_dot(wq_ref[h], hb).astype(BF16)
        half = PEER_QUERY_DIM // 2
        sc = [jnp.dot(sk_ref[h, c], q_t[c * half:(c + 1) * half], preferred_element_type=F32)
              for c in range(2)]
        for lc in range(tm // LANES):
            ls = slice(lc * LANES, (lc + 1) * LANES)
            s1, s2 = sc[0][:, ls], sc[1][:, ls]
            v1 = _top_values(s1, K)
            v2 = _top_values(s2, K)
            v2_all = jnp.concatenate(v2 + [pad_rows], axis=0)
            v1_all = jnp.concatenate(v1 + [pad_rows], axis=0)
            cands = _candidate_sums(v1, v1_all, v2, v2_all, row_id)
            best = cands[0][0:1]
            z_sum = jnp.zeros_like(best)
            last = best
            for r in range(K):
                stacked = cands[0]
                for ca in cands[1:]:
                    stacked = jnp.maximum(stacked, ca)
                m = jnp.max(stacked, axis=0, keepdims=True)
                if r < PEER_TOPK:
                    z_sum = z_sum + jnp.exp(m - best)
                    last = m
                    cands = [jnp.where(ca == m, NEG_INF, ca) for ca in cands]
            tau = 0.5 * (last + m)
            half_inv_z = 0.5 / z_sum
            chunked = (PEER_N_KEYS // PEER_CHUNK_KEYS, PEER_CHUNK_KEYS, LANES)
            th_ref[:, 0, h, :, ls] = jnp.exp(tau - s1 - v2[0]).reshape(chunked)
            e1_ref[:, 0, h, :, ls] = (jnp.exp(s1 - v1[0]) * half_inv_z).reshape(chunked)
            e2_ref[0, h, :, ls] = jnp.exp(s2 - v2[0])
        return carry

    lax.fori_loop(0, PEER_HEADS, head, 0)


def _peer_select(h2, wq_t, subkeys_bf):
    T, D = h2.shape
    tm = PEER_TOKEN_TILE
    n_tiles = T // tm
    tok_spec = pl.BlockSpec((1, PEER_HEADS, PEER_N_KEYS, tm), lambda i: (i, 0, 0, 0))
    tok_shape = jax.ShapeDtypeStruct((n_tiles, PEER_HEADS, PEER_N_KEYS, tm), F32)
    n_chunks = PEER_N_KEYS // PEER_CHUNK_KEYS
    row_spec = pl.BlockSpec((n_chunks, 1, PEER_HEADS, PEER_CHUNK_KEYS, tm), lambda i: (0, i, 0, 0, 0))
    row_shape = jax.ShapeDtypeStruct((n_chunks, n_tiles, PEER_HEADS, PEER_CHUNK_KEYS, tm), F32)
    return pl.pallas_call(
        functools.partial(_peer_select_kernel, tm=tm),
        grid=(n_tiles,),
        in_specs=[
            pl.BlockSpec((tm, D), lambda i: (i, 0)),
            pl.BlockSpec((PEER_HEADS, PEER_QUERY_DIM, D), lambda i: (0, 0, 0)),
            pl.BlockSpec((PEER_HEADS, 2, PEER_N_KEYS, PEER_QUERY_DIM // 2), lambda i: (0, 0, 0, 0)),
        ],
        out_specs=[
            pl.BlockSpec((1, D, tm), lambda i: (i, 0, 0)),
            row_spec, row_spec, tok_spec,
        ],
        out_shape=[
            jax.ShapeDtypeStruct((n_tiles, D, tm), BF16),
            row_shape, row_shape, tok_shape,
        ],
        compiler_params=_params("parallel"),
        name="peer_select",
    )(h2, wq_t, subkeys_bf)


PEER_ROW_BLOCK_KEYS = 4
PEER_PIECE_TOKENS = 2 * LANES


def _peer_mix_kernel(ht_ref, u_ref, vt_ref, th_ref, e1_ref, e2_ref, x_ref, g2_ref,
                     xo_ref, acc_sc, act_sc, e2_sc, *, tm, n_tiles):
    j = pl.program_id(1)

    @pl.when(j == 0)
    def _():
        acc_sc[...] = jnp.zeros_like(acc_sc)
        for lc in range(tm // LANES):
            e2_sc[:, :, lc] = e2_ref[:, :, :, lc * LANES:(lc + 1) * LANES]

    rb = PEER_ROW_BLOCK_KEYS * PEER_N_KEYS
    tw = PEER_PIECE_TOKENS
    n_rb = PEER_CHUNK // rb
    pieces = [(r, c) for c in range(tm // tw) for r in range(n_rb)]

    def tile(t, carry):
        def scores(p):
            r, c = pieces[p]
            return jnp.dot(u_ref[r * rb:(r + 1) * rb, :], ht_ref[t, :, c * tw:(c + 1) * tw],
                           preferred_element_type=F32)

        def gates(p, a_blk, k):
            r, c = pieces[p]
            ii = r * PEER_ROW_BLOCK_KEYS + k
            for lc in range(tw // LANES):
                chunk = c * (tw // LANES) + lc
                ls = slice(chunk * LANES, (chunk + 1) * LANES)
                gate = None
                for h in range(PEER_HEADS):
                    theta = th_ref[0, t, h, ii:ii + 1, ls]
                    e1 = e1_ref[0, t, h, ii:ii + 1, ls]
                    e2 = e2_sc[t, h, chunk]
                    term = jnp.where(e2 >= theta, e1 * e2, 0.0)
                    gate = term if gate is None else gate + term
                a = a_blk[k * PEER_N_KEYS:(k + 1) * PEER_N_KEYS, lc * LANES:(lc + 1) * LANES]
                gelu2 = a * (1.0 + lax.erf(a * (1.0 / math.sqrt(2.0))))
                act_sc[c, ii * PEER_N_KEYS:(ii + 1) * PEER_N_KEYS, lc * LANES:(lc + 1) * LANES] = (
                    gelu2 * gate).astype(BF16)

        def mix(c):
            acc_sc[t, c] += jnp.dot(vt_ref[0], act_sc[c], preferred_element_type=F32)

        a_next = scores(0)
        for p in range(len(pieces)):
            r, c = pieces[p]
            a_cur = a_next
            if p + 1 < len(pieces):
                a_next = scores(p + 1)
            for k in range(PEER_ROW_BLOCK_KEYS):
                if k == PEER_ROW_BLOCK_KEYS // 2 and r == 0 and c > 0:
                    mix(c - 1)
                gates(p, a_cur, k)
        mix(tm // tw - 1)
        return carry

    lax.fori_loop(0, n_tiles, tile, 0)

    @pl.when(j == pl.num_programs(1) - 1)
    def _():
        for t in range(n_tiles):
            rows = slice(t * tm, (t + 1) * tm)
            mixed_t = jnp.concatenate([acc_sc[t, c] for c in range(tm // tw)], axis=1)
            xo_ref[rows, :] = x_ref[rows, :] + g2_ref[0] * mixed_t.T


def _peer_mix(h_t, u_bf, v_t, theta, e1, e2, x2, gate2, B, S):
    T, D = x2.shape
    tm = PEER_TOKEN_TILE
    n_tiles = min(PEER_TILES_PER_STEP, S // tm)
    span = n_tiles * tm
    steps_per_b = S // span
    tok_spec = pl.BlockSpec((n_tiles, PEER_HEADS, PEER_N_KEYS, tm), lambda i, j: (i, 0, 0, 0),
                            pipeline_mode=pl.Buffered(1))
    row_spec = pl.BlockSpec((1, n_tiles, PEER_HEADS, PEER_CHUNK_KEYS, tm), lambda i, j: (j, i, 0, 0, 0))
    return pl.pallas_call(
        functools.partial(_peer_mix_kernel, tm=tm, n_tiles=n_tiles),
        grid=(T // span, PEER_N_EXPERTS // PEER_CHUNK),
        in_specs=[
            pl.BlockSpec((n_tiles, D, tm), lambda i, j: (i, 0, 0)),
            pl.BlockSpec((PEER_CHUNK, D), lambda i, j: (j, 0)),
            pl.BlockSpec((1, D, PEER_CHUNK), lambda i, j: (j, 0, 0)),
            row_spec, row_spec, tok_spec,
            pl.BlockSpec((span, D), lambda i, j: (i, 0)),
            pl.BlockSpec((1, 1, D), lambda i, j: (i // steps_per_b, 0, 0)),
        ],
        out_specs=pl.BlockSpec((span, D), lambda i, j: (i, 0)),
        out_shape=jax.ShapeDtypeStruct((T, D), F32),
        scratch_shapes=[pltpu.VMEM((n_tiles, tm // PEER_PIECE_TOKENS, D, PEER_PIECE_TOKENS), F32),
                        pltpu.VMEM((tm // PEER_PIECE_TOKENS, PEER_CHUNK, PEER_PIECE_TOKENS), BF16),
                        pltpu.VMEM((n_tiles, PEER_HEADS, tm // LANES, PEER_N_KEYS, LANES), F32)],
        compiler_params=_params("parallel", "arbitrary"),
        name="peer_mix",
    )(h_t, u_bf, v_t, theta, e1, e2, x2, gate2.reshape(B, 1, D))


def _final_norm_kernel(x_ref, g_ref, o_ref):
    x = x_ref[...]
    ms = jnp.mean(x * x, axis=-1, keepdims=True)
    o_ref[...] = x * lax.rsqrt(ms + NORM_EPS) * g_ref[...]


def _final_norm(x2, g):
    T, D = x2.shape
    tm = min(1024, T)
    return pl.pallas_call(
        _final_norm_kernel,
        grid=(T // tm,),
        in_specs=[pl.BlockSpec((tm, D), lambda i: (i, 0)), pl.BlockSpec((1, D), lambda i: (0, 0))],
        out_specs=pl.BlockSpec((tm, D), lambda i: (i, 0)),
        out_shape=jax.ShapeDtypeStruct((T, D), F32),
        compiler_params=_params("parallel"),
        name="final_norm",
    )(x2, g.reshape(1, D))


def kernel(x, c, norm1_g, norm2_g, ada_w, ada_b, w_in, gate_b, diff_lambda, diff_subln_g,
           w_branch_sb, w_branch_diff, w_out, peer_wq, peer_subkeys, peer_u, peer_v, final_g):
    B, S, D = x.shape
    assert D == D_MODEL and S % LANES == 0
    x2 = x.reshape(B * S, D)
    mod = _modulation(c, ada_w, ada_b)
    rope = _rope_lane_tables(S)
    for l in range(DEPTH):
        lam_init = 0.8 - 0.6 * math.exp(-0.3 * l)
        shift1, scale1, gate1, shift2, scale2, gate2 = (mod[l, :, n] for n in range(6))
        proj = _input_projection(x2, norm1_g[l], scale1, shift1, w_in[l].astype(BF16), rope, B, S)
        y_sb = _sb_attention(proj, B, S)
        y_df = _diff_attention(proj, diff_lambda[l], diff_subln_g[l], lam_init, B, S)
        x2, h2 = _merge_project(y_sb, y_df, proj, gate_b[l], w_branch_sb[l].astype(BF16),
                                w_branch_diff[l].astype(BF16), w_out[l].astype(BF16), x2, gate1,
                                norm2_g[l], scale2, shift2, B, S)
        wq_t = peer_wq[l].T.astype(BF16).reshape(PEER_HEADS, PEER_QUERY_DIM, D)
        h_t, theta, e1, e2 = _peer_select(h2, wq_t, peer_subkeys[l].astype(BF16))
        v_t = jnp.swapaxes(peer_v[l].astype(BF16).reshape(-1, PEER_CHUNK, D), 1, 2)
        x2 = _peer_mix(h_t, peer_u[l].astype(BF16), v_t,
                       theta, e1, e2, x2, gate2, B, S)
    return _final_norm(x2, final_g).reshape(B, S, D)
```

```python
import functools
import math

import jax
import jax.numpy as jnp
from jax import lax
from jax.experimental import pallas as pl
from jax.experimental.pallas import tpu as pltpu

F32 = jnp.float32
BF16 = jnp.bfloat16

D_MODEL = 1024
DEPTH = 2
SB_HEADS = 8
SB_HEAD_DIM = 64
SB_WIDTH = SB_HEADS * SB_HEAD_DIM
DIFF_HEADS = 4
DIFF_QK_DIM = 64
DIFF_V_DIM = 2 * DIFF_QK_DIM
DIFF_QK_WIDTH = DIFF_HEADS * 2 * DIFF_QK_DIM
DIFF_V_WIDTH = DIFF_HEADS * DIFF_V_DIM
ROPE_THETA = 500000.0
ROT_DIM = DIFF_QK_DIM // 4
IN_COLS = 3 * SB_WIDTH + 2 * DIFF_QK_WIDTH + DIFF_V_WIDTH + 2 * D_MODEL
PEER_HEADS = 8
PEER_N_KEYS = 128
PEER_N_EXPERTS = PEER_N_KEYS * PEER_N_KEYS
PEER_TOPK = 16
PEER_QUERY_DIM = 256
PEER_CHUNK_KEYS = 8
PEER_CHUNK = PEER_CHUNK_KEYS * PEER_N_KEYS
PEER_TOKEN_TILE = 512
PEER_TILES_PER_STEP = 2
NORM_EPS = 1e-6

LANES = 128
SUBLANES = 8
VMEM_LIMIT = 56 * 1024 * 1024
NEG_INF = float("-inf")
LOG2_E = 1.4426950408889634

IN_TILE_N = 512
COL_SB_Q, COL_SB_K, COL_SB_V = 0, SB_WIDTH // LANES, 2 * SB_WIDTH // LANES
COL_D_Q = 3 * SB_WIDTH // LANES
COL_D_K = COL_D_Q + DIFF_QK_WIDTH // LANES
COL_D_V = COL_D_K + DIFF_QK_WIDTH // LANES
COL_G = (3 * SB_WIDTH + 2 * DIFF_QK_WIDTH + DIFF_V_WIDTH) // D_MODEL


def _params(*sem):
    return pltpu.CompilerParams(dimension_semantics=sem, vmem_limit_bytes=VMEM_LIMIT)


def _nt_dot(a, b):
    return lax.dot_general(a, b, (((1,), (1,)), ((), ())), preferred_element_type=F32)


def _mod_kernel(c_ref, w_ref, b_ref, o_ref):
    c = c_ref[...]
    c_act = c * jax.nn.sigmoid(c)
    o_ref[0] = jnp.dot(c_act, w_ref[0], preferred_element_type=F32,
                       precision=lax.Precision.HIGHEST) + b_ref[0]


def _modulation(c, ada_w, ada_b):
    B, D = c.shape
    rows = SUBLANES
    c_pad = jnp.pad(c, ((0, rows - B), (0, 0)))
    out = pl.pallas_call(
        _mod_kernel,
        grid=(DEPTH, 6),
        in_specs=[
            pl.BlockSpec((rows, D), lambda l, n: (0, 0)),
            pl.BlockSpec((1, D, D), lambda l, n: (l, 0, n)),
            pl.BlockSpec((1, 1, D), lambda l, n: (l, 0, n)),
        ],
        out_specs=pl.BlockSpec((1, rows, D), lambda l, n: (l, 0, n)),
        out_shape=jax.ShapeDtypeStruct((DEPTH, rows, 6 * D), F32),
        compiler_params=_params("arbitrary", "arbitrary"),
        name="adaln_mod",
    )(c_pad, ada_w, ada_b.reshape(DEPTH, 1, 6 * D))
    return out[:, :B].reshape(DEPTH, B, 6, D)


def _norm_mod(x, g, scale, shift):
    ms = jnp.mean(x * x, axis=-1, keepdims=True)
    return (x * lax.rsqrt(ms + NORM_EPS) * g) * (1.0 + scale) + shift


def _inproj_kernel(x_ref, g_ref, sc_ref, sh_ref, w_ref, ra_ref, rb_ref, rc_ref, o_ref):
    h = _norm_mod(x_ref[...], g_ref[...], sc_ref[0], sh_ref[0]).astype(BF16)
    half = ROT_DIM // 2
    for j in range(IN_COLS // IN_TILE_N):
        cols = slice(j * IN_TILE_N, (j + 1) * IN_TILE_N)
        r = jnp.dot(h, w_ref[:, cols], preferred_element_type=F32)
        first_lane_block = j * IN_TILE_N // LANES
        if first_lane_block in (COL_SB_Q, COL_D_Q):
            r = r * (LOG2_E / math.sqrt(SB_HEAD_DIM))
        if first_lane_block in (COL_D_Q, COL_D_K):
            ra, rb, rc = ra_ref[...], rb_ref[...], rc_ref[...]
            for blk in range(IN_TILE_N // LANES):
                t = r[:, blk * LANES:(blk + 1) * LANES]
                rot = t * ra + pltpu.roll(t, LANES - half, 1) * rb + pltpu.roll(t, half, 1) * rc
                o_ref[:, j * IN_TILE_N + blk * LANES:j * IN_TILE_N + (blk + 1) * LANES] = rot.astype(BF16)
        else:
            o_ref[:, cols] = r.astype(BF16)


def _rope_lane_tables(seq_len):
    pos = jnp.arange(seq_len, dtype=F32)
    inv_freq = 1.0 / (ROPE_THETA ** (jnp.arange(0, ROT_DIM, 2, dtype=F32) / ROT_DIM))
    ang = pos[:, None] * inv_freq[None, :]
    cos, sin = jnp.cos(ang), jnp.sin(ang)
    half = ROT_DIM // 2
    lane = jnp.arange(LANES) % DIFF_QK_DIM
    f = lane % half
    first, second = lane < half, (lane >= half) & (lane < ROT_DIM)
    ra = jnp.where((first | second)[None, :], cos[:, f], 1.0)
    rb = jnp.where(first[None, :], -sin[:, f], 0.0)
    rc = jnp.where(second[None, :], sin[:, f], 0.0)
    return ra, rb, rc


def _input_projection(x2, norm_g, scale, shift, w_in_bf, rope, B, S):
    T, D = x2.shape
    tm = min(512, S)
    tiles_per_b = S // tm
    ra, rb, rc = rope
    rope_spec = pl.BlockSpec((tm, LANES), lambda i: (i % tiles_per_b, 0))
    mod_spec = pl.BlockSpec((1, 1, D), lambda i: (i // tiles_per_b, 0, 0))
    assert COL_D_K - COL_D_Q == IN_TILE_N // LANES and COL_D_V - COL_D_K == IN_TILE_N // LANES
    return pl.pallas_call(
        _inproj_kernel,
        grid=(T // tm,),
        in_specs=[
            pl.BlockSpec((tm, D), lambda i: (i, 0)),
            pl.BlockSpec((1, D), lambda i: (0, 0)),
            mod_spec, mod_spec,
            pl.BlockSpec((D, IN_COLS), lambda i: (0, 0)),
            rope_spec, rope_spec, rope_spec,
        ],
        out_specs=pl.BlockSpec((tm, IN_COLS), lambda i: (i, 0)),
        out_shape=jax.ShapeDtypeStruct((T, IN_COLS), BF16),
        compiler_params=_params("parallel"),
        name="in_proj",
    )(x2, norm_g.reshape(1, D), scale.reshape(B, 1, D), shift.reshape(B, 1, D), w_in_bf, ra, rb, rc)


def _sb_kernel(q_ref, k_ref, v_ref, o_ref, acc_sc, cs_sc, *, tb, n_pairs):
    i = pl.program_id(2)
    n_heads = 2 * n_pairs
    lane = lax.broadcasted_iota(jnp.int32, (tb, LANES), 1)
    q_heads = []
    for p in range(n_pairs):
        q = q_ref[:, p * LANES:(p + 1) * LANES]
        zero = jnp.zeros_like(q)
        q_heads += [jnp.where(lane < SB_HEAD_DIM, q, zero), jnp.where(lane >= SB_HEAD_DIM, q, zero)]
    row = lax.broadcasted_iota(jnp.int32, (tb, tb), 0)
    col = lax.broadcasted_iota(jnp.int32, (tb, tb), 1)
    strictly_before = col < row
    suffix_ones = jnp.where(row >= col, 1.0, 0.0).astype(BF16)

    acc_sc[...] = jnp.zeros_like(acc_sc)
    cs_sc[...] = jnp.zeros_like(cs_sc)

    def block(j, masked):
        start = pl.multiple_of(j * tb, tb)
        kb = k_ref[pl.ds(start, tb), :]
        vb = v_ref[pl.ds(start, tb), :]
        ys = [_nt_dot(q_heads[h], kb[:, (h // 2) * LANES:(h // 2 + 1) * LANES]) for h in range(n_heads)]
        costs = []
        for y in ys:
            neg_abs = lax.bitcast_convert_type(
                lax.bitcast_convert_type(y, jnp.uint32) | jnp.uint32(0x80000000), F32)
            cost = jnp.maximum(y, 0.0) + jnp.log2(1.0 + jnp.exp2(neg_abs))
            if masked:
                cost = jnp.where(strictly_before, cost, 0.0)
            costs.append(cost)
        sums = [jnp.dot(cost.astype(BF16), suffix_ones, preferred_element_type=F32) for cost in costs]
        for h in range(n_heads):
            later = cs_sc[h]
            w = jnp.exp2(ys[h] - sums[h] - later)
            if masked:
                w = jnp.where(strictly_before, w, 0.0)
            acc_sc[h] += jnp.dot(w.astype(BF16), vb[:, (h // 2) * LANES:(h // 2 + 1) * LANES],
                                 preferred_element_type=F32)
            cs_sc[h] = later + sums[h][:, 0:1]

    block(i, True)

    def body(jj, carry):
        block(i - 1 - jj, False)
        return carry

    lax.fori_loop(0, i, body, 0)
    for p in range(n_pairs):
        o_ref[:, p * LANES:(p + 1) * LANES] = jnp.where(
            lane < SB_HEAD_DIM, acc_sc[2 * p], acc_sc[2 * p + 1]).astype(BF16)


SB_PAIRS_PER_STEP = 4


def _sb_attention(proj, B, S):
    T = proj.shape[0]
    tb = min(256, S)
    nq = S // tb
    n_pairs = SB_PAIRS_PER_STEP
    width = n_pairs * LANES
    groups = SB_WIDTH // width
    return pl.pallas_call(
        functools.partial(_sb_kernel, tb=tb, n_pairs=n_pairs),
        grid=(B, groups, nq),
        in_specs=[
            pl.BlockSpec((tb, width), lambda b, p, i: (b * nq + i, COL_SB_Q // n_pairs + p)),
            pl.BlockSpec((S, width), lambda b, p, i: (b, COL_SB_K // n_pairs + p)),
            pl.BlockSpec((S, width), lambda b, p, i: (b, COL_SB_V // n_pairs + p)),
        ],
        out_specs=pl.BlockSpec((tb, width), lambda b, p, i: (b * nq + i, p)),
        out_shape=jax.ShapeDtypeStruct((T, SB_WIDTH), BF16),
        scratch_shapes=[pltpu.VMEM((2 * n_pairs, tb, LANES), F32),
                        pltpu.VMEM((2 * n_pairs, tb, 1), F32)],
        compiler_params=_params("parallel", "parallel", "arbitrary"),
        name="sb_attention",
    )(proj, proj, proj)


def _diff_kernel(lp_ref, g_ref, q_ref, k_ref, v_ref, o_ref, m_sc, acc_sc, *, tq, tk, lam_init):
    i = pl.program_id(1)
    lane = lax.broadcasted_iota(jnp.int32, (tq, LANES), 1)
    q_maps = []
    for h in range(DIFF_HEADS):
        q = q_ref[:, h * LANES:(h + 1) * LANES]
        zero = jnp.zeros_like(q)
        q_maps += [jnp.where(lane < DIFF_QK_DIM, q, zero), jnp.where(lane >= DIFF_QK_DIM, q, zero)]
    n_chains = len(q_maps)
    m_sc[...] = jnp.full_like(m_sc, NEG_INF)
    acc_sc[...] = jnp.zeros_like(acc_sc)
    q_pos = i * tq + lax.broadcasted_iota(jnp.int32, (tq, tk), 0)
    k_off = lax.broadcasted_iota(jnp.int32, (tq, tk), 1)
    ones = jnp.ones((tk, LANES), BF16)

    def block(j, masked):
        start = pl.multiple_of(j * tk, tk)
        kb = k_ref[pl.ds(start, tk), :]
        vb = v_ref[pl.ds(start, tk), :]
        head_cols = lambda c: slice((c // 2) * LANES, (c // 2 + 1) * LANES)
        scores = [_nt_dot(q_maps[c], kb[:, head_cols(c)]) for c in range(n_chains)]
        if masked:
            visible = j * tk + k_off <= q_pos
            scores = [jnp.where(visible, s, NEG_INF) for s in scores]
        m_prev = [m_sc[c] for c in range(n_chains)]
        m_new = [jnp.maximum(m_prev[c], jnp.max(scores[c], axis=1, keepdims=True)) for c in range(n_chains)]
        probs = [jnp.exp2(scores[c] - jnp.concatenate([m_new[c]] * (tk // LANES), axis=1))
                 for c in range(n_chains)]
        for c in range(n_chains):
            alpha = jnp.exp2(m_prev[c] - m_new[c])
            v_ext = jnp.concatenate([vb[:, head_cols(c)], ones], axis=1)
            acc_sc[c] = (jnp.concatenate([alpha, alpha], axis=1) * acc_sc[c]
                         + jnp.dot(probs[c].astype(BF16), v_ext, preferred_element_type=F32))
            m_sc[c] = m_new[c]

    n_full = (i * tq) // tk

    def body(j, carry):
        block(j, False)
        return carry

    lax.fori_loop(0, n_full, body, 0)
    block(n_full, True)

    lp = lp_ref[...]
    lam = (jnp.exp(jnp.sum(lp[0:1] * lp[1:2], axis=1, keepdims=True))
           - jnp.exp(jnp.sum(lp[2:3] * lp[3:4], axis=1, keepdims=True)) + lam_init)
    for h in range(DIFF_HEADS):
        a1, a2 = acc_sc[2 * h], acc_sc[2 * h + 1]
        o = a1[:, :LANES] / a1[:, LANES:] - lam * (a2[:, :LANES] / a2[:, LANES:])
        ms = jnp.mean(o * o, axis=-1, keepdims=True)
        y = (o * lax.rsqrt(ms + NORM_EPS) * g_ref[...]) * (1.0 - lam_init)
        o_ref[:, h * LANES:(h + 1) * LANES] = y.astype(BF16)


def _diff_attention(proj, lam_params, subln_g, lam_init, B, S):
    T = proj.shape[0]
    tq = min(256, S)
    tk = min(512, S)
    nq = S // tq
    n_chains = 2 * DIFF_HEADS
    return pl.pallas_call(
        functools.partial(_diff_kernel, tq=tq, tk=tk, lam_init=lam_init),
        grid=(B, nq),
        in_specs=[
            pl.BlockSpec((4, DIFF_QK_DIM), lambda b, i: (0, 0)),
            pl.BlockSpec((1, DIFF_V_DIM), lambda b, i: (0, 0)),
            pl.BlockSpec((tq, DIFF_QK_WIDTH), lambda b, i: (b * nq + i, COL_D_Q * LANES // DIFF_QK_WIDTH)),
            pl.BlockSpec((S, DIFF_QK_WIDTH), lambda b, i: (b, COL_D_K * LANES // DIFF_QK_WIDTH)),
            pl.BlockSpec((S, DIFF_V_WIDTH), lambda b, i: (b, COL_D_V * LANES // DIFF_V_WIDTH)),
        ],
        out_specs=pl.BlockSpec((tq, DIFF_V_WIDTH), lambda b, i: (b * nq + i, 0)),
        out_shape=jax.ShapeDtypeStruct((T, DIFF_V_WIDTH), BF16),
        scratch_shapes=[pltpu.VMEM((n_chains, tq, LANES), F32),
                        pltpu.VMEM((n_chains, tq, 2 * LANES), F32)],
        compiler_params=_params("parallel", "arbitrary"),
        name="diff_attention",
    )(lam_params, subln_g.reshape(1, DIFF_V_DIM), proj, proj, proj)


def _merge_kernel(ysb_ref, ydf_ref, gs_ref, gd_ref, gb_ref, wbs_ref, wbd_ref, wo_ref,
                  x_ref, g1_ref, n2_ref, sc2_ref, sh2_ref, xo_ref, h2_ref):
    D = x_ref.shape[1]
    a = jnp.dot(ysb_ref[...], wbs_ref[...], preferred_element_type=F32)
    b = jnp.dot(ydf_ref[...], wbd_ref[...], preferred_element_type=F32)
    gb = gb_ref[...]
    g_sb = jax.nn.sigmoid(gs_ref[...].astype(F32) + gb[:, :D])
    g_df = jax.nn.sigmoid(gd_ref[...].astype(F32) + gb[:, D:])
    merged = g_sb * a + g_df * b
    y = jnp.dot(merged.astype(BF16), wo_ref[...], preferred_element_type=F32)
    xn = x_ref[...] + g1_ref[0] * y
    xo_ref[...] = xn
    h2_ref[...] = _norm_mod(xn, n2_ref[...], sc2_ref[0], sh2_ref[0]).astype(BF16)


def _merge_project(y_sb, y_df, proj, gate_b, wbs, wbd, wo, x2, gate1, norm2_g, scale2, shift2, B, S):
    T, D = x2.shape
    tm = min(512, S)
    tiles_per_b = S // tm
    mod_spec = pl.BlockSpec((1, 1, D), lambda i: (i // tiles_per_b, 0, 0))
    const = lambda shape: pl.BlockSpec(shape, lambda i: (0, 0))
    return pl.pallas_call(
        _merge_kernel,
        grid=(T // tm,),
        in_specs=[
            pl.BlockSpec((tm, SB_WIDTH), lambda i: (i, 0)),
            pl.BlockSpec((tm, DIFF_V_WIDTH), lambda i: (i, 0)),
            pl.BlockSpec((tm, D), lambda i: (i, COL_G)),
            pl.BlockSpec((tm, D), lambda i: (i, COL_G + 1)),
            const((1, 2 * D)),
            const((SB_WIDTH, D)), const((DIFF_V_WIDTH, D)), const((D, D)),
            pl.BlockSpec((tm, D), lambda i: (i, 0)),
            mod_spec,
            const((1, D)),
            mod_spec, mod_spec,
        ],
        out_specs=[pl.BlockSpec((tm, D), lambda i: (i, 0)), pl.BlockSpec((tm, D), lambda i: (i, 0))],
        out_shape=[jax.ShapeDtypeStruct((T, D), F32), jax.ShapeDtypeStruct((T, D), BF16)],
        compiler_params=_params("parallel"),
        name="merge_out_proj",
    )(y_sb, y_df, proj, proj, gate_b.reshape(1, 2 * D), wbs, wbd, wo, x2,
      gate1.reshape(B, 1, D), norm2_g.reshape(1, D), scale2.reshape(B, 1, D), shift2.reshape(B, 1, D))


def _sort16_network():
    def merge(lo, hi, r):
        step = r * 2
        if step < hi - lo:
            yield from merge(lo, hi, step)
            yield from merge(lo + r, hi, step)
            yield from [(i, i + r) for i in range(lo + r, hi - r, step)]
        else:
            yield (lo, lo + r)

    def sort(lo, hi):
        if hi - lo >= 1:
            mid = lo + (hi - lo) // 2
            yield from sort(lo, mid)
            yield from sort(mid + 1, hi)
            yield from merge(lo, hi, 1)

    return list(sort(0, 15))


def _top_values(scores, k):
    groups = scores.shape[0] // SUBLANES
    assert groups == 16 and k <= groups + 1
    v = [scores[SUBLANES * g:SUBLANES * (g + 1)] for g in range(groups)]
    for a, b in _sort16_network():
        v[a], v[b] = jnp.maximum(v[a], v[b]), jnp.minimum(v[a], v[b])
    v.append(jnp.full_like(v[0], NEG_INF))
    vals = []
    for r in range(k):
        m = jnp.max(v[0], axis=0, keepdims=True)
        vals.append(m)
        still_needed = k - 1 - r
        if still_needed:
            hit = v[0] == m
            for d in range(min(still_needed, groups)):
                v[d] = jnp.where(hit, v[d + 1], v[d])
    return vals


def _candidate_sums(v1, v1_all, v2, v2_all, row_id):
    assert PEER_TOPK == 16 and SUBLANES == 8
    lo2 = v2_all[0:8]
    hi1 = v1_all[8:16]
    shift = lambda x, n: pltpu.roll(x, n, 0)
    groups = [v1[0] + v2_all[0:8], v1[0] + v2_all[8:16], v1[0] + v2_all[16:24],
              v1[1] + lo2]
    groups.append(jnp.where(row_id < 5, v1[2] + lo2, v1[4] + shift(lo2, 5)))
    groups.append(jnp.where(row_id < 4, v1[3] + lo2,
                            jnp.where(row_id < 6, v1[5] + shift(lo2, 4),
                                      v1[6] + shift(lo2, 6))))
    tail = shift(hi1, 2) + v2[0]
    groups.append(jnp.where(row_id < 2, v1[7] + lo2, tail))
    groups.append(jnp.where(row_id < 2, tail,
                            jnp.where(row_id == 2, v1[16] + v2[0], NEG_INF)))
    return groups


def _peer_select_kernel(h_ref, wq_ref, sk_ref, ht_ref, th_ref, e1_ref, e2_ref, *, tm):
    hb = h_ref[...]
    ht_ref[0] = hb.T
    K = PEER_TOPK + 1
    row_id = lax.broadcasted_iota(jnp.int32, (SUBLANES, LANES), 0)
    pad_rows = jnp.full((3 * SUBLANES - K, LANES), NEG_INF, F32)

    def head(h, carry):
        q_t = _nt_dot(wq_ref[h], hb).astype(BF16)
        half = PEER_QUERY_DIM // 2
        sc = [jnp.dot(sk_ref[h, c], q_t[c * half:(c + 1) * half], preferred_element_type=F32)
              for c in range(2)]
        for lc in range(tm // LANES):
            ls = slice(lc * LANES, (lc + 1) * LANES)
            s1, s2 = sc[0][:, ls], sc[1][:, ls]
            v1 = _top_values(s1, K)
            v2 = _top_values(s2, K)
            v2_all = jnp.concatenate(v2 + [pad_rows], axis=0)
            v1_all = jnp.concatenate(v1 + [pad_rows], axis=0)
            cands = _candidate_sums(v1, v1_all, v2, v2_all, row_id)
            best = cands[0][0:1]
            z_sum = jnp.zeros_like(best)
            last = best
            for r in range(K):
                stacked = cands[0]
                for ca in cands[1:]:
                    stacked = jnp.maximum(stacked, ca)
                m = jnp.max(stacked, axis=0, keepdims=True)
                if r < PEER_TOPK:
                    z_sum = z_sum + jnp.exp(m - best)
                    last = m
                    cands = [jnp.where(ca == m, NEG_INF, ca) for ca in cands]
            tau = 0.5 * (last + m)
            half_inv_z = 0.5 / z_sum
            chunked = (PEER_N_KEYS // PEER_CHUNK_KEYS, PEER_CHUNK_KEYS, LANES)
            th_ref[:, 0, h, :, ls] = jnp.exp(tau - s1 - v2[0]).reshape(chunked)
            e1_ref[:, 0, h, :, ls] = (jnp.exp(s1 - v1[0]) * half_inv_z).reshape(chunked)
            e2_ref[0, h, :, ls] = jnp.exp(s2 - v2[0])
        return carry

    lax.fori_loop(0, PEER_HEADS, head, 0)


def _peer_select(h2, wq_t, subkeys_bf):
    T, D = h2.shape
    tm = PEER_TOKEN_TILE
    n_tiles = T // tm
    tok_spec = pl.BlockSpec((1, PEER_HEADS, PEER_N_KEYS, tm), lambda i: (i, 0, 0, 0))
    tok_shape = jax.ShapeDtypeStruct((n_tiles, PEER_HEADS, PEER_N_KEYS, tm), F32)
    n_chunks = PEER_N_KEYS // PEER_CHUNK_KEYS
    row_spec = pl.BlockSpec((n_chunks, 1, PEER_HEADS, PEER_CHUNK_KEYS, tm), lambda i: (0, i, 0, 0, 0))
    row_shape = jax.ShapeDtypeStruct((n_chunks, n_tiles, PEER_HEADS, PEER_CHUNK_KEYS, tm), F32)
    return pl.pallas_call(
        functools.partial(_peer_select_kernel, tm=tm),
        grid=(n_tiles,),
        in_specs=[
            pl.BlockSpec((tm, D), lambda i: (i, 0)),
            pl.BlockSpec((PEER_HEADS, PEER_QUERY_DIM, D), lambda i: (0, 0, 0)),
            pl.BlockSpec((PEER_HEADS, 2, PEER_N_KEYS, PEER_QUERY_DIM // 2), lambda i: (0, 0, 0, 0)),
        ],
        out_specs=[
            pl.BlockSpec((1, D, tm), lambda i: (i, 0, 0)),
            row_spec, row_spec, tok_spec,
        ],
        out_shape=[
            jax.ShapeDtypeStruct((n_tiles, D, tm), BF16),
            row_shape, row_shape, tok_shape,
        ],
        compiler_params=_params("parallel"),
        name="peer_select",
    )(h2, wq_t, subkeys_bf)


PEER_ROW_BLOCK_KEYS = 4
PEER_PIECE_TOKENS = 2 * LANES


def _peer_mix_kernel(ht_ref, u_ref, vt_ref, th_ref, e1_ref, e2_ref, x_ref, g2_ref,
                     xo_ref, acc_sc, act_sc, e2_sc, *, tm, n_tiles):
    j = pl.program_id(1)

    @pl.when(j == 0)
    def _():
        acc_sc[...] = jnp.zeros_like(acc_sc)
        for lc in range(tm // LANES):
            e2_sc[:, :, lc] = e2_ref[:, :, :, lc * LANES:(lc + 1) * LANES]

    rb = PEER_ROW_BLOCK_KEYS * PEER_N_KEYS
    tw = PEER_PIECE_TOKENS
    n_rb = PEER_CHUNK // rb
    pieces = [(r, c) for c in range(tm // tw) for r in range(n_rb)]

    def tile(t, carry):
        def scores(p):
            r, c = pieces[p]
            return jnp.dot(u_ref[r * rb:(r + 1) * rb, :], ht_ref[t, :, c * tw:(c + 1) * tw],
                           preferred_element_type=F32)

        def gates(p, a_blk, k):
            r, c = pieces[p]
            ii = r * PEER_ROW_BLOCK_KEYS + k
            for lc in range(tw // LANES):
                chunk = c * (tw // LANES) + lc
                ls = slice(chunk * LANES, (chunk + 1) * LANES)
                gate = None
                for h in range(PEER_HEADS):
                    theta = th_ref[0, t, h, ii:ii + 1, ls]
                    e1 = e1_ref[0, t, h, ii:ii + 1, ls]
                    e2 = e2_sc[t, h, chunk]
                    term = jnp.where(e2 >= theta, e1 * e2, 0.0)
                    gate = term if gate is None else gate + term
                a = a_blk[k * PEER_N_KEYS:(k + 1) * PEER_N_KEYS, lc * LANES:(lc + 1) * LANES]
                gelu2 = a * (1.0 + lax.erf(a * (1.0 / math.sqrt(2.0))))
                act_sc[c, ii * PEER_N_KEYS:(ii + 1) * PEER_N_KEYS, lc * LANES:(lc + 1) * LANES] = (
                    gelu2 * gate).astype(BF16)

        def mix(c):
            acc_sc[t, c] += jnp.dot(vt_ref[0], act_sc[c], preferred_element_type=F32)

        a_next = scores(0)
        for p in range(len(pieces)):
            r, c = pieces[p]
            a_cur = a_next
            if p + 1 < len(pieces):
                a_next = scores(p + 1)
            for k in range(PEER_ROW_BLOCK_KEYS):
                if k == PEER_ROW_BLOCK_KEYS // 2 and r == 0 and c > 0:
                    mix(c - 1)
                gates(p, a_cur, k)
        mix(tm // tw - 1)
        return carry

    lax.fori_loop(0, n_tiles, tile, 0)

    @pl.when(j == pl.num_programs(1) - 1)
    def _():
        for t in range(n_tiles):
            rows = slice(t * tm, (t + 1) * tm)
            mixed_t = jnp.concatenate([acc_sc[t, c] for c in range(tm // tw)], axis=1)
            xo_ref[rows, :] = x_ref[rows, :] + g2_ref[0] * mixed_t.T


def _peer_mix(h_t, u_bf, v_t, theta, e1, e2, x2, gate2, B, S):
    T, D = x2.shape
    tm = PEER_TOKEN_TILE
    n_tiles = min(PEER_TILES_PER_STEP, S // tm)
    span = n_tiles * tm
    steps_per_b = S // span
    tok_spec = pl.BlockSpec((n_tiles, PEER_HEADS, PEER_N_KEYS, tm), lambda i, j: (i, 0, 0, 0),
                            pipeline_mode=pl.Buffered(1))
    row_spec = pl.BlockSpec((1, n_tiles, PEER_HEADS, PEER_CHUNK_KEYS, tm), lambda i, j: (j, i, 0, 0, 0))
    return pl.pallas_call(
        functools.partial(_peer_mix_kernel, tm=tm, n_tiles=n_tiles),
        grid=(T // span, PEER_N_EXPERTS // PEER_CHUNK),
        in_specs=[
            pl.BlockSpec((n_tiles, D, tm), lambda i, j: (i, 0, 0)),
            pl.BlockSpec((PEER_CHUNK, D), lambda i, j: (j, 0)),
            pl.BlockSpec((1, D, PEER_CHUNK), lambda i, j: (j, 0, 0)),
            row_spec, row_spec, tok_spec,
            pl.BlockSpec((span, D), lambda i, j: (i, 0)),
            pl.BlockSpec((1, 1, D), lambda i, j: (i // steps_per_b, 0, 0)),
        ],
        out_specs=pl.BlockSpec((span, D), lambda i, j: (i, 0)),
        out_shape=jax.ShapeDtypeStruct((T, D), F32),
        scratch_shapes=[pltpu.VMEM((n_tiles, tm // PEER_PIECE_TOKENS, D, PEER_PIECE_TOKENS), F32),
                        pltpu.VMEM((tm // PEER_PIECE_TOKENS, PEER_CHUNK, PEER_PIECE_TOKENS), BF16),
                        pltpu.VMEM((n_tiles, PEER_HEADS, tm // LANES, PEER_N_KEYS, LANES), F32)],
        compiler_params=_params("parallel", "arbitrary"),
        name="peer_mix",
    )(h_t, u_bf, v_t, theta, e1, e2, x2, gate2.reshape(B, 1, D))


def _final_norm_kernel(x_ref, g_ref, o_ref):
    x = x_ref[...]
    ms = jnp.mean(x * x, axis=-1, keepdims=True)
    o_ref[...] = x * lax.rsqrt(ms + NORM_EPS) * g_ref[...]


def _final_norm(x2, g):
    T, D = x2.shape
    tm = min(1024, T)
    return pl.pallas_call(
        _final_norm_kernel,
        grid=(T // tm,),
        in_specs=[pl.BlockSpec((tm, D), lambda i: (i, 0)), pl.BlockSpec((1, D), lambda i: (0, 0))],
        out_specs=pl.BlockSpec((tm, D), lambda i: (i, 0)),
        out_shape=jax.ShapeDtypeStruct((T, D), F32),
        compiler_params=_params("parallel"),
        name="final_norm",
    )(x2, g.reshape(1, D))


def kernel(x, c, norm1_g, norm2_g, ada_w, ada_b, w_in, gate_b, diff_lambda, diff_subln_g,
           w_branch_sb, w_branch_diff, w_out, peer_wq, peer_subkeys, peer_u, peer_v, final_g):
    B, S, D = x.shape
    assert D == D_MODEL and S % LANES == 0
    x2 = x.reshape(B * S, D)
    mod = _modulation(c, ada_w, ada_b)
    rope = _rope_lane_tables(S)
    for l in range(DEPTH):
        lam_init = 0.8 - 0.6 * math.exp(-0.3 * l)
        shift1, scale1, gate1, shift2, scale2, gate2 = (mod[l, :, n] for n in range(6))
        proj = _input_projection(x2, norm1_g[l], scale1, shift1, w_in[l].astype(BF16), rope, B, S)
        y_sb = _sb_attention(proj, B, S)
        y_df = _diff_attention(proj, diff_lambda[l], diff_subln_g[l], lam_init, B, S)
        x2, h2 = _merge_project(y_sb, y_df, proj, gate_b[l], w_branch_sb[l].astype(BF16),
                                w_branch_diff[l].astype(BF16), w_out[l].astype(BF16), x2, gate1,
                                norm2_g[l], scale2, shift2, B, S)
        wq_t = peer_wq[l].T.astype(BF16).reshape(PEER_HEADS, PEER_QUERY_DIM, D)
        h_t, theta, e1, e2 = _peer_select(h2, wq_t, peer_subkeys[l].astype(BF16))
        v_t = jnp.swapaxes(peer_v[l].astype(BF16).reshape(-1, PEER_CHUNK, D), 1, 2)
        x2 = _peer_mix(h_t, peer_u[l].astype(BF16), v_t,
                       theta, e1, e2, x2, gate2, B, S)
    return _final_norm(x2, final_g).reshape(B, S, D)
```

```python
import functools
import math

import jax
import jax.numpy as jnp
from jax import lax
from jax.experimental import pallas as pl
from jax.experimental.pallas import tpu as pltpu

F32 = jnp.float32
BF16 = jnp.bfloat16

D_MODEL = 1024
DEPTH = 2
SB_HEADS = 8
SB_HEAD_DIM = 64
SB_WIDTH = SB_HEADS * SB_HEAD_DIM
DIFF_HEADS = 4
DIFF_QK_DIM = 64
DIFF_V_DIM = 2 * DIFF_QK_DIM
DIFF_QK_WIDTH = DIFF_HEADS * 2 * DIFF_QK_DIM
DIFF_V_WIDTH = DIFF_HEADS * DIFF_V_DIM
ROPE_THETA = 500000.0
ROT_DIM = DIFF_QK_DIM // 4
IN_COLS = 3 * SB_WIDTH + 2 * DIFF_QK_WIDTH + DIFF_V_WIDTH + 2 * D_MODEL
PEER_HEADS = 8
PEER_N_KEYS = 128
PEER_N_EXPERTS = PEER_N_KEYS * PEER_N_KEYS
PEER_TOPK = 16
PEER_QUERY_DIM = 256
PEER_CHUNK_KEYS = 16
PEER_CHUNK = PEER_CHUNK_KEYS * PEER_N_KEYS
PEER_TOKEN_TILE = 512
PEER_TILES_PER_STEP = 2
NORM_EPS = 1e-6

LANES = 128
SUBLANES = 8
VMEM_LIMIT = 56 * 1024 * 1024
NEG_INF = float("-inf")
LOG2_E = 1.4426950408889634

IN_TILE_N = 512
COL_SB_Q, COL_SB_K, COL_SB_V = 0, SB_WIDTH // LANES, 2 * SB_WIDTH // LANES
COL_D_Q = 3 * SB_WIDTH // LANES
COL_D_K = COL_D_Q + DIFF_QK_WIDTH // LANES
COL_D_V = COL_D_K + DIFF_QK_WIDTH // LANES
COL_G = (3 * SB_WIDTH + 2 * DIFF_QK_WIDTH + DIFF_V_WIDTH) // D_MODEL


def _params(*sem):
    return pltpu.CompilerParams(dimension_semantics=sem, vmem_limit_bytes=VMEM_LIMIT)


def _nt_dot(a, b):
    return lax.dot_general(a, b, (((1,), (1,)), ((), ())), preferred_element_type=F32)


def _mod_kernel(c_ref, w_ref, b_ref, o_ref):
    c = c_ref[...]
    c_act = c * jax.nn.sigmoid(c)
    o_ref[0] = jnp.dot(c_act, w_ref[0], preferred_element_type=F32,
                       precision=lax.Precision.HIGHEST) + b_ref[0]


def _modulation(c, ada_w, ada_b):
    B, D = c.shape
    rows = SUBLANES
    c_pad = jnp.pad(c, ((0, rows - B), (0, 0)))
    out = pl.pallas_call(
        _mod_kernel,
        grid=(DEPTH, 6),
        in_specs=[
            pl.BlockSpec((rows, D), lambda l, n: (0, 0)),
            pl.BlockSpec((1, D, D), lambda l, n: (l, 0, n)),
            pl.BlockSpec((1, 1, D), lambda l, n: (l, 0, n)),
        ],
        out_specs=pl.BlockSpec((1, rows, D), lambda l, n: (l, 0, n)),
        out_shape=jax.ShapeDtypeStruct((DEPTH, rows, 6 * D), F32),
        compiler_params=_params("arbitrary", "arbitrary"),
        name="adaln_mod",
    )(c_pad, ada_w, ada_b.reshape(DEPTH, 1, 6 * D))
    return out[:, :B].reshape(DEPTH, B, 6, D)


def _norm_mod(x, g, scale, shift):
    ms = jnp.mean(x * x, axis=-1, keepdims=True)
    return (x * lax.rsqrt(ms + NORM_EPS) * g) * (1.0 + scale) + shift


def _inproj_kernel(x_ref, g_ref, sc_ref, sh_ref, w_ref, ra_ref, rb_ref, rc_ref, o_ref):
    h = _norm_mod(x_ref[...], g_ref[...], sc_ref[0], sh_ref[0]).astype(BF16)
    half = ROT_DIM // 2
    for j in range(IN_COLS // IN_TILE_N):
        cols = slice(j * IN_TILE_N, (j + 1) * IN_TILE_N)
        r = jnp.dot(h, w_ref[:, cols], preferred_element_type=F32)
        first_lane_block = j * IN_TILE_N // LANES
        if first_lane_block in (COL_SB_Q, COL_D_Q):
            r = r * (LOG2_E / math.sqrt(SB_HEAD_DIM))
        if first_lane_block in (COL_D_Q, COL_D_K):
            ra, rb, rc = ra_ref[...], rb_ref[...], rc_ref[...]
            for blk in range(IN_TILE_N // LANES):
                t = r[:, blk * LANES:(blk + 1) * LANES]
                rot = t * ra + pltpu.roll(t, LANES - half, 1) * rb + pltpu.roll(t, half, 1) * rc
                o_ref[:, j * IN_TILE_N + blk * LANES:j * IN_TILE_N + (blk + 1) * LANES] = rot.astype(BF16)
        else:
            o_ref[:, cols] = r.astype(BF16)


def _rope_lane_tables(seq_len):
    pos = jnp.arange(seq_len, dtype=F32)
    inv_freq = 1.0 / (ROPE_THETA ** (jnp.arange(0, ROT_DIM, 2, dtype=F32) / ROT_DIM))
    ang = pos[:, None] * inv_freq[None, :]
    cos, sin = jnp.cos(ang), jnp.sin(ang)
    half = ROT_DIM // 2
    lane = jnp.arange(LANES) % DIFF_QK_DIM
    f = lane % half
    first, second = lane < half, (lane >= half) & (lane < ROT_DIM)
    ra = jnp.where((first | second)[None, :], cos[:, f], 1.0)
    rb = jnp.where(first[None, :], -sin[:, f], 0.0)
    rc = jnp.where(second[None, :], sin[:, f], 0.0)
    return ra, rb, rc


def _input_projection(x2, norm_g, scale, shift, w_in_bf, rope, B, S):
    T, D = x2.shape
    tm = min(512, S)
    tiles_per_b = S // tm
    ra, rb, rc = rope
    rope_spec = pl.BlockSpec((tm, LANES), lambda i: (i % tiles_per_b, 0))
    mod_spec = pl.BlockSpec((1, 1, D), lambda i: (i // tiles_per_b, 0, 0))
    assert COL_D_K - COL_D_Q == IN_TILE_N // LANES and COL_D_V - COL_D_K == IN_TILE_N // LANES
    return pl.pallas_call(
        _inproj_kernel,
        grid=(T // tm,),
        in_specs=[
            pl.BlockSpec((tm, D), lambda i: (i, 0)),
            pl.BlockSpec((1, D), lambda i: (0, 0)),
            mod_spec, mod_spec,
            pl.BlockSpec((D, IN_COLS), lambda i: (0, 0)),
            rope_spec, rope_spec, rope_spec,
        ],
        out_specs=pl.BlockSpec((tm, IN_COLS), lambda i: (i, 0)),
        out_shape=jax.ShapeDtypeStruct((T, IN_COLS), BF16),
        compiler_params=_params("parallel"),
        name="in_proj",
    )(x2, norm_g.reshape(1, D), scale.reshape(B, 1, D), shift.reshape(B, 1, D), w_in_bf, ra, rb, rc)


def _sb_kernel(q_ref, k_ref, v_ref, o_ref, acc_sc, cs_sc, *, tb, n_pairs):
    i = pl.program_id(2)
    n_heads = 2 * n_pairs
    lane = lax.broadcasted_iota(jnp.int32, (tb, LANES), 1)
    q_heads = []
    for p in range(n_pairs):
        q = q_ref[:, p * LANES:(p + 1) * LANES]
        zero = jnp.zeros_like(q)
        q_heads += [jnp.where(lane < SB_HEAD_DIM, q, zero), jnp.where(lane >= SB_HEAD_DIM, q, zero)]
    row = lax.broadcasted_iota(jnp.int32, (tb, tb), 0)
    col = lax.broadcasted_iota(jnp.int32, (tb, tb), 1)
    strictly_before = col < row
    suffix_ones = jnp.where(row >= col, 1.0, 0.0).astype(BF16)

    acc_sc[...] = jnp.zeros_like(acc_sc)
    cs_sc[...] = jnp.zeros_like(cs_sc)

    def block(j, masked):
        start = pl.multiple_of(j * tb, tb)
        kb = k_ref[pl.ds(start, tb), :]
        vb = v_ref[pl.ds(start, tb), :]
        ys = [_nt_dot(q_heads[h], kb[:, (h // 2) * LANES:(h // 2 + 1) * LANES]) for h in range(n_heads)]
        costs = []
        for y in ys:
            neg_abs = lax.bitcast_convert_type(
                lax.bitcast_convert_type(y, jnp.uint32) | jnp.uint32(0x80000000), F32)
            cost = jnp.maximum(y, 0.0) + jnp.log2(1.0 + jnp.exp2(neg_abs))
            if masked:
                cost = jnp.where(strictly_before, cost, 0.0)
            costs.append(cost)
        sums = [jnp.dot(cost.astype(BF16), suffix_ones, preferred_element_type=F32) for cost in costs]
        for h in range(n_heads):
            later = cs_sc[h]
            w = jnp.exp2(ys[h] - sums[h] - later)
            if masked:
                w = jnp.where(strictly_before, w, 0.0)
            acc_sc[h] += jnp.dot(w.astype(BF16), vb[:, (h // 2) * LANES:(h // 2 + 1) * LANES],
                                 preferred_element_type=F32)
            cs_sc[h] = later + sums[h][:, 0:1]

    block(i, True)

    def body(jj, carry):
        block(i - 1 - jj, False)
        return carry

    lax.fori_loop(0, i, body, 0)
    for p in range(n_pairs):
        o_ref[:, p * LANES:(p + 1) * LANES] = jnp.where(
            lane < SB_HEAD_DIM, acc_sc[2 * p], acc_sc[2 * p + 1]).astype(BF16)


SB_PAIRS_PER_STEP = 4


def _sb_attention(proj, B, S):
    T = proj.shape[0]
    tb = min(256, S)
    nq = S // tb
    n_pairs = SB_PAIRS_PER_STEP
    width = n_pairs * LANES
    groups = SB_WIDTH // width
    return pl.pallas_call(
        functools.partial(_sb_kernel, tb=tb, n_pairs=n_pairs),
        grid=(B, groups, nq),
        in_specs=[
            pl.BlockSpec((tb, width), lambda b, p, i: (b * nq + i, COL_SB_Q // n_pairs + p)),
            pl.BlockSpec((S, width), lambda b, p, i: (b, COL_SB_K // n_pairs + p)),
            pl.BlockSpec((S, width), lambda b, p, i: (b, COL_SB_V // n_pairs + p)),
        ],
        out_specs=pl.BlockSpec((tb, width), lambda b, p, i: (b * nq + i, p)),
        out_shape=jax.ShapeDtypeStruct((T, SB_WIDTH), BF16),
        scratch_shapes=[pltpu.VMEM((2 * n_pairs, tb, LANES), F32),
                        pltpu.VMEM((2 * n_pairs, tb, 1), F32)],
        compiler_params=_params("parallel", "parallel", "arbitrary"),
        name="sb_attention",
    )(proj, proj, proj)


def _diff_kernel(lp_ref, g_ref, q_ref, k_ref, v_ref, o_ref, m_sc, acc_sc, *, tq, tk, lam_init):
    i = pl.program_id(1)
    lane = lax.broadcasted_iota(jnp.int32, (tq, LANES), 1)
    q_maps = []
    for h in range(DIFF_HEADS):
        q = q_ref[:, h * LANES:(h + 1) * LANES]
        zero = jnp.zeros_like(q)
        q_maps += [jnp.where(lane < DIFF_QK_DIM, q, zero), jnp.where(lane >= DIFF_QK_DIM, q, zero)]
    n_chains = len(q_maps)
    m_sc[...] = jnp.full_like(m_sc, NEG_INF)
    acc_sc[...] = jnp.zeros_like(acc_sc)
    q_pos = i * tq + lax.broadcasted_iota(jnp.int32, (tq, tk), 0)
    k_off = lax.broadcasted_iota(jnp.int32, (tq, tk), 1)
    ones = jnp.ones((tk, LANES), BF16)

    def block(j, masked):
        start = pl.multiple_of(j * tk, tk)
        kb = k_ref[pl.ds(start, tk), :]
        vb = v_ref[pl.ds(start, tk), :]
        head_cols = lambda c: slice((c // 2) * LANES, (c // 2 + 1) * LANES)
        scores = [_nt_dot(q_maps[c], kb[:, head_cols(c)]) for c in range(n_chains)]
        if masked:
            visible = j * tk + k_off <= q_pos
            scores = [jnp.where(visible, s, NEG_INF) for s in scores]
        m_prev = [m_sc[c] for c in range(n_chains)]
        m_new = [jnp.maximum(m_prev[c], jnp.max(scores[c], axis=1, keepdims=True)) for c in range(n_chains)]
        probs = [jnp.exp2(scores[c] - jnp.concatenate([m_new[c]] * (tk // LANES), axis=1))
                 for c in range(n_chains)]
        for c in range(n_chains):
            alpha = jnp.exp2(m_prev[c] - m_new[c])
            v_ext = jnp.concatenate([vb[:, head_cols(c)], ones], axis=1)
            acc_sc[c] = (jnp.concatenate([alpha, alpha], axis=1) * acc_sc[c]
                         + jnp.dot(probs[c].astype(BF16), v_ext, preferred_element_type=F32))
            m_sc[c] = m_new[c]

    n_full = (i * tq) // tk

    def body(j, carry):
        block(j, False)
        return carry

    lax.fori_loop(0, n_full, body, 0)
    block(n_full, True)

    lp = lp_ref[...]
    lam = (jnp.exp(jnp.sum(lp[0:1] * lp[1:2], axis=1, keepdims=True))
           - jnp.exp(jnp.sum(lp[2:3] * lp[3:4], axis=1, keepdims=True)) + lam_init)
    for h in range(DIFF_HEADS):
        a1, a2 = acc_sc[2 * h], acc_sc[2 * h + 1]
        o = a1[:, :LANES] / a1[:, LANES:] - lam * (a2[:, :LANES] / a2[:, LANES:])
        ms = jnp.mean(o * o, axis=-1, keepdims=True)
        y = (o * lax.rsqrt(ms + NORM_EPS) * g_ref[...]) * (1.0 - lam_init)
        o_ref[:, h * LANES:(h + 1) * LANES] = y.astype(BF16)


def _diff_attention(proj, lam_params, subln_g, lam_init, B, S):
    T = proj.shape[0]
    tq = min(256, S)
    tk = min(512, S)
    nq = S // tq
    n_chains = 2 * DIFF_HEADS
    return pl.pallas_call(
        functools.partial(_diff_kernel, tq=tq, tk=tk, lam_init=lam_init),
        grid=(B, nq),
        in_specs=[
            pl.BlockSpec((4, DIFF_QK_DIM), lambda b, i: (0, 0)),
            pl.BlockSpec((1, DIFF_V_DIM), lambda b, i: (0, 0)),
            pl.BlockSpec((tq, DIFF_QK_WIDTH), lambda b, i: (b * nq + i, COL_D_Q * LANES // DIFF_QK_WIDTH)),
            pl.BlockSpec((S, DIFF_QK_WIDTH), lambda b, i: (b, COL_D_K * LANES // DIFF_QK_WIDTH)),
            pl.BlockSpec((S, DIFF_V_WIDTH), lambda b, i: (b, COL_D_V * LANES // DIFF_V_WIDTH)),
        ],
        out_specs=pl.BlockSpec((tq, DIFF_V_WIDTH), lambda b, i: (b * nq + i, 0)),
        out_shape=jax.ShapeDtypeStruct((T, DIFF_V_WIDTH), BF16),
        scratch_shapes=[pltpu.VMEM((n_chains, tq, LANES), F32),
                        pltpu.VMEM((n_chains, tq, 2 * LANES), F32)],
        compiler_params=_params("parallel", "arbitrary"),
        name="diff_attention",
    )(lam_params, subln_g.reshape(1, DIFF_V_DIM), proj, proj, proj)


def _merge_kernel(ysb_ref, ydf_ref, gs_ref, gd_ref, gb_ref, wbs_ref, wbd_ref, wo_ref,
                  x_ref, g1_ref, n2_ref, sc2_ref, sh2_ref, xo_ref, h2_ref):
    D = x_ref.shape[1]
    a = jnp.dot(ysb_ref[...], wbs_ref[...], preferred_element_type=F32)
    b = jnp.dot(ydf_ref[...], wbd_ref[...], preferred_element_type=F32)
    gb = gb_ref[...]
    g_sb = jax.nn.sigmoid(gs_ref[...].astype(F32) + gb[:, :D])
    g_df = jax.nn.sigmoid(gd_ref[...].astype(F32) + gb[:, D:])
    merged = g_sb * a + g_df * b
    y = jnp.dot(merged.astype(BF16), wo_ref[...], preferred_element_type=F32)
    xn = x_ref[...] + g1_ref[0] * y
    xo_ref[...] = xn
    h2_ref[...] = _norm_mod(xn, n2_ref[...], sc2_ref[0], sh2_ref[0]).astype(BF16)


def _merge_project(y_sb, y_df, proj, gate_b, wbs, wbd, wo, x2, gate1, norm2_g, scale2, shift2, B, S):
    T, D = x2.shape
    tm = min(512, S)
    tiles_per_b = S // tm
    mod_spec = pl.BlockSpec((1, 1, D), lambda i: (i // tiles_per_b, 0, 0))
    const = lambda shape: pl.BlockSpec(shape, lambda i: (0, 0))
    return pl.pallas_call(
        _merge_kernel,
        grid=(T // tm,),
        in_specs=[
            pl.BlockSpec((tm, SB_WIDTH), lambda i: (i, 0)),
            pl.BlockSpec((tm, DIFF_V_WIDTH), lambda i: (i, 0)),
            pl.BlockSpec((tm, D), lambda i: (i, COL_G)),
            pl.BlockSpec((tm, D), lambda i: (i, COL_G + 1)),
            const((1, 2 * D)),
            const((SB_WIDTH, D)), const((DIFF_V_WIDTH, D)), const((D, D)),
            pl.BlockSpec((tm, D), lambda i: (i, 0)),
            mod_spec,
            const((1, D)),
            mod_spec, mod_spec,
        ],
        out_specs=[pl.BlockSpec((tm, D), lambda i: (i, 0)), pl.BlockSpec((tm, D), lambda i: (i, 0))],
        out_shape=[jax.ShapeDtypeStruct((T, D), F32), jax.ShapeDtypeStruct((T, D), BF16)],
        compiler_params=_params("parallel"),
        name="merge_out_proj",
    )(y_sb, y_df, proj, proj, gate_b.reshape(1, 2 * D), wbs, wbd, wo, x2,
      gate1.reshape(B, 1, D), norm2_g.reshape(1, D), scale2.reshape(B, 1, D), shift2.reshape(B, 1, D))


def _sort16_network():
    def merge(lo, hi, r):
        step = r * 2
        if step < hi - lo:
            yield from merge(lo, hi, step)
            yield from merge(lo + r, hi, step)
            yield from [(i, i + r) for i in range(lo + r, hi - r, step)]
        else:
            yield (lo, lo + r)

    def sort(lo, hi):
        if hi - lo >= 1:
            mid = lo + (hi - lo) // 2
            yield from sort(lo, mid)
            yield from sort(mid + 1, hi)
            yield from merge(lo, hi, 1)

    return list(sort(0, 15))


def _top_values(scores, k):
    groups = scores.shape[0] // SUBLANES
    assert groups == 16 and k <= groups + 1
    v = [scores[SUBLANES * g:SUBLANES * (g + 1)] for g in range(groups)]
    for a, b in _sort16_network():
        v[a], v[b] = jnp.maximum(v[a], v[b]), jnp.minimum(v[a], v[b])
    v.append(jnp.full_like(v[0], NEG_INF))
    vals = []
    for r in range(k):
        m = jnp.max(v[0], axis=0, keepdims=True)
        vals.append(m)
        still_needed = k - 1 - r
        if still_needed:
            hit = v[0] == m
            for d in range(min(still_needed, groups)):
                v[d] = jnp.where(hit, v[d + 1], v[d])
    return vals


def _candidate_sums(v1, v1_all, v2, v2_all, row_id):
    assert PEER_TOPK == 16 and SUBLANES == 8
    lo2 = v2_all[0:8]
    hi1 = v1_all[8:16]
    shift = lambda x, n: pltpu.roll(x, n, 0)
    groups = [v1[0] + v2_all[0:8], v1[0] + v2_all[8:16], v1[0] + v2_all[16:24],
              v1[1] + lo2]
    groups.append(jnp.where(row_id < 5, v1[2] + lo2, v1[4] + shift(lo2, 5)))
    groups.append(jnp.where(row_id < 4, v1[3] + lo2,
                            jnp.where(row_id < 6, v1[5] + shift(lo2, 4),
                                      v1[6] + shift(lo2, 6))))
    tail = shift(hi1, 2) + v2[0]
    groups.append(jnp.where(row_id < 2, v1[7] + lo2, tail))
    groups.append(jnp.where(row_id < 2, tail,
                            jnp.where(row_id == 2, v1[16] + v2[0], NEG_INF)))
    return groups


def _peer_select_kernel(h_ref, wq_ref, sk_ref, ht_ref, th_ref, e1_ref, e2_ref, *, tm):
    hb = h_ref[...]
    ht_ref[0] = hb.T
    K = PEER_TOPK + 1
    row_id = lax.broadcasted_iota(jnp.int32, (SUBLANES, LANES), 0)
    pad_rows = jnp.full((3 * SUBLANES - K, LANES), NEG_INF, F32)

    def head(h, carry):
        q_t = _nt_dot(wq_ref[h], hb).astype(BF16)
        half = PEER_QUERY_DIM // 2
        sc = [jnp.dot(sk_ref[h, c], q_t[c * half:(c + 1) * half], preferred_element_type=F32)
              for c in range(2)]
        for lc in range(tm // LANES):
            ls = slice(lc * LANES, (lc + 1) * LANES)
            s1, s2 = sc[0][:, ls], sc[1][:, ls]
            v1 = _top_values(s1, K)
            v2 = _top_values(s2, K)
            v2_all = jnp.concatenate(v2 + [pad_rows], axis=0)
            v1_all = jnp.concatenate(v1 + [pad_rows], axis=0)
            cands = _candidate_sums(v1, v1_all, v2, v2_all, row_id)
            best = cands[0][0:1]
            z_sum = jnp.zeros_like(best)
            last = best
            for r in range(K):
                stacked = cands[0]
                for ca in cands[1:]:
                    stacked = jnp.maximum(stacked, ca)
                m = jnp.max(stacked, axis=0, keepdims=True)
                if r < PEER_TOPK:
                    z_sum = z_sum + jnp.exp(m - best)
                    last = m
                    cands = [jnp.where(ca == m, NEG_INF, ca) for ca in cands]
            tau = 0.5 * (last + m)
            half_inv_z = 0.5 / z_sum
            chunked = (PEER_N_KEYS // PEER_CHUNK_KEYS, PEER_CHUNK_KEYS, LANES)
            th_ref[:, 0, h, :, ls] = jnp.exp(tau - s1 - v2[0]).reshape(chunked)
            e1_ref[:, 0, h, :, ls] = (jnp.exp(s1 - v1[0]) * half_inv_z).reshape(chunked)
            e2_ref[0, h, :, ls] = jnp.exp(s2 - v2[0])
        return carry

    lax.fori_loop(0, PEER_HEADS, head, 0)


def _peer_select(h2, wq_t, subkeys_bf):
    T, D = h2.shape
    tm = PEER_TOKEN_TILE
    n_tiles = T // tm
    tok_spec = pl.BlockSpec((1, PEER_HEADS, PEER_N_KEYS, tm), lambda i: (i, 0, 0, 0))
    tok_shape = jax.ShapeDtypeStruct((n_tiles, PEER_HEADS, PEER_N_KEYS, tm), F32)
    n_chunks = PEER_N_KEYS // PEER_CHUNK_KEYS
    row_spec = pl.BlockSpec((n_chunks, 1, PEER_HEADS, PEER_CHUNK_KEYS, tm), lambda i: (0, i, 0, 0, 0))
    row_shape = jax.ShapeDtypeStruct((n_chunks, n_tiles, PEER_HEADS, PEER_CHUNK_KEYS, tm), F32)
    return pl.pallas_call(
        functools.partial(_peer_select_kernel, tm=tm),
        grid=(n_tiles,),
        in_specs=[
            pl.BlockSpec((tm, D), lambda i: (i, 0)),
            pl.BlockSpec((PEER_HEADS, PEER_QUERY_DIM, D), lambda i: (0, 0, 0)),
            pl.BlockSpec((PEER_HEADS, 2, PEER_N_KEYS, PEER_QUERY_DIM // 2), lambda i: (0, 0, 0, 0)),
        ],
        out_specs=[
            pl.BlockSpec((1, D, tm), lambda i: (i, 0, 0)),
            row_spec, row_spec, tok_spec,
        ],
        out_shape=[
            jax.ShapeDtypeStruct((n_tiles, D, tm), BF16),
            row_shape, row_shape, tok_shape,
        ],
        compiler_params=_params("parallel"),
        name="peer_select",
    )(h2, wq_t, subkeys_bf)


PEER_ROW_BLOCK_KEYS = 2
PEER_PIECE_TOKENS = 2 * LANES


def _peer_mix_kernel(ht_ref, u_ref, vt_ref, th_ref, e1_ref, e2_ref, x_ref, g2_ref,
                     xo_ref, acc_sc, act_sc, e2_sc, *, tm, n_tiles):
    j = pl.program_id(1)

    @pl.when(j == 0)
    def _():
        acc_sc[...] = jnp.zeros_like(acc_sc)
        for lc in range(tm // LANES):
            e2_sc[:, :, lc] = e2_ref[:, :, :, lc * LANES:(lc + 1) * LANES]

    rb = PEER_ROW_BLOCK_KEYS * PEER_N_KEYS
    tw = PEER_PIECE_TOKENS
    n_rb = PEER_CHUNK // rb
    pieces = [(r, c) for c in range(tm // tw) for r in range(n_rb)]

    def tile(t, carry):
        def scores(p):
            r, c = pieces[p]
            return jnp.dot(u_ref[r * rb:(r + 1) * rb, :], ht_ref[t, :, c * tw:(c + 1) * tw],
                           preferred_element_type=F32)

        def gates(p, a_blk, k):
            r, c = pieces[p]
            ii = r * PEER_ROW_BLOCK_KEYS + k
            for lc in range(tw // LANES):
                chunk = c * (tw // LANES) + lc
                ls = slice(chunk * LANES, (chunk + 1) * LANES)
                gate = None
                for h in range(PEER_HEADS):
                    theta = th_ref[0, t, h, ii:ii + 1, ls]
                    e1 = e1_ref[0, t, h, ii:ii + 1, ls]
                    e2 = e2_sc[t, h, chunk]
                    term = jnp.where(e2 >= theta, e1 * e2, 0.0)
                    gate = term if gate is None else gate + term
                a = a_blk[k * PEER_N_KEYS:(k + 1) * PEER_N_KEYS, lc * LANES:(lc + 1) * LANES]
                gelu2 = a * (1.0 + lax.erf(a * (1.0 / math.sqrt(2.0))))
                act_sc[c, ii * PEER_N_KEYS:(ii + 1) * PEER_N_KEYS, lc * LANES:(lc + 1) * LANES] = (
                    gelu2 * gate).astype(BF16)

        def mix(c):
            acc_sc[t, c] += jnp.dot(vt_ref[0], act_sc[c], preferred_element_type=F32)

        a_next = scores(0)
        for p in range(len(pieces)):
            r, c = pieces[p]
            a_cur = a_next
            if p + 1 < len(pieces):
                a_next = scores(p + 1)
            for k in range(PEER_ROW_BLOCK_KEYS):
                if k == PEER_ROW_BLOCK_KEYS // 2 and r == 0 and c > 0:
                    mix(c - 1)
                gates(p, a_cur, k)
        mix(tm // tw - 1)
        return carry

    lax.fori_loop(0, n_tiles, tile, 0)

    @pl.when(j == pl.num_programs(1) - 1)
    def _():
        for t in range(n_tiles):
            rows = slice(t * tm, (t + 1) * tm)
            mixed_t = jnp.concatenate([acc_sc[t, c] for c in range(tm // tw)], axis=1)
            xo_ref[rows, :] = x_ref[rows, :] + g2_ref[0] * mixed_t.T


def _peer_mix(h_t, u_bf, v_t, theta, e1, e2, x2, gate2, B, S):
    T, D = x2.shape
    tm = PEER_TOKEN_TILE
    n_tiles = min(PEER_TILES_PER_STEP, S // tm)
    span = n_tiles * tm
    steps_per_b = S // span
    tok_spec = pl.BlockSpec((n_tiles, PEER_HEADS, PEER_N_KEYS, tm), lambda i, j: (i, 0, 0, 0),
                            pipeline_mode=pl.Buffered(1))
    row_spec = pl.BlockSpec((1, n_tiles, PEER_HEADS, PEER_CHUNK_KEYS, tm), lambda i, j: (j, i, 0, 0, 0))
    return pl.pallas_call(
        functools.partial(_peer_mix_kernel, tm=tm, n_tiles=n_tiles),
        grid=(T // span, PEER_N_EXPERTS // PEER_CHUNK),
        in_specs=[
            pl.BlockSpec((n_tiles, D, tm), lambda i, j: (i, 0, 0)),
            pl.BlockSpec((PEER_CHUNK, D), lambda i, j: (j, 0)),
            pl.BlockSpec((1, D, PEER_CHUNK), lambda i, j: (j, 0, 0)),
            row_spec, row_spec, tok_spec,
            pl.BlockSpec((span, D), lambda i, j: (i, 0)),
            pl.BlockSpec((1, 1, D), lambda i, j: (i // steps_per_b, 0, 0)),
        ],
        out_specs=pl.BlockSpec((span, D), lambda i, j: (i, 0)),
        out_shape=jax.ShapeDtypeStruct((T, D), F32),
        scratch_shapes=[pltpu.VMEM((n_tiles, tm // PEER_PIECE_TOKENS, D, PEER_PIECE_TOKENS), F32),
                        pltpu.VMEM((tm // PEER_PIECE_TOKENS, PEER_CHUNK, PEER_PIECE_TOKENS), BF16),
                        pltpu.VMEM((n_tiles, PEER_HEADS, tm // LANES, PEER_N_KEYS, LANES), F32)],
        compiler_params=_params("parallel", "arbitrary"),
        name="peer_mix",
    )(h_t, u_bf, v_t, theta, e1, e2, x2, gate2.reshape(B, 1, D))


def _final_norm_kernel(x_ref, g_ref, o_ref):
    x = x_ref[...]
    ms = jnp.mean(x * x, axis=-1, keepdims=True)
    o_ref[...] = x * lax.rsqrt(ms + NORM_EPS) * g_ref[...]


def _final_norm(x2, g):
    T, D = x2.shape
    tm = min(1024, T)
    return pl.pallas_call(
        _final_norm_kernel,
        grid=(T // tm,),
        in_specs=[pl.BlockSpec((tm, D), lambda i: (i, 0)), pl.BlockSpec((1, D), lambda i: (0, 0))],
        out_specs=pl.BlockSpec((tm, D), lambda i: (i, 0)),
        out_shape=jax.ShapeDtypeStruct((T, D), F32),
        compiler_params=_params("parallel"),
        name="final_norm",
    )(x2, g.reshape(1, D))


def kernel(x, c, norm1_g, norm2_g, ada_w, ada_b, w_in, gate_b, diff_lambda, diff_subln_g,
           w_branch_sb, w_branch_diff, w_out, peer_wq, peer_subkeys, peer_u, peer_v, final_g):
    B, S, D = x.shape
    assert D == D_MODEL and S % LANES == 0
    x2 = x.reshape(B * S, D)
    mod = _modulation(c, ada_w, ada_b)
    rope = _rope_lane_tables(S)
    for l in range(DEPTH):
        lam_init = 0.8 - 0.6 * math.exp(-0.3 * l)
        shift1, scale1, gate1, shift2, scale2, gate2 = (mod[l, :, n] for n in range(6))
        proj = _input_projection(x2, norm1_g[l], scale1, shift1, w_in[l].astype(BF16), rope, B, S)
        y_sb = _sb_attention(proj, B, S)
        y_df = _diff_attention(proj, diff_lambda[l], diff_subln_g[l], lam_init, B, S)
        x2, h2 = _merge_project(y_sb, y_df, proj, gate_b[l], w_branch_sb[l].astype(BF16),
                                w_branch_diff[l].astype(BF16), w_out[l].astype(BF16), x2, gate1,
                                norm2_g[l], scale2, shift2, B, S)
        wq_t = peer_wq[l].T.astype(BF16).reshape(PEER_HEADS, PEER_QUERY_DIM, D)
        h_t, theta, e1, e2 = _peer_select(h2, wq_t, peer_subkeys[l].astype(BF16))
        v_t = jnp.swapaxes(peer_v[l].astype(BF16).reshape(-1, PEER_CHUNK, D), 1, 2)
        x2 = _peer_mix(h_t, peer_u[l].astype(BF16), v_t,
                       theta, e1, e2, x2, gate2, B, S)
    return _final_norm(x2, final_g).reshape(B, S, D)
```

```python
import functools
import math

import jax
import jax.numpy as jnp
from jax import lax
from jax.experimental import pallas as pl
from jax.experimental.pallas import tpu as pltpu

F32 = jnp.float32
BF16 = jnp.bfloat16

D_MODEL = 1024
DEPTH = 2
SB_HEADS = 8
SB_HEAD_DIM = 64
SB_WIDTH = SB_HEADS * SB_HEAD_DIM
DIFF_HEADS = 4
DIFF_QK_DIM = 64
DIFF_V_DIM = 2 * DIFF_QK_DIM
DIFF_QK_WIDTH = DIFF_HEADS * 2 * DIFF_QK_DIM
DIFF_V_WIDTH = DIFF_HEADS * DIFF_V_DIM
ROPE_THETA = 500000.0
ROT_DIM = DIFF_QK_DIM // 4
IN_COLS = 3 * SB_WIDTH + 2 * DIFF_QK_WIDTH + DIFF_V_WIDTH + 2 * D_MODEL
PEER_HEADS = 8
PEER_N_KEYS = 128
PEER_N_EXPERTS = PEER_N_KEYS * PEER_N_KEYS
PEER_TOPK = 16
PEER_QUERY_DIM = 256
PEER_CHUNK_KEYS = 16
PEER_CHUNK = PEER_CHUNK_KEYS * PEER_N_KEYS
PEER_TOKEN_TILE = 512
PEER_TILES_PER_STEP = 2
NORM_EPS = 1e-6

LANES = 128
SUBLANES = 8
VMEM_LIMIT = 60 * 1024 * 1024
NEG_INF = float("-inf")
LOG2_E = 1.4426950408889634

IN_TILE_N = 512
COL_SB_Q, COL_SB_K, COL_SB_V = 0, SB_WIDTH // LANES, 2 * SB_WIDTH // LANES
COL_D_Q = 3 * SB_WIDTH // LANES
COL_D_K = COL_D_Q + DIFF_QK_WIDTH // LANES
COL_D_V = COL_D_K + DIFF_QK_WIDTH // LANES
COL_G = (3 * SB_WIDTH + 2 * DIFF_QK_WIDTH + DIFF_V_WIDTH) // D_MODEL


def _params(*sem):
    return pltpu.CompilerParams(dimension_semantics=sem, vmem_limit_bytes=VMEM_LIMIT)


def _nt_dot(a, b):
    return lax.dot_general(a, b, (((1,), (1,)), ((), ())), preferred_element_type=F32)


def _mod_kernel(c_ref, w_ref, b_ref, o_ref):
    c = c_ref[...]
    c_act = c * jax.nn.sigmoid(c)
    o_ref[0] = jnp.dot(c_act, w_ref[0], preferred_element_type=F32,
                       precision=lax.Precision.HIGHEST) + b_ref[0]


def _modulation(c, ada_w, ada_b):
    B, D = c.shape
    rows = SUBLANES
    c_pad = jnp.pad(c, ((0, rows - B), (0, 0)))
    out = pl.pallas_call(
        _mod_kernel,
        grid=(DEPTH, 6),
        in_specs=[
            pl.BlockSpec((rows, D), lambda l, n: (0, 0)),
            pl.BlockSpec((1, D, D), lambda l, n: (l, 0, n)),
            pl.BlockSpec((1, 1, D), lambda l, n: (l, 0, n)),
        ],
        out_specs=pl.BlockSpec((1, rows, D), lambda l, n: (l, 0, n)),
        out_shape=jax.ShapeDtypeStruct((DEPTH, rows, 6 * D), F32),
        compiler_params=_params("arbitrary", "arbitrary"),
        name="adaln_mod",
    )(c_pad, ada_w, ada_b.reshape(DEPTH, 1, 6 * D))
    return out[:, :B].reshape(DEPTH, B, 6, D)


def _norm_mod(x, g, scale, shift):
    ms = jnp.mean(x * x, axis=-1, keepdims=True)
    return (x * lax.rsqrt(ms + NORM_EPS) * g) * (1.0 + scale) + shift


def _inproj_kernel(x_ref, g_ref, sc_ref, sh_ref, w_ref, ra_ref, rb_ref, rc_ref, o_ref):
    h = _norm_mod(x_ref[...], g_ref[...], sc_ref[0], sh_ref[0]).astype(BF16)
    half = ROT_DIM // 2
    for j in range(IN_COLS // IN_TILE_N):
        cols = slice(j * IN_TILE_N, (j + 1) * IN_TILE_N)
        r = jnp.dot(h, w_ref[:, cols], preferred_element_type=F32)
        first_lane_block = j * IN_TILE_N // LANES
        if first_lane_block in (COL_SB_Q, COL_D_Q):
            r = r * (LOG2_E / math.sqrt(SB_HEAD_DIM))
        if first_lane_block in (COL_D_Q, COL_D_K):
            ra, rb, rc = ra_ref[...], rb_ref[...], rc_ref[...]
            for blk in range(IN_TILE_N // LANES):
                t = r[:, blk * LANES:(blk + 1) * LANES]
                rot = t * ra + pltpu.roll(t, LANES - half, 1) * rb + pltpu.roll(t, half, 1) * rc
                o_ref[:, j * IN_TILE_N + blk * LANES:j * IN_TILE_N + (blk + 1) * LANES] = rot.astype(BF16)
        else:
            o_ref[:, cols] = r.astype(BF16)


def _rope_lane_tables(seq_len):
    pos = jnp.arange(seq_len, dtype=F32)
    inv_freq = 1.0 / (ROPE_THETA ** (jnp.arange(0, ROT_DIM, 2, dtype=F32) / ROT_DIM))
    ang = pos[:, None] * inv_freq[None, :]
    cos, sin = jnp.cos(ang), jnp.sin(ang)
    half = ROT_DIM // 2
    lane = jnp.arange(LANES) % DIFF_QK_DIM
    f = lane % half
    first, second = lane < half, (lane >= half) & (lane < ROT_DIM)
    ra = jnp.where((first | second)[None, :], cos[:, f], 1.0)
    rb = jnp.where(first[None, :], -sin[:, f], 0.0)
    rc = jnp.where(second[None, :], sin[:, f], 0.0)
    return ra, rb, rc


def _input_projection(x2, norm_g, scale, shift, w_in_bf, rope, B, S):
    T, D = x2.shape
    tm = min(512, S)
    tiles_per_b = S // tm
    ra, rb, rc = rope
    rope_spec = pl.BlockSpec((tm, LANES), lambda i: (i % tiles_per_b, 0))
    mod_spec = pl.BlockSpec((1, 1, D), lambda i: (i // tiles_per_b, 0, 0))
    assert COL_D_K - COL_D_Q == IN_TILE_N // LANES and COL_D_V - COL_D_K == IN_TILE_N // LANES
    return pl.pallas_call(
        _inproj_kernel,
        grid=(T // tm,),
        in_specs=[
            pl.BlockSpec((tm, D), lambda i: (i, 0)),
            pl.BlockSpec((1, D), lambda i: (0, 0)),
            mod_spec, mod_spec,
            pl.BlockSpec((D, IN_COLS), lambda i: (0, 0)),
            rope_spec, rope_spec, rope_spec,
        ],
        out_specs=pl.BlockSpec((tm, IN_COLS), lambda i: (i, 0)),
        out_shape=jax.ShapeDtypeStruct((T, IN_COLS), BF16),
        compiler_params=_params("parallel"),
        name="in_proj",
    )(x2, norm_g.reshape(1, D), scale.reshape(B, 1, D), shift.reshape(B, 1, D), w_in_bf, ra, rb, rc)


def _sb_kernel(q_ref, k_ref, v_ref, o_ref, acc_sc, cs_sc, *, tb, n_pairs):
    i = pl.program_id(2)
    n_heads = 2 * n_pairs
    lane = lax.broadcasted_iota(jnp.int32, (tb, LANES), 1)
    q_heads = []
    for p in range(n_pairs):
        q = q_ref[:, p * LANES:(p + 1) * LANES]
        zero = jnp.zeros_like(q)
        q_heads += [jnp.where(lane < SB_HEAD_DIM, q, zero), jnp.where(lane >= SB_HEAD_DIM, q, zero)]
    row = lax.broadcasted_iota(jnp.int32, (tb, tb), 0)
    col = lax.broadcasted_iota(jnp.int32, (tb, tb), 1)
    strictly_before = col < row
    suffix_ones = jnp.where(row >= col, 1.0, 0.0).astype(BF16)

    acc_sc[...] = jnp.zeros_like(acc_sc)
    cs_sc[...] = jnp.zeros_like(cs_sc)

    def block(j, masked):
        start = pl.multiple_of(j * tb, tb)
        kb = k_ref[pl.ds(start, tb), :]
        vb = v_ref[pl.ds(start, tb), :]
        ys = [_nt_dot(q_heads[h], kb[:, (h // 2) * LANES:(h // 2 + 1) * LANES]) for h in range(n_heads)]
        costs = []
        for y in ys:
            neg_abs = lax.bitcast_convert_type(
                lax.bitcast_convert_type(y, jnp.uint32) | jnp.uint32(0x80000000), F32)
            cost = jnp.maximum(y, 0.0) + jnp.log2(1.0 + jnp.exp2(neg_abs))
            if masked:
                cost = jnp.where(strictly_before, cost, 0.0)
            costs.append(cost)
        sums = [jnp.dot(cost.astype(BF16), suffix_ones, preferred_element_type=F32) for cost in costs]
        for h in range(n_heads):
            later = cs_sc[h]
            w = jnp.exp2(ys[h] - sums[h] - later)
            if masked:
                w = jnp.where(strictly_before, w, 0.0)
            acc_sc[h] += jnp.dot(w.astype(BF16), vb[:, (h // 2) * LANES:(h // 2 + 1) * LANES],
                                 preferred_element_type=F32)
            cs_sc[h] = later + sums[h][:, 0:1]

    block(i, True)

    def body(jj, carry):
        block(i - 1 - jj, False)
        return carry

    lax.fori_loop(0, i, body, 0)
    for p in range(n_pairs):
        o_ref[:, p * LANES:(p + 1) * LANES] = jnp.where(
            lane < SB_HEAD_DIM, acc_sc[2 * p], acc_sc[2 * p + 1]).astype(BF16)


SB_PAIRS_PER_STEP = 4


def _sb_attention(proj, B, S):
    T = proj.shape[0]
    tb = min(256, S)
    nq = S // tb
    n_pairs = SB_PAIRS_PER_STEP
    width = n_pairs * LANES
    groups = SB_WIDTH // width
    return pl.pallas_call(
        functools.partial(_sb_kernel, tb=tb, n_pairs=n_pairs),
        grid=(B, groups, nq),
        in_specs=[
            pl.BlockSpec((tb, width), lambda b, p, i: (b * nq + i, COL_SB_Q // n_pairs + p)),
            pl.BlockSpec((S, width), lambda b, p, i: (b, COL_SB_K // n_pairs + p)),
            pl.BlockSpec((S, width), lambda b, p, i: (b, COL_SB_V // n_pairs + p)),
        ],
        out_specs=pl.BlockSpec((tb, width), lambda b, p, i: (b * nq + i, p)),
        out_shape=jax.ShapeDtypeStruct((T, SB_WIDTH), BF16),
        scratch_shapes=[pltpu.VMEM((2 * n_pairs, tb, LANES), F32),
                        pltpu.VMEM((2 * n_pairs, tb, 1), F32)],
        compiler_params=_params("parallel", "parallel", "arbitrary"),
        name="sb_attention",
    )(proj, proj, proj)


def _diff_kernel(lp_ref, g_ref, q_ref, k_ref, v_ref, o_ref, m_sc, acc_sc, *, tq, tk, lam_init):
    i = pl.program_id(1)
    lane = lax.broadcasted_iota(jnp.int32, (tq, LANES), 1)
    q_maps = []
    for h in range(DIFF_HEADS):
        q = q_ref[:, h * LANES:(h + 1) * LANES]
        zero = jnp.zeros_like(q)
        q_maps += [jnp.where(lane < DIFF_QK_DIM, q, zero), jnp.where(lane >= DIFF_QK_DIM, q, zero)]
    n_chains = len(q_maps)
    m_sc[...] = jnp.full_like(m_sc, NEG_INF)
    acc_sc[...] = jnp.zeros_like(acc_sc)
    q_pos = i * tq + lax.broadcasted_iota(jnp.int32, (tq, tk), 0)
    k_off = lax.broadcasted_iota(jnp.int32, (tq, tk), 1)
    ones = jnp.ones((tk, LANES), BF16)

    def block(j, masked):
        start = pl.multiple_of(j * tk, tk)
        kb = k_ref[pl.ds(start, tk), :]
        vb = v_ref[pl.ds(start, tk), :]
        head_cols = lambda c: slice((c // 2) * LANES, (c // 2 + 1) * LANES)
        scores = [_nt_dot(q_maps[c], kb[:, head_cols(c)]) for c in range(n_chains)]
        if masked:
            visible = j * tk + k_off <= q_pos
            scores = [jnp.where(visible, s, NEG_INF) for s in scores]
        m_prev = [m_sc[c] for c in range(n_chains)]
        m_new = [jnp.maximum(m_prev[c], jnp.max(scores[c], axis=1, keepdims=True)) for c in range(n_chains)]
        probs = [jnp.exp2(scores[c] - jnp.concatenate([m_new[c]] * (tk // LANES), axis=1))
                 for c in range(n_chains)]
        for c in range(n_chains):
            alpha = jnp.exp2(m_prev[c] - m_new[c])
            v_ext = jnp.concatenate([vb[:, head_cols(c)], ones], axis=1)
            acc_sc[c] = (jnp.concatenate([alpha, alpha], axis=1) * acc_sc[c]
                         + jnp.dot(probs[c].astype(BF16), v_ext, preferred_element_type=F32))
            m_sc[c] = m_new[c]

    n_full = (i * tq) // tk

    def body(j, carry):
        block(j, False)
        return carry

    lax.fori_loop(0, n_full, body, 0)
    block(n_full, True)

    lp = lp_ref[...]
    lam = (jnp.exp(jnp.sum(lp[0:1] * lp[1:2], axis=1, keepdims=True))
           - jnp.exp(jnp.sum(lp[2:3] * lp[3:4], axis=1, keepdims=True)) + lam_init)
    for h in range(DIFF_HEADS):
        a1, a2 = acc_sc[2 * h], acc_sc[2 * h + 1]
        o = a1[:, :LANES] / a1[:, LANES:] - lam * (a2[:, :LANES] / a2[:, LANES:])
        ms = jnp.mean(o * o, axis=-1, keepdims=True)
        y = (o * lax.rsqrt(ms + NORM_EPS) * g_ref[...]) * (1.0 - lam_init)
        o_ref[:, h * LANES:(h + 1) * LANES] = y.astype(BF16)


def _diff_attention(proj, lam_params, subln_g, lam_init, B, S):
    T = proj.shape[0]
    tq = min(256, S)
    tk = min(512, S)
    nq = S // tq
    n_chains = 2 * DIFF_HEADS
    return pl.pallas_call(
        functools.partial(_diff_kernel, tq=tq, tk=tk, lam_init=lam_init),
        grid=(B, nq),
        in_specs=[
            pl.BlockSpec((4, DIFF_QK_DIM), lambda b, i: (0, 0)),
            pl.BlockSpec((1, DIFF_V_DIM), lambda b, i: (0, 0)),
            pl.BlockSpec((tq, DIFF_QK_WIDTH), lambda b, i: (b * nq + i, COL_D_Q * LANES // DIFF_QK_WIDTH)),
            pl.BlockSpec((S, DIFF_QK_WIDTH), lambda b, i: (b, COL_D_K * LANES // DIFF_QK_WIDTH)),
            pl.BlockSpec((S, DIFF_V_WIDTH), lambda b, i: (b, COL_D_V * LANES // DIFF_V_WIDTH)),
        ],
        out_specs=pl.BlockSpec((tq, DIFF_V_WIDTH), lambda b, i: (b * nq + i, 0)),
        out_shape=jax.ShapeDtypeStruct((T, DIFF_V_WIDTH), BF16),
        scratch_shapes=[pltpu.VMEM((n_chains, tq, LANES), F32),
                        pltpu.VMEM((n_chains, tq, 2 * LANES), F32)],
        compiler_params=_params("parallel", "arbitrary"),
        name="diff_attention",
    )(lam_params, subln_g.reshape(1, DIFF_V_DIM), proj, proj, proj)


def _merge_kernel(ysb_ref, ydf_ref, gs_ref, gd_ref, gb_ref, wbs_ref, wbd_ref, wo_ref,
                  x_ref, g1_ref, n2_ref, sc2_ref, sh2_ref, xo_ref, h2_ref):
    D = x_ref.shape[1]
    a = jnp.dot(ysb_ref[...], wbs_ref[...], preferred_element_type=F32)
    b = jnp.dot(ydf_ref[...], wbd_ref[...], preferred_element_type=F32)
    gb = gb_ref[...]
    g_sb = jax.nn.sigmoid(gs_ref[...].astype(F32) + gb[:, :D])
    g_df = jax.nn.sigmoid(gd_ref[...].astype(F32) + gb[:, D:])
    merged = g_sb * a + g_df * b
    y = jnp.dot(merged.astype(BF16), wo_ref[...], preferred_element_type=F32)
    xn = x_ref[...] + g1_ref[0] * y
    xo_ref[...] = xn
    h2_ref[...] = _norm_mod(xn, n2_ref[...], sc2_ref[0], sh2_ref[0]).astype(BF16)


def _merge_project(y_sb, y_df, proj, gate_b, wbs, wbd, wo, x2, gate1, norm2_g, scale2, shift2, B, S):
    T, D = x2.shape
    tm = min(512, S)
    tiles_per_b = S // tm
    mod_spec = pl.BlockSpec((1, 1, D), lambda i: (i // tiles_per_b, 0, 0))
    const = lambda shape: pl.BlockSpec(shape, lambda i: (0, 0))
    return pl.pallas_call(
        _merge_kernel,
        grid=(T // tm,),
        in_specs=[
            pl.BlockSpec((tm, SB_WIDTH), lambda i: (i, 0)),
            pl.BlockSpec((tm, DIFF_V_WIDTH), lambda i: (i, 0)),
            pl.BlockSpec((tm, D), lambda i: (i, COL_G)),
            pl.BlockSpec((tm, D), lambda i: (i, COL_G + 1)),
            const((1, 2 * D)),
            const((SB_WIDTH, D)), const((DIFF_V_WIDTH, D)), const((D, D)),
            pl.BlockSpec((tm, D), lambda i: (i, 0)),
            mod_spec,
            const((1, D)),
            mod_spec, mod_spec,
        ],
        out_specs=[pl.BlockSpec((tm, D), lambda i: (i, 0)), pl.BlockSpec((tm, D), lambda i: (i, 0))],
        out_shape=[jax.ShapeDtypeStruct((T, D), F32), jax.ShapeDtypeStruct((T, D), BF16)],
        compiler_params=_params("parallel"),
        name="merge_out_proj",
    )(y_sb, y_df, proj, proj, gate_b.reshape(1, 2 * D), wbs, wbd, wo, x2,
      gate1.reshape(B, 1, D), norm2_g.reshape(1, D), scale2.reshape(B, 1, D), shift2.reshape(B, 1, D))


def _sort16_network():
    def merge(lo, hi, r):
        step = r * 2
        if step < hi - lo:
            yield from merge(lo, hi, step)
            yield from merge(lo + r, hi, step)
            yield from [(i, i + r) for i in range(lo + r, hi - r, step)]
        else:
            yield (lo, lo + r)

    def sort(lo, hi):
        if hi - lo >= 1:
            mid = lo + (hi - lo) // 2
            yield from sort(lo, mid)
            yield from sort(mid + 1, hi)
            yield from merge(lo, hi, 1)

    return list(sort(0, 15))


def _top_values(scores, k):
    groups = scores.shape[0] // SUBLANES
    assert groups == 16 and k <= groups + 1
    v = [scores[SUBLANES * g:SUBLANES * (g + 1)] for g in range(groups)]
    for a, b in _sort16_network():
        v[a], v[b] = jnp.maximum(v[a], v[b]), jnp.minimum(v[a], v[b])
    v.append(jnp.full_like(v[0], NEG_INF))
    vals = []
    for r in range(k):
        m = jnp.max(v[0], axis=0, keepdims=True)
        vals.append(m)
        still_needed = k - 1 - r
        if still_needed:
            hit = v[0] == m
            for d in range(min(still_needed, groups)):
                v[d] = jnp.where(hit, v[d + 1], v[d])
    return vals


def _candidate_sums(v1, v1_all, v2, v2_all, row_id):
    assert PEER_TOPK == 16 and SUBLANES == 8
    lo2 = v2_all[0:8]
    hi1 = v1_all[8:16]
    shift = lambda x, n: pltpu.roll(x, n, 0)
    groups = [v1[0] + v2_all[0:8], v1[0] + v2_all[8:16], v1[0] + v2_all[16:24],
              v1[1] + lo2]
    groups.append(jnp.where(row_id < 5, v1[2] + lo2, v1[4] + shift(lo2, 5)))
    groups.append(jnp.where(row_id < 4, v1[3] + lo2,
                            jnp.where(row_id < 6, v1[5] + shift(lo2, 4),
                                      v1[6] + shift(lo2, 6))))
    tail = shift(hi1, 2) + v2[0]
    groups.append(jnp.where(row_id < 2, v1[7] + lo2, tail))
    groups.append(jnp.where(row_id < 2, tail,
                            jnp.where(row_id == 2, v1[16] + v2[0], NEG_INF)))
    return groups


def _peer_select_kernel(h_ref, wq_ref, sk_ref, ht_ref, th_ref, e1_ref, e2_ref, *, tm):
    hb = h_ref[...]
    ht_ref[0] = hb.T
    K = PEER_TOPK + 1
    row_id = lax.broadcasted_iota(jnp.int32, (SUBLANES, LANES), 0)
    pad_rows = jnp.full((3 * SUBLANES - K, LANES), NEG_INF, F32)

    def head(h, carry):
        q_t = _nt_dot(wq_ref[h], hb).astype(BF16)
        half = PEER_QUERY_DIM // 2
        sc = [jnp.dot(sk_ref[h, c], q_t[c * half:(c + 1) * half], preferred_element_type=F32)
              for c in range(2)]
        for lc in range(tm // LANES):
            ls = slice(lc * LANES, (lc + 1) * LANES)
            s1, s2 = sc[0][:, ls], sc[1][:, ls]
            v1 = _top_values(s1, K)
            v2 = _top_values(s2, K)
            v2_all = jnp.concatenate(v2 + [pad_rows], axis=0)
            v1_all = jnp.concatenate(v1 + [pad_rows], axis=0)
            cands = _candidate_sums(v1, v1_all, v2, v2_all, row_id)
            best = cands[0][0:1]
            z_sum = jnp.zeros_like(best)
            last = best
            for r in range(K):
                stacked = cands[0]
                for ca in cands[1:]:
                    stacked = jnp.maximum(stacked, ca)
                m = jnp.max(stacked, axis=0, keepdims=True)
                if r < PEER_TOPK:
                    z_sum = z_sum + jnp.exp(m - best)
                    last = m
                    cands = [jnp.where(ca == m, NEG_INF, ca) for ca in cands]
            tau = 0.5 * (last + m)
            half_inv_z = 0.5 / z_sum
            chunked = (PEER_N_KEYS // PEER_CHUNK_KEYS, PEER_CHUNK_KEYS, LANES)
            th_ref[:, 0, h, :, ls] = jnp.exp(tau - s1 - v2[0]).reshape(chunked)
            e1_ref[:, 0, h, :, ls] = (jnp.exp(s1 - v1[0]) * half_inv_z).reshape(chunked)
            e2_ref[0, h, :, ls] = jnp.exp(s2 - v2[0])
        return carry

    lax.fori_loop(0, PEER_HEADS, head, 0)


def _peer_select(h2, wq_t, subkeys_bf):
    T, D = h2.shape
    tm = PEER_TOKEN_TILE
    n_tiles = T // tm
    tok_spec = pl.BlockSpec((1, PEER_HEADS, PEER_N_KEYS, tm), lambda i: (i, 0, 0, 0))
    tok_shape = jax.ShapeDtypeStruct((n_tiles, PEER_HEADS, PEER_N_KEYS, tm), F32)
    n_chunks = PEER_N_KEYS // PEER_CHUNK_KEYS
    row_spec = pl.BlockSpec((n_chunks, 1, PEER_HEADS, PEER_CHUNK_KEYS, tm), lambda i: (0, i, 0, 0, 0))
    row_shape = jax.ShapeDtypeStruct((n_chunks, n_tiles, PEER_HEADS, PEER_CHUNK_KEYS, tm), F32)
    return pl.pallas_call(
        functools.partial(_peer_select_kernel, tm=tm),
        grid=(n_tiles,),
        in_specs=[
            pl.BlockSpec((tm, D), lambda i: (i, 0)),
            pl.BlockSpec((PEER_HEADS, PEER_QUERY_DIM, D), lambda i: (0, 0, 0)),
            pl.BlockSpec((PEER_HEADS, 2, PEER_N_KEYS, PEER_QUERY_DIM // 2), lambda i: (0, 0, 0, 0)),
        ],
        out_specs=[
            pl.BlockSpec((1, D, tm), lambda i: (i, 0, 0)),
            row_spec, row_spec, tok_spec,
        ],
        out_shape=[
            jax.ShapeDtypeStruct((n_tiles, D, tm), BF16),
            row_shape, row_shape, tok_shape,
        ],
        compiler_params=_params("parallel"),
        name="peer_select",
    )(h2, wq_t, subkeys_bf)


PEER_ROW_BLOCK_KEYS = 2
PEER_PIECE_TOKENS = 2 * LANES


def _peer_mix_kernel(ht_ref, u_ref, vt_ref, th_ref, e1_ref, e2_ref, x_ref, g2_ref, fg_ref,
                     xo_ref, acc_sc, act_sc, e2_sc, *, tm, n_tiles, final_norm):
    j = pl.program_id(1)

    @pl.when(j == 0)
    def _():
        acc_sc[...] = jnp.zeros_like(acc_sc)
        for lc in range(tm // LANES):
            e2_sc[:, :, lc] = e2_ref[:, :, :, lc * LANES:(lc + 1) * LANES]

    rb = PEER_ROW_BLOCK_KEYS * PEER_N_KEYS
    tw = PEER_PIECE_TOKENS
    n_rb = PEER_CHUNK // rb
    pieces = [(r, c) for c in range(tm // tw) for r in range(n_rb)]

    def tile(t, carry):
        def scores(p):
            r, c = pieces[p]
            return jnp.dot(u_ref[r * rb:(r + 1) * rb, :], ht_ref[t, :, c * tw:(c + 1) * tw],
                           preferred_element_type=F32)

        def gates(p, a_blk, k):
            r, c = pieces[p]
            ii = r * PEER_ROW_BLOCK_KEYS + k
            for lc in range(tw // LANES):
                chunk = c * (tw // LANES) + lc
                ls = slice(chunk * LANES, (chunk + 1) * LANES)
                gate = None
                for h in range(PEER_HEADS):
                    theta = th_ref[0, t, h, ii:ii + 1, ls]
                    e1 = e1_ref[0, t, h, ii:ii + 1, ls]
                    e2 = e2_sc[t, h, chunk]
                    term = jnp.where(e2 >= theta, e1 * e2, 0.0)
                    gate = term if gate is None else gate + term
                a = a_blk[k * PEER_N_KEYS:(k + 1) * PEER_N_KEYS, lc * LANES:(lc + 1) * LANES]
                gelu2 = a * (1.0 + lax.erf(a * (1.0 / math.sqrt(2.0))))
                act_sc[c, ii * PEER_N_KEYS:(ii + 1) * PEER_N_KEYS, lc * LANES:(lc + 1) * LANES] = (
                    gelu2 * gate).astype(BF16)

        def mix(c):
            acc_sc[t, c] += jnp.dot(vt_ref[0], act_sc[c], preferred_element_type=F32)

        a_next = scores(0)
        for p in range(len(pieces)):
            r, c = pieces[p]
            a_cur = a_next
            if p + 1 < len(pieces):
                a_next = scores(p + 1)
            for k in range(PEER_ROW_BLOCK_KEYS):
                if k == PEER_ROW_BLOCK_KEYS // 2 and r == 0 and c > 0:
                    mix(c - 1)
                gates(p, a_cur, k)
        mix(tm // tw - 1)
        return carry

    lax.fori_loop(0, n_tiles, tile, 0)

    @pl.when(j == pl.num_programs(1) - 1)
    def _():
        for t in range(n_tiles):
            rows = slice(t * tm, (t + 1) * tm)
            mixed_t = jnp.concatenate([acc_sc[t, c] for c in range(tm // tw)], axis=1)
            xn = x_ref[rows, :] + g2_ref[0] * mixed_t.T
            if final_norm:
                ms = jnp.mean(xn * xn, axis=-1, keepdims=True)
                xn = xn * lax.rsqrt(ms + NORM_EPS) * fg_ref[...]
            xo_ref[rows, :] = xn


def _peer_mix(h_t, u_bf, v_t, theta, e1, e2, x2, gate2, final_g, final_norm, B, S):
    T, D = x2.shape
    tm = PEER_TOKEN_TILE
    n_tiles = min(PEER_TILES_PER_STEP, S // tm)
    span = n_tiles * tm
    steps_per_b = S // span
    tok_spec = pl.BlockSpec((n_tiles, PEER_HEADS, PEER_N_KEYS, tm), lambda i, j: (i, 0, 0, 0),
                            pipeline_mode=pl.Buffered(1))
    row_spec = pl.BlockSpec((1, n_tiles, PEER_HEADS, PEER_CHUNK_KEYS, tm), lambda i, j: (j, i, 0, 0, 0))
    return pl.pallas_call(
        functools.partial(_peer_mix_kernel, tm=tm, n_tiles=n_tiles, final_norm=final_norm),
        grid=(T // span, PEER_N_EXPERTS // PEER_CHUNK),
        in_specs=[
            pl.BlockSpec((n_tiles, D, tm), lambda i, j: (i, 0, 0)),
            pl.BlockSpec((PEER_CHUNK, D), lambda i, j: (j, 0)),
            pl.BlockSpec((1, D, PEER_CHUNK), lambda i, j: (j, 0, 0)),
            row_spec, row_spec, tok_spec,
            pl.BlockSpec((span, D), lambda i, j: (i, 0)),
            pl.BlockSpec((1, 1, D), lambda i, j: (i // steps_per_b, 0, 0)),
            pl.BlockSpec((1, D), lambda i, j: (0, 0)),
        ],
        out_specs=pl.BlockSpec((span, D), lambda i, j: (i, 0)),
        out_shape=jax.ShapeDtypeStruct((T, D), F32),
        scratch_shapes=[pltpu.VMEM((n_tiles, tm // PEER_PIECE_TOKENS, D, PEER_PIECE_TOKENS), F32),
                        pltpu.VMEM((tm // PEER_PIECE_TOKENS, PEER_CHUNK, PEER_PIECE_TOKENS), BF16),
                        pltpu.VMEM((n_tiles, PEER_HEADS, tm // LANES, PEER_N_KEYS, LANES), F32)],
        compiler_params=_params("parallel", "arbitrary"),
        name="peer_mix",
    )(h_t, u_bf, v_t, theta, e1, e2, x2, gate2.reshape(B, 1, D), final_g.reshape(1, D))


def kernel(x, c, norm1_g, norm2_g, ada_w, ada_b, w_in, gate_b, diff_lambda, diff_subln_g,
           w_branch_sb, w_branch_diff, w_out, peer_wq, peer_subkeys, peer_u, peer_v, final_g):
    B, S, D = x.shape
    assert D == D_MODEL and S % LANES == 0
    x2 = x.reshape(B * S, D)
    mod = _modulation(c, ada_w, ada_b)
    rope = _rope_lane_tables(S)
    for l in range(DEPTH):
        lam_init = 0.8 - 0.6 * math.exp(-0.3 * l)
        shift1, scale1, gate1, shift2, scale2, gate2 = (mod[l, :, n] for n in range(6))
        proj = _input_projection(x2, norm1_g[l], scale1, shift1, w_in[l].astype(BF16), rope, B, S)
        y_sb = _sb_attention(proj, B, S)
        y_df = _diff_attention(proj, diff_lambda[l], diff_subln_g[l], lam_init, B, S)
        x2, h2 = _merge_project(y_sb, y_df, proj, gate_b[l], w_branch_sb[l].astype(BF16),
                                w_branch_diff[l].astype(BF16), w_out[l].astype(BF16), x2, gate1,
                                norm2_g[l], scale2, shift2, B, S)
        wq_t = peer_wq[l].T.astype(BF16).reshape(PEER_HEADS, PEER_QUERY_DIM, D)
        h_t, theta, e1, e2 = _peer_select(h2, wq_t, peer_subkeys[l].astype(BF16))
        v_t = jnp.swapaxes(peer_v[l].astype(BF16).reshape(-1, PEER_CHUNK, D), 1, 2)
        x2 = _peer_mix(h_t, peer_u[l].astype(BF16), v_t,
                       theta, e1, e2, x2, gate2, final_g, l == DEPTH - 1, B, S)
    return x2.reshape(B, S, D)
```

```python
import functools
import math

import jax
import jax.numpy as jnp
from jax import lax
from jax.experimental import pallas as pl
from jax.experimental.pallas import tpu as pltpu

F32 = jnp.float32
BF16 = jnp.bfloat16

D_MODEL = 1024
DEPTH = 2
SB_HEADS = 8
SB_HEAD_DIM = 64
SB_WIDTH = SB_HEADS * SB_HEAD_DIM
DIFF_HEADS = 4
DIFF_QK_DIM = 64
DIFF_V_DIM = 2 * DIFF_QK_DIM
DIFF_QK_WIDTH = DIFF_HEADS * 2 * DIFF_QK_DIM
DIFF_V_WIDTH = DIFF_HEADS * DIFF_V_DIM
ROPE_THETA = 500000.0
ROT_DIM = DIFF_QK_DIM // 4
IN_COLS = 3 * SB_WIDTH + 2 * DIFF_QK_WIDTH + DIFF_V_WIDTH + 2 * D_MODEL
PEER_HEADS = 8
PEER_N_KEYS = 128
PEER_N_EXPERTS = PEER_N_KEYS * PEER_N_KEYS
PEER_TOPK = 16
PEER_QUERY_DIM = 256
PEER_CHUNK_KEYS = 16
PEER_CHUNK = PEER_CHUNK_KEYS * PEER_N_KEYS
PEER_TOKEN_TILE = 512
PEER_TILES_PER_STEP = 2
NORM_EPS = 1e-6

LANES = 128
SUBLANES = 8
VMEM_LIMIT = 60 * 1024 * 1024
NEG_INF = float("-inf")
LOG2_E = 1.4426950408889634

IN_TILE_N = 512
COL_SB_Q, COL_SB_K, COL_SB_V = 0, SB_WIDTH // LANES, 2 * SB_WIDTH // LANES
COL_D_Q = 3 * SB_WIDTH // LANES
COL_D_K = COL_D_Q + DIFF_QK_WIDTH // LANES
COL_D_V = COL_D_K + DIFF_QK_WIDTH // LANES
COL_G = (3 * SB_WIDTH + 2 * DIFF_QK_WIDTH + DIFF_V_WIDTH) // D_MODEL


def _params(*sem):
    return pltpu.CompilerParams(dimension_semantics=sem, vmem_limit_bytes=VMEM_LIMIT)


def _nt_dot(a, b):
    return lax.dot_general(a, b, (((1,), (1,)), ((), ())), preferred_element_type=F32)


def _mod_kernel(c_ref, w_ref, b_ref, o_ref):
    c = c_ref[...]
    c_act = c * jax.nn.sigmoid(c)
    o_ref[0] = jnp.dot(c_act, w_ref[0], preferred_element_type=F32,
                       precision=lax.Precision.HIGHEST) + b_ref[0]


def _modulation(c, ada_w, ada_b):
    B, D = c.shape
    rows = SUBLANES
    c_pad = jnp.pad(c, ((0, rows - B), (0, 0)))
    out = pl.pallas_call(
        _mod_kernel,
        grid=(DEPTH, 6),
        in_specs=[
            pl.BlockSpec((rows, D), lambda l, n: (0, 0)),
            pl.BlockSpec((1, D, D), lambda l, n: (l, 0, n)),
            pl.BlockSpec((1, 1, D), lambda l, n: (l, 0, n)),
        ],
        out_specs=pl.BlockSpec((1, rows, D), lambda l, n: (l, 0, n)),
        out_shape=jax.ShapeDtypeStruct((DEPTH, rows, 6 * D), F32),
        compiler_params=_params("arbitrary", "arbitrary"),
        name="adaln_mod",
    )(c_pad, ada_w, ada_b.reshape(DEPTH, 1, 6 * D))
    return out[:, :B].reshape(DEPTH, B, 6, D)


def _norm_mod(x, g, scale, shift):
    ms = jnp.mean(x * x, axis=-1, keepdims=True)
    return (x * lax.rsqrt(ms + NORM_EPS) * g) * (1.0 + scale) + shift


def _inproj_kernel(x_ref, g_ref, sc_ref, sh_ref, w_ref, ra_ref, rb_ref, rc_ref, o_ref):
    h = _norm_mod(x_ref[...], g_ref[...], sc_ref[0], sh_ref[0]).astype(BF16)
    half = ROT_DIM // 2
    for j in range(IN_COLS // IN_TILE_N):
        cols = slice(j * IN_TILE_N, (j + 1) * IN_TILE_N)
        r = jnp.dot(h, w_ref[:, cols], preferred_element_type=F32)
        first_lane_block = j * IN_TILE_N // LANES
        if first_lane_block in (COL_SB_Q, COL_D_Q):
            r = r * (LOG2_E / math.sqrt(SB_HEAD_DIM))
        if first_lane_block in (COL_D_Q, COL_D_K):
            ra, rb, rc = ra_ref[...], rb_ref[...], rc_ref[...]
            for blk in range(IN_TILE_N // LANES):
                t = r[:, blk * LANES:(blk + 1) * LANES]
                rot = t * ra + pltpu.roll(t, LANES - half, 1) * rb + pltpu.roll(t, half, 1) * rc
                o_ref[:, j * IN_TILE_N + blk * LANES:j * IN_TILE_N + (blk + 1) * LANES] = rot.astype(BF16)
        else:
            o_ref[:, cols] = r.astype(BF16)


def _rope_lane_tables(seq_len):
    pos = jnp.arange(seq_len, dtype=F32)
    inv_freq = 1.0 / (ROPE_THETA ** (jnp.arange(0, ROT_DIM, 2, dtype=F32) / ROT_DIM))
    ang = pos[:, None] * inv_freq[None, :]
    cos, sin = jnp.cos(ang), jnp.sin(ang)
    half = ROT_DIM // 2
    lane = jnp.arange(LANES) % DIFF_QK_DIM
    f = lane % half
    first, second = lane < half, (lane >= half) & (lane < ROT_DIM)
    ra = jnp.where((first | second)[None, :], cos[:, f], 1.0)
    rb = jnp.where(first[None, :], -sin[:, f], 0.0)
    rc = jnp.where(second[None, :], sin[:, f], 0.0)
    return ra, rb, rc


def _input_projection(x2, norm_g, scale, shift, w_in_bf, rope, B, S):
    T, D = x2.shape
    tm = min(512, S)
    tiles_per_b = S // tm
    ra, rb, rc = rope
    rope_spec = pl.BlockSpec((tm, LANES), lambda i: (i % tiles_per_b, 0))
    mod_spec = pl.BlockSpec((1, 1, D), lambda i: (i // tiles_per_b, 0, 0))
    assert COL_D_K - COL_D_Q == IN_TILE_N // LANES and COL_D_V - COL_D_K == IN_TILE_N // LANES
    return pl.pallas_call(
        _inproj_kernel,
        grid=(T // tm,),
        in_specs=[
            pl.BlockSpec((tm, D), lambda i: (i, 0)),
            pl.BlockSpec((1, D), lambda i: (0, 0)),
            mod_spec, mod_spec,
            pl.BlockSpec((D, IN_COLS), lambda i: (0, 0)),
            rope_spec, rope_spec, rope_spec,
        ],
        out_specs=pl.BlockSpec((tm, IN_COLS), lambda i: (i, 0)),
        out_shape=jax.ShapeDtypeStruct((T, IN_COLS), BF16),
        compiler_params=_params("parallel"),
        name="in_proj",
    )(x2, norm_g.reshape(1, D), scale.reshape(B, 1, D), shift.reshape(B, 1, D), w_in_bf, ra, rb, rc)


def _sb_kernel(q_ref, k_ref, v_ref, o_ref, acc_sc, cs_sc, *, tb, n_pairs):
    i = pl.program_id(2)
    n_heads = 2 * n_pairs
    lane = lax.broadcasted_iota(jnp.int32, (tb, LANES), 1)
    q_heads = []
    for p in range(n_pairs):
        q = q_ref[:, p * LANES:(p + 1) * LANES]
        zero = jnp.zeros_like(q)
        q_heads += [jnp.where(lane < SB_HEAD_DIM, q, zero), jnp.where(lane >= SB_HEAD_DIM, q, zero)]
    row = lax.broadcasted_iota(jnp.int32, (tb, tb), 0)
    col = lax.broadcasted_iota(jnp.int32, (tb, tb), 1)
    strictly_before = col < row
    suffix_ones = jnp.where(row >= col, 1.0, 0.0).astype(BF16)

    acc_sc[...] = jnp.zeros_like(acc_sc)
    cs_sc[...] = jnp.zeros_like(cs_sc)

    def block(j, masked):
        start = pl.multiple_of(j * tb, tb)
        kb = k_ref[pl.ds(start, tb), :]
        vb = v_ref[pl.ds(start, tb), :]
        ys = [_nt_dot(q_heads[h], kb[:, (h // 2) * LANES:(h // 2 + 1) * LANES]) for h in range(n_heads)]
        costs = []
        for y in ys:
            cost = jnp.where(y > 30.0, y, jnp.log2(1.0 + jnp.exp2(y)))
            if masked:
                cost = jnp.where(strictly_before, cost, 0.0)
            costs.append(cost)
        sums = [jnp.dot(cost.astype(BF16), suffix_ones, preferred_element_type=F32) for cost in costs]
        for h in range(n_heads):
            later = cs_sc[h]
            w = jnp.exp2(ys[h] - sums[h] - later)
            if masked:
                w = jnp.where(strictly_before, w, 0.0)
            acc_sc[h] += jnp.dot(w.astype(BF16), vb[:, (h // 2) * LANES:(h // 2 + 1) * LANES],
                                 preferred_element_type=F32)
            cs_sc[h] = later + sums[h][:, 0:1]

    block(i, True)

    def body(jj, carry):
        block(i - 1 - jj, False)
        return carry

    lax.fori_loop(0, i, body, 0)
    for p in range(n_pairs):
        o_ref[:, p * LANES:(p + 1) * LANES] = jnp.where(
            lane < SB_HEAD_DIM, acc_sc[2 * p], acc_sc[2 * p + 1]).astype(BF16)


SB_PAIRS_PER_STEP = 4


def _sb_attention(proj, B, S):
    T = proj.shape[0]
    tb = min(256, S)
    nq = S // tb
    n_pairs = SB_PAIRS_PER_STEP
    width = n_pairs * LANES
    groups = SB_WIDTH // width
    return pl.pallas_call(
        functools.partial(_sb_kernel, tb=tb, n_pairs=n_pairs),
        grid=(B, groups, nq),
        in_specs=[
            pl.BlockSpec((tb, width), lambda b, p, i: (b * nq + i, COL_SB_Q // n_pairs + p)),
            pl.BlockSpec((S, width), lambda b, p, i: (b, COL_SB_K // n_pairs + p)),
            pl.BlockSpec((S, width), lambda b, p, i: (b, COL_SB_V // n_pairs + p)),
        ],
        out_specs=pl.BlockSpec((tb, width), lambda b, p, i: (b * nq + i, p)),
        out_shape=jax.ShapeDtypeStruct((T, SB_WIDTH), BF16),
        scratch_shapes=[pltpu.VMEM((2 * n_pairs, tb, LANES), F32),
                        pltpu.VMEM((2 * n_pairs, tb, 1), F32)],
        compiler_params=_params("parallel", "parallel", "arbitrary"),
        name="sb_attention",
    )(proj, proj, proj)


def _diff_kernel(lp_ref, g_ref, q_ref, k_ref, v_ref, o_ref, m_sc, acc_sc, *, tq, tk, lam_init):
    i = pl.program_id(1)
    lane = lax.broadcasted_iota(jnp.int32, (tq, LANES), 1)
    q_maps = []
    for h in range(DIFF_HEADS):
        q = q_ref[:, h * LANES:(h + 1) * LANES]
        zero = jnp.zeros_like(q)
        q_maps += [jnp.where(lane < DIFF_QK_DIM, q, zero), jnp.where(lane >= DIFF_QK_DIM, q, zero)]
    n_chains = len(q_maps)
    m_sc[...] = jnp.full_like(m_sc, NEG_INF)
    acc_sc[...] = jnp.zeros_like(acc_sc)
    q_pos = i * tq + lax.broadcasted_iota(jnp.int32, (tq, tk), 0)
    k_off = lax.broadcasted_iota(jnp.int32, (tq, tk), 1)
    ones = jnp.ones((tk, LANES), BF16)

    def block(j, masked):
        start = pl.multiple_of(j * tk, tk)
        kb = k_ref[pl.ds(start, tk), :]
        vb = v_ref[pl.ds(start, tk), :]
        head_cols = lambda c: slice((c // 2) * LANES, (c // 2 + 1) * LANES)
        scores = [_nt_dot(q_maps[c], kb[:, head_cols(c)]) for c in range(n_chains)]
        if masked:
            visible = j * tk + k_off <= q_pos
            scores = [jnp.where(visible, s, NEG_INF) for s in scores]
        m_prev = [m_sc[c] for c in range(n_chains)]
        m_new = [jnp.maximum(m_prev[c], jnp.max(scores[c], axis=1, keepdims=True)) for c in range(n_chains)]
        probs = [jnp.exp2(scores[c] - jnp.concatenate([m_new[c]] * (tk // LANES), axis=1))
                 for c in range(n_chains)]
        for c in range(n_chains):
            alpha = jnp.exp2(m_prev[c] - m_new[c])
            v_ext = jnp.concatenate([vb[:, head_cols(c)], ones], axis=1)
            acc_sc[c] = (jnp.concatenate([alpha, alpha], axis=1) * acc_sc[c]
                         + jnp.dot(probs[c].astype(BF16), v_ext, preferred_element_type=F32))
            m_sc[c] = m_new[c]

    n_full = (i * tq) // tk

    def body(j, carry):
        block(j, False)
        return carry

    lax.fori_loop(0, n_full, body, 0)
    block(n_full, True)

    lp = lp_ref[...]
    lam = (jnp.exp(jnp.sum(lp[0:1] * lp[1:2], axis=1, keepdims=True))
           - jnp.exp(jnp.sum(lp[2:3] * lp[3:4], axis=1, keepdims=True)) + lam_init)
    for h in range(DIFF_HEADS):
        a1, a2 = acc_sc[2 * h], acc_sc[2 * h + 1]
        o = a1[:, :LANES] / a1[:, LANES:] - lam * (a2[:, :LANES] / a2[:, LANES:])
        ms = jnp.mean(o * o, axis=-1, keepdims=True)
        y = (o * lax.rsqrt(ms + NORM_EPS) * g_ref[...]) * (1.0 - lam_init)
        o_ref[:, h * LANES:(h + 1) * LANES] = y.astype(BF16)


def _diff_attention(proj, lam_params, subln_g, lam_init, B, S):
    T = proj.shape[0]
    tq = min(256, S)
    tk = min(512, S)
    nq = S // tq
    n_chains = 2 * DIFF_HEADS
    return pl.pallas_call(
        functools.partial(_diff_kernel, tq=tq, tk=tk, lam_init=lam_init),
        grid=(B, nq),
        in_specs=[
            pl.BlockSpec((4, DIFF_QK_DIM), lambda b, i: (0, 0)),
            pl.BlockSpec((1, DIFF_V_DIM), lambda b, i: (0, 0)),
            pl.BlockSpec((tq, DIFF_QK_WIDTH), lambda b, i: (b * nq + i, COL_D_Q * LANES // DIFF_QK_WIDTH)),
            pl.BlockSpec((S, DIFF_QK_WIDTH), lambda b, i: (b, COL_D_K * LANES // DIFF_QK_WIDTH)),
            pl.BlockSpec((S, DIFF_V_WIDTH), lambda b, i: (b, COL_D_V * LANES // DIFF_V_WIDTH)),
        ],
        out_specs=pl.BlockSpec((tq, DIFF_V_WIDTH), lambda b, i: (b * nq + i, 0)),
        out_shape=jax.ShapeDtypeStruct((T, DIFF_V_WIDTH), BF16),
        scratch_shapes=[pltpu.VMEM((n_chains, tq, LANES), F32),
                        pltpu.VMEM((n_chains, tq, 2 * LANES), F32)],
        compiler_params=_params("parallel", "arbitrary"),
        name="diff_attention",
    )(lam_params, subln_g.reshape(1, DIFF_V_DIM), proj, proj, proj)


def _merge_kernel(ysb_ref, ydf_ref, gs_ref, gd_ref, gb_ref, wbs_ref, wbd_ref, wo_ref,
                  x_ref, g1_ref, n2_ref, sc2_ref, sh2_ref, xo_ref, h2_ref):
    D = x_ref.shape[1]
    a = jnp.dot(ysb_ref[...], wbs_ref[...], preferred_element_type=F32)
    b = jnp.dot(ydf_ref[...], wbd_ref[...], preferred_element_type=F32)
    gb = gb_ref[...]
    g_sb = jax.nn.sigmoid(gs_ref[...].astype(F32) + gb[:, :D])
    g_df = jax.nn.sigmoid(gd_ref[...].astype(F32) + gb[:, D:])
    merged = g_sb * a + g_df * b
    y = jnp.dot(merged.astype(BF16), wo_ref[...], preferred_element_type=F32)
    xn = x_ref[...] + g1_ref[0] * y
    xo_ref[...] = xn
    h2_ref[...] = _norm_mod(xn, n2_ref[...], sc2_ref[0], sh2_ref[0]).astype(BF16)


def _merge_project(y_sb, y_df, proj, gate_b, wbs, wbd, wo, x2, gate1, norm2_g, scale2, shift2, B, S):
    T, D = x2.shape
    tm = min(512, S)
    tiles_per_b = S // tm
    mod_spec = pl.BlockSpec((1, 1, D), lambda i: (i // tiles_per_b, 0, 0))
    const = lambda shape: pl.BlockSpec(shape, lambda i: (0, 0))
    return pl.pallas_call(
        _merge_kernel,
        grid=(T // tm,),
        in_specs=[
            pl.BlockSpec((tm, SB_WIDTH), lambda i: (i, 0)),
            pl.BlockSpec((tm, DIFF_V_WIDTH), lambda i: (i, 0)),
            pl.BlockSpec((tm, D), lambda i: (i, COL_G)),
            pl.BlockSpec((tm, D), lambda i: (i, COL_G + 1)),
            const((1, 2 * D)),
            const((SB_WIDTH, D)), const((DIFF_V_WIDTH, D)), const((D, D)),
            pl.BlockSpec((tm, D), lambda i: (i, 0)),
            mod_spec,
            const((1, D)),
            mod_spec, mod_spec,
        ],
        out_specs=[pl.BlockSpec((tm, D), lambda i: (i, 0)), pl.BlockSpec((tm, D), lambda i: (i, 0))],
        out_shape=[jax.ShapeDtypeStruct((T, D), F32), jax.ShapeDtypeStruct((T, D), BF16)],
        compiler_params=_params("parallel"),
        name="merge_out_proj",
    )(y_sb, y_df, proj, proj, gate_b.reshape(1, 2 * D), wbs, wbd, wo, x2,
      gate1.reshape(B, 1, D), norm2_g.reshape(1, D), scale2.reshape(B, 1, D), shift2.reshape(B, 1, D))


def _sort16_network():
    def merge(lo, hi, r):
        step = r * 2
        if step < hi - lo:
            yield from merge(lo, hi, step)
            yield from merge(lo + r, hi, step)
            yield from [(i, i + r) for i in range(lo + r, hi - r, step)]
        else:
            yield (lo, lo + r)

    def sort(lo, hi):
        if hi - lo >= 1:
            mid = lo + (hi - lo) // 2
            yield from sort(lo, mid)
            yield from sort(mid + 1, hi)
            yield from merge(lo, hi, 1)

    return list(sort(0, 15))


def _top_values(scores, k):
    groups = scores.shape[0] // SUBLANES
    assert groups == 16 and k <= groups + 1
    v = [scores[SUBLANES * g:SUBLANES * (g + 1)] for g in range(groups)]
    for a, b in _sort16_network():
        v[a], v[b] = jnp.maximum(v[a], v[b]), jnp.minimum(v[a], v[b])
    v.append(jnp.full_like(v[0], NEG_INF))
    vals = []
    for r in range(k):
        m = jnp.max(v[0], axis=0, keepdims=True)
        vals.append(m)
        still_needed = k - 1 - r
        if still_needed:
            hit = v[0] == m
            for d in range(min(still_needed, groups)):
                v[d] = jnp.where(hit, v[d + 1], v[d])
    return vals


def _candidate_sums(v1, v1_all, v2, v2_all, row_id):
    assert PEER_TOPK == 16 and SUBLANES == 8
    lo2 = v2_all[0:8]
    hi1 = v1_all[8:16]
    shift = lambda x, n: pltpu.roll(x, n, 0)
    groups = [v1[0] + v2_all[0:8], v1[0] + v2_all[8:16], v1[0] + v2_all[16:24],
              v1[1] + lo2]
    groups.append(jnp.where(row_id < 5, v1[2] + lo2, v1[4] + shift(lo2, 5)))
    groups.append(jnp.where(row_id < 4, v1[3] + lo2,
                            jnp.where(row_id < 6, v1[5] + shift(lo2, 4),
                                      v1[6] + shift(lo2, 6))))
    tail = shift(hi1, 2) + v2[0]
    groups.append(jnp.where(row_id < 2, v1[7] + lo2, tail))
    groups.append(jnp.where(row_id < 2, tail,
                            jnp.where(row_id == 2, v1[16] + v2[0], NEG_INF)))
    return groups


def _peer_select_kernel(h_ref, wq_ref, sk_ref, ht_ref, th_ref, e1_ref, e2_ref, *, tm):
    hb = h_ref[...]
    ht_ref[0] = hb.T
    K = PEER_TOPK + 1
    row_id = lax.broadcasted_iota(jnp.int32, (SUBLANES, LANES), 0)
    pad_rows = jnp.full((3 * SUBLANES - K, LANES), NEG_INF, F32)

    def head(h, carry):
        q_t = _nt_dot(wq_ref[h], hb).astype(BF16)
        half = PEER_QUERY_DIM // 2
        sc = [jnp.dot(sk_ref[h, c], q_t[c * half:(c + 1) * half], preferred_element_type=F32)
              for c in range(2)]
        for lc in range(tm // LANES):
            ls = slice(lc * LANES, (lc + 1) * LANES)
            s1, s2 = sc[0][:, ls], sc[1][:, ls]
            v1 = _top_values(s1, K)
            v2 = _top_values(s2, K)
            v2_all = jnp.concatenate(v2 + [pad_rows], axis=0)
            v1_all = jnp.concatenate(v1 + [pad_rows], axis=0)
            cands = _candidate_sums(v1, v1_all, v2, v2_all, row_id)
            best = cands[0][0:1]
            z_sum = jnp.zeros_like(best)
            last = best
            for r in range(K):
                stacked = cands[0]
                for ca in cands[1:]:
                    stacked = jnp.maximum(stacked, ca)
                m = jnp.max(stacked, axis=0, keepdims=True)
                if r < PEER_TOPK:
                    z_sum = z_sum + jnp.exp(m - best)
                    last = m
                    cands = [jnp.where(ca == m, NEG_INF, ca) for ca in cands]
            tau = 0.5 * (last + m)
            half_inv_z = 0.5 / z_sum
            chunked = (PEER_N_KEYS // PEER_CHUNK_KEYS, PEER_CHUNK_KEYS, LANES)
            th_ref[:, 0, h, :, ls] = jnp.exp(tau - s1 - v2[0]).reshape(chunked)
            e1_ref[:, 0, h, :, ls] = (jnp.exp(s1 - v1[0]) * half_inv_z).reshape(chunked)
            e2_ref[0, h, :, ls] = jnp.exp(s2 - v2[0])
        return carry

    lax.fori_loop(0, PEER_HEADS, head, 0)


def _peer_select(h2, wq_t, subkeys_bf):
    T, D = h2.shape
    tm = PEER_TOKEN_TILE
    n_tiles = T // tm
    tok_spec = pl.BlockSpec((1, PEER_HEADS, PEER_N_KEYS, tm), lambda i: (i, 0, 0, 0))
    tok_shape = jax.ShapeDtypeStruct((n_tiles, PEER_HEADS, PEER_N_KEYS, tm), F32)
    n_chunks = PEER_N_KEYS // PEER_CHUNK_KEYS
    row_spec = pl.BlockSpec((n_chunks, 1, PEER_HEADS, PEER_CHUNK_KEYS, tm), lambda i: (0, i, 0, 0, 0))
    row_shape = jax.ShapeDtypeStruct((n_chunks, n_tiles, PEER_HEADS, PEER_CHUNK_KEYS, tm), F32)
    return pl.pallas_call(
        functools.partial(_peer_select_kernel, tm=tm),
        grid=(n_tiles,),
        in_specs=[
            pl.BlockSpec((tm, D), lambda i: (i, 0)),
            pl.BlockSpec((PEER_HEADS, PEER_QUERY_DIM, D), lambda i: (0, 0, 0)),
            pl.BlockSpec((PEER_HEADS, 2, PEER_N_KEYS, PEER_QUERY_DIM // 2), lambda i: (0, 0, 0, 0)),
        ],
        out_specs=[
            pl.BlockSpec((1, D, tm), lambda i: (i, 0, 0)),
            row_spec, row_spec, tok_spec,
        ],
        out_shape=[
            jax.ShapeDtypeStruct((n_tiles, D, tm), BF16),
            row_shape, row_shape, tok_shape,
        ],
        compiler_params=_params("parallel"),
        name="peer_select",
    )(h2, wq_t, subkeys_bf)


PEER_ROW_BLOCK_KEYS = 2
PEER_PIECE_TOKENS = 2 * LANES


def _peer_mix_kernel(ht_ref, u_ref, vt_ref, th_ref, e1_ref, e2_ref, x_ref, g2_ref, fg_ref,
                     xo_ref, acc_sc, act_sc, e2_sc, *, tm, n_tiles, final_norm):
    j = pl.program_id(1)

    @pl.when(j == 0)
    def _():
        acc_sc[...] = jnp.zeros_like(acc_sc)
        for lc in range(tm // LANES):
            e2_sc[:, :, lc] = e2_ref[:, :, :, lc * LANES:(lc + 1) * LANES]

    rb = PEER_ROW_BLOCK_KEYS * PEER_N_KEYS
    tw = PEER_PIECE_TOKENS
    n_rb = PEER_CHUNK // rb
    pieces = [(r, c) for c in range(tm // tw) for r in range(n_rb)]

    def tile(t, carry):
        def scores(p):
            r, c = pieces[p]
            return jnp.dot(u_ref[r * rb:(r + 1) * rb, :], ht_ref[t, :, c * tw:(c + 1) * tw],
                           preferred_element_type=F32)

        def gates(p, a_blk, k):
            r, c = pieces[p]
            ii = r * PEER_ROW_BLOCK_KEYS + k
            for lc in range(tw // LANES):
                chunk = c * (tw // LANES) + lc
                ls = slice(chunk * LANES, (chunk + 1) * LANES)
                gate = None
                for h in range(PEER_HEADS):
                    theta = th_ref[0, t, h, ii:ii + 1, ls]
                    e1 = e1_ref[0, t, h, ii:ii + 1, ls]
                    e2 = e2_sc[t, h, chunk]
                    term = jnp.where(e2 >= theta, e1 * e2, 0.0)
                    gate = term if gate is None else gate + term
                a = a_blk[k * PEER_N_KEYS:(k + 1) * PEER_N_KEYS, lc * LANES:(lc + 1) * LANES]
                gelu2 = a * (1.0 + lax.erf(a * (1.0 / math.sqrt(2.0))))
                act_sc[c, ii * PEER_N_KEYS:(ii + 1) * PEER_N_KEYS, lc * LANES:(lc + 1) * LANES] = (
                    gelu2 * gate).astype(BF16)

        def mix(c):
            acc_sc[t, c] += jnp.dot(vt_ref[0], act_sc[c], preferred_element_type=F32)

        a_next = scores(0)
        for p in range(len(pieces)):
            r, c = pieces[p]
            a_cur = a_next
            if p + 1 < len(pieces):
                a_next = scores(p + 1)
            for k in range(PEER_ROW_BLOCK_KEYS):
                if k == PEER_ROW_BLOCK_KEYS // 2 and r == 0 and c > 0:
                    mix(c - 1)
                gates(p, a_cur, k)
        mix(tm // tw - 1)
        return carry

    lax.fori_loop(0, n_tiles, tile, 0)

    @pl.when(j == pl.num_programs(1) - 1)
    def _():
        for t in range(n_tiles):
            rows = slice(t * tm, (t + 1) * tm)
            mixed_t = jnp.concatenate([acc_sc[t, c] for c in range(tm // tw)], axis=1)
            xn = x_ref[rows, :] + g2_ref[0] * mixed_t.T
            if final_norm:
                ms = jnp.mean(xn * xn, axis=-1, keepdims=True)
                xn = xn * lax.rsqrt(ms + NORM_EPS) * fg_ref[...]
            xo_ref[rows, :] = xn


def _peer_mix(h_t, u_bf, v_t, theta, e1, e2, x2, gate2, final_g, final_norm, B, S):
    T, D = x2.shape
    tm = PEER_TOKEN_TILE
    n_tiles = min(PEER_TILES_PER_STEP, S // tm)
    span = n_tiles * tm
    steps_per_b = S // span
    tok_spec = pl.BlockSpec((n_tiles, PEER_HEADS, PEER_N_KEYS, tm), lambda i, j: (i, 0, 0, 0),
                            pipeline_mode=pl.Buffered(1))
    row_spec = pl.BlockSpec((1, n_tiles, PEER_HEADS, PEER_CHUNK_KEYS, tm), lambda i, j: (j, i, 0, 0, 0))
    return pl.pallas_call(
        functools.partial(_peer_mix_kernel, tm=tm, n_tiles=n_tiles, final_norm=final_norm),
        grid=(T // span, PEER_N_EXPERTS // PEER_CHUNK),
        in_specs=[
            pl.BlockSpec((n_tiles, D, tm), lambda i, j: (i, 0, 0)),
            pl.BlockSpec((PEER_CHUNK, D), lambda i, j: (j, 0)),
            pl.BlockSpec((1, D, PEER_CHUNK), lambda i, j: (j, 0, 0)),
            row_spec, row_spec, tok_spec,
            pl.BlockSpec((span, D), lambda i, j: (i, 0)),
            pl.BlockSpec((1, 1, D), lambda i, j: (i // steps_per_b, 0, 0)),
            pl.BlockSpec((1, D), lambda i, j: (0, 0)),
        ],
        out_specs=pl.BlockSpec((span, D), lambda i, j: (i, 0)),
        out_shape=jax.ShapeDtypeStruct((T, D), F32),
        scratch_shapes=[pltpu.VMEM((n_tiles, tm // PEER_PIECE_TOKENS, D, PEER_PIECE_TOKENS), F32),
                        pltpu.VMEM((tm // PEER_PIECE_TOKENS, PEER_CHUNK, PEER_PIECE_TOKENS), BF16),
                        pltpu.VMEM((n_tiles, PEER_HEADS, tm // LANES, PEER_N_KEYS, LANES), F32)],
        compiler_params=_params("parallel", "arbitrary"),
        name="peer_mix",
    )(h_t, u_bf, v_t, theta, e1, e2, x2, gate2.reshape(B, 1, D), final_g.reshape(1, D))


def kernel(x, c, norm1_g, norm2_g, ada_w, ada_b, w_in, gate_b, diff_lambda, diff_subln_g,
           w_branch_sb, w_branch_diff, w_out, peer_wq, peer_subkeys, peer_u, peer_v, final_g):
    B, S, D = x.shape
    assert D == D_MODEL and S % LANES == 0
    x2 = x.reshape(B * S, D)
    mod = _modulation(c, ada_w, ada_b)
    rope = _rope_lane_tables(S)
    for l in range(DEPTH):
        lam_init = 0.8 - 0.6 * math.exp(-0.3 * l)
        shift1, scale1, gate1, shift2, scale2, gate2 = (mod[l, :, n] for n in range(6))
        proj = _input_projection(x2, norm1_g[l], scale1, shift1, w_in[l].astype(BF16), rope, B, S)
        y_sb = _sb_attention(proj, B, S)
        y_df = _diff_attention(proj, diff_lambda[l], diff_subln_g[l], lam_init, B, S)
        x2, h2 = _merge_project(y_sb, y_df, proj, gate_b[l], w_branch_sb[l].astype(BF16),
                                w_branch_diff[l].astype(BF16), w_out[l].astype(BF16), x2, gate1,
                                norm2_g[l], scale2, shift2, B, S)
        wq_t = peer_wq[l].T.astype(BF16).reshape(PEER_HEADS, PEER_QUERY_DIM, D)
        h_t, theta, e1, e2 = _peer_select(h2, wq_t, peer_subkeys[l].astype(BF16))
        v_t = jnp.swapaxes(peer_v[l].astype(BF16).reshape(-1, PEER_CHUNK, D), 1, 2)
        x2 = _peer_mix(h_t, peer_u[l].astype(BF16), v_t,
                       theta, e1, e2, x2, gate2, final_g, l == DEPTH - 1, B, S)
    return x2.reshape(B, S, D)
```

```python
import functools
import math

import jax
import jax.numpy as jnp
from jax import lax
from jax.experimental import pallas as pl
from jax.experimental.pallas import tpu as pltpu

F32 = jnp.float32
BF16 = jnp.bfloat16

D_MODEL = 1024
DEPTH = 2
SB_HEADS = 8
SB_HEAD_DIM = 64
SB_WIDTH = SB_HEADS * SB_HEAD_DIM
DIFF_HEADS = 4
DIFF_QK_DIM = 64
DIFF_V_DIM = 2 * DIFF_QK_DIM
DIFF_QK_WIDTH = DIFF_HEADS * 2 * DIFF_QK_DIM
DIFF_V_WIDTH = DIFF_HEADS * DIFF_V_DIM
ROPE_THETA = 500000.0
ROT_DIM = DIFF_QK_DIM // 4
IN_COLS = 3 * SB_WIDTH + 2 * DIFF_QK_WIDTH + DIFF_V_WIDTH + 2 * D_MODEL
PEER_HEADS = 8
PEER_N_KEYS = 128
PEER_N_EXPERTS = PEER_N_KEYS * PEER_N_KEYS
PEER_TOPK = 16
PEER_QUERY_DIM = 256
PEER_CHUNK_KEYS = 16
PEER_CHUNK = PEER_CHUNK_KEYS * PEER_N_KEYS
PEER_TOKEN_TILE = 512
PEER_TILES_PER_STEP = 2
NORM_EPS = 1e-6

LANES = 128
SUBLANES = 8
VMEM_LIMIT = 60 * 1024 * 1024
NEG_INF = float("-inf")
LOG2_E = 1.4426950408889634

IN_TILE_N = 512
COL_SB_Q, COL_SB_K, COL_SB_V = 0, SB_WIDTH // LANES, 2 * SB_WIDTH // LANES
COL_D_Q = 3 * SB_WIDTH // LANES
COL_D_K = COL_D_Q + DIFF_QK_WIDTH // LANES
COL_D_V = COL_D_K + DIFF_QK_WIDTH // LANES
COL_G = (3 * SB_WIDTH + 2 * DIFF_QK_WIDTH + DIFF_V_WIDTH) // D_MODEL


def _params(*sem):
    return pltpu.CompilerParams(dimension_semantics=sem, vmem_limit_bytes=VMEM_LIMIT)


def _nt_dot(a, b):
    return lax.dot_general(a, b, (((1,), (1,)), ((), ())), preferred_element_type=F32)


def _mod_kernel(c_ref, w_ref, b_ref, o_ref):
    c = c_ref[...]
    c_act = c * jax.nn.sigmoid(c)
    o_ref[0] = jnp.dot(c_act, w_ref[0], preferred_element_type=F32,
                       precision=lax.Precision.HIGHEST) + b_ref[0]


def _modulation(c, ada_w, ada_b):
    B, D = c.shape
    rows = SUBLANES
    c_pad = jnp.pad(c, ((0, rows - B), (0, 0)))
    out = pl.pallas_call(
        _mod_kernel,
        grid=(DEPTH, 6),
        in_specs=[
            pl.BlockSpec((rows, D), lambda l, n: (0, 0)),
            pl.BlockSpec((1, D, D), lambda l, n: (l, 0, n)),
            pl.BlockSpec((1, 1, D), lambda l, n: (l, 0, n)),
        ],
        out_specs=pl.BlockSpec((1, rows, D), lambda l, n: (l, 0, n)),
        out_shape=jax.ShapeDtypeStruct((DEPTH, rows, 6 * D), F32),
        compiler_params=_params("arbitrary", "arbitrary"),
        name="adaln_mod",
    )(c_pad, ada_w, ada_b.reshape(DEPTH, 1, 6 * D))
    return out[:, :B].reshape(DEPTH, B, 6, D)


def _norm_mod(x, g, scale, shift):
    ms = jnp.mean(x * x, axis=-1, keepdims=True)
    return (x * lax.rsqrt(ms + NORM_EPS) * g) * (1.0 + scale) + shift


def _inproj_kernel(x_ref, g_ref, sc_ref, sh_ref, w_ref, ra_ref, rb_ref, rc_ref, o_ref):
    h = _norm_mod(x_ref[...], g_ref[...], sc_ref[0], sh_ref[0]).astype(BF16)
    half = ROT_DIM // 2
    for j in range(IN_COLS // IN_TILE_N):
        cols = slice(j * IN_TILE_N, (j + 1) * IN_TILE_N)
        r = jnp.dot(h, w_ref[:, cols], preferred_element_type=F32)
        first_lane_block = j * IN_TILE_N // LANES
        if first_lane_block in (COL_SB_Q, COL_D_Q):
            r = r * (LOG2_E / math.sqrt(SB_HEAD_DIM))
        if first_lane_block in (COL_D_Q, COL_D_K):
            ra, rb, rc = ra_ref[...], rb_ref[...], rc_ref[...]
            for blk in range(IN_TILE_N // LANES):
                t = r[:, blk * LANES:(blk + 1) * LANES]
                rot = t * ra + pltpu.roll(t, LANES - half, 1) * rb + pltpu.roll(t, half, 1) * rc
                o_ref[:, j * IN_TILE_N + blk * LANES:j * IN_TILE_N + (blk + 1) * LANES] = rot.astype(BF16)
        else:
            o_ref[:, cols] = r.astype(BF16)


def _rope_lane_tables(seq_len):
    pos = jnp.arange(seq_len, dtype=F32)
    inv_freq = 1.0 / (ROPE_THETA ** (jnp.arange(0, ROT_DIM, 2, dtype=F32) / ROT_DIM))
    ang = pos[:, None] * inv_freq[None, :]
    cos, sin = jnp.cos(ang), jnp.sin(ang)
    half = ROT_DIM // 2
    lane = jnp.arange(LANES) % DIFF_QK_DIM
    f = lane % half
    first, second = lane < half, (lane >= half) & (lane < ROT_DIM)
    ra = jnp.where((first | second)[None, :], cos[:, f], 1.0)
    rb = jnp.where(first[None, :], -sin[:, f], 0.0)
    rc = jnp.where(second[None, :], sin[:, f], 0.0)
    return ra, rb, rc


def _input_projection(x2, norm_g, scale, shift, w_in_bf, rope, B, S):
    T, D = x2.shape
    tm = min(512, S)
    tiles_per_b = S // tm
    ra, rb, rc = rope
    rope_spec = pl.BlockSpec((tm, LANES), lambda i: (i % tiles_per_b, 0))
    mod_spec = pl.BlockSpec((1, 1, D), lambda i: (i // tiles_per_b, 0, 0))
    assert COL_D_K - COL_D_Q == IN_TILE_N // LANES and COL_D_V - COL_D_K == IN_TILE_N // LANES
    return pl.pallas_call(
        _inproj_kernel,
        grid=(T // tm,),
        in_specs=[
            pl.BlockSpec((tm, D), lambda i: (i, 0)),
            pl.BlockSpec((1, D), lambda i: (0, 0)),
            mod_spec, mod_spec,
            pl.BlockSpec((D, IN_COLS), lambda i: (0, 0)),
            rope_spec, rope_spec, rope_spec,
        ],
        out_specs=pl.BlockSpec((tm, IN_COLS), lambda i: (i, 0)),
        out_shape=jax.ShapeDtypeStruct((T, IN_COLS), BF16),
        compiler_params=_params("parallel"),
        name="in_proj",
    )(x2, norm_g.reshape(1, D), scale.reshape(B, 1, D), shift.reshape(B, 1, D), w_in_bf, ra, rb, rc)


def _sb_kernel(q_ref, k_ref, v_ref, o_ref, acc_sc, cs_sc, *, tb, n_pairs):
    i = pl.program_id(2)
    n_heads = 2 * n_pairs
    lane = lax.broadcasted_iota(jnp.int32, (tb, LANES), 1)
    q_heads = []
    for p in range(n_pairs):
        q = q_ref[:, p * LANES:(p + 1) * LANES]
        zero = jnp.zeros_like(q)
        q_heads += [jnp.where(lane < SB_HEAD_DIM, q, zero), jnp.where(lane >= SB_HEAD_DIM, q, zero)]
    row = lax.broadcasted_iota(jnp.int32, (tb, tb), 0)
    col = lax.broadcasted_iota(jnp.int32, (tb, tb), 1)
    strictly_before = col < row
    suffix_ones = jnp.where(row >= col, 1.0, 0.0).astype(BF16)

    acc_sc[...] = jnp.zeros_like(acc_sc)
    cs_sc[...] = jnp.zeros_like(cs_sc)

    def block(j, masked):
        start = pl.multiple_of(j * tb, tb)
        kb = k_ref[pl.ds(start, tb), :]
        vb = v_ref[pl.ds(start, tb), :]
        def logits(heads):
            return [_nt_dot(q_heads[h], kb[:, (h // 2) * LANES:(h // 2 + 1) * LANES]) for h in heads]

        def suffix_sums(ys):
            costs = []
            for y in ys:
                cost = jnp.where(y > 30.0, y, jnp.log2(1.0 + jnp.exp2(y)))
                if masked:
                    cost = jnp.where(strictly_before, cost, 0.0)
                costs.append(cost)
            return [jnp.dot(cost.astype(BF16), suffix_ones, preferred_element_type=F32) for cost in costs]

        def accumulate(heads, ys, sums):
            for h, y, s in zip(heads, ys, sums):
                later = cs_sc[h]
                w = jnp.exp2(y - s - later)
                if masked:
                    w = jnp.where(strictly_before, w, 0.0)
                acc_sc[h] += jnp.dot(w.astype(BF16), vb[:, (h // 2) * LANES:(h // 2 + 1) * LANES],
                                     preferred_element_type=F32)
                cs_sc[h] = later + s[:, 0:1]

        groups = [list(range(g, g + SB_GROUP_HEADS)) for g in range(0, n_heads, SB_GROUP_HEADS)]
        ys_next = logits(groups[0])
        for g, heads in enumerate(groups):
            ys_cur = ys_next
            sums = suffix_sums(ys_cur)
            if g + 1 < len(groups):
                ys_next = logits(groups[g + 1])
            accumulate(heads, ys_cur, sums)

    block(i, True)

    def body(jj, carry):
        block(i - 1 - jj, False)
        return carry

    lax.fori_loop(0, i, body, 0)
    for p in range(n_pairs):
        o_ref[:, p * LANES:(p + 1) * LANES] = jnp.where(
            lane < SB_HEAD_DIM, acc_sc[2 * p], acc_sc[2 * p + 1]).astype(BF16)


SB_PAIRS_PER_STEP = 4
SB_GROUP_HEADS = 4


def _sb_attention(proj, B, S):
    T = proj.shape[0]
    tb = min(256, S)
    nq = S // tb
    n_pairs = SB_PAIRS_PER_STEP
    width = n_pairs * LANES
    groups = SB_WIDTH // width
    return pl.pallas_call(
        functools.partial(_sb_kernel, tb=tb, n_pairs=n_pairs),
        grid=(B, groups, nq),
        in_specs=[
            pl.BlockSpec((tb, width), lambda b, p, i: (b * nq + i, COL_SB_Q // n_pairs + p)),
            pl.BlockSpec((S, width), lambda b, p, i: (b, COL_SB_K // n_pairs + p)),
            pl.BlockSpec((S, width), lambda b, p, i: (b, COL_SB_V // n_pairs + p)),
        ],
        out_specs=pl.BlockSpec((tb, width), lambda b, p, i: (b * nq + i, p)),
        out_shape=jax.ShapeDtypeStruct((T, SB_WIDTH), BF16),
        scratch_shapes=[pltpu.VMEM((2 * n_pairs, tb, LANES), F32),
                        pltpu.VMEM((2 * n_pairs, tb, 1), F32)],
        compiler_params=_params("parallel", "parallel", "arbitrary"),
        name="sb_attention",
    )(proj, proj, proj)


def _diff_kernel(lp_ref, g_ref, q_ref, k_ref, v_ref, o_ref, m_sc, acc_sc, *, tq, tk, lam_init):
    i = pl.program_id(1)
    lane = lax.broadcasted_iota(jnp.int32, (tq, LANES), 1)
    q_maps = []
    for h in range(DIFF_HEADS):
        q = q_ref[:, h * LANES:(h + 1) * LANES]
        zero = jnp.zeros_like(q)
        q_maps += [jnp.where(lane < DIFF_QK_DIM, q, zero), jnp.where(lane >= DIFF_QK_DIM, q, zero)]
    n_chains = len(q_maps)
    m_sc[...] = jnp.full_like(m_sc, NEG_INF)
    acc_sc[...] = jnp.zeros_like(acc_sc)
    q_pos = i * tq + lax.broadcasted_iota(jnp.int32, (tq, tk), 0)
    k_off = lax.broadcasted_iota(jnp.int32, (tq, tk), 1)
    ones = jnp.ones((tk, LANES), BF16)

    def block(j, masked):
        start = pl.multiple_of(j * tk, tk)
        kb = k_ref[pl.ds(start, tk), :]
        vb = v_ref[pl.ds(start, tk), :]
        head_cols = lambda c: slice((c // 2) * LANES, (c // 2 + 1) * LANES)
        scores = [_nt_dot(q_maps[c], kb[:, head_cols(c)]) for c in range(n_chains)]
        if masked:
            visible = j * tk + k_off <= q_pos
            scores = [jnp.where(visible, s, NEG_INF) for s in scores]
        m_prev = [m_sc[c] for c in range(n_chains)]
        m_new = [jnp.maximum(m_prev[c], jnp.max(scores[c], axis=1, keepdims=True)) for c in range(n_chains)]
        probs = [jnp.exp2(scores[c] - jnp.concatenate([m_new[c]] * (tk // LANES), axis=1))
                 for c in range(n_chains)]
        for c in range(n_chains):
            alpha = jnp.exp2(m_prev[c] - m_new[c])
            v_ext = jnp.concatenate([vb[:, head_cols(c)], ones], axis=1)
            acc_sc[c] = (jnp.concatenate([alpha, alpha], axis=1) * acc_sc[c]
                         + jnp.dot(probs[c].astype(BF16), v_ext, preferred_element_type=F32))
            m_sc[c] = m_new[c]

    n_full = (i * tq) // tk

    def body(j, carry):
        block(j, False)
        return carry

    lax.fori_loop(0, n_full, body, 0)
    block(n_full, True)

    lp = lp_ref[...]
    lam = (jnp.exp(jnp.sum(lp[0:1] * lp[1:2], axis=1, keepdims=True))
           - jnp.exp(jnp.sum(lp[2:3] * lp[3:4], axis=1, keepdims=True)) + lam_init)
    for h in range(DIFF_HEADS):
        a1, a2 = acc_sc[2 * h], acc_sc[2 * h + 1]
        o = a1[:, :LANES] / a1[:, LANES:] - lam * (a2[:, :LANES] / a2[:, LANES:])
        ms = jnp.mean(o * o, axis=-1, keepdims=True)
        y = (o * lax.rsqrt(ms + NORM_EPS) * g_ref[...]) * (1.0 - lam_init)
        o_ref[:, h * LANES:(h + 1) * LANES] = y.astype(BF16)


def _diff_attention(proj, lam_params, subln_g, lam_init, B, S):
    T = proj.shape[0]
    tq = min(256, S)
    tk = min(512, S)
    nq = S // tq
    n_chains = 2 * DIFF_HEADS
    return pl.pallas_call(
        functools.partial(_diff_kernel, tq=tq, tk=tk, lam_init=lam_init),
        grid=(B, nq),
        in_specs=[
            pl.BlockSpec((4, DIFF_QK_DIM), lambda b, i: (0, 0)),
            pl.BlockSpec((1, DIFF_V_DIM), lambda b, i: (0, 0)),
            pl.BlockSpec((tq, DIFF_QK_WIDTH), lambda b, i: (b * nq + i, COL_D_Q * LANES // DIFF_QK_WIDTH)),
            pl.BlockSpec((S, DIFF_QK_WIDTH), lambda b, i: (b, COL_D_K * LANES // DIFF_QK_WIDTH)),
            pl.BlockSpec((S, DIFF_V_WIDTH), lambda b, i: (b, COL_D_V * LANES // DIFF_V_WIDTH)),
        ],
        out_specs=pl.BlockSpec((tq, DIFF_V_WIDTH), lambda b, i: (b * nq + i, 0)),
        out_shape=jax.ShapeDtypeStruct((T, DIFF_V_WIDTH), BF16),
        scratch_shapes=[pltpu.VMEM((n_chains, tq, LANES), F32),
                        pltpu.VMEM((n_chains, tq, 2 * LANES), F32)],
        compiler_params=_params("parallel", "arbitrary"),
        name="diff_attention",
    )(lam_params, subln_g.reshape(1, DIFF_V_DIM), proj, proj, proj)


def _merge_kernel(ysb_ref, ydf_ref, gs_ref, gd_ref, gb_ref, wbs_ref, wbd_ref, wo_ref,
                  x_ref, g1_ref, n2_ref, sc2_ref, sh2_ref, xo_ref, h2_ref):
    D = x_ref.shape[1]
    a = jnp.dot(ysb_ref[...], wbs_ref[...], preferred_element_type=F32)
    b = jnp.dot(ydf_ref[...], wbd_ref[...], preferred_element_type=F32)
    gb = gb_ref[...]
    g_sb = jax.nn.sigmoid(gs_ref[...].astype(F32) + gb[:, :D])
    g_df = jax.nn.sigmoid(gd_ref[...].astype(F32) + gb[:, D:])
    merged = g_sb * a + g_df * b
    y = jnp.dot(merged.astype(BF16), wo_ref[...], preferred_element_type=F32)
    xn = x_ref[...] + g1_ref[0] * y
    xo_ref[...] = xn
    h2_ref[...] = _norm_mod(xn, n2_ref[...], sc2_ref[0], sh2_ref[0]).astype(BF16)


def _merge_project(y_sb, y_df, proj, gate_b, wbs, wbd, wo, x2, gate1, norm2_g, scale2, shift2, B, S):
    T, D = x2.shape
    tm = min(512, S)
    tiles_per_b = S // tm
    mod_spec = pl.BlockSpec((1, 1, D), lambda i: (i // tiles_per_b, 0, 0))
    const = lambda shape: pl.BlockSpec(shape, lambda i: (0, 0))
    return pl.pallas_call(
        _merge_kernel,
        grid=(T // tm,),
        in_specs=[
            pl.BlockSpec((tm, SB_WIDTH), lambda i: (i, 0)),
            pl.BlockSpec((tm, DIFF_V_WIDTH), lambda i: (i, 0)),
            pl.BlockSpec((tm, D), lambda i: (i, COL_G)),
            pl.BlockSpec((tm, D), lambda i: (i, COL_G + 1)),
            const((1, 2 * D)),
            const((SB_WIDTH, D)), const((DIFF_V_WIDTH, D)), const((D, D)),
            pl.BlockSpec((tm, D), lambda i: (i, 0)),
            mod_spec,
            const((1, D)),
            mod_spec, mod_spec,
        ],
        out_specs=[pl.BlockSpec((tm, D), lambda i: (i, 0)), pl.BlockSpec((tm, D), lambda i: (i, 0))],
        out_shape=[jax.ShapeDtypeStruct((T, D), F32), jax.ShapeDtypeStruct((T, D), BF16)],
        compiler_params=_params("parallel"),
        name="merge_out_proj",
    )(y_sb, y_df, proj, proj, gate_b.reshape(1, 2 * D), wbs, wbd, wo, x2,
      gate1.reshape(B, 1, D), norm2_g.reshape(1, D), scale2.reshape(B, 1, D), shift2.reshape(B, 1, D))


def _sort16_network():
    def merge(lo, hi, r):
        step = r * 2
        if step < hi - lo:
            yield from merge(lo, hi, step)
            yield from merge(lo + r, hi, step)
            yield from [(i, i + r) for i in range(lo + r, hi - r, step)]
        else:
            yield (lo, lo + r)

    def sort(lo, hi):
        if hi - lo >= 1:
            mid = lo + (hi - lo) // 2
            yield from sort(lo, mid)
            yield from sort(mid + 1, hi)
            yield from merge(lo, hi, 1)

    return list(sort(0, 15))


def _top_values(scores, k):
    groups = scores.shape[0] // SUBLANES
    assert groups == 16 and k <= groups + 1
    v = [scores[SUBLANES * g:SUBLANES * (g + 1)] for g in range(groups)]
    for a, b in _sort16_network():
        v[a], v[b] = jnp.maximum(v[a], v[b]), jnp.minimum(v[a], v[b])
    v.append(jnp.full_like(v[0], NEG_INF))
    vals = []
    for r in range(k):
        m = jnp.max(v[0], axis=0, keepdims=True)
        vals.append(m)
        still_needed = k - 1 - r
        if still_needed:
            hit = v[0] == m
            for d in range(min(still_needed, groups)):
                v[d] = jnp.where(hit, v[d + 1], v[d])
    return vals


def _candidate_sums(v1, v1_all, v2, v2_all, row_id):
    assert PEER_TOPK == 16 and SUBLANES == 8
    lo2 = v2_all[0:8]
    hi1 = v1_all[8:16]
    shift = lambda x, n: pltpu.roll(x, n, 0)
    groups = [v1[0] + v2_all[0:8], v1[0] + v2_all[8:16], v1[0] + v2_all[16:24],
              v1[1] + lo2]
    groups.append(jnp.where(row_id < 5, v1[2] + lo2, v1[4] + shift(lo2, 5)))
    groups.append(jnp.where(row_id < 4, v1[3] + lo2,
                            jnp.where(row_id < 6, v1[5] + shift(lo2, 4),
                                      v1[6] + shift(lo2, 6))))
    tail = shift(hi1, 2) + v2[0]
    groups.append(jnp.where(row_id < 2, v1[7] + lo2, tail))
    groups.append(jnp.where(row_id < 2, tail,
                            jnp.where(row_id == 2, v1[16] + v2[0], NEG_INF)))
    return groups


def _peer_select_kernel(h_ref, wq_ref, sk_ref, ht_ref, th_ref, e1_ref, e2_ref, *, tm):
    hb = h_ref[...]
    ht_ref[0] = hb.T
    K = PEER_TOPK + 1
    row_id = lax.broadcasted_iota(jnp.int32, (SUBLANES, LANES), 0)
    pad_rows = jnp.full((3 * SUBLANES - K, LANES), NEG_INF, F32)

    def head(h, carry):
        q_t = _nt_dot(wq_ref[h], hb).astype(BF16)
        half = PEER_QUERY_DIM // 2
        sc = [jnp.dot(sk_ref[h, c], q_t[c * half:(c + 1) * half], preferred_element_type=F32)
              for c in range(2)]
        for lc in range(tm // LANES):
            ls = slice(lc * LANES, (lc + 1) * LANES)
            s1, s2 = sc[0][:, ls], sc[1][:, ls]
            v1 = _top_values(s1, K)
            v2 = _top_values(s2, K)
            v2_all = jnp.concatenate(v2 + [pad_rows], axis=0)
            v1_all = jnp.concatenate(v1 + [pad_rows], axis=0)
            cands = _candidate_sums(v1, v1_all, v2, v2_all, row_id)
            best = cands[0][0:1]
            z_sum = jnp.zeros_like(best)
            last = best
            for r in range(K):
                stacked = cands[0]
                for ca in cands[1:]:
                    stacked = jnp.maximum(stacked, ca)
                m = jnp.max(stacked, axis=0, keepdims=True)
                if r < PEER_TOPK:
                    z_sum = z_sum + jnp.exp(m - best)
                    last = m
                    cands = [jnp.where(ca == m, NEG_INF, ca) for ca in cands]
            tau = 0.5 * (last + m)
            half_inv_z = 0.5 / z_sum
            chunked = (PEER_N_KEYS // PEER_CHUNK_KEYS, PEER_CHUNK_KEYS, LANES)
            th_ref[:, 0, h, :, ls] = jnp.exp(tau - s1 - v2[0]).reshape(chunked)
            e1_ref[:, 0, h, :, ls] = (jnp.exp(s1 - v1[0]) * half_inv_z).reshape(chunked)
            e2_ref[0, h, :, ls] = jnp.exp(s2 - v2[0])
        return carry

    lax.fori_loop(0, PEER_HEADS, head, 0)


def _peer_select(h2, wq_t, subkeys_bf):
    T, D = h2.shape
    tm = PEER_TOKEN_TILE
    n_tiles = T // tm
    tok_spec = pl.BlockSpec((1, PEER_HEADS, PEER_N_KEYS, tm), lambda i: (i, 0, 0, 0))
    tok_shape = jax.ShapeDtypeStruct((n_tiles, PEER_HEADS, PEER_N_KEYS, tm), F32)
    n_chunks = PEER_N_KEYS // PEER_CHUNK_KEYS
    row_spec = pl.BlockSpec((n_chunks, 1, PEER_HEADS, PEER_CHUNK_KEYS, tm), lambda i: (0, i, 0, 0, 0))
    row_shape = jax.ShapeDtypeStruct((n_chunks, n_tiles, PEER_HEADS, PEER_CHUNK_KEYS, tm), F32)
    return pl.pallas_call(
        functools.partial(_peer_select_kernel, tm=tm),
        grid=(n_tiles,),
        in_specs=[
            pl.BlockSpec((tm, D), lambda i: (i, 0)),
            pl.BlockSpec((PEER_HEADS, PEER_QUERY_DIM, D), lambda i: (0, 0, 0)),
            pl.BlockSpec((PEER_HEADS, 2, PEER_N_KEYS, PEER_QUERY_DIM // 2), lambda i: (0, 0, 0, 0)),
        ],
        out_specs=[
            pl.BlockSpec((1, D, tm), lambda i: (i, 0, 0)),
            row_spec, row_spec, tok_spec,
        ],
        out_shape=[
            jax.ShapeDtypeStruct((n_tiles, D, tm), BF16),
            row_shape, row_shape, tok_shape,
        ],
        compiler_params=_params("parallel"),
        name="peer_select",
    )(h2, wq_t, subkeys_bf)


PEER_ROW_BLOCK_KEYS = 2
PEER_PIECE_TOKENS = 2 * LANES


def _peer_mix_kernel(ht_ref, u_ref, vt_ref, th_ref, e1_ref, e2_ref, x_ref, g2_ref, fg_ref,
                     xo_ref, acc_sc, act_sc, e2_sc, *, tm, n_tiles, final_norm):
    j = pl.program_id(1)

    @pl.when(j == 0)
    def _():
        acc_sc[...] = jnp.zeros_like(acc_sc)
        for lc in range(tm // LANES):
            e2_sc[:, :, lc] = e2_ref[:, :, :, lc * LANES:(lc + 1) * LANES]

    rb = PEER_ROW_BLOCK_KEYS * PEER_N_KEYS
    tw = PEER_PIECE_TOKENS
    n_rb = PEER_CHUNK // rb
    pieces = [(r, c) for c in range(tm // tw) for r in range(n_rb)]

    def tile(t, carry):
        def scores(p):
            r, c = pieces[p]
            return jnp.dot(u_ref[r * rb:(r + 1) * rb, :], ht_ref[t, :, c * tw:(c + 1) * tw],
                           preferred_element_type=F32)

        def gates(p, a_blk, k):
            r, c = pieces[p]
            ii = r * PEER_ROW_BLOCK_KEYS + k
            for lc in range(tw // LANES):
                chunk = c * (tw // LANES) + lc
                ls = slice(chunk * LANES, (chunk + 1) * LANES)
                gate = None
                for h in range(PEER_HEADS):
                    theta = th_ref[0, t, h, ii:ii + 1, ls]
                    e1 = e1_ref[0, t, h, ii:ii + 1, ls]
                    e2 = e2_sc[t, h, chunk]
                    term = jnp.where(e2 >= theta, e1 * e2, 0.0)
                    gate = term if gate is None else gate + term
                a = a_blk[k * PEER_N_KEYS:(k + 1) * PEER_N_KEYS, lc * LANES:(lc + 1) * LANES]
                gelu2 = a * (1.0 + lax.erf(a * (1.0 / math.sqrt(2.0))))
                act_sc[c, ii * PEER_N_KEYS:(ii + 1) * PEER_N_KEYS, lc * LANES:(lc + 1) * LANES] = (
                    gelu2 * gate).astype(BF16)

        def mix(c):
            acc_sc[t, c] += jnp.dot(vt_ref[0], act_sc[c], preferred_element_type=F32)

        a_next = scores(0)
        for p in range(len(pieces)):
            r, c = pieces[p]
            a_cur = a_next
            if p + 1 < len(pieces):
                a_next = scores(p + 1)
            for k in range(PEER_ROW_BLOCK_KEYS):
                if k == PEER_ROW_BLOCK_KEYS // 2 and r == 0 and c > 0:
                    mix(c - 1)
                gates(p, a_cur, k)
        mix(tm // tw - 1)
        return carry

    lax.fori_loop(0, n_tiles, tile, 0)

    @pl.when(j == pl.num_programs(1) - 1)
    def _():
        for t in range(n_tiles):
            rows = slice(t * tm, (t + 1) * tm)
            mixed_t = jnp.concatenate([acc_sc[t, c] for c in range(tm // tw)], axis=1)
            xn = x_ref[rows, :] + g2_ref[0] * mixed_t.T
            if final_norm:
                ms = jnp.mean(xn * xn, axis=-1, keepdims=True)
                xn = xn * lax.rsqrt(ms + NORM_EPS) * fg_ref[...]
            xo_ref[rows, :] = xn


def _peer_mix(h_t, u_bf, v_t, theta, e1, e2, x2, gate2, final_g, final_norm, B, S):
    T, D = x2.shape
    tm = PEER_TOKEN_TILE
    n_tiles = min(PEER_TILES_PER_STEP, S // tm)
    span = n_tiles * tm
    steps_per_b = S // span
    tok_spec = pl.BlockSpec((n_tiles, PEER_HEADS, PEER_N_KEYS, tm), lambda i, j: (i, 0, 0, 0),
                            pipeline_mode=pl.Buffered(1))
    row_spec = pl.BlockSpec((1, n_tiles, PEER_HEADS, PEER_CHUNK_KEYS, tm), lambda i, j: (j, i, 0, 0, 0))
    return pl.pallas_call(
        functools.partial(_peer_mix_kernel, tm=tm, n_tiles=n_tiles, final_norm=final_norm),
        grid=(T // span, PEER_N_EXPERTS // PEER_CHUNK),
        in_specs=[
            pl.BlockSpec((n_tiles, D, tm), lambda i, j: (i, 0, 0)),
            pl.BlockSpec((PEER_CHUNK, D), lambda i, j: (j, 0)),
            pl.BlockSpec((1, D, PEER_CHUNK), lambda i, j: (j, 0, 0)),
            row_spec, row_spec, tok_spec,
            pl.BlockSpec((span, D), lambda i, j: (i, 0)),
            pl.BlockSpec((1, 1, D), lambda i, j: (i // steps_per_b, 0, 0)),
            pl.BlockSpec((1, D), lambda i, j: (0, 0)),
        ],
        out_specs=pl.BlockSpec((span, D), lambda i, j: (i, 0)),
        out_shape=jax.ShapeDtypeStruct((T, D), F32),
        scratch_shapes=[pltpu.VMEM((n_tiles, tm // PEER_PIECE_TOKENS, D, PEER_PIECE_TOKENS), F32),
                        pltpu.VMEM((tm // PEER_PIECE_TOKENS, PEER_CHUNK, PEER_PIECE_TOKENS), BF16),
                        pltpu.VMEM((n_tiles, PEER_HEADS, tm // LANES, PEER_N_KEYS, LANES), F32)],
        compiler_params=_params("parallel", "arbitrary"),
        name="peer_mix",
    )(h_t, u_bf, v_t, theta, e1, e2, x2, gate2.reshape(B, 1, D), final_g.reshape(1, D))


def kernel(x, c, norm1_g, norm2_g, ada_w, ada_b, w_in, gate_b, diff_lambda, diff_subln_g,
           w_branch_sb, w_branch_diff, w_out, peer_wq, peer_subkeys, peer_u, peer_v, final_g):
    B, S, D = x.shape
    assert D == D_MODEL and S % LANES == 0
    x2 = x.reshape(B * S, D)
    mod = _modulation(c, ada_w, ada_b)
    rope = _rope_lane_tables(S)
    for l in range(DEPTH):
        lam_init = 0.8 - 0.6 * math.exp(-0.3 * l)
        shift1, scale1, gate1, shift2, scale2, gate2 = (mod[l, :, n] for n in range(6))
        proj = _input_projection(x2, norm1_g[l], scale1, shift1, w_in[l].astype(BF16), rope, B, S)
        y_sb = _sb_attention(proj, B, S)
        y_df = _diff_attention(proj, diff_lambda[l], diff_subln_g[l], lam_init, B, S)
        x2, h2 = _merge_project(y_sb, y_df, proj, gate_b[l], w_branch_sb[l].astype(BF16),
                                w_branch_diff[l].astype(BF16), w_out[l].astype(BF16), x2, gate1,
                                norm2_g[l], scale2, shift2, B, S)
        wq_t = peer_wq[l].T.astype(BF16).reshape(PEER_HEADS, PEER_QUERY_DIM, D)
        h_t, theta, e1, e2 = _peer_select(h2, wq_t, peer_subkeys[l].astype(BF16))
        v_t = jnp.swapaxes(peer_v[l].astype(BF16).reshape(-1, PEER_CHUNK, D), 1, 2)
        x2 = _peer_mix(h_t, peer_u[l].astype(BF16), v_t,
                       theta, e1, e2, x2, gate2, final_g, l == DEPTH - 1, B, S)
    return x2.reshape(B, S, D)
```

```python
import functools
import math

import jax
import jax.numpy as jnp
from jax import lax
from jax.experimental import pallas as pl
from jax.experimental.pallas import tpu as pltpu

F32 = jnp.float32
BF16 = jnp.bfloat16

D_MODEL = 1024
DEPTH = 2
SB_HEADS = 8
SB_HEAD_DIM = 64
SB_WIDTH = SB_HEADS * SB_HEAD_DIM
DIFF_HEADS = 4
DIFF_QK_DIM = 64
DIFF_V_DIM = 2 * DIFF_QK_DIM
DIFF_QK_WIDTH = DIFF_HEADS * 2 * DIFF_QK_DIM
DIFF_V_WIDTH = DIFF_HEADS * DIFF_V_DIM
ROPE_THETA = 500000.0
ROT_DIM = DIFF_QK_DIM // 4
IN_COLS = 3 * SB_WIDTH + 2 * DIFF_QK_WIDTH + DIFF_V_WIDTH + 2 * D_MODEL
PEER_HEADS = 8
PEER_N_KEYS = 128
PEER_N_EXPERTS = PEER_N_KEYS * PEER_N_KEYS
PEER_TOPK = 16
PEER_QUERY_DIM = 256
PEER_CHUNK_KEYS = 16
PEER_CHUNK = PEER_CHUNK_KEYS * PEER_N_KEYS
PEER_TOKEN_TILE = 512
PEER_TILES_PER_STEP = 2
NORM_EPS = 1e-6

LANES = 128
SUBLANES = 8
VMEM_LIMIT = 60 * 1024 * 1024
NEG_INF = float("-inf")
LOG2_E = 1.4426950408889634

IN_TILE_N = 512
COL_SB_Q, COL_SB_K, COL_SB_V = 0, SB_WIDTH // LANES, 2 * SB_WIDTH // LANES
COL_D_Q = 3 * SB_WIDTH // LANES
COL_D_K = COL_D_Q + DIFF_QK_WIDTH // LANES
COL_D_V = COL_D_K + DIFF_QK_WIDTH // LANES
COL_G = (3 * SB_WIDTH + 2 * DIFF_QK_WIDTH + DIFF_V_WIDTH) // D_MODEL


def _params(*sem):
    return pltpu.CompilerParams(dimension_semantics=sem, vmem_limit_bytes=VMEM_LIMIT)


def _nt_dot(a, b):
    return lax.dot_general(a, b, (((1,), (1,)), ((), ())), preferred_element_type=F32)


def _mod_kernel(c_ref, w_ref, b_ref, o_ref):
    c = c_ref[...]
    c_act = c * jax.nn.sigmoid(c)
    o_ref[0] = jnp.dot(c_act, w_ref[0], preferred_element_type=F32,
                       precision=lax.Precision.HIGHEST) + b_ref[0]


def _modulation(c, ada_w, ada_b):
    B, D = c.shape
    rows = SUBLANES
    c_pad = jnp.pad(c, ((0, rows - B), (0, 0)))
    out = pl.pallas_call(
        _mod_kernel,
        grid=(DEPTH, 6),
        in_specs=[
            pl.BlockSpec((rows, D), lambda l, n: (0, 0)),
            pl.BlockSpec((1, D, D), lambda l, n: (l, 0, n)),
            pl.BlockSpec((1, 1, D), lambda l, n: (l, 0, n)),
        ],
        out_specs=pl.BlockSpec((1, rows, D), lambda l, n: (l, 0, n)),
        out_shape=jax.ShapeDtypeStruct((DEPTH, rows, 6 * D), F32),
        compiler_params=_params("arbitrary", "arbitrary"),
        name="adaln_mod",
    )(c_pad, ada_w, ada_b.reshape(DEPTH, 1, 6 * D))
    return out[:, :B].reshape(DEPTH, B, 6, D)


def _norm_mod(x, g, scale, shift):
    ms = jnp.mean(x * x, axis=-1, keepdims=True)
    return (x * lax.rsqrt(ms + NORM_EPS) * g) * (1.0 + scale) + shift


def _inproj_kernel(x_ref, g_ref, sc_ref, sh_ref, w_ref, ra_ref, rb_ref, rc_ref, o_ref):
    h = _norm_mod(x_ref[...], g_ref[...], sc_ref[0], sh_ref[0]).astype(BF16)
    half = ROT_DIM // 2
    for j in range(IN_COLS // IN_TILE_N):
        cols = slice(j * IN_TILE_N, (j + 1) * IN_TILE_N)
        r = jnp.dot(h, w_ref[:, cols], preferred_element_type=F32)
        first_lane_block = j * IN_TILE_N // LANES
        if first_lane_block in (COL_SB_Q, COL_D_Q):
            r = r * (LOG2_E / math.sqrt(SB_HEAD_DIM))
        if first_lane_block in (COL_D_Q, COL_D_K):
            ra, rb, rc = ra_ref[...], rb_ref[...], rc_ref[...]
            for blk in range(IN_TILE_N // LANES):
                t = r[:, blk * LANES:(blk + 1) * LANES]
                rot = t * ra + pltpu.roll(t, LANES - half, 1) * rb + pltpu.roll(t, half, 1) * rc
                o_ref[:, j * IN_TILE_N + blk * LANES:j * IN_TILE_N + (blk + 1) * LANES] = rot.astype(BF16)
        else:
            o_ref[:, cols] = r.astype(BF16)


def _rope_lane_tables(seq_len):
    pos = jnp.arange(seq_len, dtype=F32)
    inv_freq = 1.0 / (ROPE_THETA ** (jnp.arange(0, ROT_DIM, 2, dtype=F32) / ROT_DIM))
    ang = pos[:, None] * inv_freq[None, :]
    cos, sin = jnp.cos(ang), jnp.sin(ang)
    half = ROT_DIM // 2
    lane = jnp.arange(LANES) % DIFF_QK_DIM
    f = lane % half
    first, second = lane < half, (lane >= half) & (lane < ROT_DIM)
    ra = jnp.where((first | second)[None, :], cos[:, f], 1.0)
    rb = jnp.where(first[None, :], -sin[:, f], 0.0)
    rc = jnp.where(second[None, :], sin[:, f], 0.0)
    return ra, rb, rc


def _input_projection(x2, norm_g, scale, shift, w_in_bf, rope, B, S):
    T, D = x2.shape
    tm = min(512, S)
    tiles_per_b = S // tm
    ra, rb, rc = rope
    rope_spec = pl.BlockSpec((tm, LANES), lambda i: (i % tiles_per_b, 0))
    mod_spec = pl.BlockSpec((1, 1, D), lambda i: (i // tiles_per_b, 0, 0))
    assert COL_D_K - COL_D_Q == IN_TILE_N // LANES and COL_D_V - COL_D_K == IN_TILE_N // LANES
    return pl.pallas_call(
        _inproj_kernel,
        grid=(T // tm,),
        in_specs=[
            pl.BlockSpec((tm, D), lambda i: (i, 0)),
            pl.BlockSpec((1, D), lambda i: (0, 0)),
            mod_spec, mod_spec,
            pl.BlockSpec((D, IN_COLS), lambda i: (0, 0)),
            rope_spec, rope_spec, rope_spec,
        ],
        out_specs=pl.BlockSpec((tm, IN_COLS), lambda i: (i, 0)),
        out_shape=jax.ShapeDtypeStruct((T, IN_COLS), BF16),
        compiler_params=_params("parallel"),
        name="in_proj",
    )(x2, norm_g.reshape(1, D), scale.reshape(B, 1, D), shift.reshape(B, 1, D), w_in_bf, ra, rb, rc)


def _sb_kernel(q_ref, k_ref, v_ref, o_ref, acc_sc, cs_sc, *, tb, n_pairs):
    i = pl.program_id(2)
    n_heads = 2 * n_pairs
    lane = lax.broadcasted_iota(jnp.int32, (tb, LANES), 1)
    q_heads = []
    for p in range(n_pairs):
        q = q_ref[:, p * LANES:(p + 1) * LANES]
        zero = jnp.zeros_like(q)
        q_heads += [jnp.where(lane < SB_HEAD_DIM, q, zero), jnp.where(lane >= SB_HEAD_DIM, q, zero)]
    row = lax.broadcasted_iota(jnp.int32, (tb, tb), 0)
    col = lax.broadcasted_iota(jnp.int32, (tb, tb), 1)
    strictly_before = col < row
    suffix_ones = jnp.where(row >= col, 1.0, 0.0).astype(BF16)

    acc_sc[...] = jnp.zeros_like(acc_sc)
    cs_sc[...] = jnp.zeros_like(cs_sc)

    groups = [list(range(g, g + SB_GROUP_HEADS)) for g in range(0, n_heads, SB_GROUP_HEADS)]

    def blocks(js, masked):
        kv = []
        for j in js:
            start = pl.multiple_of(j * tb, tb)
            kv.append((k_ref[pl.ds(start, tb), :], v_ref[pl.ds(start, tb), :]))
        items = [(b, heads) for b in range(len(js)) for heads in groups]

        def logits(item):
            b, heads = item
            return [_nt_dot(q_heads[h], kv[b][0][:, (h // 2) * LANES:(h // 2 + 1) * LANES]) for h in heads]

        def suffix_sums(ys):
            costs = []
            for y in ys:
                cost = jnp.where(y > 30.0, y, jnp.log2(1.0 + jnp.exp2(y)))
                if masked:
                    cost = jnp.where(strictly_before, cost, 0.0)
                costs.append(cost)
            return [jnp.dot(cost.astype(BF16), suffix_ones, preferred_element_type=F32) for cost in costs]

        def accumulate(item, ys, sums):
            b, heads = item
            for h, y, s in zip(heads, ys, sums):
                later = cs_sc[h]
                w = jnp.exp2(y - s - later)
                if masked:
                    w = jnp.where(strictly_before, w, 0.0)
                acc_sc[h] += jnp.dot(w.astype(BF16), kv[b][1][:, (h // 2) * LANES:(h // 2 + 1) * LANES],
                                     preferred_element_type=F32)
                cs_sc[h] = later + s[:, 0:1]

        ys_next = logits(items[0])
        for n, item in enumerate(items):
            ys_cur = ys_next
            sums = suffix_sums(ys_cur)
            if n + 1 < len(items):
                ys_next = logits(items[n + 1])
            accumulate(item, ys_cur, sums)

    blocks([i], True)

    @pl.when(i % 2 == 1)
    def _():
        blocks([i - 1], False)

    def body(pp, carry):
        first = i - (i % 2) - 1 - 2 * pp
        blocks([first, first - 1], False)
        return carry

    lax.fori_loop(0, i // 2, body, 0)
    for p in range(n_pairs):
        o_ref[:, p * LANES:(p + 1) * LANES] = jnp.where(
            lane < SB_HEAD_DIM, acc_sc[2 * p], acc_sc[2 * p + 1]).astype(BF16)


SB_PAIRS_PER_STEP = 4
SB_GROUP_HEADS = 4


def _sb_attention(proj, B, S):
    T = proj.shape[0]
    tb = min(256, S)
    nq = S // tb
    n_pairs = SB_PAIRS_PER_STEP
    width = n_pairs * LANES
    groups = SB_WIDTH // width
    return pl.pallas_call(
        functools.partial(_sb_kernel, tb=tb, n_pairs=n_pairs),
        grid=(B, groups, nq),
        in_specs=[
            pl.BlockSpec((tb, width), lambda b, p, i: (b * nq + i, COL_SB_Q // n_pairs + p)),
            pl.BlockSpec((S, width), lambda b, p, i: (b, COL_SB_K // n_pairs + p)),
            pl.BlockSpec((S, width), lambda b, p, i: (b, COL_SB_V // n_pairs + p)),
        ],
        out_specs=pl.BlockSpec((tb, width), lambda b, p, i: (b * nq + i, p)),
        out_shape=jax.ShapeDtypeStruct((T, SB_WIDTH), BF16),
        scratch_shapes=[pltpu.VMEM((2 * n_pairs, tb, LANES), F32),
                        pltpu.VMEM((2 * n_pairs, tb, 1), F32)],
        compiler_params=_params("parallel", "parallel", "arbitrary"),
        name="sb_attention",
    )(proj, proj, proj)


def _diff_kernel(lp_ref, g_ref, q_ref, k_ref, v_ref, o_ref, m_sc, acc_sc, *, tq, tk, lam_init):
    i = pl.program_id(1)
    lane = lax.broadcasted_iota(jnp.int32, (tq, LANES), 1)
    q_maps = []
    for h in range(DIFF_HEADS):
        q = q_ref[:, h * LANES:(h + 1) * LANES]
        zero = jnp.zeros_like(q)
        q_maps += [jnp.where(lane < DIFF_QK_DIM, q, zero), jnp.where(lane >= DIFF_QK_DIM, q, zero)]
    n_chains = len(q_maps)
    m_sc[...] = jnp.full_like(m_sc, NEG_INF)
    acc_sc[...] = jnp.zeros_like(acc_sc)
    q_pos = i * tq + lax.broadcasted_iota(jnp.int32, (tq, tk), 0)
    k_off = lax.broadcasted_iota(jnp.int32, (tq, tk), 1)
    ones = jnp.ones((tk, LANES), BF16)

    head_cols = lambda c: slice((c // 2) * LANES, (c // 2 + 1) * LANES)
    chains = list(range(n_chains))

    def blocks(js, masked):
        kv = []
        for j in js:
            start = pl.multiple_of(j * tk, tk)
            kv.append((k_ref[pl.ds(start, tk), :], v_ref[pl.ds(start, tk), :]))

        def logits(b):
            return [_nt_dot(q_maps[c], kv[b][0][:, head_cols(c)]) for c in chains]

        def softmax_step(b, scores):
            if masked:
                visible = js[b] * tk + k_off <= q_pos
                scores = [jnp.where(visible, s, NEG_INF) for s in scores]
            m_prev = [m_sc[c] for c in chains]
            m_new = [jnp.maximum(mp, jnp.max(s, axis=1, keepdims=True)) for mp, s in zip(m_prev, scores)]
            probs = [jnp.exp2(s - jnp.concatenate([mn] * (tk // LANES), axis=1))
                     for s, mn in zip(scores, m_new)]
            for c, mn in zip(chains, m_new):
                m_sc[c] = mn
            return m_prev, m_new, probs

        def accumulate(b, m_prev, m_new, probs):
            for c, mp, mn, p in zip(chains, m_prev, m_new, probs):
                alpha = jnp.exp2(mp - mn)
                v_ext = jnp.concatenate([kv[b][1][:, head_cols(c)], ones], axis=1)
                acc_sc[c] = (jnp.concatenate([alpha, alpha], axis=1) * acc_sc[c]
                             + jnp.dot(p.astype(BF16), v_ext, preferred_element_type=F32))

        s_next = logits(0)
        for b in range(len(js)):
            s_cur = s_next
            stats = softmax_step(b, s_cur)
            if b + 1 < len(js):
                s_next = logits(b + 1)
            accumulate(b, *stats)

    n_full = (i * tq) // tk

    @pl.when(n_full % 2 == 1)
    def _():
        blocks([0], False)

    def body(pp, carry):
        first = n_full % 2 + 2 * pp
        blocks([first, first + 1], False)
        return carry

    lax.fori_loop(0, n_full // 2, body, 0)
    blocks([n_full], True)

    lp = lp_ref[...]
    lam = (jnp.exp(jnp.sum(lp[0:1] * lp[1:2], axis=1, keepdims=True))
           - jnp.exp(jnp.sum(lp[2:3] * lp[3:4], axis=1, keepdims=True)) + lam_init)
    for h in range(DIFF_HEADS):
        a1, a2 = acc_sc[2 * h], acc_sc[2 * h + 1]
        o = a1[:, :LANES] / a1[:, LANES:] - lam * (a2[:, :LANES] / a2[:, LANES:])
        ms = jnp.mean(o * o, axis=-1, keepdims=True)
        y = (o * lax.rsqrt(ms + NORM_EPS) * g_ref[...]) * (1.0 - lam_init)
        o_ref[:, h * LANES:(h + 1) * LANES] = y.astype(BF16)


def _diff_attention(proj, lam_params, subln_g, lam_init, B, S):
    T = proj.shape[0]
    tq = min(256, S)
    tk = min(512, S)
    nq = S // tq
    n_chains = 2 * DIFF_HEADS
    return pl.pallas_call(
        functools.partial(_diff_kernel, tq=tq, tk=tk, lam_init=lam_init),
        grid=(B, nq),
        in_specs=[
            pl.BlockSpec((4, DIFF_QK_DIM), lambda b, i: (0, 0)),
            pl.BlockSpec((1, DIFF_V_DIM), lambda b, i: (0, 0)),
            pl.BlockSpec((tq, DIFF_QK_WIDTH), lambda b, i: (b * nq + i, COL_D_Q * LANES // DIFF_QK_WIDTH)),
            pl.BlockSpec((S, DIFF_QK_WIDTH), lambda b, i: (b, COL_D_K * LANES // DIFF_QK_WIDTH)),
            pl.BlockSpec((S, DIFF_V_WIDTH), lambda b, i: (b, COL_D_V * LANES // DIFF_V_WIDTH)),
        ],
        out_specs=pl.BlockSpec((tq, DIFF_V_WIDTH), lambda b, i: (b * nq + i, 0)),
        out_shape=jax.ShapeDtypeStruct((T, DIFF_V_WIDTH), BF16),
        scratch_shapes=[pltpu.VMEM((n_chains, tq, LANES), F32),
                        pltpu.VMEM((n_chains, tq, 2 * LANES), F32)],
        compiler_params=_params("parallel", "arbitrary"),
        name="diff_attention",
    )(lam_params, subln_g.reshape(1, DIFF_V_DIM), proj, proj, proj)


def _merge_kernel(ysb_ref, ydf_ref, gs_ref, gd_ref, gb_ref, wbs_ref, wbd_ref, wo_ref,
                  x_ref, g1_ref, n2_ref, sc2_ref, sh2_ref, xo_ref, h2_ref):
    D = x_ref.shape[1]
    a = jnp.dot(ysb_ref[...], wbs_ref[...], preferred_element_type=F32)
    b = jnp.dot(ydf_ref[...], wbd_ref[...], preferred_element_type=F32)
    gb = gb_ref[...]
    g_sb = jax.nn.sigmoid(gs_ref[...].astype(F32) + gb[:, :D])
    g_df = jax.nn.sigmoid(gd_ref[...].astype(F32) + gb[:, D:])
    merged = g_sb * a + g_df * b
    y = jnp.dot(merged.astype(BF16), wo_ref[...], preferred_element_type=F32)
    xn = x_ref[...] + g1_ref[0] * y
    xo_ref[...] = xn
    h2_ref[...] = _norm_mod(xn, n2_ref[...], sc2_ref[0], sh2_ref[0]).astype(BF16)


def _merge_project(y_sb, y_df, proj, gate_b, wbs, wbd, wo, x2, gate1, norm2_g, scale2, shift2, B, S):
    T, D = x2.shape
    tm = min(512, S)
    tiles_per_b = S // tm
    mod_spec = pl.BlockSpec((1, 1, D), lambda i: (i // tiles_per_b, 0, 0))
    const = lambda shape: pl.BlockSpec(shape, lambda i: (0, 0))
    return pl.pallas_call(
        _merge_kernel,
        grid=(T // tm,),
        in_specs=[
            pl.BlockSpec((tm, SB_WIDTH), lambda i: (i, 0)),
            pl.BlockSpec((tm, DIFF_V_WIDTH), lambda i: (i, 0)),
            pl.BlockSpec((tm, D), lambda i: (i, COL_G)),
            pl.BlockSpec((tm, D), lambda i: (i, COL_G + 1)),
            const((1, 2 * D)),
            const((SB_WIDTH, D)), const((DIFF_V_WIDTH, D)), const((D, D)),
            pl.BlockSpec((tm, D), lambda i: (i, 0)),
            mod_spec,
            const((1, D)),
            mod_spec, mod_spec,
        ],
        out_specs=[pl.BlockSpec((tm, D), lambda i: (i, 0)), pl.BlockSpec((tm, D), lambda i: (i, 0))],
        out_shape=[jax.ShapeDtypeStruct((T, D), F32), jax.ShapeDtypeStruct((T, D), BF16)],
        compiler_params=_params("parallel"),
        name="merge_out_proj",
    )(y_sb, y_df, proj, proj, gate_b.reshape(1, 2 * D), wbs, wbd, wo, x2,
      gate1.reshape(B, 1, D), norm2_g.reshape(1, D), scale2.reshape(B, 1, D), shift2.reshape(B, 1, D))


def _sort16_network():
    def merge(lo, hi, r):
        step = r * 2
        if step < hi - lo:
            yield from merge(lo, hi, step)
            yield from merge(lo + r, hi, step)
            yield from [(i, i + r) for i in range(lo + r, hi - r, step)]
        else:
            yield (lo, lo + r)

    def sort(lo, hi):
        if hi - lo >= 1:
            mid = lo + (hi - lo) // 2
            yield from sort(lo, mid)
            yield from sort(mid + 1, hi)
            yield from merge(lo, hi, 1)

    return list(sort(0, 15))


def _top_values(scores, k):
    groups = scores.shape[0] // SUBLANES
    assert groups == 16 and k <= groups + 1
    v = [scores[SUBLANES * g:SUBLANES * (g + 1)] for g in range(groups)]
    for a, b in _sort16_network():
        v[a], v[b] = jnp.maximum(v[a], v[b]), jnp.minimum(v[a], v[b])
    v.append(jnp.full_like(v[0], NEG_INF))
    vals = []
    for r in range(k):
        m = jnp.max(v[0], axis=0, keepdims=True)
        vals.append(m)
        still_needed = k - 1 - r
        if still_needed:
            hit = v[0] == m
            for d in range(min(still_needed, groups)):
                v[d] = jnp.where(hit, v[d + 1], v[d])
    return vals


def _candidate_sums(v1, v1_all, v2, v2_all, row_id):
    assert PEER_TOPK == 16 and SUBLANES == 8
    lo2 = v2_all[0:8]
    hi1 = v1_all[8:16]
    shift = lambda x, n: pltpu.roll(x, n, 0)
    groups = [v1[0] + v2_all[0:8], v1[0] + v2_all[8:16], v1[0] + v2_all[16:24],
              v1[1] + lo2]
    groups.append(jnp.where(row_id < 5, v1[2] + lo2, v1[4] + shift(lo2, 5)))
    groups.append(jnp.where(row_id < 4, v1[3] + lo2,
                            jnp.where(row_id < 6, v1[5] + shift(lo2, 4),
                                      v1[6] + shift(lo2, 6))))
    tail = shift(hi1, 2) + v2[0]
    groups.append(jnp.where(row_id < 2, v1[7] + lo2, tail))
    groups.append(jnp.where(row_id < 2, tail,
                            jnp.where(row_id == 2, v1[16] + v2[0], NEG_INF)))
    return groups


def _peer_select_kernel(h_ref, wq_ref, sk_ref, ht_ref, th_ref, e1_ref, e2_ref, *, tm):
    hb = h_ref[...]
    ht_ref[0] = hb.T
    K = PEER_TOPK + 1
    row_id = lax.broadcasted_iota(jnp.int32, (SUBLANES, LANES), 0)
    pad_rows = jnp.full((3 * SUBLANES - K, LANES), NEG_INF, F32)

    def head(h, carry):
        q_t = _nt_dot(wq_ref[h], hb).astype(BF16)
        half = PEER_QUERY_DIM // 2
        sc = [jnp.dot(sk_ref[h, c], q_t[c * half:(c + 1) * half], preferred_element_type=F32)
              for c in range(2)]
        for lc in range(tm // LANES):
            ls = slice(lc * LANES, (lc + 1) * LANES)
            s1, s2 = sc[0][:, ls], sc[1][:, ls]
            v1 = _top_values(s1, K)
            v2 = _top_values(s2, K)
            v2_all = jnp.concatenate(v2 + [pad_rows], axis=0)
            v1_all = jnp.concatenate(v1 + [pad_rows], axis=0)
            cands = _candidate_sums(v1, v1_all, v2, v2_all, row_id)
            best = cands[0][0:1]
            z_sum = jnp.zeros_like(best)
            last = best
            for r in range(K):
                stacked = cands[0]
                for ca in cands[1:]:
                    stacked = jnp.maximum(stacked, ca)
                m = jnp.max(stacked, axis=0, keepdims=True)
                if r < PEER_TOPK:
                    z_sum = z_sum + jnp.exp(m - best)
                    last = m
                    cands = [jnp.where(ca == m, NEG_INF, ca) for ca in cands]
            tau = 0.5 * (last + m)
            half_inv_z = 0.5 / z_sum
            chunked = (PEER_N_KEYS // PEER_CHUNK_KEYS, PEER_CHUNK_KEYS, LANES)
            th_ref[:, 0, h, :, ls] = jnp.exp(tau - s1 - v2[0]).reshape(chunked)
            e1_ref[:, 0, h, :, ls] = (jnp.exp(s1 - v1[0]) * half_inv_z).reshape(chunked)
            e2_ref[0, h, :, ls] = jnp.exp(s2 - v2[0])
        return carry

    lax.fori_loop(0, PEER_HEADS, head, 0)


def _peer_select(h2, wq_t, subkeys_bf):
    T, D = h2.shape
    tm = PEER_TOKEN_TILE
    n_tiles = T // tm
    tok_spec = pl.BlockSpec((1, PEER_HEADS, PEER_N_KEYS, tm), lambda i: (i, 0, 0, 0))
    tok_shape = jax.ShapeDtypeStruct((n_tiles, PEER_HEADS, PEER_N_KEYS, tm), F32)
    n_chunks = PEER_N_KEYS // PEER_CHUNK_KEYS
    row_spec = pl.BlockSpec((n_chunks, 1, PEER_HEADS, PEER_CHUNK_KEYS, tm), lambda i: (0, i, 0, 0, 0))
    row_shape = jax.ShapeDtypeStruct((n_chunks, n_tiles, PEER_HEADS, PEER_CHUNK_KEYS, tm), F32)
    return pl.pallas_call(
        functools.partial(_peer_select_kernel, tm=tm),
        grid=(n_tiles,),
        in_specs=[
            pl.BlockSpec((tm, D), lambda i: (i, 0)),
            pl.BlockSpec((PEER_HEADS, PEER_QUERY_DIM, D), lambda i: (0, 0, 0)),
            pl.BlockSpec((PEER_HEADS, 2, PEER_N_KEYS, PEER_QUERY_DIM // 2), lambda i: (0, 0, 0, 0)),
        ],
        out_specs=[
            pl.BlockSpec((1, D, tm), lambda i: (i, 0, 0)),
            row_spec, row_spec, tok_spec,
        ],
        out_shape=[
            jax.ShapeDtypeStruct((n_tiles, D, tm), BF16),
            row_shape, row_shape, tok_shape,
        ],
        compiler_params=_params("parallel"),
        name="peer_select",
    )(h2, wq_t, subkeys_bf)


PEER_ROW_BLOCK_KEYS = 2
PEER_PIECE_TOKENS = 2 * LANES


def _peer_mix_kernel(ht_ref, u_ref, vt_ref, th_ref, e1_ref, e2_ref, x_ref, g2_ref, fg_ref,
                     xo_ref, acc_sc, act_sc, e2_sc, *, tm, n_tiles, final_norm):
    j = pl.program_id(1)

    @pl.when(j == 0)
    def _():
        acc_sc[...] = jnp.zeros_like(acc_sc)
        for lc in range(tm // LANES):
            e2_sc[:, :, lc] = e2_ref[:, :, :, lc * LANES:(lc + 1) * LANES]

    rb = PEER_ROW_BLOCK_KEYS * PEER_N_KEYS
    tw = PEER_PIECE_TOKENS
    n_rb = PEER_CHUNK // rb
    pieces = [(r, c) for c in range(tm // tw) for r in range(n_rb)]

    def tile(t, carry):
        def scores(p):
            r, c = pieces[p]
            return jnp.dot(u_ref[r * rb:(r + 1) * rb, :], ht_ref[t, :, c * tw:(c + 1) * tw],
                           preferred_element_type=F32)

        def gates(p, a_blk, k):
            r, c = pieces[p]
            ii = r * PEER_ROW_BLOCK_KEYS + k
            for lc in range(tw // LANES):
                chunk = c * (tw // LANES) + lc
                ls = slice(chunk * LANES, (chunk + 1) * LANES)
                gate = None
                for h in range(PEER_HEADS):
                    theta = th_ref[0, t, h, ii:ii + 1, ls]
                    e1 = e1_ref[0, t, h, ii:ii + 1, ls]
                    e2 = e2_sc[t, h, chunk]
                    term = jnp.where(e2 >= theta, e1 * e2, 0.0)
                    gate = term if gate is None else gate + term
                a = a_blk[k * PEER_N_KEYS:(k + 1) * PEER_N_KEYS, lc * LANES:(lc + 1) * LANES]
                gelu2 = a * (1.0 + lax.erf(a * (1.0 / math.sqrt(2.0))))
                act_sc[c, ii * PEER_N_KEYS:(ii + 1) * PEER_N_KEYS, lc * LANES:(lc + 1) * LANES] = (
                    gelu2 * gate).astype(BF16)

        def mix(c):
            acc_sc[t, c] += jnp.dot(vt_ref[0], act_sc[c], preferred_element_type=F32)

        a_next = scores(0)
        for p in range(len(pieces)):
            r, c = pieces[p]
            a_cur = a_next
            if p + 1 < len(pieces):
                a_next = scores(p + 1)
            for k in range(PEER_ROW_BLOCK_KEYS):
                if k == PEER_ROW_BLOCK_KEYS // 2 and r == 0 and c > 0:
                    mix(c - 1)
                gates(p, a_cur, k)
        mix(tm // tw - 1)
        return carry

    lax.fori_loop(0, n_tiles, tile, 0)

    @pl.when(j == pl.num_programs(1) - 1)
    def _():
        for t in range(n_tiles):
            rows = slice(t * tm, (t + 1) * tm)
            mixed_t = jnp.concatenate([acc_sc[t, c] for c in range(tm // tw)], axis=1)
            xn = x_ref[rows, :] + g2_ref[0] * mixed_t.T
            if final_norm:
                ms = jnp.mean(xn * xn, axis=-1, keepdims=True)
                xn = xn * lax.rsqrt(ms + NORM_EPS) * fg_ref[...]
            xo_ref[rows, :] = xn


def _peer_mix(h_t, u_bf, v_t, theta, e1, e2, x2, gate2, final_g, final_norm, B, S):
    T, D = x2.shape
    tm = PEER_TOKEN_TILE
    n_tiles = min(PEER_TILES_PER_STEP, S // tm)
    span = n_tiles * tm
    steps_per_b = S // span
    tok_spec = pl.BlockSpec((n_tiles, PEER_HEADS, PEER_N_KEYS, tm), lambda i, j: (i, 0, 0, 0),
                            pipeline_mode=pl.Buffered(1))
    row_spec = pl.BlockSpec((1, n_tiles, PEER_HEADS, PEER_CHUNK_KEYS, tm), lambda i, j: (j, i, 0, 0, 0))
    return pl.pallas_call(
        functools.partial(_peer_mix_kernel, tm=tm, n_tiles=n_tiles, final_norm=final_norm),
        grid=(T // span, PEER_N_EXPERTS // PEER_CHUNK),
        in_specs=[
            pl.BlockSpec((n_tiles, D, tm), lambda i, j: (i, 0, 0)),
            pl.BlockSpec((PEER_CHUNK, D), lambda i, j: (j, 0)),
            pl.BlockSpec((1, D, PEER_CHUNK), lambda i, j: (j, 0, 0)),
            row_spec, row_spec, tok_spec,
            pl.BlockSpec((span, D), lambda i, j: (i, 0)),
            pl.BlockSpec((1, 1, D), lambda i, j: (i // steps_per_b, 0, 0)),
            pl.BlockSpec((1, D), lambda i, j: (0, 0)),
        ],
        out_specs=pl.BlockSpec((span, D), lambda i, j: (i, 0)),
        out_shape=jax.ShapeDtypeStruct((T, D), F32),
        scratch_shapes=[pltpu.VMEM((n_tiles, tm // PEER_PIECE_TOKENS, D, PEER_PIECE_TOKENS), F32),
                        pltpu.VMEM((tm // PEER_PIECE_TOKENS, PEER_CHUNK, PEER_PIECE_TOKENS), BF16),
                        pltpu.VMEM((n_tiles, PEER_HEADS, tm // LANES, PEER_N_KEYS, LANES), F32)],
        compiler_params=_params("parallel", "arbitrary"),
        name="peer_mix",
    )(h_t, u_bf, v_t, theta, e1, e2, x2, gate2.reshape(B, 1, D), final_g.reshape(1, D))


def kernel(x, c, norm1_g, norm2_g, ada_w, ada_b, w_in, gate_b, diff_lambda, diff_subln_g,
           w_branch_sb, w_branch_diff, w_out, peer_wq, peer_subkeys, peer_u, peer_v, final_g):
    B, S, D = x.shape
    assert D == D_MODEL and S % LANES == 0
    x2 = x.reshape(B * S, D)
    mod = _modulation(c, ada_w, ada_b)
    rope = _rope_lane_tables(S)
    for l in range(DEPTH):
        lam_init = 0.8 - 0.6 * math.exp(-0.3 * l)
        shift1, scale1, gate1, shift2, scale2, gate2 = (mod[l, :, n] for n in range(6))
        proj = _input_projection(x2, norm1_g[l], scale1, shift1, w_in[l].astype(BF16), rope, B, S)
        y_sb = _sb_attention(proj, B, S)
        y_df = _diff_attention(proj, diff_lambda[l], diff_subln_g[l], lam_init, B, S)
        x2, h2 = _merge_project(y_sb, y_df, proj, gate_b[l], w_branch_sb[l].astype(BF16),
                                w_branch_diff[l].astype(BF16), w_out[l].astype(BF16), x2, gate1,
                                norm2_g[l], scale2, shift2, B, S)
        wq_t = peer_wq[l].T.astype(BF16).reshape(PEER_HEADS, PEER_QUERY_DIM, D)
        h_t, theta, e1, e2 = _peer_select(h2, wq_t, peer_subkeys[l].astype(BF16))
        v_t = jnp.swapaxes(peer_v[l].astype(BF16).reshape(-1, PEER_CHUNK, D), 1, 2)
        x2 = _peer_mix(h_t, peer_u[l].astype(BF16), v_t,
                       theta, e1, e2, x2, gate2, final_g, l == DEPTH - 1, B, S)
    return x2.reshape(B, S, D)
```

```python
import functools
import math

import jax
import jax.numpy as jnp
from jax import lax
from jax.experimental import pallas as pl
from jax.experimental.pallas import tpu as pltpu

F32 = jnp.float32
BF16 = jnp.bfloat16

D_MODEL = 1024
DEPTH = 2
SB_HEADS = 8
SB_HEAD_DIM = 64
SB_WIDTH = SB_HEADS * SB_HEAD_DIM
DIFF_HEADS = 4
DIFF_QK_DIM = 64
DIFF_V_DIM = 2 * DIFF_QK_DIM
DIFF_QK_WIDTH = DIFF_HEADS * 2 * DIFF_QK_DIM
DIFF_V_WIDTH = DIFF_HEADS * DIFF_V_DIM
ROPE_THETA = 500000.0
ROT_DIM = DIFF_QK_DIM // 4
IN_COLS = 3 * SB_WIDTH + 2 * DIFF_QK_WIDTH + DIFF_V_WIDTH + 2 * D_MODEL
PEER_HEADS = 8
PEER_N_KEYS = 128
PEER_N_EXPERTS = PEER_N_KEYS * PEER_N_KEYS
PEER_TOPK = 16
PEER_QUERY_DIM = 256
PEER_CHUNK_KEYS = 16
PEER_CHUNK = PEER_CHUNK_KEYS * PEER_N_KEYS
PEER_TOKEN_TILE = 512
PEER_TILES_PER_STEP = 2
NORM_EPS = 1e-6

LANES = 128
SUBLANES = 8
VMEM_LIMIT = 60 * 1024 * 1024
NEG_INF = float("-inf")
LOG2_E = 1.4426950408889634

IN_TILE_N = 512
COL_SB_Q, COL_SB_K, COL_SB_V = 0, SB_WIDTH // LANES, 2 * SB_WIDTH // LANES
COL_D_Q = 3 * SB_WIDTH // LANES
COL_D_K = COL_D_Q + DIFF_QK_WIDTH // LANES
COL_D_V = COL_D_K + DIFF_QK_WIDTH // LANES
COL_G = (3 * SB_WIDTH + 2 * DIFF_QK_WIDTH + DIFF_V_WIDTH) // D_MODEL


def _params(*sem):
    return pltpu.CompilerParams(dimension_semantics=sem, vmem_limit_bytes=VMEM_LIMIT)


def _nt_dot(a, b):
    return lax.dot_general(a, b, (((1,), (1,)), ((), ())), preferred_element_type=F32)


def _mod_kernel(c_ref, w_ref, b_ref, o_ref):
    c = c_ref[...]
    c_act = c * jax.nn.sigmoid(c)
    o_ref[0] = jnp.dot(c_act, w_ref[0], preferred_element_type=F32,
                       precision=lax.Precision.HIGHEST) + b_ref[0]


def _modulation(c, ada_w, ada_b):
    B, D = c.shape
    rows = SUBLANES
    c_pad = jnp.pad(c, ((0, rows - B), (0, 0)))
    out = pl.pallas_call(
        _mod_kernel,
        grid=(DEPTH, 6),
        in_specs=[
            pl.BlockSpec((rows, D), lambda l, n: (0, 0)),
            pl.BlockSpec((1, D, D), lambda l, n: (l, 0, n)),
            pl.BlockSpec((1, 1, D), lambda l, n: (l, 0, n)),
        ],
        out_specs=pl.BlockSpec((1, rows, D), lambda l, n: (l, 0, n)),
        out_shape=jax.ShapeDtypeStruct((DEPTH, rows, 6 * D), F32),
        compiler_params=_params("arbitrary", "arbitrary"),
        name="adaln_mod",
    )(c_pad, ada_w, ada_b.reshape(DEPTH, 1, 6 * D))
    return out[:, :B].reshape(DEPTH, B, 6, D)


def _norm_mod(x, g, scale, shift):
    ms = jnp.mean(x * x, axis=-1, keepdims=True)
    return (x * lax.rsqrt(ms + NORM_EPS) * g) * (1.0 + scale) + shift


def _inproj_kernel(x_ref, g_ref, sc_ref, sh_ref, w_ref, ra_ref, rb_ref, rc_ref, o_ref):
    h = _norm_mod(x_ref[...], g_ref[...], sc_ref[0], sh_ref[0]).astype(BF16)
    half = ROT_DIM // 2
    for j in range(IN_COLS // IN_TILE_N):
        cols = slice(j * IN_TILE_N, (j + 1) * IN_TILE_N)
        r = jnp.dot(h, w_ref[:, cols], preferred_element_type=F32)
        first_lane_block = j * IN_TILE_N // LANES
        if first_lane_block in (COL_SB_Q, COL_D_Q):
            r = r * (LOG2_E / math.sqrt(SB_HEAD_DIM))
        if first_lane_block in (COL_D_Q, COL_D_K):
            ra, rb, rc = ra_ref[...], rb_ref[...], rc_ref[...]
            for blk in range(IN_TILE_N // LANES):
                t = r[:, blk * LANES:(blk + 1) * LANES]
                rot = t * ra + pltpu.roll(t, LANES - half, 1) * rb + pltpu.roll(t, half, 1) * rc
                o_ref[:, j * IN_TILE_N + blk * LANES:j * IN_TILE_N + (blk + 1) * LANES] = rot.astype(BF16)
        else:
            o_ref[:, cols] = r.astype(BF16)


def _rope_lane_tables(seq_len):
    pos = jnp.arange(seq_len, dtype=F32)
    inv_freq = 1.0 / (ROPE_THETA ** (jnp.arange(0, ROT_DIM, 2, dtype=F32) / ROT_DIM))
    ang = pos[:, None] * inv_freq[None, :]
    cos, sin = jnp.cos(ang), jnp.sin(ang)
    half = ROT_DIM // 2
    lane = jnp.arange(LANES) % DIFF_QK_DIM
    f = lane % half
    first, second = lane < half, (lane >= half) & (lane < ROT_DIM)
    ra = jnp.where((first | second)[None, :], cos[:, f], 1.0)
    rb = jnp.where(first[None, :], -sin[:, f], 0.0)
    rc = jnp.where(second[None, :], sin[:, f], 0.0)
    return ra, rb, rc


def _input_projection(x2, norm_g, scale, shift, w_in_bf, rope, B, S):
    T, D = x2.shape
    tm = min(512, S)
    tiles_per_b = S // tm
    ra, rb, rc = rope
    rope_spec = pl.BlockSpec((tm, LANES), lambda i: (i % tiles_per_b, 0))
    mod_spec = pl.BlockSpec((1, 1, D), lambda i: (i // tiles_per_b, 0, 0))
    assert COL_D_K - COL_D_Q == IN_TILE_N // LANES and COL_D_V - COL_D_K == IN_TILE_N // LANES
    return pl.pallas_call(
        _inproj_kernel,
        grid=(T // tm,),
        in_specs=[
            pl.BlockSpec((tm, D), lambda i: (i, 0)),
            pl.BlockSpec((1, D), lambda i: (0, 0)),
            mod_spec, mod_spec,
            pl.BlockSpec((D, IN_COLS), lambda i: (0, 0)),
            rope_spec, rope_spec, rope_spec,
        ],
        out_specs=pl.BlockSpec((tm, IN_COLS), lambda i: (i, 0)),
        out_shape=jax.ShapeDtypeStruct((T, IN_COLS), BF16),
        compiler_params=_params("parallel"),
        name="in_proj",
    )(x2, norm_g.reshape(1, D), scale.reshape(B, 1, D), shift.reshape(B, 1, D), w_in_bf, ra, rb, rc)


def _sb_kernel(q_ref, k_ref, v_ref, o_ref, acc_sc, cs_sc, *, tb, n_pairs):
    i = pl.program_id(2)
    n_heads = 2 * n_pairs
    lane = lax.broadcasted_iota(jnp.int32, (tb, LANES), 1)
    q_heads = []
    for p in range(n_pairs):
        q = q_ref[:, p * LANES:(p + 1) * LANES]
        zero = jnp.zeros_like(q)
        q_heads += [jnp.where(lane < SB_HEAD_DIM, q, zero), jnp.where(lane >= SB_HEAD_DIM, q, zero)]
    row = lax.broadcasted_iota(jnp.int32, (tb, tb), 0)
    col = lax.broadcasted_iota(jnp.int32, (tb, tb), 1)
    strictly_before = col < row
    suffix_ones = jnp.where(row >= col, 1.0, 0.0).astype(BF16)

    acc_sc[...] = jnp.zeros_like(acc_sc)
    cs_sc[...] = jnp.zeros_like(cs_sc)

    groups = [list(range(g, g + SB_GROUP_HEADS)) for g in range(0, n_heads, SB_GROUP_HEADS)]

    def blocks(js, masked):
        kv = []
        for j in js:
            start = pl.multiple_of(j * tb, tb)
            kv.append((k_ref[pl.ds(start, tb), :], v_ref[pl.ds(start, tb), :]))
        items = [(b, heads) for b in range(len(js)) for heads in groups]

        def logits(item):
            b, heads = item
            return [_nt_dot(q_heads[h], kv[b][0][:, (h // 2) * LANES:(h // 2 + 1) * LANES]) for h in heads]

        def suffix_sums(ys):
            costs = []
            for y in ys:
                cost = jnp.where(y > 30.0, y, jnp.log2(1.0 + jnp.exp2(y)))
                if masked:
                    cost = jnp.where(strictly_before, cost, 0.0)
                costs.append(cost)
            return [jnp.dot(cost.astype(BF16), suffix_ones, preferred_element_type=F32) for cost in costs]

        def accumulate(item, ys, sums):
            b, heads = item
            for h, y, s in zip(heads, ys, sums):
                later = cs_sc[h]
                w = jnp.exp2(y - s - later)
                if masked:
                    w = jnp.where(strictly_before, w, 0.0)
                acc_sc[h] += jnp.dot(w.astype(BF16), kv[b][1][:, (h // 2) * LANES:(h // 2 + 1) * LANES],
                                     preferred_element_type=F32)
                cs_sc[h] = later + s[:, 0:1]

        ys_next = logits(items[0])
        for n, item in enumerate(items):
            ys_cur = ys_next
            sums = suffix_sums(ys_cur)
            if n + 1 < len(items):
                ys_next = logits(items[n + 1])
            accumulate(item, ys_cur, sums)

    blocks([i], True)

    @pl.when(i % 2 == 1)
    def _():
        blocks([i - 1], False)

    def body(pp, carry):
        first = i - (i % 2) - 1 - 2 * pp
        blocks([first, first - 1], False)
        return carry

    lax.fori_loop(0, i // 2, body, 0)
    for p in range(n_pairs):
        o_ref[:, p * LANES:(p + 1) * LANES] = jnp.where(
            lane < SB_HEAD_DIM, acc_sc[2 * p], acc_sc[2 * p + 1]).astype(BF16)


SB_PAIRS_PER_STEP = 4
SB_GROUP_HEADS = 4


def _sb_attention(proj, B, S):
    T = proj.shape[0]
    tb = min(256, S)
    nq = S // tb
    n_pairs = SB_PAIRS_PER_STEP
    width = n_pairs * LANES
    groups = SB_WIDTH // width
    return pl.pallas_call(
        functools.partial(_sb_kernel, tb=tb, n_pairs=n_pairs),
        grid=(B, groups, nq),
        in_specs=[
            pl.BlockSpec((tb, width), lambda b, p, i: (b * nq + i, COL_SB_Q // n_pairs + p)),
            pl.BlockSpec((S, width), lambda b, p, i: (b, COL_SB_K // n_pairs + p)),
            pl.BlockSpec((S, width), lambda b, p, i: (b, COL_SB_V // n_pairs + p)),
        ],
        out_specs=pl.BlockSpec((tb, width), lambda b, p, i: (b * nq + i, p)),
        out_shape=jax.ShapeDtypeStruct((T, SB_WIDTH), BF16),
        scratch_shapes=[pltpu.VMEM((2 * n_pairs, tb, LANES), F32),
                        pltpu.VMEM((2 * n_pairs, tb, 1), F32)],
        compiler_params=_params("parallel", "parallel", "arbitrary"),
        name="sb_attention",
    )(proj, proj, proj)


def _diff_kernel(lp_ref, g_ref, q_ref, k_ref, v_ref, o_ref, m_sc, acc_sc, *, tq, tk, lam_init):
    i = pl.program_id(1)
    lane = lax.broadcasted_iota(jnp.int32, (tq, LANES), 1)
    q_maps = []
    for h in range(DIFF_HEADS):
        q = q_ref[:, h * LANES:(h + 1) * LANES]
        zero = jnp.zeros_like(q)
        q_maps += [jnp.where(lane < DIFF_QK_DIM, q, zero), jnp.where(lane >= DIFF_QK_DIM, q, zero)]
    n_chains = len(q_maps)
    m_sc[...] = jnp.full_like(m_sc, NEG_INF)
    acc_sc[...] = jnp.zeros_like(acc_sc)
    q_pos = i * tq + lax.broadcasted_iota(jnp.int32, (tq, tk), 0)
    k_off = lax.broadcasted_iota(jnp.int32, (tq, tk), 1)
    ones = jnp.ones((tk, LANES), BF16)

    head_cols = lambda c: slice((c // 2) * LANES, (c // 2 + 1) * LANES)
    chains = list(range(n_chains))

    def blocks(js, masked):
        kv = []
        for j in js:
            start = pl.multiple_of(j * tk, tk)
            kv.append((k_ref[pl.ds(start, tk), :], v_ref[pl.ds(start, tk), :]))

        def logits(b):
            return [_nt_dot(q_maps[c], kv[b][0][:, head_cols(c)]) for c in chains]

        def softmax_step(b, scores):
            if masked:
                visible = js[b] * tk + k_off <= q_pos
                scores = [jnp.where(visible, s, NEG_INF) for s in scores]
            m_prev = [m_sc[c] for c in chains]
            m_new = [jnp.maximum(mp, jnp.max(s, axis=1, keepdims=True)) for mp, s in zip(m_prev, scores)]
            probs = [jnp.exp2(s - jnp.concatenate([mn] * (tk // LANES), axis=1))
                     for s, mn in zip(scores, m_new)]
            for c, mn in zip(chains, m_new):
                m_sc[c] = mn
            return m_prev, m_new, probs

        def accumulate(b, m_prev, m_new, probs):
            for c, mp, mn, p in zip(chains, m_prev, m_new, probs):
                alpha = jnp.exp2(mp - mn)
                v_ext = jnp.concatenate([kv[b][1][:, head_cols(c)], ones], axis=1)
                acc_sc[c] = (jnp.concatenate([alpha, alpha], axis=1) * acc_sc[c]
                             + jnp.dot(p.astype(BF16), v_ext, preferred_element_type=F32))

        s_next = logits(0)
        for b in range(len(js)):
            s_cur = s_next
            stats = softmax_step(b, s_cur)
            if b + 1 < len(js):
                s_next = logits(b + 1)
            accumulate(b, *stats)

    n_full = (i * tq) // tk

    @pl.when(n_full % 2 == 1)
    def _():
        blocks([0], False)

    def body(pp, carry):
        first = n_full % 2 + 2 * pp
        blocks([first, first + 1], False)
        return carry

    lax.fori_loop(0, n_full // 2, body, 0)
    blocks([n_full], True)

    lp = lp_ref[...]
    lam = (jnp.exp(jnp.sum(lp[0:1] * lp[1:2], axis=1, keepdims=True))
           - jnp.exp(jnp.sum(lp[2:3] * lp[3:4], axis=1, keepdims=True)) + lam_init)
    for h in range(DIFF_HEADS):
        a1, a2 = acc_sc[2 * h], acc_sc[2 * h + 1]
        o = a1[:, :LANES] / a1[:, LANES:] - lam * (a2[:, :LANES] / a2[:, LANES:])
        ms = jnp.mean(o * o, axis=-1, keepdims=True)
        y = (o * lax.rsqrt(ms + NORM_EPS) * g_ref[...]) * (1.0 - lam_init)
        o_ref[:, h * LANES:(h + 1) * LANES] = y.astype(BF16)


def _diff_attention(proj, lam_params, subln_g, lam_init, B, S):
    T = proj.shape[0]
    tq = min(256, S)
    tk = min(512, S)
    nq = S // tq
    n_chains = 2 * DIFF_HEADS
    return pl.pallas_call(
        functools.partial(_diff_kernel, tq=tq, tk=tk, lam_init=lam_init),
        grid=(B, nq),
        in_specs=[
            pl.BlockSpec((4, DIFF_QK_DIM), lambda b, i: (0, 0)),
            pl.BlockSpec((1, DIFF_V_DIM), lambda b, i: (0, 0)),
            pl.BlockSpec((tq, DIFF_QK_WIDTH), lambda b, i: (b * nq + i, COL_D_Q * LANES // DIFF_QK_WIDTH)),
            pl.BlockSpec((S, DIFF_QK_WIDTH), lambda b, i: (b, COL_D_K * LANES // DIFF_QK_WIDTH)),
            pl.BlockSpec((S, DIFF_V_WIDTH), lambda b, i: (b, COL_D_V * LANES // DIFF_V_WIDTH)),
        ],
        out_specs=pl.BlockSpec((tq, DIFF_V_WIDTH), lambda b, i: (b * nq + i, 0)),
        out_shape=jax.ShapeDtypeStruct((T, DIFF_V_WIDTH), BF16),
        scratch_shapes=[pltpu.VMEM((n_chains, tq, LANES), F32),
                        pltpu.VMEM((n_chains, tq, 2 * LANES), F32)],
        compiler_params=_params("parallel", "arbitrary"),
        name="diff_attention",
    )(lam_params, subln_g.reshape(1, DIFF_V_DIM), proj, proj, proj)


def _merge_kernel(ysb_ref, ydf_ref, gs_ref, gd_ref, gb_ref, wbs_ref, wbd_ref, wo_ref,
                  x_ref, g1_ref, n2_ref, sc2_ref, sh2_ref, xo_ref, h2_ref):
    D = x_ref.shape[1]
    a = jnp.dot(ysb_ref[...], wbs_ref[...], preferred_element_type=F32)
    b = jnp.dot(ydf_ref[...], wbd_ref[...], preferred_element_type=F32)
    gb = gb_ref[...]
    g_sb = jax.nn.sigmoid(gs_ref[...].astype(F32) + gb[:, :D])
    g_df = jax.nn.sigmoid(gd_ref[...].astype(F32) + gb[:, D:])
    merged = g_sb * a + g_df * b
    y = jnp.dot(merged.astype(BF16), wo_ref[...], preferred_element_type=F32)
    xn = x_ref[...] + g1_ref[0] * y
    xo_ref[...] = xn
    h2_ref[...] = _norm_mod(xn, n2_ref[...], sc2_ref[0], sh2_ref[0]).astype(BF16)


def _merge_project(y_sb, y_df, proj, gate_b, wbs, wbd, wo, x2, gate1, norm2_g, scale2, shift2, B, S):
    T, D = x2.shape
    tm = min(512, S)
    tiles_per_b = S // tm
    mod_spec = pl.BlockSpec((1, 1, D), lambda i: (i // tiles_per_b, 0, 0))
    const = lambda shape: pl.BlockSpec(shape, lambda i: (0, 0))
    return pl.pallas_call(
        _merge_kernel,
        grid=(T // tm,),
        in_specs=[
            pl.BlockSpec((tm, SB_WIDTH), lambda i: (i, 0)),
            pl.BlockSpec((tm, DIFF_V_WIDTH), lambda i: (i, 0)),
            pl.BlockSpec((tm, D), lambda i: (i, COL_G)),
            pl.BlockSpec((tm, D), lambda i: (i, COL_G + 1)),
            const((1, 2 * D)),
            const((SB_WIDTH, D)), const((DIFF_V_WIDTH, D)), const((D, D)),
            pl.BlockSpec((tm, D), lambda i: (i, 0)),
            mod_spec,
            const((1, D)),
            mod_spec, mod_spec,
        ],
        out_specs=[pl.BlockSpec((tm, D), lambda i: (i, 0)), pl.BlockSpec((tm, D), lambda i: (i, 0))],
        out_shape=[jax.ShapeDtypeStruct((T, D), F32), jax.ShapeDtypeStruct((T, D), BF16)],
        compiler_params=_params("parallel"),
        name="merge_out_proj",
    )(y_sb, y_df, proj, proj, gate_b.reshape(1, 2 * D), wbs, wbd, wo, x2,
      gate1.reshape(B, 1, D), norm2_g.reshape(1, D), scale2.reshape(B, 1, D), shift2.reshape(B, 1, D))


def _sort16_network():
    def merge(lo, hi, r):
        step = r * 2
        if step < hi - lo:
            yield from merge(lo, hi, step)
            yield from merge(lo + r, hi, step)
            yield from [(i, i + r) for i in range(lo + r, hi - r, step)]
        else:
            yield (lo, lo + r)

    def sort(lo, hi):
        if hi - lo >= 1:
            mid = lo + (hi - lo) // 2
            yield from sort(lo, mid)
            yield from sort(mid + 1, hi)
            yield from merge(lo, hi, 1)

    return list(sort(0, 15))


def _top_values(scores, k):
    groups = scores.shape[0] // SUBLANES
    assert groups == 16 and k <= groups + 1
    v = [scores[SUBLANES * g:SUBLANES * (g + 1)] for g in range(groups)]
    for a, b in _sort16_network():
        v[a], v[b] = jnp.maximum(v[a], v[b]), jnp.minimum(v[a], v[b])
    v.append(jnp.full_like(v[0], NEG_INF))
    vals = []
    for r in range(k):
        m = jnp.max(v[0], axis=0, keepdims=True)
        vals.append(m)
        still_needed = k - 1 - r
        if still_needed:
            hit = v[0] == m
            for d in range(min(still_needed, groups)):
                v[d] = jnp.where(hit, v[d + 1], v[d])
    return vals


def _candidate_sums(v1, v1_all, v2, v2_all, row_id):
    assert PEER_TOPK == 16 and SUBLANES == 8
    lo2 = v2_all[0:8]
    hi1 = v1_all[8:16]
    shift = lambda x, n: pltpu.roll(x, n, 0)
    groups = [v1[0] + v2_all[0:8], v1[0] + v2_all[8:16], v1[0] + v2_all[16:24],
              v1[1] + lo2]
    groups.append(jnp.where(row_id < 5, v1[2] + lo2, v1[4] + shift(lo2, 5)))
    groups.append(jnp.where(row_id < 4, v1[3] + lo2,
                            jnp.where(row_id < 6, v1[5] + shift(lo2, 4),
                                      v1[6] + shift(lo2, 6))))
    tail = shift(hi1, 2) + v2[0]
    groups.append(jnp.where(row_id < 2, v1[7] + lo2, tail))
    groups.append(jnp.where(row_id < 2, tail,
                            jnp.where(row_id == 2, v1[16] + v2[0], NEG_INF)))
    return groups


def _peer_select_kernel(h_ref, wq_ref, sk_ref, ht_ref, th_ref, e1_ref, e2_ref, *, tm):
    hb = h_ref[...]
    ht_ref[0] = hb.T
    K = PEER_TOPK + 1
    row_id = lax.broadcasted_iota(jnp.int32, (SUBLANES, LANES), 0)
    pad_rows = jnp.full((3 * SUBLANES - K, LANES), NEG_INF, F32)

    def head(h, carry):
        q_t = _nt_dot(wq_ref[h], hb).astype(BF16)
        half = PEER_QUERY_DIM // 2
        sc = [jnp.dot(sk_ref[h, c], q_t[c * half:(c + 1) * half], preferred_element_type=F32)
              for c in range(2)]
        for lc in range(tm // LANES):
            ls = slice(lc * LANES, (lc + 1) * LANES)
            s1, s2 = sc[0][:, ls], sc[1][:, ls]
            v1 = _top_values(s1, K)
            v2 = _top_values(s2, K)
            v2_all = jnp.concatenate(v2 + [pad_rows], axis=0)
            v1_all = jnp.concatenate(v1 + [pad_rows], axis=0)
            cands = _candidate_sums(v1, v1_all, v2, v2_all, row_id)
            best = cands[0][0:1]
            z_sum = jnp.zeros_like(best)
            last = best
            for r in range(K):
                stacked = cands[0]
                for ca in cands[1:]:
                    stacked = jnp.maximum(stacked, ca)
                m = jnp.max(stacked, axis=0, keepdims=True)
                if r < PEER_TOPK:
                    z_sum = z_sum + jnp.exp(m - best)
                    last = m
                    cands = [jnp.where(ca == m, NEG_INF, ca) for ca in cands]
            tau = 0.5 * (last + m)
            half_inv_z = 0.5 / z_sum
            chunked = (PEER_N_KEYS // PEER_CHUNK_KEYS, PEER_CHUNK_KEYS, LANES)
            th_ref[:, 0, h, :, ls] = jnp.exp(tau - s1 - v2[0]).reshape(chunked)
            e1_ref[:, 0, h, :, ls] = (jnp.exp(s1 - v1[0]) * half_inv_z).reshape(chunked)
            e2_ref[0, h, :, ls] = jnp.exp(s2 - v2[0])
        return carry

    lax.fori_loop(0, PEER_HEADS, head, 0)


def _peer_select(h2, wq_t, subkeys_bf):
    T, D = h2.shape
    tm = PEER_TOKEN_TILE
    n_tiles = T // tm
    tok_spec = pl.BlockSpec((1, PEER_HEADS, PEER_N_KEYS, tm), lambda i: (i, 0, 0, 0))
    tok_shape = jax.ShapeDtypeStruct((n_tiles, PEER_HEADS, PEER_N_KEYS, tm), F32)
    n_chunks = PEER_N_KEYS // PEER_CHUNK_KEYS
    row_spec = pl.BlockSpec((n_chunks, 1, PEER_HEADS, PEER_CHUNK_KEYS, tm), lambda i: (0, i, 0, 0, 0))
    row_shape = jax.ShapeDtypeStruct((n_chunks, n_tiles, PEER_HEADS, PEER_CHUNK_KEYS, tm), F32)
    return pl.pallas_call(
        functools.partial(_peer_select_kernel, tm=tm),
        grid=(n_tiles,),
        in_specs=[
            pl.BlockSpec((tm, D), lambda i: (i, 0)),
            pl.BlockSpec((PEER_HEADS, PEER_QUERY_DIM, D), lambda i: (0, 0, 0)),
            pl.BlockSpec((PEER_HEADS, 2, PEER_N_KEYS, PEER_QUERY_DIM // 2), lambda i: (0, 0, 0, 0)),
        ],
        out_specs=[
            pl.BlockSpec((1, D, tm), lambda i: (i, 0, 0)),
            row_spec, row_spec, tok_spec,
        ],
        out_shape=[
            jax.ShapeDtypeStruct((n_tiles, D, tm), BF16),
            row_shape, row_shape, tok_shape,
        ],
        compiler_params=_params("parallel"),
        name="peer_select",
    )(h2, wq_t, subkeys_bf)


PEER_ROW_BLOCK_KEYS = 2
PEER_PIECE_TOKENS = 2 * LANES


def _peer_mix_kernel(ht_ref, u_ref, vt_ref, th_ref, e1_ref, e2_ref, x_ref, g2_ref, fg_ref,
                     xo_ref, acc_sc, act_sc, e2_sc, *, tm, n_tiles, final_norm):
    j = pl.program_id(1)

    @pl.when(j == 0)
    def _():
        acc_sc[...] = jnp.zeros_like(acc_sc)
        for lc in range(tm // LANES):
            e2_sc[:, :, lc] = e2_ref[:, :, :, lc * LANES:(lc + 1) * LANES]

    rb = PEER_ROW_BLOCK_KEYS * PEER_N_KEYS
    tw = PEER_PIECE_TOKENS
    n_rb = PEER_CHUNK // rb
    n_cg = tm // tw
    pieces = [(t, r, c) for t in range(n_tiles) for c in range(n_cg) for r in range(n_rb)]

    def scores(piece):
        t, r, c = piece
        return jnp.dot(u_ref[r * rb:(r + 1) * rb, :], ht_ref[t, :, c * tw:(c + 1) * tw],
                       preferred_element_type=F32)

    def gates(piece, a_blk, k):
        t, r, c = piece
        ii = r * PEER_ROW_BLOCK_KEYS + k
        for lc in range(tw // LANES):
            chunk = c * (tw // LANES) + lc
            ls = slice(chunk * LANES, (chunk + 1) * LANES)
            gate = None
            for h in range(PEER_HEADS):
                theta = th_ref[0, t, h, ii:ii + 1, ls]
                e1 = e1_ref[0, t, h, ii:ii + 1, ls]
                e2 = e2_sc[t, h, chunk]
                term = jnp.where(e2 >= theta, e1 * e2, 0.0)
                gate = term if gate is None else gate + term
            a = a_blk[k * PEER_N_KEYS:(k + 1) * PEER_N_KEYS, lc * LANES:(lc + 1) * LANES]
            gelu2 = a * (1.0 + lax.erf(a * (1.0 / math.sqrt(2.0))))
            act_sc[c, ii * PEER_N_KEYS:(ii + 1) * PEER_N_KEYS, lc * LANES:(lc + 1) * LANES] = (
                gelu2 * gate).astype(BF16)

    def mix(t, c):
        acc_sc[t, c] += jnp.dot(vt_ref[0], act_sc[c], preferred_element_type=F32)

    a_next = scores(pieces[0])
    finished = None
    for n, piece in enumerate(pieces):
        t, r, c = piece
        a_cur = a_next
        if n + 1 < len(pieces):
            a_next = scores(pieces[n + 1])
        for k in range(PEER_ROW_BLOCK_KEYS):
            if k == PEER_ROW_BLOCK_KEYS // 2 and r == 0 and finished is not None:
                mix(*finished)
                finished = None
            gates(piece, a_cur, k)
        if r == n_rb - 1:
            finished = (t, c)
    mix(*finished)

    @pl.when(j == pl.num_programs(1) - 1)
    def _():
        for t in range(n_tiles):
            rows = slice(t * tm, (t + 1) * tm)
            mixed_t = jnp.concatenate([acc_sc[t, c] for c in range(tm // tw)], axis=1)
            xn = x_ref[rows, :] + g2_ref[0] * mixed_t.T
            if final_norm:
                ms = jnp.mean(xn * xn, axis=-1, keepdims=True)
                xn = xn * lax.rsqrt(ms + NORM_EPS) * fg_ref[...]
            xo_ref[rows, :] = xn


def _peer_mix(h_t, u_bf, v_t, theta, e1, e2, x2, gate2, final_g, final_norm, B, S):
    T, D = x2.shape
    tm = PEER_TOKEN_TILE
    n_tiles = min(PEER_TILES_PER_STEP, S // tm)
    span = n_tiles * tm
    steps_per_b = S // span
    tok_spec = pl.BlockSpec((n_tiles, PEER_HEADS, PEER_N_KEYS, tm), lambda i, j: (i, 0, 0, 0),
                            pipeline_mode=pl.Buffered(1))
    row_spec = pl.BlockSpec((1, n_tiles, PEER_HEADS, PEER_CHUNK_KEYS, tm), lambda i, j: (j, i, 0, 0, 0))
    return pl.pallas_call(
        functools.partial(_peer_mix_kernel, tm=tm, n_tiles=n_tiles, final_norm=final_norm),
        grid=(T // span, PEER_N_EXPERTS // PEER_CHUNK),
        in_specs=[
            pl.BlockSpec((n_tiles, D, tm), lambda i, j: (i, 0, 0)),
            pl.BlockSpec((PEER_CHUNK, D), lambda i, j: (j, 0)),
            pl.BlockSpec((1, D, PEER_CHUNK), lambda i, j: (j, 0, 0)),
            row_spec, row_spec, tok_spec,
            pl.BlockSpec((span, D), lambda i, j: (i, 0)),
            pl.BlockSpec((1, 1, D), lambda i, j: (i // steps_per_b, 0, 0)),
            pl.BlockSpec((1, D), lambda i, j: (0, 0)),
        ],
        out_specs=pl.BlockSpec((span, D), lambda i, j: (i, 0)),
        out_shape=jax.ShapeDtypeStruct((T, D), F32),
        scratch_shapes=[pltpu.VMEM((n_tiles, tm // PEER_PIECE_TOKENS, D, PEER_PIECE_TOKENS), F32),
                        pltpu.VMEM((tm // PEER_PIECE_TOKENS, PEER_CHUNK, PEER_PIECE_TOKENS), BF16),
                        pltpu.VMEM((n_tiles, PEER_HEADS, tm // LANES, PEER_N_KEYS, LANES), F32)],
        compiler_params=_params("parallel", "arbitrary"),
        name="peer_mix",
    )(h_t, u_bf, v_t, theta, e1, e2, x2, gate2.reshape(B, 1, D), final_g.reshape(1, D))


def kernel(x, c, norm1_g, norm2_g, ada_w, ada_b, w_in, gate_b, diff_lambda, diff_subln_g,
           w_branch_sb, w_branch_diff, w_out, peer_wq, peer_subkeys, peer_u, peer_v, final_g):
    B, S, D = x.shape
    assert D == D_MODEL and S % LANES == 0
    x2 = x.reshape(B * S, D)
    mod = _modulation(c, ada_w, ada_b)
    rope = _rope_lane_tables(S)
    for l in range(DEPTH):
        lam_init = 0.8 - 0.6 * math.exp(-0.3 * l)
        shift1, scale1, gate1, shift2, scale2, gate2 = (mod[l, :, n] for n in range(6))
        proj = _input_projection(x2, norm1_g[l], scale1, shift1, w_in[l].astype(BF16), rope, B, S)
        y_sb = _sb_attention(proj, B, S)
        y_df = _diff_attention(proj, diff_lambda[l], diff_subln_g[l], lam_init, B, S)
        x2, h2 = _merge_project(y_sb, y_df, proj, gate_b[l], w_branch_sb[l].astype(BF16),
                                w_branch_diff[l].astype(BF16), w_out[l].astype(BF16), x2, gate1,
                                norm2_g[l], scale2, shift2, B, S)
        wq_t = peer_wq[l].T.astype(BF16).reshape(PEER_HEADS, PEER_QUERY_DIM, D)
        h_t, theta, e1, e2 = _peer_select(h2, wq_t, peer_subkeys[l].astype(BF16))
        v_t = jnp.swapaxes(peer_v[l].astype(BF16).reshape(-1, PEER_CHUNK, D), 1, 2)
        x2 = _peer_mix(h_t, peer_u[l].astype(BF16), v_t,
                       theta, e1, e2, x2, gate2, final_g, l == DEPTH - 1, B, S)
    return x2.reshape(B, S, D)
```

```python
import functools
import math

import jax
import jax.numpy as jnp
from jax import lax
from jax.experimental import pallas as pl
from jax.experimental.pallas import tpu as pltpu

F32 = jnp.float32
BF16 = jnp.bfloat16

D_MODEL = 1024
DEPTH = 2
SB_HEADS = 8
SB_HEAD_DIM = 64
SB_WIDTH = SB_HEADS * SB_HEAD_DIM
DIFF_HEADS = 4
DIFF_QK_DIM = 64
DIFF_V_DIM = 2 * DIFF_QK_DIM
DIFF_QK_WIDTH = DIFF_HEADS * 2 * DIFF_QK_DIM
DIFF_V_WIDTH = DIFF_HEADS * DIFF_V_DIM
ROPE_THETA = 500000.0
ROT_DIM = DIFF_QK_DIM // 4
IN_COLS = 3 * SB_WIDTH + 2 * DIFF_QK_WIDTH + DIFF_V_WIDTH + 2 * D_MODEL
PEER_HEADS = 8
PEER_N_KEYS = 128
PEER_N_EXPERTS = PEER_N_KEYS * PEER_N_KEYS
PEER_TOPK = 16
PEER_QUERY_DIM = 256
PEER_CHUNK_KEYS = 16
PEER_CHUNK = PEER_CHUNK_KEYS * PEER_N_KEYS
PEER_TOKEN_TILE = 512
PEER_TILES_PER_STEP = 2
NORM_EPS = 1e-6

LANES = 128
SUBLANES = 8
VMEM_LIMIT = 60 * 1024 * 1024
NEG_INF = float("-inf")
LOG2_E = 1.4426950408889634

IN_TILE_N = 512
COL_SB_Q, COL_SB_K, COL_SB_V = 0, SB_WIDTH // LANES, 2 * SB_WIDTH // LANES
COL_D_Q = 3 * SB_WIDTH // LANES
COL_D_K = COL_D_Q + DIFF_QK_WIDTH // LANES
COL_D_V = COL_D_K + DIFF_QK_WIDTH // LANES
COL_G = (3 * SB_WIDTH + 2 * DIFF_QK_WIDTH + DIFF_V_WIDTH) // D_MODEL


def _params(*sem):
    return pltpu.CompilerParams(dimension_semantics=sem, vmem_limit_bytes=VMEM_LIMIT)


def _nt_dot(a, b):
    return lax.dot_general(a, b, (((1,), (1,)), ((), ())), preferred_element_type=F32)


def _mod_kernel(c_ref, w_ref, b_ref, o_ref):
    c = c_ref[...]
    c_act = c * jax.nn.sigmoid(c)
    o_ref[0] = jnp.dot(c_act, w_ref[0], preferred_element_type=F32,
                       precision=lax.Precision.HIGHEST) + b_ref[0]


def _modulation(c, ada_w, ada_b):
    B, D = c.shape
    rows = SUBLANES
    c_pad = jnp.pad(c, ((0, rows - B), (0, 0)))
    out = pl.pallas_call(
        _mod_kernel,
        grid=(DEPTH, 6),
        in_specs=[
            pl.BlockSpec((rows, D), lambda l, n: (0, 0)),
            pl.BlockSpec((1, D, D), lambda l, n: (l, 0, n)),
            pl.BlockSpec((1, 1, D), lambda l, n: (l, 0, n)),
        ],
        out_specs=pl.BlockSpec((1, rows, D), lambda l, n: (l, 0, n)),
        out_shape=jax.ShapeDtypeStruct((DEPTH, rows, 6 * D), F32),
        compiler_params=_params("arbitrary", "arbitrary"),
        name="adaln_mod",
    )(c_pad, ada_w, ada_b.reshape(DEPTH, 1, 6 * D))
    return out[:, :B].reshape(DEPTH, B, 6, D)


def _norm_mod(x, g, scale, shift):
    ms = jnp.mean(x * x, axis=-1, keepdims=True)
    return (x * lax.rsqrt(ms + NORM_EPS) * g) * (1.0 + scale) + shift


def _inproj_kernel(x_ref, g_ref, sc_ref, sh_ref, w_ref, ra_ref, rb_ref, rc_ref, o_ref):
    h = _norm_mod(x_ref[...], g_ref[...], sc_ref[0], sh_ref[0]).astype(BF16)
    half = ROT_DIM // 2
    for j in range(IN_COLS // IN_TILE_N):
        cols = slice(j * IN_TILE_N, (j + 1) * IN_TILE_N)
        r = jnp.dot(h, w_ref[:, cols], preferred_element_type=F32)
        first_lane_block = j * IN_TILE_N // LANES
        if first_lane_block in (COL_SB_Q, COL_D_Q):
            r = r * (LOG2_E / math.sqrt(SB_HEAD_DIM))
        if first_lane_block in (COL_D_Q, COL_D_K):
            ra, rb, rc = ra_ref[...], rb_ref[...], rc_ref[...]
            for blk in range(IN_TILE_N // LANES):
                t = r[:, blk * LANES:(blk + 1) * LANES]
                rot = t * ra + pltpu.roll(t, LANES - half, 1) * rb + pltpu.roll(t, half, 1) * rc
                o_ref[:, j * IN_TILE_N + blk * LANES:j * IN_TILE_N + (blk + 1) * LANES] = rot.astype(BF16)
        else:
            o_ref[:, cols] = r.astype(BF16)


def _rope_lane_tables(seq_len):
    pos = jnp.arange(seq_len, dtype=F32)
    inv_freq = 1.0 / (ROPE_THETA ** (jnp.arange(0, ROT_DIM, 2, dtype=F32) / ROT_DIM))
    ang = pos[:, None] * inv_freq[None, :]
    cos, sin = jnp.cos(ang), jnp.sin(ang)
    half = ROT_DIM // 2
    lane = jnp.arange(LANES) % DIFF_QK_DIM
    f = lane % half
    first, second = lane < half, (lane >= half) & (lane < ROT_DIM)
    ra = jnp.where((first | second)[None, :], cos[:, f], 1.0)
    rb = jnp.where(first[None, :], -sin[:, f], 0.0)
    rc = jnp.where(second[None, :], sin[:, f], 0.0)
    return ra, rb, rc


def _input_projection(x2, norm_g, scale, shift, w_in_bf, rope, B, S):
    T, D = x2.shape
    tm = min(512, S)
    tiles_per_b = S // tm
    ra, rb, rc = rope
    rope_spec = pl.BlockSpec((tm, LANES), lambda i: (i % tiles_per_b, 0))
    mod_spec = pl.BlockSpec((1, 1, D), lambda i: (i // tiles_per_b, 0, 0))
    assert COL_D_K - COL_D_Q == IN_TILE_N // LANES and COL_D_V - COL_D_K == IN_TILE_N // LANES
    return pl.pallas_call(
        _inproj_kernel,
        grid=(T // tm,),
        in_specs=[
            pl.BlockSpec((tm, D), lambda i: (i, 0)),
            pl.BlockSpec((1, D), lambda i: (0, 0)),
            mod_spec, mod_spec,
            pl.BlockSpec((D, IN_COLS), lambda i: (0, 0)),
            rope_spec, rope_spec, rope_spec,
        ],
        out_specs=pl.BlockSpec((tm, IN_COLS), lambda i: (i, 0)),
        out_shape=jax.ShapeDtypeStruct((T, IN_COLS), BF16),
        compiler_params=_params("parallel"),
        name="in_proj",
    )(x2, norm_g.reshape(1, D), scale.reshape(B, 1, D), shift.reshape(B, 1, D), w_in_bf, ra, rb, rc)


def _sb_kernel(q_ref, k_ref, v_ref, o_ref, acc_sc, cs_sc, *, tb, n_pairs):
    i = pl.program_id(2)
    n_heads = 2 * n_pairs
    lane = lax.broadcasted_iota(jnp.int32, (tb, LANES), 1)
    q_heads = []
    for p in range(n_pairs):
        q = q_ref[:, p * LANES:(p + 1) * LANES]
        zero = jnp.zeros_like(q)
        q_heads += [jnp.where(lane < SB_HEAD_DIM, q, zero), jnp.where(lane >= SB_HEAD_DIM, q, zero)]
    row = lax.broadcasted_iota(jnp.int32, (tb, tb), 0)
    col = lax.broadcasted_iota(jnp.int32, (tb, tb), 1)
    strictly_before = col < row
    suffix_ones = jnp.where(row >= col, 1.0, 0.0).astype(BF16)

    acc_sc[...] = jnp.zeros_like(acc_sc)
    cs_sc[...] = jnp.zeros_like(cs_sc)

    groups = [list(range(g, g + SB_GROUP_HEADS)) for g in range(0, n_heads, SB_GROUP_HEADS)]

    def blocks(js, masked):
        kv = []
        for j in js:
            start = pl.multiple_of(j * tb, tb)
            kv.append((k_ref[pl.ds(start, tb), :], v_ref[pl.ds(start, tb), :]))
        items = [(b, heads) for b in range(len(js)) for heads in groups]

        def logits(item):
            b, heads = item
            return [_nt_dot(q_heads[h], kv[b][0][:, (h // 2) * LANES:(h // 2 + 1) * LANES]) for h in heads]

        def suffix_sums(ys):
            costs = []
            for y in ys:
                cost = jnp.where(y > 30.0, y, jnp.log2(1.0 + jnp.exp2(y)))
                if masked:
                    cost = jnp.where(strictly_before, cost, 0.0)
                costs.append(cost)
            return [jnp.dot(cost.astype(BF16), suffix_ones, preferred_element_type=F32) for cost in costs]

        def accumulate(item, ys, sums):
            b, heads = item
            for h, y, s in zip(heads, ys, sums):
                later = cs_sc[h]
                w = jnp.exp2(y - s - later)
                if masked:
                    w = jnp.where(strictly_before, w, 0.0)
                acc_sc[h] += jnp.dot(w.astype(BF16), kv[b][1][:, (h // 2) * LANES:(h // 2 + 1) * LANES],
                                     preferred_element_type=F32)
                cs_sc[h] = later + s[:, 0:1]

        ys_next = logits(items[0])
        for n, item in enumerate(items):
            ys_cur = ys_next
            sums = suffix_sums(ys_cur)
            if n + 1 < len(items):
                ys_next = logits(items[n + 1])
            accumulate(item, ys_cur, sums)

    blocks([i], True)

    @pl.when(i % 2 == 1)
    def _():
        blocks([i - 1], False)

    def body(pp, carry):
        first = i - (i % 2) - 1 - 2 * pp
        blocks([first, first - 1], False)
        return carry

    lax.fori_loop(0, i // 2, body, 0)
    for p in range(n_pairs):
        o_ref[:, p * LANES:(p + 1) * LANES] = jnp.where(
            lane < SB_HEAD_DIM, acc_sc[2 * p], acc_sc[2 * p + 1]).astype(BF16)


SB_PAIRS_PER_STEP = 4
SB_GROUP_HEADS = 4


def _sb_attention(proj, B, S):
    T = proj.shape[0]
    tb = min(256, S)
    nq = S // tb
    n_pairs = SB_PAIRS_PER_STEP
    width = n_pairs * LANES
    groups = SB_WIDTH // width
    return pl.pallas_call(
        functools.partial(_sb_kernel, tb=tb, n_pairs=n_pairs),
        grid=(B, groups, nq),
        in_specs=[
            pl.BlockSpec((tb, width), lambda b, p, i: (b * nq + i, COL_SB_Q // n_pairs + p)),
            pl.BlockSpec((S, width), lambda b, p, i: (b, COL_SB_K // n_pairs + p)),
            pl.BlockSpec((S, width), lambda b, p, i: (b, COL_SB_V // n_pairs + p)),
        ],
        out_specs=pl.BlockSpec((tb, width), lambda b, p, i: (b * nq + i, p)),
        out_shape=jax.ShapeDtypeStruct((T, SB_WIDTH), BF16),
        scratch_shapes=[pltpu.VMEM((2 * n_pairs, tb, LANES), F32),
                        pltpu.VMEM((2 * n_pairs, tb, 1), F32)],
        compiler_params=_params("parallel", "parallel", "arbitrary"),
        name="sb_attention",
    )(proj, proj, proj)


def _diff_kernel(lp_ref, g_ref, q_ref, k_ref, v_ref, o_ref, m_sc, acc_sc, *, tq, tk, lam_init):
    i = pl.program_id(1)
    lane = lax.broadcasted_iota(jnp.int32, (tq, LANES), 1)
    q_maps = []
    for h in range(DIFF_HEADS):
        q = q_ref[:, h * LANES:(h + 1) * LANES]
        zero = jnp.zeros_like(q)
        q_maps += [jnp.where(lane < DIFF_QK_DIM, q, zero), jnp.where(lane >= DIFF_QK_DIM, q, zero)]
    n_chains = len(q_maps)
    m_sc[...] = jnp.full_like(m_sc, NEG_INF)
    acc_sc[...] = jnp.zeros_like(acc_sc)
    q_pos = i * tq + lax.broadcasted_iota(jnp.int32, (tq, tk), 0)
    k_off = lax.broadcasted_iota(jnp.int32, (tq, tk), 1)
    ones = jnp.ones((tk, LANES), BF16)

    head_cols = lambda c: slice((c // 2) * LANES, (c // 2 + 1) * LANES)
    chains = list(range(n_chains))

    def blocks(js, masked):
        kv = []
        for j in js:
            start = pl.multiple_of(j * tk, tk)
            kv.append((k_ref[pl.ds(start, tk), :], v_ref[pl.ds(start, tk), :]))

        def logits(b):
            return [_nt_dot(q_maps[c], kv[b][0][:, head_cols(c)]) for c in chains]

        def softmax_step(b, scores):
            if masked:
                visible = js[b] * tk + k_off <= q_pos
                scores = [jnp.where(visible, s, NEG_INF) for s in scores]
            m_prev = [m_sc[c] for c in chains]
            m_new = [jnp.maximum(mp, jnp.max(s, axis=1, keepdims=True)) for mp, s in zip(m_prev, scores)]
            probs = [jnp.exp2(s - jnp.concatenate([mn] * (tk // LANES), axis=1))
                     for s, mn in zip(scores, m_new)]
            for c, mn in zip(chains, m_new):
                m_sc[c] = mn
            return m_prev, m_new, probs

        def accumulate(b, m_prev, m_new, probs):
            for c, mp, mn, p in zip(chains, m_prev, m_new, probs):
                alpha = jnp.exp2(mp - mn)
                v_ext = jnp.concatenate([kv[b][1][:, head_cols(c)], ones], axis=1)
                acc_sc[c] = (jnp.concatenate([alpha, alpha], axis=1) * acc_sc[c]
                             + jnp.dot(p.astype(BF16), v_ext, preferred_element_type=F32))

        s_next = logits(0)
        for b in range(len(js)):
            s_cur = s_next
            stats = softmax_step(b, s_cur)
            if b + 1 < len(js):
                s_next = logits(b + 1)
            accumulate(b, *stats)

    n_full = (i * tq) // tk

    @pl.when(n_full % 2 == 1)
    def _():
        blocks([0], False)

    def body(pp, carry):
        first = n_full % 2 + 2 * pp
        blocks([first, first + 1], False)
        return carry

    lax.fori_loop(0, n_full // 2, body, 0)
    blocks([n_full], True)

    lp = lp_ref[...]
    lam = (jnp.exp(jnp.sum(lp[0:1] * lp[1:2], axis=1, keepdims=True))
           - jnp.exp(jnp.sum(lp[2:3] * lp[3:4], axis=1, keepdims=True)) + lam_init)
    for h in range(DIFF_HEADS):
        a1, a2 = acc_sc[2 * h], acc_sc[2 * h + 1]
        o = a1[:, :LANES] / a1[:, LANES:] - lam * (a2[:, :LANES] / a2[:, LANES:])
        ms = jnp.mean(o * o, axis=-1, keepdims=True)
        y = (o * lax.rsqrt(ms + NORM_EPS) * g_ref[...]) * (1.0 - lam_init)
        o_ref[:, h * LANES:(h + 1) * LANES] = y.astype(BF16)


def _diff_attention(proj, lam_params, subln_g, lam_init, B, S):
    T = proj.shape[0]
    tq = min(256, S)
    tk = min(512, S)
    nq = S // tq
    n_chains = 2 * DIFF_HEADS
    return pl.pallas_call(
        functools.partial(_diff_kernel, tq=tq, tk=tk, lam_init=lam_init),
        grid=(B, nq),
        in_specs=[
            pl.BlockSpec((4, DIFF_QK_DIM), lambda b, i: (0, 0)),
            pl.BlockSpec((1, DIFF_V_DIM), lambda b, i: (0, 0)),
            pl.BlockSpec((tq, DIFF_QK_WIDTH), lambda b, i: (b * nq + i, COL_D_Q * LANES // DIFF_QK_WIDTH)),
            pl.BlockSpec((S, DIFF_QK_WIDTH), lambda b, i: (b, COL_D_K * LANES // DIFF_QK_WIDTH)),
            pl.BlockSpec((S, DIFF_V_WIDTH), lambda b, i: (b, COL_D_V * LANES // DIFF_V_WIDTH)),
        ],
        out_specs=pl.BlockSpec((tq, DIFF_V_WIDTH), lambda b, i: (b * nq + i, 0)),
        out_shape=jax.ShapeDtypeStruct((T, DIFF_V_WIDTH), BF16),
        scratch_shapes=[pltpu.VMEM((n_chains, tq, LANES), F32),
                        pltpu.VMEM((n_chains, tq, 2 * LANES), F32)],
        compiler_params=_params("parallel", "arbitrary"),
        name="diff_attention",
    )(lam_params, subln_g.reshape(1, DIFF_V_DIM), proj, proj, proj)


def _merge_kernel(ysb_ref, ydf_ref, gs_ref, gd_ref, gb_ref, wbs_ref, wbd_ref, wo_ref,
                  x_ref, g1_ref, n2_ref, sc2_ref, sh2_ref, xo_ref, h2_ref):
    D = x_ref.shape[1]
    a = jnp.dot(ysb_ref[...], wbs_ref[...], preferred_element_type=F32)
    b = jnp.dot(ydf_ref[...], wbd_ref[...], preferred_element_type=F32)
    gb = gb_ref[...]
    g_sb = jax.nn.sigmoid(gs_ref[...].astype(F32) + gb[:, :D])
    g_df = jax.nn.sigmoid(gd_ref[...].astype(F32) + gb[:, D:])
    merged = g_sb * a + g_df * b
    y = jnp.dot(merged.astype(BF16), wo_ref[...], preferred_element_type=F32)
    xn = x_ref[...] + g1_ref[0] * y
    xo_ref[...] = xn
    h2_ref[...] = _norm_mod(xn, n2_ref[...], sc2_ref[0], sh2_ref[0]).astype(BF16)


def _merge_project(y_sb, y_df, proj, gate_b, wbs, wbd, wo, x2, gate1, norm2_g, scale2, shift2, B, S):
    T, D = x2.shape
    tm = min(512, S)
    tiles_per_b = S // tm
    mod_spec = pl.BlockSpec((1, 1, D), lambda i: (i // tiles_per_b, 0, 0))
    const = lambda shape: pl.BlockSpec(shape, lambda i: (0, 0))
    return pl.pallas_call(
        _merge_kernel,
        grid=(T // tm,),
        in_specs=[
            pl.BlockSpec((tm, SB_WIDTH), lambda i: (i, 0)),
            pl.BlockSpec((tm, DIFF_V_WIDTH), lambda i: (i, 0)),
            pl.BlockSpec((tm, D), lambda i: (i, COL_G)),
            pl.BlockSpec((tm, D), lambda i: (i, COL_G + 1)),
            const((1, 2 * D)),
            const((SB_WIDTH, D)), const((DIFF_V_WIDTH, D)), const((D, D)),
            pl.BlockSpec((tm, D), lambda i: (i, 0)),
            mod_spec,
            const((1, D)),
            mod_spec, mod_spec,
        ],
        out_specs=[pl.BlockSpec((tm, D), lambda i: (i, 0)), pl.BlockSpec((tm, D), lambda i: (i, 0))],
        out_shape=[jax.ShapeDtypeStruct((T, D), F32), jax.ShapeDtypeStruct((T, D), BF16)],
        compiler_params=_params("parallel"),
        name="merge_out_proj",
    )(y_sb, y_df, proj, proj, gate_b.reshape(1, 2 * D), wbs, wbd, wo, x2,
      gate1.reshape(B, 1, D), norm2_g.reshape(1, D), scale2.reshape(B, 1, D), shift2.reshape(B, 1, D))


def _sort16_network():
    def merge(lo, hi, r):
        step = r * 2
        if step < hi - lo:
            yield from merge(lo, hi, step)
            yield from merge(lo + r, hi, step)
            yield from [(i, i + r) for i in range(lo + r, hi - r, step)]
        else:
            yield (lo, lo + r)

    def sort(lo, hi):
        if hi - lo >= 1:
            mid = lo + (hi - lo) // 2
            yield from sort(lo, mid)
            yield from sort(mid + 1, hi)
            yield from merge(lo, hi, 1)

    return list(sort(0, 15))


def _top_values(scores, k):
    groups = scores.shape[0] // SUBLANES
    assert groups == 16 and k <= groups + 1
    v = [scores[SUBLANES * g:SUBLANES * (g + 1)] for g in range(groups)]
    for a, b in _sort16_network():
        v[a], v[b] = jnp.maximum(v[a], v[b]), jnp.minimum(v[a], v[b])
    v.append(jnp.full_like(v[0], NEG_INF))
    vals = []
    for r in range(k):
        m = jnp.max(v[0], axis=0, keepdims=True)
        vals.append(m)
        still_needed = k - 1 - r
        if still_needed:
            hit = v[0] == m
            for d in range(min(still_needed, groups)):
                v[d] = jnp.where(hit, v[d + 1], v[d])
    return vals


def _candidate_sums(v1, v1_all, v2, v2_all, row_id):
    assert PEER_TOPK == 16 and SUBLANES == 8
    lo2 = v2_all[0:8]
    hi1 = v1_all[8:16]
    shift = lambda x, n: pltpu.roll(x, n, 0)
    groups = [v1[0] + v2_all[0:8], v1[0] + v2_all[8:16], v1[0] + v2_all[16:24],
              v1[1] + lo2]
    groups.append(jnp.where(row_id < 5, v1[2] + lo2, v1[4] + shift(lo2, 5)))
    groups.append(jnp.where(row_id < 4, v1[3] + lo2,
                            jnp.where(row_id < 6, v1[5] + shift(lo2, 4),
                                      v1[6] + shift(lo2, 6))))
    tail = shift(hi1, 2) + v2[0]
    groups.append(jnp.where(row_id < 2, v1[7] + lo2, tail))
    groups.append(jnp.where(row_id < 2, tail,
                            jnp.where(row_id == 2, v1[16] + v2[0], NEG_INF)))
    return groups


PEER_SELECT_GROUP = 8


def _peer_select_kernel(h_ref, wq_ref, sk_ref, ht_ref, th_ref, e1_ref, e2_ref, *, tm):
    hb = h_ref[...]
    ht_ref[0] = hb.T
    K = PEER_TOPK + 1
    row_id = lax.broadcasted_iota(jnp.int32, (SUBLANES, LANES), 0)
    pad_rows = jnp.full((3 * SUBLANES - K, LANES), NEG_INF, F32)

    def half_scores(h):
        q_t = _nt_dot(wq_ref[h], hb).astype(BF16)
        half = PEER_QUERY_DIM // 2
        return [jnp.dot(sk_ref[h, c], q_t[c * half:(c + 1) * half], preferred_element_type=F32)
                for c in range(2)]

    def select(h, sc):
        for lc in range(tm // LANES):
            ls = slice(lc * LANES, (lc + 1) * LANES)
            s1, s2 = sc[0][:, ls], sc[1][:, ls]
            v1 = _top_values(s1, K)
            v2 = _top_values(s2, K)
            v2_all = jnp.concatenate(v2 + [pad_rows], axis=0)
            v1_all = jnp.concatenate(v1 + [pad_rows], axis=0)
            cands = _candidate_sums(v1, v1_all, v2, v2_all, row_id)
            best = cands[0][0:1]
            z_sum = jnp.zeros_like(best)
            last = best
            for r in range(K):
                stacked = cands[0]
                for ca in cands[1:]:
                    stacked = jnp.maximum(stacked, ca)
                m = jnp.max(stacked, axis=0, keepdims=True)
                if r < PEER_TOPK:
                    z_sum = z_sum + jnp.exp(m - best)
                    last = m
                    cands = [jnp.where(ca == m, NEG_INF, ca) for ca in cands]
            tau = 0.5 * (last + m)
            half_inv_z = 0.5 / z_sum
            chunked = (PEER_N_KEYS // PEER_CHUNK_KEYS, PEER_CHUNK_KEYS, LANES)
            th_ref[:, 0, h, :, ls] = jnp.exp(tau - s1 - v2[0]).reshape(chunked)
            e1_ref[:, 0, h, :, ls] = (jnp.exp(s1 - v1[0]) * half_inv_z).reshape(chunked)
            e2_ref[0, h, :, ls] = jnp.exp(s2 - v2[0])

    def head_group(g, carry):
        h0 = g * PEER_SELECT_GROUP
        sc_next = half_scores(h0)
        for k in range(PEER_SELECT_GROUP):
            sc_cur = sc_next
            if k + 1 < PEER_SELECT_GROUP:
                sc_next = half_scores(h0 + k + 1)
            select(h0 + k, sc_cur)
        return carry

    lax.fori_loop(0, PEER_HEADS // PEER_SELECT_GROUP, head_group, 0)


def _peer_select(h2, wq_t, subkeys_bf):
    T, D = h2.shape
    tm = PEER_TOKEN_TILE
    n_tiles = T // tm
    tok_spec = pl.BlockSpec((1, PEER_HEADS, PEER_N_KEYS, tm), lambda i: (i, 0, 0, 0))
    tok_shape = jax.ShapeDtypeStruct((n_tiles, PEER_HEADS, PEER_N_KEYS, tm), F32)
    n_chunks = PEER_N_KEYS // PEER_CHUNK_KEYS
    row_spec = pl.BlockSpec((n_chunks, 1, PEER_HEADS, PEER_CHUNK_KEYS, tm), lambda i: (0, i, 0, 0, 0))
    row_shape = jax.ShapeDtypeStruct((n_chunks, n_tiles, PEER_HEADS, PEER_CHUNK_KEYS, tm), F32)
    return pl.pallas_call(
        functools.partial(_peer_select_kernel, tm=tm),
        grid=(n_tiles,),
        in_specs=[
            pl.BlockSpec((tm, D), lambda i: (i, 0)),
            pl.BlockSpec((PEER_HEADS, PEER_QUERY_DIM, D), lambda i: (0, 0, 0)),
            pl.BlockSpec((PEER_HEADS, 2, PEER_N_KEYS, PEER_QUERY_DIM // 2), lambda i: (0, 0, 0, 0)),
        ],
        out_specs=[
            pl.BlockSpec((1, D, tm), lambda i: (i, 0, 0)),
            row_spec, row_spec, tok_spec,
        ],
        out_shape=[
            jax.ShapeDtypeStruct((n_tiles, D, tm), BF16),
            row_shape, row_shape, tok_shape,
        ],
        compiler_params=_params("parallel"),
        name="peer_select",
    )(h2, wq_t, subkeys_bf)


PEER_ROW_BLOCK_KEYS = 2
PEER_PIECE_TOKENS = 2 * LANES


def _peer_mix_kernel(ht_ref, u_ref, vt_ref, th_ref, e1_ref, e2_ref, x_ref, g2_ref, fg_ref,
                     xo_ref, acc_sc, act_sc, e2_sc, *, tm, n_tiles, final_norm):
    j = pl.program_id(1)

    @pl.when(j == 0)
    def _():
        acc_sc[...] = jnp.zeros_like(acc_sc)
        for lc in range(tm // LANES):
            e2_sc[:, :, lc] = e2_ref[:, :, :, lc * LANES:(lc + 1) * LANES]

    rb = PEER_ROW_BLOCK_KEYS * PEER_N_KEYS
    tw = PEER_PIECE_TOKENS
    n_rb = PEER_CHUNK // rb
    pieces = [(r, c) for c in range(tm // tw) for r in range(n_rb)]

    def tile(t, carry):
        def scores(p):
            r, c = pieces[p]
            return jnp.dot(u_ref[r * rb:(r + 1) * rb, :], ht_ref[t, :, c * tw:(c + 1) * tw],
                           preferred_element_type=F32)

        def gates(p, a_blk, k):
            r, c = pieces[p]
            ii = r * PEER_ROW_BLOCK_KEYS + k
            for lc in range(tw // LANES):
                chunk = c * (tw // LANES) + lc
                ls = slice(chunk * LANES, (chunk + 1) * LANES)
                gate = None
                for h in range(PEER_HEADS):
                    theta = th_ref[0, t, h, ii:ii + 1, ls]
                    e1 = e1_ref[0, t, h, ii:ii + 1, ls]
                    e2 = e2_sc[t, h, chunk]
                    term = jnp.where(e2 >= theta, e1 * e2, 0.0)
                    gate = term if gate is None else gate + term
                a = a_blk[k * PEER_N_KEYS:(k + 1) * PEER_N_KEYS, lc * LANES:(lc + 1) * LANES]
                gelu2 = a * (1.0 + lax.erf(a * (1.0 / math.sqrt(2.0))))
                act_sc[c, ii * PEER_N_KEYS:(ii + 1) * PEER_N_KEYS, lc * LANES:(lc + 1) * LANES] = (
                    gelu2 * gate).astype(BF16)

        def mix(c):
            acc_sc[t, c] += jnp.dot(vt_ref[0], act_sc[c], preferred_element_type=F32)

        a_next = scores(0)
        for p in range(len(pieces)):
            r, c = pieces[p]
            a_cur = a_next
            if p + 1 < len(pieces):
                a_next = scores(p + 1)
            for k in range(PEER_ROW_BLOCK_KEYS):
                if k == PEER_ROW_BLOCK_KEYS // 2 and r == 0 and c > 0:
                    mix(c - 1)
                gates(p, a_cur, k)
        mix(tm // tw - 1)
        return carry

    lax.fori_loop(0, n_tiles, tile, 0)

    @pl.when(j == pl.num_programs(1) - 1)
    def _():
        for t in range(n_tiles):
            rows = slice(t * tm, (t + 1) * tm)
            mixed_t = jnp.concatenate([acc_sc[t, c] for c in range(tm // tw)], axis=1)
            xn = x_ref[rows, :] + g2_ref[0] * mixed_t.T
            if final_norm:
                ms = jnp.mean(xn * xn, axis=-1, keepdims=True)
                xn = xn * lax.rsqrt(ms + NORM_EPS) * fg_ref[...]
            xo_ref[rows, :] = xn


def _peer_mix(h_t, u_bf, v_t, theta, e1, e2, x2, gate2, final_g, final_norm, B, S):
    T, D = x2.shape
    tm = PEER_TOKEN_TILE
    n_tiles = min(PEER_TILES_PER_STEP, S // tm)
    span = n_tiles * tm
    steps_per_b = S // span
    tok_spec = pl.BlockSpec((n_tiles, PEER_HEADS, PEER_N_KEYS, tm), lambda i, j: (i, 0, 0, 0),
                            pipeline_mode=pl.Buffered(1))
    row_spec = pl.BlockSpec((1, n_tiles, PEER_HEADS, PEER_CHUNK_KEYS, tm), lambda i, j: (j, i, 0, 0, 0))
    return pl.pallas_call(
        functools.partial(_peer_mix_kernel, tm=tm, n_tiles=n_tiles, final_norm=final_norm),
        grid=(T // span, PEER_N_EXPERTS // PEER_CHUNK),
        in_specs=[
            pl.BlockSpec((n_tiles, D, tm), lambda i, j: (i, 0, 0)),
            pl.BlockSpec((PEER_CHUNK, D), lambda i, j: (j, 0)),
            pl.BlockSpec((1, D, PEER_CHUNK), lambda i, j: (j, 0, 0)),
            row_spec, row_spec, tok_spec,
            pl.BlockSpec((span, D), lambda i, j: (i, 0)),
            pl.BlockSpec((1, 1, D), lambda i, j: (i // steps_per_b, 0, 0)),
            pl.BlockSpec((1, D), lambda i, j: (0, 0)),
        ],
        out_specs=pl.BlockSpec((span, D), lambda i, j: (i, 0)),
        out_shape=jax.ShapeDtypeStruct((T, D), F32),
        scratch_shapes=[pltpu.VMEM((n_tiles, tm // PEER_PIECE_TOKENS, D, PEER_PIECE_TOKENS), F32),
                        pltpu.VMEM((tm // PEER_PIECE_TOKENS, PEER_CHUNK, PEER_PIECE_TOKENS), BF16),
                        pltpu.VMEM((n_tiles, PEER_HEADS, tm // LANES, PEER_N_KEYS, LANES), F32)],
        compiler_params=_params("parallel", "arbitrary"),
        name="peer_mix",
    )(h_t, u_bf, v_t, theta, e1, e2, x2, gate2.reshape(B, 1, D), final_g.reshape(1, D))


def kernel(x, c, norm1_g, norm2_g, ada_w, ada_b, w_in, gate_b, diff_lambda, diff_subln_g,
           w_branch_sb, w_branch_diff, w_out, peer_wq, peer_subkeys, peer_u, peer_v, final_g):
    B, S, D = x.shape
    assert D == D_MODEL and S % LANES == 0
    x2 = x.reshape(B * S, D)
    mod = _modulation(c, ada_w, ada_b)
    rope = _rope_lane_tables(S)
    for l in range(DEPTH):
        lam_init = 0.8 - 0.6 * math.exp(-0.3 * l)
        shift1, scale1, gate1, shift2, scale2, gate2 = (mod[l, :, n] for n in range(6))
        proj = _input_projection(x2, norm1_g[l], scale1, shift1, w_in[l].astype(BF16), rope, B, S)
        y_sb = _sb_attention(proj, B, S)
        y_df = _diff_attention(proj, diff_lambda[l], diff_subln_g[l], lam_init, B, S)
        x2, h2 = _merge_project(y_sb, y_df, proj, gate_b[l], w_branch_sb[l].astype(BF16),
                                w_branch_diff[l].astype(BF16), w_out[l].astype(BF16), x2, gate1,
                                norm2_g[l], scale2, shift2, B, S)
        wq_t = peer_wq[l].T.astype(BF16).reshape(PEER_HEADS, PEER_QUERY_DIM, D)
        h_t, theta, e1, e2 = _peer_select(h2, wq_t, peer_subkeys[l].astype(BF16))
        v_t = jnp.swapaxes(peer_v[l].astype(BF16).reshape(-1, PEER_CHUNK, D), 1, 2)
        x2 = _peer_mix(h_t, peer_u[l].astype(BF16), v_t,
                       theta, e1, e2, x2, gate2, final_g, l == DEPTH - 1, B, S)
    return x2.reshape(B, S, D)
```

```python
import functools
import math

import jax
import jax.numpy as jnp
from jax import lax
from jax.experimental import pallas as pl
from jax.experimental.pallas import tpu as pltpu

F32 = jnp.float32
BF16 = jnp.bfloat16

D_MODEL = 1024
DEPTH = 2
SB_HEADS = 8
SB_HEAD_DIM = 64
SB_WIDTH = SB_HEADS * SB_HEAD_DIM
DIFF_HEADS = 4
DIFF_QK_DIM = 64
DIFF_V_DIM = 2 * DIFF_QK_DIM
DIFF_QK_WIDTH = DIFF_HEADS * 2 * DIFF_QK_DIM
DIFF_V_WIDTH = DIFF_HEADS * DIFF_V_DIM
ROPE_THETA = 500000.0
ROT_DIM = DIFF_QK_DIM // 4
IN_COLS = 3 * SB_WIDTH + 2 * DIFF_QK_WIDTH + DIFF_V_WIDTH + 2 * D_MODEL
PEER_HEADS = 8
PEER_N_KEYS = 128
PEER_N_EXPERTS = PEER_N_KEYS * PEER_N_KEYS
PEER_TOPK = 16
PEER_QUERY_DIM = 256
PEER_CHUNK_KEYS = 16
PEER_CHUNK = PEER_CHUNK_KEYS * PEER_N_KEYS
PEER_TOKEN_TILE = 512
PEER_TILES_PER_STEP = 2
NORM_EPS = 1e-6

LANES = 128
SUBLANES = 8
VMEM_LIMIT = 60 * 1024 * 1024
NEG_INF = float("-inf")
LOG2_E = 1.4426950408889634

IN_TILE_N = 512
COL_SB_Q, COL_SB_K, COL_SB_V = 0, SB_WIDTH // LANES, 2 * SB_WIDTH // LANES
COL_D_Q = 3 * SB_WIDTH // LANES
COL_D_K = COL_D_Q + DIFF_QK_WIDTH // LANES
COL_D_V = COL_D_K + DIFF_QK_WIDTH // LANES
COL_G = (3 * SB_WIDTH + 2 * DIFF_QK_WIDTH + DIFF_V_WIDTH) // D_MODEL


def _params(*sem):
    return pltpu.CompilerParams(dimension_semantics=sem, vmem_limit_bytes=VMEM_LIMIT)


def _nt_dot(a, b):
    return lax.dot_general(a, b, (((1,), (1,)), ((), ())), preferred_element_type=F32)


def _mod_kernel(c_ref, w_ref, b_ref, o_ref):
    c = c_ref[...]
    c_act = c * jax.nn.sigmoid(c)
    o_ref[0] = jnp.dot(c_act, w_ref[0], preferred_element_type=F32,
                       precision=lax.Precision.HIGHEST) + b_ref[0]


def _modulation(c, ada_w, ada_b):
    B, D = c.shape
    rows = SUBLANES
    c_pad = jnp.pad(c, ((0, rows - B), (0, 0)))
    out = pl.pallas_call(
        _mod_kernel,
        grid=(DEPTH, 6),
        in_specs=[
            pl.BlockSpec((rows, D), lambda l, n: (0, 0)),
            pl.BlockSpec((1, D, D), lambda l, n: (l, 0, n)),
            pl.BlockSpec((1, 1, D), lambda l, n: (l, 0, n)),
        ],
        out_specs=pl.BlockSpec((1, rows, D), lambda l, n: (l, 0, n)),
        out_shape=jax.ShapeDtypeStruct((DEPTH, rows, 6 * D), F32),
        compiler_params=_params("arbitrary", "arbitrary"),
        name="adaln_mod",
    )(c_pad, ada_w, ada_b.reshape(DEPTH, 1, 6 * D))
    return out[:, :B].reshape(DEPTH, B, 6, D)


def _norm_mod(x, g, scale, shift):
    ms = jnp.mean(x * x, axis=-1, keepdims=True)
    return (x * lax.rsqrt(ms + NORM_EPS) * g) * (1.0 + scale) + shift


def _inproj_kernel(x_ref, g_ref, sc_ref, sh_ref, w_ref, ra_ref, rb_ref, rc_ref, o_ref):
    h = _norm_mod(x_ref[...], g_ref[...], sc_ref[0], sh_ref[0]).astype(BF16)
    half = ROT_DIM // 2
    for j in range(IN_COLS // IN_TILE_N):
        cols = slice(j * IN_TILE_N, (j + 1) * IN_TILE_N)
        r = jnp.dot(h, w_ref[:, cols], preferred_element_type=F32)
        first_lane_block = j * IN_TILE_N // LANES
        if first_lane_block in (COL_SB_Q, COL_D_Q):
            r = r * (LOG2_E / math.sqrt(SB_HEAD_DIM))
        if first_lane_block in (COL_D_Q, COL_D_K):
            ra, rb, rc = ra_ref[...], rb_ref[...], rc_ref[...]
            for blk in range(IN_TILE_N // LANES):
                t = r[:, blk * LANES:(blk + 1) * LANES]
                rot = t * ra + pltpu.roll(t, LANES - half, 1) * rb + pltpu.roll(t, half, 1) * rc
                o_ref[:, j * IN_TILE_N + blk * LANES:j * IN_TILE_N + (blk + 1) * LANES] = rot.astype(BF16)
        else:
            o_ref[:, cols] = r.astype(BF16)


def _rope_lane_tables(seq_len):
    pos = jnp.arange(seq_len, dtype=F32)
    inv_freq = 1.0 / (ROPE_THETA ** (jnp.arange(0, ROT_DIM, 2, dtype=F32) / ROT_DIM))
    ang = pos[:, None] * inv_freq[None, :]
    cos, sin = jnp.cos(ang), jnp.sin(ang)
    half = ROT_DIM // 2
    lane = jnp.arange(LANES) % DIFF_QK_DIM
    f = lane % half
    first, second = lane < half, (lane >= half) & (lane < ROT_DIM)
    ra = jnp.where((first | second)[None, :], cos[:, f], 1.0)
    rb = jnp.where(first[None, :], -sin[:, f], 0.0)
    rc = jnp.where(second[None, :], sin[:, f], 0.0)
    return ra, rb, rc


def _input_projection(x2, norm_g, scale, shift, w_in_bf, rope, B, S):
    T, D = x2.shape
    tm = min(512, S)
    tiles_per_b = S // tm
    ra, rb, rc = rope
    rope_spec = pl.BlockSpec((tm, LANES), lambda i: (i % tiles_per_b, 0))
    mod_spec = pl.BlockSpec((1, 1, D), lambda i: (i // tiles_per_b, 0, 0))
    assert COL_D_K - COL_D_Q == IN_TILE_N // LANES and COL_D_V - COL_D_K == IN_TILE_N // LANES
    return pl.pallas_call(
        _inproj_kernel,
        grid=(T // tm,),
        in_specs=[
            pl.BlockSpec((tm, D), lambda i: (i, 0)),
            pl.BlockSpec((1, D), lambda i: (0, 0)),
            mod_spec, mod_spec,
            pl.BlockSpec((D, IN_COLS), lambda i: (0, 0)),
            rope_spec, rope_spec, rope_spec,
        ],
        out_specs=pl.BlockSpec((tm, IN_COLS), lambda i: (i, 0)),
        out_shape=jax.ShapeDtypeStruct((T, IN_COLS), BF16),
        compiler_params=_params("parallel"),
        name="in_proj",
    )(x2, norm_g.reshape(1, D), scale.reshape(B, 1, D), shift.reshape(B, 1, D), w_in_bf, ra, rb, rc)


def _sb_kernel(q_ref, k_ref, v_ref, o_ref, acc_sc, cs_sc, *, tb, n_pairs):
    i = pl.program_id(2)
    n_heads = 2 * n_pairs
    lane = lax.broadcasted_iota(jnp.int32, (tb, LANES), 1)
    q_heads = []
    for p in range(n_pairs):
        q = q_ref[:, p * LANES:(p + 1) * LANES]
        zero = jnp.zeros_like(q)
        q_heads += [jnp.where(lane < SB_HEAD_DIM, q, zero), jnp.where(lane >= SB_HEAD_DIM, q, zero)]
    row = lax.broadcasted_iota(jnp.int32, (tb, tb), 0)
    col = lax.broadcasted_iota(jnp.int32, (tb, tb), 1)
    strictly_before = col < row
    suffix_ones = jnp.where(row >= col, 1.0, 0.0).astype(BF16)

    acc_sc[...] = jnp.zeros_like(acc_sc)
    cs_sc[...] = jnp.zeros_like(cs_sc)

    groups = [list(range(g, g + SB_GROUP_HEADS)) for g in range(0, n_heads, SB_GROUP_HEADS)]

    def blocks(js, masked):
        kv = []
        for j in js:
            start = pl.multiple_of(j * tb, tb)
            kv.append((k_ref[pl.ds(start, tb), :], v_ref[pl.ds(start, tb), :]))
        items = [(b, heads) for b in range(len(js)) for heads in groups]

        def logits(item):
            b, heads = item
            return [_nt_dot(q_heads[h], kv[b][0][:, (h // 2) * LANES:(h // 2 + 1) * LANES]) for h in heads]

        def suffix_sums(ys):
            costs = []
            for y in ys:
                cost = jnp.where(y > 30.0, y, jnp.log2(1.0 + jnp.exp2(y)))
                if masked:
                    cost = jnp.where(strictly_before, cost, 0.0)
                costs.append(cost)
            return [jnp.dot(cost.astype(BF16), suffix_ones, preferred_element_type=F32) for cost in costs]

        def accumulate(item, ys, sums):
            b, heads = item
            for h, y, s in zip(heads, ys, sums):
                later = cs_sc[h]
                w = jnp.exp2(y - s - later)
                if masked:
                    w = jnp.where(strictly_before, w, 0.0)
                acc_sc[h] += jnp.dot(w.astype(BF16), kv[b][1][:, (h // 2) * LANES:(h // 2 + 1) * LANES],
                                     preferred_element_type=F32)
                cs_sc[h] = later + s[:, 0:1]

        ys_next = logits(items[0])
        for n, item in enumerate(items):
            ys_cur = ys_next
            sums = suffix_sums(ys_cur)
            if n + 1 < len(items):
                ys_next = logits(items[n + 1])
            accumulate(item, ys_cur, sums)

    blocks([i], True)

    @pl.when(i % 2 == 1)
    def _():
        blocks([i - 1], False)

    def body(pp, carry):
        first = i - (i % 2) - 1 - 2 * pp
        blocks([first, first - 1], False)
        return carry

    lax.fori_loop(0, i // 2, body, 0)
    for p in range(n_pairs):
        o_ref[:, p * LANES:(p + 1) * LANES] = jnp.where(
            lane < SB_HEAD_DIM, acc_sc[2 * p], acc_sc[2 * p + 1]).astype(BF16)


SB_PAIRS_PER_STEP = 4
SB_GROUP_HEADS = 4


def _sb_attention(proj, B, S):
    T = proj.shape[0]
    tb = min(256, S)
    nq = S // tb
    n_pairs = SB_PAIRS_PER_STEP
    width = n_pairs * LANES
    groups = SB_WIDTH // width
    return pl.pallas_call(
        functools.partial(_sb_kernel, tb=tb, n_pairs=n_pairs),
        grid=(B, groups, nq),
        in_specs=[
            pl.BlockSpec((tb, width), lambda b, p, i: (b * nq + i, COL_SB_Q // n_pairs + p)),
            pl.BlockSpec((S, width), lambda b, p, i: (b, COL_SB_K // n_pairs + p)),
            pl.BlockSpec((S, width), lambda b, p, i: (b, COL_SB_V // n_pairs + p)),
        ],
        out_specs=pl.BlockSpec((tb, width), lambda b, p, i: (b * nq + i, p)),
        out_shape=jax.ShapeDtypeStruct((T, SB_WIDTH), BF16),
        scratch_shapes=[pltpu.VMEM((2 * n_pairs, tb, LANES), F32),
                        pltpu.VMEM((2 * n_pairs, tb, 1), F32)],
        compiler_params=_params("parallel", "parallel", "arbitrary"),
        name="sb_attention",
    )(proj, proj, proj)


def _diff_kernel(lp_ref, g_ref, q_ref, k_ref, v_ref, o_ref, m_sc, acc_sc, *, tq, tk, lam_init):
    i = pl.program_id(1)
    lane = lax.broadcasted_iota(jnp.int32, (tq, LANES), 1)
    q_maps = []
    for h in range(DIFF_HEADS):
        q = q_ref[:, h * LANES:(h + 1) * LANES]
        zero = jnp.zeros_like(q)
        q_maps += [jnp.where(lane < DIFF_QK_DIM, q, zero), jnp.where(lane >= DIFF_QK_DIM, q, zero)]
    n_chains = len(q_maps)
    m_sc[...] = jnp.full_like(m_sc, NEG_INF)
    acc_sc[...] = jnp.zeros_like(acc_sc)
    q_pos = i * tq + lax.broadcasted_iota(jnp.int32, (tq, tk), 0)
    k_off = lax.broadcasted_iota(jnp.int32, (tq, tk), 1)
    ones = jnp.ones((tk, LANES), BF16)

    head_cols = lambda c: slice((c // 2) * LANES, (c // 2 + 1) * LANES)
    chains = list(range(n_chains))

    def blocks(js, masked):
        kv = []
        for j in js:
            start = pl.multiple_of(j * tk, tk)
            kv.append((k_ref[pl.ds(start, tk), :], v_ref[pl.ds(start, tk), :]))

        def logits(b):
            return [_nt_dot(q_maps[c], kv[b][0][:, head_cols(c)]) for c in chains]

        def softmax_step(b, scores):
            if masked:
                visible = js[b] * tk + k_off <= q_pos
                scores = [jnp.where(visible, s, NEG_INF) for s in scores]
            m_prev = [m_sc[c] for c in chains]
            m_new = [jnp.maximum(mp, jnp.max(s, axis=1, keepdims=True)) for mp, s in zip(m_prev, scores)]
            probs = [jnp.exp2(s - jnp.concatenate([mn] * (tk // LANES), axis=1))
                     for s, mn in zip(scores, m_new)]
            for c, mn in zip(chains, m_new):
                m_sc[c] = mn
            return m_prev, m_new, probs

        def accumulate(b, m_prev, m_new, probs):
            for c, mp, mn, p in zip(chains, m_prev, m_new, probs):
                alpha = jnp.exp2(mp - mn)
                v_ext = jnp.concatenate([kv[b][1][:, head_cols(c)], ones], axis=1)
                acc_sc[c] = (jnp.concatenate([alpha, alpha], axis=1) * acc_sc[c]
                             + jnp.dot(p.astype(BF16), v_ext, preferred_element_type=F32))

        s_next = logits(0)
        for b in range(len(js)):
            s_cur = s_next
            stats = softmax_step(b, s_cur)
            if b + 1 < len(js):
                s_next = logits(b + 1)
            accumulate(b, *stats)

    n_full = (i * tq) // tk

    @pl.when(n_full % 2 == 1)
    def _():
        blocks([0], False)

    def body(pp, carry):
        first = n_full % 2 + 2 * pp
        blocks([first, first + 1], False)
        return carry

    lax.fori_loop(0, n_full // 2, body, 0)
    blocks([n_full], True)

    lp = lp_ref[...]
    lam = (jnp.exp(jnp.sum(lp[0:1] * lp[1:2], axis=1, keepdims=True))
           - jnp.exp(jnp.sum(lp[2:3] * lp[3:4], axis=1, keepdims=True)) + lam_init)
    for h in range(DIFF_HEADS):
        a1, a2 = acc_sc[2 * h], acc_sc[2 * h + 1]
        o = a1[:, :LANES] / a1[:, LANES:] - lam * (a2[:, :LANES] / a2[:, LANES:])
        ms = jnp.mean(o * o, axis=-1, keepdims=True)
        y = (o * lax.rsqrt(ms + NORM_EPS) * g_ref[...]) * (1.0 - lam_init)
        o_ref[:, h * LANES:(h + 1) * LANES] = y.astype(BF16)


def _diff_attention(proj, lam_params, subln_g, lam_init, B, S):
    T = proj.shape[0]
    tq = min(256, S)
    tk = min(512, S)
    nq = S // tq
    n_chains = 2 * DIFF_HEADS
    return pl.pallas_call(
        functools.partial(_diff_kernel, tq=tq, tk=tk, lam_init=lam_init),
        grid=(B, nq),
        in_specs=[
            pl.BlockSpec((4, DIFF_QK_DIM), lambda b, i: (0, 0)),
            pl.BlockSpec((1, DIFF_V_DIM), lambda b, i: (0, 0)),
            pl.BlockSpec((tq, DIFF_QK_WIDTH), lambda b, i: (b * nq + i, COL_D_Q * LANES // DIFF_QK_WIDTH)),
            pl.BlockSpec((S, DIFF_QK_WIDTH), lambda b, i: (b, COL_D_K * LANES // DIFF_QK_WIDTH)),
            pl.BlockSpec((S, DIFF_V_WIDTH), lambda b, i: (b, COL_D_V * LANES // DIFF_V_WIDTH)),
        ],
        out_specs=pl.BlockSpec((tq, DIFF_V_WIDTH), lambda b, i: (b * nq + i, 0)),
        out_shape=jax.ShapeDtypeStruct((T, DIFF_V_WIDTH), BF16),
        scratch_shapes=[pltpu.VMEM((n_chains, tq, LANES), F32),
                        pltpu.VMEM((n_chains, tq, 2 * LANES), F32)],
        compiler_params=_params("parallel", "arbitrary"),
        name="diff_attention",
    )(lam_params, subln_g.reshape(1, DIFF_V_DIM), proj, proj, proj)


def _merge_kernel(ysb_ref, ydf_ref, gs_ref, gd_ref, gb_ref, wbs_ref, wbd_ref, wo_ref,
                  x_ref, g1_ref, n2_ref, sc2_ref, sh2_ref, xo_ref, h2_ref):
    D = x_ref.shape[1]
    a = jnp.dot(ysb_ref[...], wbs_ref[...], preferred_element_type=F32)
    b = jnp.dot(ydf_ref[...], wbd_ref[...], preferred_element_type=F32)
    gb = gb_ref[...]
    g_sb = jax.nn.sigmoid(gs_ref[...].astype(F32) + gb[:, :D])
    g_df = jax.nn.sigmoid(gd_ref[...].astype(F32) + gb[:, D:])
    merged = g_sb * a + g_df * b
    y = jnp.dot(merged.astype(BF16), wo_ref[...], preferred_element_type=F32)
    xn = x_ref[...] + g1_ref[0] * y
    xo_ref[...] = xn
    h2_ref[...] = _norm_mod(xn, n2_ref[...], sc2_ref[0], sh2_ref[0]).astype(BF16)


def _merge_project(y_sb, y_df, proj, gate_b, wbs, wbd, wo, x2, gate1, norm2_g, scale2, shift2, B, S):
    T, D = x2.shape
    tm = min(512, S)
    tiles_per_b = S // tm
    mod_spec = pl.BlockSpec((1, 1, D), lambda i: (i // tiles_per_b, 0, 0))
    const = lambda shape: pl.BlockSpec(shape, lambda i: (0, 0))
    return pl.pallas_call(
        _merge_kernel,
        grid=(T // tm,),
        in_specs=[
            pl.BlockSpec((tm, SB_WIDTH), lambda i: (i, 0)),
            pl.BlockSpec((tm, DIFF_V_WIDTH), lambda i: (i, 0)),
            pl.BlockSpec((tm, D), lambda i: (i, COL_G)),
            pl.BlockSpec((tm, D), lambda i: (i, COL_G + 1)),
            const((1, 2 * D)),
            const((SB_WIDTH, D)), const((DIFF_V_WIDTH, D)), const((D, D)),
            pl.BlockSpec((tm, D), lambda i: (i, 0)),
            mod_spec,
            const((1, D)),
            mod_spec, mod_spec,
        ],
        out_specs=[pl.BlockSpec((tm, D), lambda i: (i, 0)), pl.BlockSpec((tm, D), lambda i: (i, 0))],
        out_shape=[jax.ShapeDtypeStruct((T, D), F32), jax.ShapeDtypeStruct((T, D), BF16)],
        compiler_params=_params("parallel"),
        name="merge_out_proj",
    )(y_sb, y_df, proj, proj, gate_b.reshape(1, 2 * D), wbs, wbd, wo, x2,
      gate1.reshape(B, 1, D), norm2_g.reshape(1, D), scale2.reshape(B, 1, D), shift2.reshape(B, 1, D))


def _sort16_network():
    def merge(lo, hi, r):
        step = r * 2
        if step < hi - lo:
            yield from merge(lo, hi, step)
            yield from merge(lo + r, hi, step)
            yield from [(i, i + r) for i in range(lo + r, hi - r, step)]
        else:
            yield (lo, lo + r)

    def sort(lo, hi):
        if hi - lo >= 1:
            mid = lo + (hi - lo) // 2
            yield from sort(lo, mid)
            yield from sort(mid + 1, hi)
            yield from merge(lo, hi, 1)

    return list(sort(0, 15))


def _top_values(scores, k):
    groups = scores.shape[0] // SUBLANES
    assert groups == 16 and k <= groups + 1
    v = [scores[SUBLANES * g:SUBLANES * (g + 1)] for g in range(groups)]
    for a, b in _sort16_network():
        v[a], v[b] = jnp.maximum(v[a], v[b]), jnp.minimum(v[a], v[b])
    v.append(jnp.full_like(v[0], NEG_INF))
    vals = []
    for r in range(k):
        m = jnp.max(v[0], axis=0, keepdims=True)
        vals.append(m)
        still_needed = k - 1 - r
        if still_needed:
            hit = v[0] == m
            for d in range(min(still_needed, groups)):
                v[d] = jnp.where(hit, v[d + 1], v[d])
    return vals


def _candidate_sums(v1, v1_all, v2, v2_all, row_id):
    assert PEER_TOPK == 16 and SUBLANES == 8
    lo2 = v2_all[0:8]
    hi1 = v1_all[8:16]
    shift = lambda x, n: pltpu.roll(x, n, 0)
    groups = [v1[0] + v2_all[0:8], v1[0] + v2_all[8:16], v1[0] + v2_all[16:24],
              v1[1] + lo2]
    groups.append(jnp.where(row_id < 5, v1[2] + lo2, v1[4] + shift(lo2, 5)))
    groups.append(jnp.where(row_id < 4, v1[3] + lo2,
                            jnp.where(row_id < 6, v1[5] + shift(lo2, 4),
                                      v1[6] + shift(lo2, 6))))
    tail = shift(hi1, 2) + v2[0]
    groups.append(jnp.where(row_id < 2, v1[7] + lo2, tail))
    groups.append(jnp.where(row_id < 2, tail,
                            jnp.where(row_id == 2, v1[16] + v2[0], NEG_INF)))
    return groups


PEER_SELECT_GROUP = 8


def _peer_select_kernel(h_ref, wq_ref, sk_ref, ht_ref, th_ref, e1_ref, e2_ref, *, tm):
    hb = h_ref[...]
    ht_ref[0] = hb.T
    K = PEER_TOPK + 1
    row_id = lax.broadcasted_iota(jnp.int32, (SUBLANES, LANES), 0)
    pad_rows = jnp.full((3 * SUBLANES - K, LANES), NEG_INF, F32)

    def half_scores(h):
        q_t = _nt_dot(wq_ref[h], hb).astype(BF16)
        half = PEER_QUERY_DIM // 2
        return [jnp.dot(sk_ref[h, c], q_t[c * half:(c + 1) * half], preferred_element_type=F32)
                for c in range(2)]

    def select(h, sc):
        for lc in range(tm // LANES):
            ls = slice(lc * LANES, (lc + 1) * LANES)
            s1, s2 = sc[0][:, ls], sc[1][:, ls]
            v1 = _top_values(s1, K)
            v2 = _top_values(s2, K)
            v2_all = jnp.concatenate(v2 + [pad_rows], axis=0)
            v1_all = jnp.concatenate(v1 + [pad_rows], axis=0)
            cands = _candidate_sums(v1, v1_all, v2, v2_all, row_id)
            best = cands[0][0:1]
            z_sum = jnp.zeros_like(best)
            last = best
            for r in range(K):
                stacked = cands[0]
                for ca in cands[1:]:
                    stacked = jnp.maximum(stacked, ca)
                m = jnp.max(stacked, axis=0, keepdims=True)
                if r < PEER_TOPK:
                    z_sum = z_sum + jnp.exp(m - best)
                    last = m
                    cands = [jnp.where(ca == m, NEG_INF, ca) for ca in cands]
            tau = 0.5 * (last + m)
            half_inv_z = 0.5 / z_sum
            chunked = (PEER_N_KEYS // PEER_CHUNK_KEYS, PEER_CHUNK_KEYS, LANES)
            th_ref[:, 0, h, :, ls] = jnp.exp(tau - s1 - v2[0]).reshape(chunked)
            e1_ref[:, 0, h, :, ls] = (jnp.exp(s1 - v1[0]) * half_inv_z).reshape(chunked)
            e2_ref[0, h, :, ls] = jnp.exp(s2 - v2[0])

    def head_group(g, carry):
        h0 = g * PEER_SELECT_GROUP
        sc_next = half_scores(h0)
        for k in range(PEER_SELECT_GROUP):
            sc_cur = sc_next
            if k + 1 < PEER_SELECT_GROUP:
                sc_next = half_scores(h0 + k + 1)
            select(h0 + k, sc_cur)
        return carry

    lax.fori_loop(0, PEER_HEADS // PEER_SELECT_GROUP, head_group, 0)


def _peer_select(h2, wq_t, subkeys_bf):
    T, D = h2.shape
    tm = PEER_TOKEN_TILE
    n_tiles = T // tm
    tok_spec = pl.BlockSpec((1, PEER_HEADS, PEER_N_KEYS, tm), lambda i: (i, 0, 0, 0))
    tok_shape = jax.ShapeDtypeStruct((n_tiles, PEER_HEADS, PEER_N_KEYS, tm), F32)
    n_chunks = PEER_N_KEYS // PEER_CHUNK_KEYS
    row_spec = pl.BlockSpec((n_chunks, 1, PEER_HEADS, PEER_CHUNK_KEYS, tm), lambda i: (0, i, 0, 0, 0))
    row_shape = jax.ShapeDtypeStruct((n_chunks, n_tiles, PEER_HEADS, PEER_CHUNK_KEYS, tm), F32)
    return pl.pallas_call(
        functools.partial(_peer_select_kernel, tm=tm),
        grid=(n_tiles,),
        in_specs=[
            pl.BlockSpec((tm, D), lambda i: (i, 0)),
            pl.BlockSpec((PEER_HEADS, PEER_QUERY_DIM, D), lambda i: (0, 0, 0)),
            pl.BlockSpec((PEER_HEADS, 2, PEER_N_KEYS, PEER_QUERY_DIM // 2), lambda i: (0, 0, 0, 0)),
        ],
        out_specs=[
            pl.BlockSpec((1, D, tm), lambda i: (i, 0, 0)),
            row_spec, row_spec, tok_spec,
        ],
        out_shape=[
            jax.ShapeDtypeStruct((n_tiles, D, tm), BF16),
            row_shape, row_shape, tok_shape,
        ],
        compiler_params=_params("parallel"),
        name="peer_select",
    )(h2, wq_t, subkeys_bf)


PEER_ROW_BLOCK_KEYS = 2
PEER_PIECE_TOKENS = 2 * LANES


def _peer_mix_kernel(ht_ref, u_ref, vt_ref, th_ref, e1_ref, e2_ref, x_ref, g2_ref, fg_ref,
                     xo_ref, acc_sc, act_sc, e2_sc, *, tm, n_tiles, final_norm):
    j = pl.program_id(1)

    @pl.when(j == 0)
    def _():
        acc_sc[...] = jnp.zeros_like(acc_sc)
        for lc in range(tm // LANES):
            e2_sc[:, :, lc] = e2_ref[:, :, :, lc * LANES:(lc + 1) * LANES]

    rb = PEER_ROW_BLOCK_KEYS * PEER_N_KEYS
    tw = PEER_PIECE_TOKENS
    n_rb = PEER_CHUNK // rb
    pieces = [(r, c) for c in range(tm // tw) for r in range(n_rb)]

    def tile(t, carry):
        def scores(p):
            r, c = pieces[p]
            return jnp.dot(u_ref[r * rb:(r + 1) * rb, :], ht_ref[t, :, c * tw:(c + 1) * tw],
                           preferred_element_type=F32)

        def gates(p, a_blk, k):
            r, c = pieces[p]
            ii = r * PEER_ROW_BLOCK_KEYS + k
            for lc in range(tw // LANES):
                chunk = c * (tw // LANES) + lc
                ls = slice(chunk * LANES, (chunk + 1) * LANES)
                gate = None
                for h in range(PEER_HEADS):
                    theta = th_ref[0, t, h, ii:ii + 1, ls]
                    e1 = e1_ref[0, t, h, ii:ii + 1, ls]
                    e2 = e2_sc[t, h, chunk]
                    term = jnp.where(e2 >= theta, e1 * e2, 0.0)
                    gate = term if gate is None else gate + term
                a = a_blk[k * PEER_N_KEYS:(k + 1) * PEER_N_KEYS, lc * LANES:(lc + 1) * LANES]
                gelu2 = a * (1.0 + lax.erf(a * (1.0 / math.sqrt(2.0))))
                act_sc[c, ii * PEER_N_KEYS:(ii + 1) * PEER_N_KEYS, lc * LANES:(lc + 1) * LANES] = (
                    gelu2 * gate).astype(BF16)

        def mix(c):
            acc_sc[t, c] += jnp.dot(vt_ref[0], act_sc[c], preferred_element_type=F32)

        a_next = scores(0)
        for p in range(len(pieces)):
            r, c = pieces[p]
            a_cur = a_next
            if p + 1 < len(pieces):
                a_next = scores(p + 1)
            for k in range(PEER_ROW_BLOCK_KEYS):
                if k == PEER_ROW_BLOCK_KEYS // 2 and r == 0 and c > 0:
                    mix(c - 1)
                gates(p, a_cur, k)
        mix(tm // tw - 1)
        return carry

    lax.fori_loop(0, n_tiles, tile, 0)

    @pl.when(j == pl.num_programs(1) - 1)
    def _():
        for t in range(n_tiles):
            rows = slice(t * tm, (t + 1) * tm)
            mixed_t = jnp.concatenate([acc_sc[t, c] for c in range(tm // tw)], axis=1)
            xn = x_ref[rows, :] + g2_ref[0] * mixed_t.T
            if final_norm:
                ms = jnp.mean(xn * xn, axis=-1, keepdims=True)
                xn = xn * lax.rsqrt(ms + NORM_EPS) * fg_ref[...]
            xo_ref[rows, :] = xn


def _peer_mix(h_t, u_bf, v_t, theta, e1, e2, x2, gate2, final_g, final_norm, B, S):
    T, D = x2.shape
    tm = PEER_TOKEN_TILE
    n_tiles = min(PEER_TILES_PER_STEP, S // tm)
    span = n_tiles * tm
    steps_per_b = S // span
    tok_spec = pl.BlockSpec((n_tiles, PEER_HEADS, PEER_N_KEYS, tm), lambda i, j: (i, 0, 0, 0),
                            pipeline_mode=pl.Buffered(1))
    row_spec = pl.BlockSpec((1, n_tiles, PEER_HEADS, PEER_CHUNK_KEYS, tm), lambda i, j: (j, i, 0, 0, 0))
    return pl.pallas_call(
        functools.partial(_peer_mix_kernel, tm=tm, n_tiles=n_tiles, final_norm=final_norm),
        grid=(T // span, PEER_N_EXPERTS // PEER_CHUNK),
        in_specs=[
            pl.BlockSpec((n_tiles, D, tm), lambda i, j: (i, 0, 0)),
            pl.BlockSpec((PEER_CHUNK, D), lambda i, j: (j, 0)),
            pl.BlockSpec((1, D, PEER_CHUNK), lambda i, j: (j, 0, 0)),
            row_spec, row_spec, tok_spec,
            pl.BlockSpec((span, D), lambda i, j: (i, 0)),
            pl.BlockSpec((1, 1, D), lambda i, j: (i // steps_per_b, 0, 0)),
            pl.BlockSpec((1, D), lambda i, j: (0, 0)),
        ],
        out_specs=pl.BlockSpec((span, D), lambda i, j: (i, 0)),
        out_shape=jax.ShapeDtypeStruct((T, D), F32),
        scratch_shapes=[pltpu.VMEM((n_tiles, tm // PEER_PIECE_TOKENS, D, PEER_PIECE_TOKENS), F32),
                        pltpu.VMEM((tm // PEER_PIECE_TOKENS, PEER_CHUNK, PEER_PIECE_TOKENS), BF16),
                        pltpu.VMEM((n_tiles, PEER_HEADS, tm // LANES, PEER_N_KEYS, LANES), F32)],
        compiler_params=_params("parallel", "arbitrary"),
        name="peer_mix",
    )(h_t, u_bf, v_t, theta, e1, e2, x2, gate2.reshape(B, 1, D), final_g.reshape(1, D))


def _expert_tables_kernel(u_ref, v_ref, ub_ref, vt_ref):
    ub_ref[...] = u_ref[0].astype(BF16)
    vt_ref[0] = v_ref[0].astype(BF16).T


def _expert_tables(u_all, v_all, layer):
    _, E, D = u_all.shape
    n_chunks = E // PEER_CHUNK
    return pl.pallas_call(
        _expert_tables_kernel,
        grid=(n_chunks,),
        in_specs=[pl.BlockSpec((1, PEER_CHUNK, D), lambda i: (layer, i, 0)),
                  pl.BlockSpec((1, PEER_CHUNK, D), lambda i: (layer, i, 0))],
        out_specs=[pl.BlockSpec((PEER_CHUNK, D), lambda i: (i, 0)),
                   pl.BlockSpec((1, D, PEER_CHUNK), lambda i: (i, 0, 0))],
        out_shape=[jax.ShapeDtypeStruct((E, D), BF16),
                   jax.ShapeDtypeStruct((n_chunks, D, PEER_CHUNK), BF16)],
        compiler_params=_params("parallel"),
        name="expert_tables",
    )(u_all, v_all)


def kernel(x, c, norm1_g, norm2_g, ada_w, ada_b, w_in, gate_b, diff_lambda, diff_subln_g,
           w_branch_sb, w_branch_diff, w_out, peer_wq, peer_subkeys, peer_u, peer_v, final_g):
    B, S, D = x.shape
    assert D == D_MODEL and S % LANES == 0
    x2 = x.reshape(B * S, D)
    mod = _modulation(c, ada_w, ada_b)
    rope = _rope_lane_tables(S)
    for l in range(DEPTH):
        lam_init = 0.8 - 0.6 * math.exp(-0.3 * l)
        shift1, scale1, gate1, shift2, scale2, gate2 = (mod[l, :, n] for n in range(6))
        proj = _input_projection(x2, norm1_g[l], scale1, shift1, w_in[l].astype(BF16), rope, B, S)
        y_sb = _sb_attention(proj, B, S)
        y_df = _diff_attention(proj, diff_lambda[l], diff_subln_g[l], lam_init, B, S)
        x2, h2 = _merge_project(y_sb, y_df, proj, gate_b[l], w_branch_sb[l].astype(BF16),
                                w_branch_diff[l].astype(BF16), w_out[l].astype(BF16), x2, gate1,
                                norm2_g[l], scale2, shift2, B, S)
        wq_t = peer_wq[l].T.astype(BF16).reshape(PEER_HEADS, PEER_QUERY_DIM, D)
        h_t, theta, e1, e2 = _peer_select(h2, wq_t, peer_subkeys[l].astype(BF16))
        u_bf, v_t = _expert_tables(peer_u, peer_v, l)
        x2 = _peer_mix(h_t, u_bf, v_t,
                       theta, e1, e2, x2, gate2, final_g, l == DEPTH - 1, B, S)
    return x2.reshape(B, S, D)
```

```python
import functools
import math

import jax
import jax.numpy as jnp
from jax import lax
from jax.experimental import pallas as pl
from jax.experimental.pallas import tpu as pltpu

F32 = jnp.float32
BF16 = jnp.bfloat16

D_MODEL = 1024
DEPTH = 2
SB_HEADS = 8
SB_HEAD_DIM = 64
SB_WIDTH = SB_HEADS * SB_HEAD_DIM
DIFF_HEADS = 4
DIFF_QK_DIM = 64
DIFF_V_DIM = 2 * DIFF_QK_DIM
DIFF_QK_WIDTH = DIFF_HEADS * 2 * DIFF_QK_DIM
DIFF_V_WIDTH = DIFF_HEADS * DIFF_V_DIM
ROPE_THETA = 500000.0
ROT_DIM = DIFF_QK_DIM // 4
IN_COLS = 3 * SB_WIDTH + 2 * DIFF_QK_WIDTH + DIFF_V_WIDTH + 2 * D_MODEL
PEER_HEADS = 8
PEER_N_KEYS = 128
PEER_N_EXPERTS = PEER_N_KEYS * PEER_N_KEYS
PEER_TOPK = 16
PEER_QUERY_DIM = 256
PEER_CHUNK_KEYS = 16
PEER_CHUNK = PEER_CHUNK_KEYS * PEER_N_KEYS
PEER_TOKEN_TILE = 512
PEER_TILES_PER_STEP = 2
NORM_EPS = 1e-6

LANES = 128
SUBLANES = 8
VMEM_LIMIT = 60 * 1024 * 1024
NEG_INF = float("-inf")
LOG2_E = 1.4426950408889634

IN_TILE_N = 512
COL_SB_Q, COL_SB_K, COL_SB_V = 0, SB_WIDTH // LANES, 2 * SB_WIDTH // LANES
COL_D_Q = 3 * SB_WIDTH // LANES
COL_D_K = COL_D_Q + DIFF_QK_WIDTH // LANES
COL_D_V = COL_D_K + DIFF_QK_WIDTH // LANES
COL_G = (3 * SB_WIDTH + 2 * DIFF_QK_WIDTH + DIFF_V_WIDTH) // D_MODEL


def _params(*sem):
    return pltpu.CompilerParams(dimension_semantics=sem, vmem_limit_bytes=VMEM_LIMIT)


def _nt_dot(a, b):
    return lax.dot_general(a, b, (((1,), (1,)), ((), ())), preferred_element_type=F32)


def _mod_kernel(c_ref, w_ref, b_ref, o_ref):
    c = c_ref[...]
    c_act = c * jax.nn.sigmoid(c)
    o_ref[0] = jnp.dot(c_act, w_ref[0], preferred_element_type=F32,
                       precision=lax.Precision.HIGHEST) + b_ref[0]


def _modulation(c, ada_w, ada_b):
    B, D = c.shape
    rows = SUBLANES
    c_pad = jnp.pad(c, ((0, rows - B), (0, 0)))
    out = pl.pallas_call(
        _mod_kernel,
        grid=(DEPTH, 6),
        in_specs=[
            pl.BlockSpec((rows, D), lambda l, n: (0, 0)),
            pl.BlockSpec((1, D, D), lambda l, n: (l, 0, n)),
            pl.BlockSpec((1, 1, D), lambda l, n: (l, 0, n)),
        ],
        out_specs=pl.BlockSpec((1, rows, D), lambda l, n: (l, 0, n)),
        out_shape=jax.ShapeDtypeStruct((DEPTH, rows, 6 * D), F32),
        compiler_params=_params("arbitrary", "arbitrary"),
        name="adaln_mod",
    )(c_pad, ada_w, ada_b.reshape(DEPTH, 1, 6 * D))
    return out[:, :B].reshape(DEPTH, B, 6, D)


def _norm_mod(x, g, scale, shift):
    ms = jnp.mean(x * x, axis=-1, keepdims=True)
    return (x * lax.rsqrt(ms + NORM_EPS) * g) * (1.0 + scale) + shift


def _inproj_kernel(x_ref, g_ref, sc_ref, sh_ref, w_ref, ra_ref, rb_ref, rc_ref, o_ref):
    h = _norm_mod(x_ref[...], g_ref[...], sc_ref[0], sh_ref[0]).astype(BF16)
    half = ROT_DIM // 2
    for j in range(IN_COLS // IN_TILE_N):
        cols = slice(j * IN_TILE_N, (j + 1) * IN_TILE_N)
        r = jnp.dot(h, w_ref[:, cols], preferred_element_type=F32)
        first_lane_block = j * IN_TILE_N // LANES
        if first_lane_block in (COL_SB_Q, COL_D_Q):
            r = r * (LOG2_E / math.sqrt(SB_HEAD_DIM))
        if first_lane_block in (COL_D_Q, COL_D_K):
            ra, rb, rc = ra_ref[...], rb_ref[...], rc_ref[...]
            for blk in range(IN_TILE_N // LANES):
                t = r[:, blk * LANES:(blk + 1) * LANES]
                rot = t * ra + pltpu.roll(t, LANES - half, 1) * rb + pltpu.roll(t, half, 1) * rc
                o_ref[:, j * IN_TILE_N + blk * LANES:j * IN_TILE_N + (blk + 1) * LANES] = rot.astype(BF16)
        else:
            o_ref[:, cols] = r.astype(BF16)


def _rope_lane_tables(seq_len):
    pos = jnp.arange(seq_len, dtype=F32)
    inv_freq = 1.0 / (ROPE_THETA ** (jnp.arange(0, ROT_DIM, 2, dtype=F32) / ROT_DIM))
    ang = pos[:, None] * inv_freq[None, :]
    cos, sin = jnp.cos(ang), jnp.sin(ang)
    half = ROT_DIM // 2
    lane = jnp.arange(LANES) % DIFF_QK_DIM
    f = lane % half
    first, second = lane < half, (lane >= half) & (lane < ROT_DIM)
    ra = jnp.where((first | second)[None, :], cos[:, f], 1.0)
    rb = jnp.where(first[None, :], -sin[:, f], 0.0)
    rc = jnp.where(second[None, :], sin[:, f], 0.0)
    return ra, rb, rc


def _input_projection(x2, norm_g, scale, shift, w_in_bf, rope, B, S):
    T, D = x2.shape
    tm = min(512, S)
    tiles_per_b = S // tm
    ra, rb, rc = rope
    rope_spec = pl.BlockSpec((tm, LANES), lambda i: (i % tiles_per_b, 0))
    mod_spec = pl.BlockSpec((1, 1, D), lambda i: (i // tiles_per_b, 0, 0))
    assert COL_D_K - COL_D_Q == IN_TILE_N // LANES and COL_D_V - COL_D_K == IN_TILE_N // LANES
    return pl.pallas_call(
        _inproj_kernel,
        grid=(T // tm,),
        in_specs=[
            pl.BlockSpec((tm, D), lambda i: (i, 0)),
            pl.BlockSpec((1, D), lambda i: (0, 0)),
            mod_spec, mod_spec,
            pl.BlockSpec((D, IN_COLS), lambda i: (0, 0)),
            rope_spec, rope_spec, rope_spec,
        ],
        out_specs=pl.BlockSpec((tm, IN_COLS), lambda i: (i, 0)),
        out_shape=jax.ShapeDtypeStruct((T, IN_COLS), BF16),
        compiler_params=_params("parallel"),
        name="in_proj",
    )(x2, norm_g.reshape(1, D), scale.reshape(B, 1, D), shift.reshape(B, 1, D), w_in_bf, ra, rb, rc)


def _sb_kernel(q_ref, k_ref, v_ref, o_ref, acc_sc, cs_sc, *, tb, n_pairs):
    i = pl.program_id(2)
    n_heads = 2 * n_pairs
    lane = lax.broadcasted_iota(jnp.int32, (tb, LANES), 1)
    q_heads = []
    for p in range(n_pairs):
        q = q_ref[:, p * LANES:(p + 1) * LANES]
        zero = jnp.zeros_like(q)
        q_heads += [jnp.where(lane < SB_HEAD_DIM, q, zero), jnp.where(lane >= SB_HEAD_DIM, q, zero)]
    row = lax.broadcasted_iota(jnp.int32, (tb, tb), 0)
    col = lax.broadcasted_iota(jnp.int32, (tb, tb), 1)
    strictly_before = col < row
    suffix_ones = jnp.where(row >= col, 1.0, 0.0).astype(BF16)

    acc_sc[...] = jnp.zeros_like(acc_sc)
    cs_sc[...] = jnp.zeros_like(cs_sc)

    groups = [list(range(g, g + SB_GROUP_HEADS)) for g in range(0, n_heads, SB_GROUP_HEADS)]

    def blocks(js, masked):
        kv = []
        for j in js:
            start = pl.multiple_of(j * tb, tb)
            kv.append((k_ref[pl.ds(start, tb), :], v_ref[pl.ds(start, tb), :]))
        items = [(b, heads) for b in range(len(js)) for heads in groups]

        def logits(item):
            b, heads = item
            return [_nt_dot(q_heads[h], kv[b][0][:, (h // 2) * LANES:(h // 2 + 1) * LANES]) for h in heads]

        def suffix_sums(ys):
            costs = []
            for y in ys:
                cost = jnp.where(y > 30.0, y, jnp.log2(1.0 + jnp.exp2(y)))
                if masked:
                    cost = jnp.where(strictly_before, cost, 0.0)
                costs.append(cost)
            return [jnp.dot(cost.astype(BF16), suffix_ones, preferred_element_type=F32) for cost in costs]

        def accumulate(item, ys, sums):
            b, heads = item
            for h, y, s in zip(heads, ys, sums):
                later = cs_sc[h]
                w = jnp.exp2(y - s - later)
                if masked:
                    w = jnp.where(strictly_before, w, 0.0)
                acc_sc[h] += jnp.dot(w.astype(BF16), kv[b][1][:, (h // 2) * LANES:(h // 2 + 1) * LANES],
                                     preferred_element_type=F32)
                cs_sc[h] = later + s[:, 0:1]

        ys_next = logits(items[0])
        for n, item in enumerate(items):
            ys_cur = ys_next
            sums = suffix_sums(ys_cur)
            if n + 1 < len(items):
                ys_next = logits(items[n + 1])
            accumulate(item, ys_cur, sums)

    blocks([i], True)

    @pl.when(i % 2 == 1)
    def _():
        blocks([i - 1], False)

    def body(pp, carry):
        first = i - (i % 2) - 1 - 2 * pp
        blocks([first, first - 1], False)
        return carry

    lax.fori_loop(0, i // 2, body, 0)
    for p in range(n_pairs):
        o_ref[:, p * LANES:(p + 1) * LANES] = jnp.where(
            lane < SB_HEAD_DIM, acc_sc[2 * p], acc_sc[2 * p + 1]).astype(BF16)


SB_PAIRS_PER_STEP = 4
SB_GROUP_HEADS = 4


def _sb_attention(proj, B, S):
    T = proj.shape[0]
    tb = min(256, S)
    nq = S // tb
    n_pairs = SB_PAIRS_PER_STEP
    width = n_pairs * LANES
    groups = SB_WIDTH // width
    return pl.pallas_call(
        functools.partial(_sb_kernel, tb=tb, n_pairs=n_pairs),
        grid=(B, groups, nq),
        in_specs=[
            pl.BlockSpec((tb, width), lambda b, p, i: (b * nq + i, COL_SB_Q // n_pairs + p)),
            pl.BlockSpec((S, width), lambda b, p, i: (b, COL_SB_K // n_pairs + p)),
            pl.BlockSpec((S, width), lambda b, p, i: (b, COL_SB_V // n_pairs + p)),
        ],
        out_specs=pl.BlockSpec((tb, width), lambda b, p, i: (b * nq + i, p)),
        out_shape=jax.ShapeDtypeStruct((T, SB_WIDTH), BF16),
        scratch_shapes=[pltpu.VMEM((2 * n_pairs, tb, LANES), F32),
                        pltpu.VMEM((2 * n_pairs, tb, 1), F32)],
        compiler_params=_params("parallel", "parallel", "arbitrary"),
        name="sb_attention",
    )(proj, proj, proj)


def _diff_kernel(lp_ref, g_ref, q_ref, k_ref, v_ref, o_ref, m_sc, acc_sc, *, tq, tk, lam_init):
    i = pl.program_id(1)
    lane = lax.broadcasted_iota(jnp.int32, (tq, LANES), 1)
    q_maps = []
    for h in range(DIFF_HEADS):
        q = q_ref[:, h * LANES:(h + 1) * LANES]
        zero = jnp.zeros_like(q)
        q_maps += [jnp.where(lane < DIFF_QK_DIM, q, zero), jnp.where(lane >= DIFF_QK_DIM, q, zero)]
    n_chains = len(q_maps)
    m_sc[...] = jnp.full_like(m_sc, NEG_INF)
    acc_sc[...] = jnp.zeros_like(acc_sc)
    q_pos = i * tq + lax.broadcasted_iota(jnp.int32, (tq, tk), 0)
    k_off = lax.broadcasted_iota(jnp.int32, (tq, tk), 1)
    ones = jnp.ones((tk, LANES), BF16)

    head_cols = lambda c: slice((c // 2) * LANES, (c // 2 + 1) * LANES)
    chains = list(range(n_chains))

    def blocks(js, masked):
        kv = []
        for j in js:
            start = pl.multiple_of(j * tk, tk)
            kv.append((k_ref[pl.ds(start, tk), :], v_ref[pl.ds(start, tk), :]))

        def logits(b):
            return [_nt_dot(q_maps[c], kv[b][0][:, head_cols(c)]) for c in chains]

        def softmax_step(b, scores):
            if masked:
                visible = js[b] * tk + k_off <= q_pos
                scores = [jnp.where(visible, s, NEG_INF) for s in scores]
            m_prev = [m_sc[c] for c in chains]
            m_new = [jnp.maximum(mp, jnp.max(s, axis=1, keepdims=True)) for mp, s in zip(m_prev, scores)]
            probs = [jnp.exp2(s - jnp.concatenate([mn] * (tk // LANES), axis=1))
                     for s, mn in zip(scores, m_new)]
            for c, mn in zip(chains, m_new):
                m_sc[c] = mn
            return m_prev, m_new, probs

        def accumulate(b, m_prev, m_new, probs):
            for c, mp, mn, p in zip(chains, m_prev, m_new, probs):
                alpha = jnp.exp2(mp - mn)
                v_ext = jnp.concatenate([kv[b][1][:, head_cols(c)], ones], axis=1)
                acc_sc[c] = (jnp.concatenate([alpha, alpha], axis=1) * acc_sc[c]
                             + jnp.dot(p.astype(BF16), v_ext, preferred_element_type=F32))

        s_next = logits(0)
        for b in range(len(js)):
            s_cur = s_next
            stats = softmax_step(b, s_cur)
            if b + 1 < len(js):
                s_next = logits(b + 1)
            accumulate(b, *stats)

    n_full = (i * tq) // tk

    @pl.when(n_full % 2 == 1)
    def _():
        blocks([0], False)

    def body(pp, carry):
        first = n_full % 2 + 2 * pp
        blocks([first, first + 1], False)
        return carry

    lax.fori_loop(0, n_full // 2, body, 0)
    blocks([n_full], True)

    lp = lp_ref[...]
    lam = (jnp.exp(jnp.sum(lp[0:1] * lp[1:2], axis=1, keepdims=True))
           - jnp.exp(jnp.sum(lp[2:3] * lp[3:4], axis=1, keepdims=True)) + lam_init)
    for h in range(DIFF_HEADS):
        a1, a2 = acc_sc[2 * h], acc_sc[2 * h + 1]
        o = a1[:, :LANES] / a1[:, LANES:] - lam * (a2[:, :LANES] / a2[:, LANES:])
        ms = jnp.mean(o * o, axis=-1, keepdims=True)
        y = (o * lax.rsqrt(ms + NORM_EPS) * g_ref[...]) * (1.0 - lam_init)
        o_ref[:, h * LANES:(h + 1) * LANES] = y.astype(BF16)


def _diff_attention(proj, lam_params, subln_g, lam_init, B, S):
    T = proj.shape[0]
    tq = min(256, S)
    tk = min(512, S)
    nq = S // tq
    n_chains = 2 * DIFF_HEADS
    return pl.pallas_call(
        functools.partial(_diff_kernel, tq=tq, tk=tk, lam_init=lam_init),
        grid=(B, nq),
        in_specs=[
            pl.BlockSpec((4, DIFF_QK_DIM), lambda b, i: (0, 0)),
            pl.BlockSpec((1, DIFF_V_DIM), lambda b, i: (0, 0)),
            pl.BlockSpec((tq, DIFF_QK_WIDTH), lambda b, i: (b * nq + i, COL_D_Q * LANES // DIFF_QK_WIDTH)),
            pl.BlockSpec((S, DIFF_QK_WIDTH), lambda b, i: (b, COL_D_K * LANES // DIFF_QK_WIDTH)),
            pl.BlockSpec((S, DIFF_V_WIDTH), lambda b, i: (b, COL_D_V * LANES // DIFF_V_WIDTH)),
        ],
        out_specs=pl.BlockSpec((tq, DIFF_V_WIDTH), lambda b, i: (b * nq + i, 0)),
        out_shape=jax.ShapeDtypeStruct((T, DIFF_V_WIDTH), BF16),
        scratch_shapes=[pltpu.VMEM((n_chains, tq, LANES), F32),
                        pltpu.VMEM((n_chains, tq, 2 * LANES), F32)],
        compiler_params=_params("parallel", "arbitrary"),
        name="diff_attention",
    )(lam_params, subln_g.reshape(1, DIFF_V_DIM), proj, proj, proj)


def _merge_kernel(ysb_ref, ydf_ref, gs_ref, gd_ref, gb_ref, wbs_ref, wbd_ref, wo_ref,
                  x_ref, g1_ref, n2_ref, sc2_ref, sh2_ref, xo_ref, h2_ref):
    D = x_ref.shape[1]
    a = jnp.dot(ysb_ref[...], wbs_ref[...], preferred_element_type=F32)
    b = jnp.dot(ydf_ref[...], wbd_ref[...], preferred_element_type=F32)
    gb = gb_ref[...]
    g_sb = jax.nn.sigmoid(gs_ref[...].astype(F32) + gb[:, :D])
    g_df = jax.nn.sigmoid(gd_ref[...].astype(F32) + gb[:, D:])
    merged = g_sb * a + g_df * b
    y = jnp.dot(merged.astype(BF16), wo_ref[...], preferred_element_type=F32)
    xn = x_ref[...] + g1_ref[0] * y
    xo_ref[...] = xn
    h2_ref[...] = _norm_mod(xn, n2_ref[...], sc2_ref[0], sh2_ref[0]).astype(BF16)


def _merge_project(y_sb, y_df, proj, gate_b, wbs, wbd, wo, x2, gate1, norm2_g, scale2, shift2, B, S):
    T, D = x2.shape
    tm = min(512, S)
    tiles_per_b = S // tm
    mod_spec = pl.BlockSpec((1, 1, D), lambda i: (i // tiles_per_b, 0, 0))
    const = lambda shape: pl.BlockSpec(shape, lambda i: (0, 0))
    return pl.pallas_call(
        _merge_kernel,
        grid=(T // tm,),
        in_specs=[
            pl.BlockSpec((tm, SB_WIDTH), lambda i: (i, 0)),
            pl.BlockSpec((tm, DIFF_V_WIDTH), lambda i: (i, 0)),
            pl.BlockSpec((tm, D), lambda i: (i, COL_G)),
            pl.BlockSpec((tm, D), lambda i: (i, COL_G + 1)),
            const((1, 2 * D)),
            const((SB_WIDTH, D)), const((DIFF_V_WIDTH, D)), const((D, D)),
            pl.BlockSpec((tm, D), lambda i: (i, 0)),
            mod_spec,
            const((1, D)),
            mod_spec, mod_spec,
        ],
        out_specs=[pl.BlockSpec((tm, D), lambda i: (i, 0)), pl.BlockSpec((tm, D), lambda i: (i, 0))],
        out_shape=[jax.ShapeDtypeStruct((T, D), F32), jax.ShapeDtypeStruct((T, D), BF16)],
        compiler_params=_params("parallel"),
        name="merge_out_proj",
    )(y_sb, y_df, proj, proj, gate_b.reshape(1, 2 * D), wbs, wbd, wo, x2,
      gate1.reshape(B, 1, D), norm2_g.reshape(1, D), scale2.reshape(B, 1, D), shift2.reshape(B, 1, D))


def _sort16_network():
    def merge(lo, hi, r):
        step = r * 2
        if step < hi - lo:
            yield from merge(lo, hi, step)
            yield from merge(lo + r, hi, step)
            yield from [(i, i + r) for i in range(lo + r, hi - r, step)]
        else:
            yield (lo, lo + r)

    def sort(lo, hi):
        if hi - lo >= 1:
            mid = lo + (hi - lo) // 2
            yield from sort(lo, mid)
            yield from sort(mid + 1, hi)
            yield from merge(lo, hi, 1)

    return list(sort(0, 15))


def _top_values(scores, k):
    groups = scores.shape[0] // SUBLANES
    assert groups == 16 and k <= groups + 1
    v = [scores[SUBLANES * g:SUBLANES * (g + 1)] for g in range(groups)]
    for a, b in _sort16_network():
        v[a], v[b] = jnp.maximum(v[a], v[b]), jnp.minimum(v[a], v[b])
    v.append(jnp.full_like(v[0], NEG_INF))
    vals = []
    for r in range(k):
        m = jnp.max(v[0], axis=0, keepdims=True)
        vals.append(m)
        still_needed = k - 1 - r
        if still_needed:
            hit = v[0] == m
            for d in range(min(still_needed, groups)):
                v[d] = jnp.where(hit, v[d + 1], v[d])
    return vals


def _candidate_sums(v1, v1_all, v2, v2_all, row_id):
    assert PEER_TOPK == 16 and SUBLANES == 8
    lo2 = v2_all[0:8]
    hi1 = v1_all[8:16]
    shift = lambda x, n: pltpu.roll(x, n, 0)
    groups = [v1[0] + v2_all[0:8], v1[0] + v2_all[8:16], v1[0] + v2_all[16:24],
              v1[1] + lo2]
    groups.append(jnp.where(row_id < 5, v1[2] + lo2, v1[4] + shift(lo2, 5)))
    groups.append(jnp.where(row_id < 4, v1[3] + lo2,
                            jnp.where(row_id < 6, v1[5] + shift(lo2, 4),
                                      v1[6] + shift(lo2, 6))))
    tail = shift(hi1, 2) + v2[0]
    groups.append(jnp.where(row_id < 2, v1[7] + lo2, tail))
    groups.append(jnp.where(row_id < 2, tail,
                            jnp.where(row_id == 2, v1[16] + v2[0], NEG_INF)))
    return groups


PEER_SELECT_GROUP = 8


def _peer_select_kernel(h_ref, wq_ref, sk_ref, ht_ref, th_ref, e1_ref, e2_ref, *, tm):
    hb = h_ref[...]
    ht_ref[0] = hb.T
    K = PEER_TOPK + 1
    row_id = lax.broadcasted_iota(jnp.int32, (SUBLANES, LANES), 0)
    pad_rows = jnp.full((3 * SUBLANES - K, LANES), NEG_INF, F32)

    def half_scores(h):
        q_t = _nt_dot(wq_ref[h], hb).astype(BF16)
        half = PEER_QUERY_DIM // 2
        return [jnp.dot(sk_ref[h, c], q_t[c * half:(c + 1) * half], preferred_element_type=F32)
                for c in range(2)]

    def select(h, sc):
        for lc in range(tm // LANES):
            ls = slice(lc * LANES, (lc + 1) * LANES)
            s1, s2 = sc[0][:, ls], sc[1][:, ls]
            v1 = _top_values(s1, K)
            v2 = _top_values(s2, K)
            v2_all = jnp.concatenate(v2 + [pad_rows], axis=0)
            v1_all = jnp.concatenate(v1 + [pad_rows], axis=0)
            cands = _candidate_sums(v1, v1_all, v2, v2_all, row_id)
            best = cands[0][0:1]
            z_sum = jnp.zeros_like(best)
            last = best
            for r in range(K):
                stacked = cands[0]
                for ca in cands[1:]:
                    stacked = jnp.maximum(stacked, ca)
                m = jnp.max(stacked, axis=0, keepdims=True)
                if r < PEER_TOPK:
                    z_sum = z_sum + jnp.exp(m - best)
                    last = m
                    cands = [jnp.where(ca == m, NEG_INF, ca) for ca in cands]
            tau = 0.5 * (last + m)
            half_inv_z = 0.5 / z_sum
            chunked = (PEER_N_KEYS // PEER_CHUNK_KEYS, PEER_CHUNK_KEYS, LANES)
            th_ref[:, 0, h, :, ls] = jnp.exp(tau - s1 - v2[0]).reshape(chunked)
            e1_ref[:, 0, h, :, ls] = (jnp.exp(s1 - v1[0]) * half_inv_z).reshape(chunked)
            e2_ref[0, h, :, ls] = jnp.exp(s2 - v2[0])

    def head_group(g, carry):
        h0 = g * PEER_SELECT_GROUP
        sc_next = half_scores(h0)
        for k in range(PEER_SELECT_GROUP):
            sc_cur = sc_next
            if k + 1 < PEER_SELECT_GROUP:
                sc_next = half_scores(h0 + k + 1)
            select(h0 + k, sc_cur)
        return carry

    lax.fori_loop(0, PEER_HEADS // PEER_SELECT_GROUP, head_group, 0)


def _peer_select(h2, wq_t, subkeys_bf):
    T, D = h2.shape
    tm = PEER_TOKEN_TILE
    n_tiles = T // tm
    tok_spec = pl.BlockSpec((1, PEER_HEADS, PEER_N_KEYS, tm), lambda i: (i, 0, 0, 0))
    tok_shape = jax.ShapeDtypeStruct((n_tiles, PEER_HEADS, PEER_N_KEYS, tm), F32)
    n_chunks = PEER_N_KEYS // PEER_CHUNK_KEYS
    row_spec = pl.BlockSpec((n_chunks, 1, PEER_HEADS, PEER_CHUNK_KEYS, tm), lambda i: (0, i, 0, 0, 0))
    row_shape = jax.ShapeDtypeStruct((n_chunks, n_tiles, PEER_HEADS, PEER_CHUNK_KEYS, tm), F32)
    return pl.pallas_call(
        functools.partial(_peer_select_kernel, tm=tm),
        grid=(n_tiles,),
        in_specs=[
            pl.BlockSpec((tm, D), lambda i: (i, 0)),
            pl.BlockSpec((PEER_HEADS, PEER_QUERY_DIM, D), lambda i: (0, 0, 0)),
            pl.BlockSpec((PEER_HEADS, 2, PEER_N_KEYS, PEER_QUERY_DIM // 2), lambda i: (0, 0, 0, 0)),
        ],
        out_specs=[
            pl.BlockSpec((1, D, tm), lambda i: (i, 0, 0)),
            row_spec, row_spec, tok_spec,
        ],
        out_shape=[
            jax.ShapeDtypeStruct((n_tiles, D, tm), BF16),
            row_shape, row_shape, tok_shape,
        ],
        compiler_params=_params("parallel"),
        name="peer_select",
    )(h2, wq_t, subkeys_bf)


PEER_ROW_BLOCK_KEYS = 2
PEER_PIECE_TOKENS = 2 * LANES


def _peer_mix_kernel(ht_ref, u_ref, vt_ref, th_ref, e1_ref, e2_ref, x_ref, g2_ref, fg_ref,
                     xo_ref, acc_sc, act_sc, e2_sc, *, tm, n_tiles, final_norm):
    j = pl.program_id(1)

    @pl.when(j == 0)
    def _():
        acc_sc[...] = jnp.zeros_like(acc_sc)
        for lc in range(tm // LANES):
            e2_sc[:, :, lc] = e2_ref[:, :, :, lc * LANES:(lc + 1) * LANES]

    rb = PEER_ROW_BLOCK_KEYS * PEER_N_KEYS
    tw = PEER_PIECE_TOKENS
    n_rb = PEER_CHUNK // rb
    pieces = [(r, c) for c in range(tm // tw) for r in range(n_rb)]

    def tile(t, carry):
        def scores(p):
            r, c = pieces[p]
            u_rows = pltpu.bitcast(u_ref[r * rb // 2:(r + 1) * rb // 2, :], BF16)
            return jnp.dot(u_rows, ht_ref[t, :, c * tw:(c + 1) * tw],
                           preferred_element_type=F32)

        def gates(p, a_blk, k):
            r, c = pieces[p]
            ii = r * PEER_ROW_BLOCK_KEYS + k
            for lc in range(tw // LANES):
                chunk = c * (tw // LANES) + lc
                ls = slice(chunk * LANES, (chunk + 1) * LANES)
                gate = None
                for h in range(PEER_HEADS):
                    theta = th_ref[0, t, h, ii:ii + 1, ls]
                    e1 = e1_ref[0, t, h, ii:ii + 1, ls]
                    e2 = e2_sc[t, h, chunk]
                    term = jnp.where(e2 >= theta, e1 * e2, 0.0)
                    gate = term if gate is None else gate + term
                a = a_blk[k * PEER_N_KEYS:(k + 1) * PEER_N_KEYS, lc * LANES:(lc + 1) * LANES]
                gelu2 = a * (1.0 + lax.erf(a * (1.0 / math.sqrt(2.0))))
                act_sc[c, ii * PEER_N_KEYS:(ii + 1) * PEER_N_KEYS, lc * LANES:(lc + 1) * LANES] = (
                    gelu2 * gate).astype(BF16)

        def mix(c):
            acc_sc[t, c] += jnp.dot(pltpu.bitcast(vt_ref[0], BF16), act_sc[c], preferred_element_type=F32)

        a_next = scores(0)
        for p in range(len(pieces)):
            r, c = pieces[p]
            a_cur = a_next
            if p + 1 < len(pieces):
                a_next = scores(p + 1)
            for k in range(PEER_ROW_BLOCK_KEYS):
                gates(p, a_cur, k)
        for c in range(tm // tw):
            mix(c)
        return carry

    lax.fori_loop(0, n_tiles, tile, 0)

    @pl.when(j == pl.num_programs(1) - 1)
    def _():
        for t in range(n_tiles):
            rows = slice(t * tm, (t + 1) * tm)
            mixed_t = jnp.concatenate([acc_sc[t, c] for c in range(tm // tw)], axis=1)
            xn = x_ref[rows, :] + g2_ref[0] * mixed_t.T
            if final_norm:
                ms = jnp.mean(xn * xn, axis=-1, keepdims=True)
                xn = xn * lax.rsqrt(ms + NORM_EPS) * fg_ref[...]
            xo_ref[rows, :] = xn


def _peer_mix(h_t, u_bf, v_t, theta, e1, e2, x2, gate2, final_g, final_norm, B, S):
    T, D = x2.shape
    tm = PEER_TOKEN_TILE
    n_tiles = min(PEER_TILES_PER_STEP, S // tm)
    span = n_tiles * tm
    steps_per_b = S // span
    tok_spec = pl.BlockSpec((n_tiles, PEER_HEADS, PEER_N_KEYS, tm), lambda i, j: (i, 0, 0, 0),
                            pipeline_mode=pl.Buffered(1))
    row_spec = pl.BlockSpec((1, n_tiles, PEER_HEADS, PEER_CHUNK_KEYS, tm), lambda i, j: (j, i, 0, 0, 0))
    return pl.pallas_call(
        functools.partial(_peer_mix_kernel, tm=tm, n_tiles=n_tiles, final_norm=final_norm),
        grid=(T // span, PEER_N_EXPERTS // PEER_CHUNK),
        in_specs=[
            pl.BlockSpec((n_tiles, D, tm), lambda i, j: (i, 0, 0)),
            pl.BlockSpec((PEER_CHUNK // 2, D), lambda i, j: (j, 0)),
            pl.BlockSpec((1, D // 2, PEER_CHUNK), lambda i, j: (j, 0, 0)),
            row_spec, row_spec, tok_spec,
            pl.BlockSpec((span, D), lambda i, j: (i, 0)),
            pl.BlockSpec((1, 1, D), lambda i, j: (i // steps_per_b, 0, 0)),
            pl.BlockSpec((1, D), lambda i, j: (0, 0)),
        ],
        out_specs=pl.BlockSpec((span, D), lambda i, j: (i, 0)),
        out_shape=jax.ShapeDtypeStruct((T, D), F32),
        scratch_shapes=[pltpu.VMEM((n_tiles, tm // PEER_PIECE_TOKENS, D, PEER_PIECE_TOKENS), F32),
                        pltpu.VMEM((tm // PEER_PIECE_TOKENS, PEER_CHUNK, PEER_PIECE_TOKENS), BF16),
                        pltpu.VMEM((n_tiles, PEER_HEADS, tm // LANES, PEER_N_KEYS, LANES), F32)],
        compiler_params=_params("parallel", "arbitrary"),
        name="peer_mix",
    )(h_t, u_bf, v_t, theta, e1, e2, x2, gate2.reshape(B, 1, D), final_g.reshape(1, D))


def _expert_tables_kernel(u_ref, v_ref, ub_ref, vt_ref):
    ub_ref[...] = pltpu.bitcast(u_ref[0].astype(BF16), jnp.uint32)
    vt_ref[0] = pltpu.bitcast(v_ref[0].astype(BF16).T, jnp.uint32)


def _expert_tables(u_all, v_all, layer):
    _, E, D = u_all.shape
    n_chunks = E // PEER_CHUNK
    return pl.pallas_call(
        _expert_tables_kernel,
        grid=(n_chunks,),
        in_specs=[pl.BlockSpec((1, PEER_CHUNK, D), lambda i: (layer, i, 0)),
                  pl.BlockSpec((1, PEER_CHUNK, D), lambda i: (layer, i, 0))],
        out_specs=[pl.BlockSpec((PEER_CHUNK // 2, D), lambda i: (i, 0)),
                   pl.BlockSpec((1, D // 2, PEER_CHUNK), lambda i: (i, 0, 0))],
        out_shape=[jax.ShapeDtypeStruct((E // 2, D), jnp.uint32),
                   jax.ShapeDtypeStruct((n_chunks, D // 2, PEER_CHUNK), jnp.uint32)],
        compiler_params=_params("parallel"),
        name="expert_tables",
    )(u_all, v_all)


def kernel(x, c, norm1_g, norm2_g, ada_w, ada_b, w_in, gate_b, diff_lambda, diff_subln_g,
           w_branch_sb, w_branch_diff, w_out, peer_wq, peer_subkeys, peer_u, peer_v, final_g):
    B, S, D = x.shape
    assert D == D_MODEL and S % LANES == 0
    x2 = x.reshape(B * S, D)
    mod = _modulation(c, ada_w, ada_b)
    rope = _rope_lane_tables(S)
    for l in range(DEPTH):
        lam_init = 0.8 - 0.6 * math.exp(-0.3 * l)
        shift1, scale1, gate1, shift2, scale2, gate2 = (mod[l, :, n] for n in range(6))
        proj = _input_projection(x2, norm1_g[l], scale1, shift1, w_in[l].astype(BF16), rope, B, S)
        y_sb = _sb_attention(proj, B, S)
        y_df = _diff_attention(proj, diff_lambda[l], diff_subln_g[l], lam_init, B, S)
        x2, h2 = _merge_project(y_sb, y_df, proj, gate_b[l], w_branch_sb[l].astype(BF16),
                                w_branch_diff[l].astype(BF16), w_out[l].astype(BF16), x2, gate1,
                                norm2_g[l], scale2, shift2, B, S)
        wq_t = peer_wq[l].T.astype(BF16).reshape(PEER_HEADS, PEER_QUERY_DIM, D)
        h_t, theta, e1, e2 = _peer_select(h2, wq_t, peer_subkeys[l].astype(BF16))
        u_bf, v_t = _expert_tables(peer_u, peer_v, l)
        x2 = _peer_mix(h_t, u_bf, v_t,
                       theta, e1, e2, x2, gate2, final_g, l == DEPTH - 1, B, S)
    return x2.reshape(B, S, D)
```
